```python
import jax
import jax.numpy as jnp
from jax import lax
import numpy as np

D_MODEL = 1024
BATCH = 2
SEQ = 8192
DEPTH = 2

GRID_W = 64
CTX_LEN = 256
N_AB_LAYERS = (DEPTH + 1) // 2
N_C_LAYERS = DEPTH // 2

RW_HEADS = 8
RW_HEAD_DIM = 64
RW_DIM = RW_HEADS * RW_HEAD_DIM
DECAY_LORA = 64
ICLR_LORA = 64
GATE_LORA = 128
RW_COLS = 3 * RW_DIM + 2 * DECAY_LORA + 2 * ICLR_LORA + GATE_LORA
RW_SPLITS = (RW_DIM, 2 * RW_DIM, 3 * RW_DIM, 3 * RW_DIM + DECAY_LORA, 3 * RW_DIM + 2 * DECAY_LORA,
             3 * RW_DIM + 2 * DECAY_LORA + ICLR_LORA, 3 * RW_DIM + 2 * DECAY_LORA + 2 * ICLR_LORA)
LN_X_EPS = 64e-5

MLA_HEADS = 8
MLA_NOPE = 64
MLA_ROPE = 32
MLA_V = 64
MLA_QK = MLA_NOPE + MLA_ROPE
MLA_Q_RANK = 384
MLA_KV_RANK = 256
MLA_COLS = MLA_Q_RANK + MLA_KV_RANK + MLA_ROPE
AB_IN = RW_COLS + MLA_COLS
AB_OUT = RW_DIM + MLA_HEADS * MLA_V

GQA_HEADS = 16
GQA_KV_HEADS = 4
GQA_HEAD_DIM = 64
GQA_IN = (GQA_HEADS + 2 * GQA_KV_HEADS) * GQA_HEAD_DIM

N_EXPERTS = 16
N_GROUPS = 4
EXPERTS_PER_GROUP = N_EXPERTS // N_GROUPS
TOP_K = 2
EXPERT_FF = 256
SHARED_FF = 256

Q_BLOCK = 128
ROPE_THETA = 10000.0
NORM_EPS = 1e-6

kernel_name = 'hybrid_rwkv7_mla_gqa_moe_dit'


def rmsnorm(x, g):
    xf = x.astype(jnp.float32)
    y = xf * lax.rsqrt(jnp.mean(xf * xf, axis=-1, keepdims=True) + NORM_EPS)
    return y.astype(x.dtype) * g


def ada_mods(cvec, w, b):
    m = jax.nn.silu(cvec) @ w + b
    return jnp.split(m[:, None, :], 6, axis=-1)


def modulate(h, shift, scale):
    return h * (1.0 + scale) + shift


def rope_1d(x, pos):
    half = x.shape[-1] // 2
    inv = ROPE_THETA ** (-jnp.arange(half, dtype=jnp.float32) / half)
    ang = pos[:, None] * inv[None, :]
    cos = jnp.cos(ang)[None, :, None, :].astype(x.dtype)
    sin = jnp.sin(ang)[None, :, None, :].astype(x.dtype)
    x1, x2 = x[..., :half], x[..., half:]
    return jnp.concatenate([x1 * cos - x2 * sin, x2 * cos + x1 * sin], axis=-1)


def axial_rope(x, row, col):
    h = x.shape[-1] // 2
    return jnp.concatenate([rope_1d(x[..., :h], row), rope_1d(x[..., h:], col)], axis=-1)


def ctx_attention(q, k, v, scale):
    B, L, H, d = q.shape
    Hk = k.shape[2]
    qg = q.reshape(B, L, Hk, H // Hk, d)
    s = jnp.einsum('bqkgd,bskd->bkgqs', qg, k).astype(jnp.float32) * scale
    p = jax.nn.softmax(s, axis=-1).astype(v.dtype)
    o = jnp.einsum('bkgqs,bskd->bqkgd', p, v)
    return o.reshape(B, L, H, v.shape[-1])


def latent_attention(q, k_lat, v_lat, k_ctx, v_ctx, scale):
    B, S, H, d = q.shape
    Hk = k_lat.shape[2]
    G = H // Hk
    dv = v_lat.shape[-1]
    n_ctx = k_ctx.shape[1]
    n_blk = S // Q_BLOCK
    qb = q.reshape(B, n_blk, Q_BLOCK, Hk, G, d).transpose(1, 0, 2, 3, 4, 5)

    def one_block(qblk):
        s_ctx = jnp.einsum('bqkgd,bskd->bkgqs', qblk, k_ctx)
        s_lat = jnp.einsum('bqkgd,bskd->bkgqs', qblk, k_lat)
        s = jnp.concatenate([s_ctx, s_lat], axis=-1).astype(jnp.float32) * scale
        p = jax.nn.softmax(s, axis=-1).astype(v_lat.dtype)
        return (jnp.einsum('bkgqs,bskd->bqkgd', p[..., :n_ctx], v_ctx)
                + jnp.einsum('bkgqs,bskd->bqkgd', p[..., n_ctx:], v_lat))

    o = lax.map(one_block, qb)
    return o.transpose(1, 0, 2, 3, 4, 5).reshape(B, S, H, dv)


def centred_shift_mix(p, mu):
    prev = jnp.pad(p[:, :-1], ((0, 0), (1, 0), (0, 0)))
    nxt = jnp.pad(p[:, 1:], ((0, 0), (0, 1), (0, 0)))
    return p + mu * (0.5 * (prev + nxt) - p)


def wkv7_scan(state0, r, w, k, v, a_vec, b_vec, reverse):
    xs = tuple(t.transpose(1, 0, 2, 3) for t in (r, w, k, v, a_vec, b_vec))

    def step(S, inp):
        r_t, w_t, k_t, v_t, a_t, b_t = inp
        sa = jnp.einsum('bhij,bhj->bhi', S, a_t)
        S = (S * w_t[:, :, None, :] + sa[..., None] * b_t[:, :, None, :]
             + v_t[..., None] * k_t[:, :, None, :])
        return S, jnp.einsum('bhij,bhj->bhi', S, r_t)

    s_fin, ys = lax.scan(step, state0, xs, reverse=reverse)
    return s_fin, ys.transpose(1, 0, 2, 3)


def rwkv_prep(p, prm):
    p = centred_shift_mix(p, prm['mu'])
    r, k, v, w1f, w1b, a1f, a1b, g1 = jnp.split(p, RW_SPLITS, axis=-1)
    B, L = p.shape[:2]

    def heads(t):
        return t.astype(jnp.float32).reshape(B, L, RW_HEADS, RW_HEAD_DIM)

    kk = heads(k * prm['k_k'])
    kk = kk * lax.rsqrt(jnp.sum(kk * kk, axis=-1, keepdims=True) + 1e-12)
    kf = k.astype(jnp.float32)
    dirs = []
    for d, (w1, a1) in enumerate(((w1f, a1f), (w1b, a1b))):
        z = (prm['w0'][d] + jnp.tanh(w1) @ prm['w2'][d]).astype(jnp.float32)
        decay = jnp.exp(-jnp.exp(-jax.nn.softplus(-z) - 0.5))
        a = jax.nn.sigmoid((prm['a0'][d] + a1 @ prm['a2'][d]).astype(jnp.float32))
        k_d = kf * (1.0 + (a - 1.0) * prm['k_a'])
        dirs.append((heads(decay), heads(k_d), -kk, kk * heads(a)))
    g = jax.nn.sigmoid(g1) @ prm['g2']
    return heads(r), heads(v), dirs, g


def rwkv_scans(r, v, dirs, states0):
    ys, finals = [], []
    for d, (decay, k_d, a_vec, b_vec) in enumerate(dirs):
        s_fin, y_d = wkv7_scan(states0[d], r, decay, k_d, v, a_vec, b_vec, reverse=(d == 1))
        ys.append(y_d)
        finals.append(s_fin)
    return ys[0] + ys[1], (finals[0], finals[1])


def rwkv_out(y, r, v, dirs, g, prm):
    B, L = y.shape[:2]
    mean = jnp.mean(y, axis=-1, keepdims=True)
    var = jnp.mean(jnp.square(y - mean), axis=-1, keepdims=True)
    yn = ((y - mean) * lax.rsqrt(var + LN_X_EPS)).reshape(B, L, RW_DIM) * prm['ln_w'] + prm['ln_b']
    k_sum = dirs[0][1] + dirs[1][1]
    bonus = jnp.sum(r * k_sum * prm['r_k'], axis=-1, keepdims=True) * v
    return ((yn + bonus.reshape(B, L, RW_DIM)) * g).astype(g.dtype)


def mla_qkv(p, prm, pos):
    B, L = p.shape[:2]
    q_a, kv_a, k_rope = jnp.split(p, [MLA_Q_RANK, MLA_Q_RANK + MLA_KV_RANK], axis=-1)
    q = (rmsnorm(q_a, prm['g_qa']) @ prm['w_q_up']).reshape(B, L, MLA_HEADS, MLA_QK)
    kv = (rmsnorm(kv_a, prm['g_kva']) @ prm['w_kv_up']).reshape(B, L, MLA_HEADS, MLA_NOPE + MLA_V)
    k_nope, v = kv[..., :MLA_NOPE], kv[..., MLA_NOPE:]
    k_rope = jnp.broadcast_to(k_rope[:, :, None, :], (B, L, MLA_HEADS, MLA_ROPE))
    k = jnp.concatenate([k_nope, k_rope], axis=-1)
    q = rmsnorm(q, prm['g_q'])
    k = rmsnorm(k, prm['g_k'])
    if pos is not None:
        row, col = pos
        q = jnp.concatenate([q[..., :MLA_NOPE], axial_rope(q[..., MLA_NOPE:], row, col)], axis=-1)
        k = jnp.concatenate([k[..., :MLA_NOPE], axial_rope(k[..., MLA_NOPE:], row, col)], axis=-1)
    return q, k, v


def mixer_ab(hc, hl, row, col, need_ctx, prm):
    B, S = hl.shape[:2]
    pc = hc @ prm['w_in']
    pl = hl @ prm['w_in']
    r_c, v_c, dirs_c, g_c = rwkv_prep(pc[..., :RW_COLS], prm)
    r_l, v_l, dirs_l, g_l = rwkv_prep(pl[..., :RW_COLS], prm)
    zero = jnp.zeros((B, RW_HEADS, RW_HEAD_DIM, RW_HEAD_DIM), jnp.float32)
    y_c, states_c = rwkv_scans(r_c, v_c, dirs_c, (zero, zero))
    y_l, _ = rwkv_scans(r_l, v_l, dirs_l, states_c)
    q_c, k_c, vm_c = mla_qkv(pc[..., RW_COLS:], prm, None)
    q_l, k_l, vm_l = mla_qkv(pl[..., RW_COLS:], prm, (row, col))
    scale = MLA_QK ** -0.5
    o_l = latent_attention(q_l, k_l, vm_l, k_c, vm_c, scale).reshape(B, S, MLA_HEADS * MLA_V)
    out_l = jnp.concatenate([rwkv_out(y_l, r_l, v_l, dirs_l, g_l, prm), o_l], axis=-1) @ prm['w_out']
    if not need_ctx:
        return None, out_l
    C = hc.shape[1]
    o_c = ctx_attention(q_c, k_c, vm_c, scale).reshape(B, C, MLA_HEADS * MLA_V)
    out_c = jnp.concatenate([rwkv_out(y_c, r_c, v_c, dirs_c, g_c, prm), o_c], axis=-1) @ prm['w_out']
    return out_c, out_l


def gqa_qkv(h, prm, pos):
    B, L = h.shape[:2]
    p = h @ prm['w_in']
    q, k, v = jnp.split(p, [GQA_HEADS * GQA_HEAD_DIM, (GQA_HEADS + GQA_KV_HEADS) * GQA_HEAD_DIM], axis=-1)
    q = rmsnorm(q.reshape(B, L, GQA_HEADS, GQA_HEAD_DIM), prm['g_q'])
    k = rmsnorm(k.reshape(B, L, GQA_KV_HEADS, GQA_HEAD_DIM), prm['g_k'])
    v = v.reshape(B, L, GQA_KV_HEADS, GQA_HEAD_DIM)
    if pos is not None:
        q = axial_rope(q, pos[0], pos[1])
        k = axial_rope(k, pos[0], pos[1])
    return q, k, v


def mixer_gqa(hc, hl, row, col, need_ctx, prm):
    B, S = hl.shape[:2]
    q_c, k_c, v_c = gqa_qkv(hc, prm, None)
    q_l, k_l, v_l = gqa_qkv(hl, prm, (row, col))
    scale = GQA_HEAD_DIM ** -0.5
    out_l = latent_attention(q_l, k_l, v_l, k_c, v_c, scale).reshape(B, S, GQA_HEADS * GQA_HEAD_DIM) @ prm['w_out']
    if not need_ctx:
        return None, out_l
    C = hc.shape[1]
    out_c = ctx_attention(q_c, k_c, v_c, scale).reshape(B, C, GQA_HEADS * GQA_HEAD_DIM) @ prm['w_out']
    return out_c, out_l


def moe_ffn(h, prm):
    B, L, D = h.shape
    t = h.reshape(B * L, D)
    scores = jax.nn.sigmoid((t @ prm['router_w']).astype(jnp.float32))
    grouped = (scores + prm['router_b']).reshape(-1, N_GROUPS, EXPERTS_PER_GROUP)
    group_score = jnp.sum(lax.top_k(grouped, 2)[0], axis=-1)
    grp = jnp.argmax(group_score, axis=-1)
    in_group = jnp.take_along_axis(grouped, grp[:, None, None], axis=1)[:, 0]
    _, local = lax.top_k(in_group, TOP_K)
    idx = grp[:, None] * EXPERTS_PER_GROUP + local
    wts = jnp.take_along_axis(scores, idx, axis=-1)
    wts = wts / jnp.sum(wts, axis=-1, keepdims=True)
    combine = jnp.einsum('tk,tke->te', wts, jax.nn.one_hot(idx, N_EXPERTS, dtype=jnp.float32)).astype(h.dtype)
    g = jnp.einsum('td,edf->tef', t, prm['w_gate'])
    u = jnp.einsum('td,edf->tef', t, prm['w_up'])
    y = jnp.einsum('tef,efd->td', jax.nn.silu(g) * u * combine[..., None], prm['w_down'])
    y = y + (jax.nn.silu(t @ prm['sh_gate']) * (t @ prm['sh_up'])) @ prm['sh_down']
    return y.reshape(B, L, D)


def setup_inputs(seed: int = 0) -> dict:
    key = jax.random.key(seed)
    ks = iter(jax.random.split(key, 64))

    def nrm(shape, scale):
        return jax.random.normal(next(ks), shape, jnp.float32) * scale

    def unif(shape, lo, hi):
        return jax.random.uniform(next(ks), shape, jnp.float32, lo, hi)

    def gain(shape):
        return 1.0 + nrm(shape, 0.05)

    D = D_MODEL
    A, Cn = N_AB_LAYERS, N_C_LAYERS
    return {
        'x': nrm((BATCH, SEQ, D), 1.0),
        'c': nrm((BATCH, D), 1.0),
        'ctx': nrm((BATCH, CTX_LEN, D), 1.0),
        'c_ctx': nrm((D,), 1.0),
        'ada_w': nrm((DEPTH, D, 6 * D), 0.5 * D ** -0.5),
        'ada_b': nrm((DEPTH, 6 * D), 0.02),
        'norm1_g': gain((DEPTH, D)),
        'norm2_g': gain((DEPTH, D)),
        'ab_w_in': nrm((A, D, AB_IN), D ** -0.5),
        'ab_w_out': nrm((A, AB_OUT, D), AB_OUT ** -0.5),
        'rw_mu': unif((A, RW_COLS), 0.0, 1.0),
        'rw_w0': unif((A, 2, RW_DIM), -6.0, -1.0),
        'rw_w2': nrm((A, 2, DECAY_LORA, RW_DIM), 0.5 * DECAY_LORA ** -0.5),
        'rw_a0': nrm((A, 2, RW_DIM), 0.5),
        'rw_a2': nrm((A, 2, ICLR_LORA, RW_DIM), 0.5 * ICLR_LORA ** -0.5),
        'rw_k_k': 0.85 + nrm((A, RW_DIM), 0.05),
        'rw_k_a': gain((A, RW_DIM)),
        'rw_r_k': nrm((A, RW_HEADS, RW_HEAD_DIM), 0.1),
        'rw_g2': nrm((A, GATE_LORA, RW_DIM), GATE_LORA ** -0.5),
        'rw_ln_w': gain((A, RW_DIM)),
        'rw_ln_b': nrm((A, RW_DIM), 0.02),
        'mla_g_qa': gain((A, MLA_Q_RANK)),
        'mla_w_q_up': nrm((A, MLA_Q_RANK, MLA_HEADS * MLA_QK), MLA_Q_RANK ** -0.5),
        'mla_g_kva': gain((A, MLA_KV_RANK)),
        'mla_w_kv_up': nrm((A, MLA_KV_RANK, MLA_HEADS * (MLA_NOPE + MLA_V)), MLA_KV_RANK ** -0.5),
        'mla_g_q': gain((A, MLA_QK)),
        'mla_g_k': gain((A, MLA_QK)),
        'gqa_w_in': nrm((Cn, D, GQA_IN), D ** -0.5),
        'gqa_w_out': nrm((Cn, GQA_HEADS * GQA_HEAD_DIM, D), (GQA_HEADS * GQA_HEAD_DIM) ** -0.5),
        'gqa_g_q': gain((Cn, GQA_HEAD_DIM)),
        'gqa_g_k': gain((Cn, GQA_HEAD_DIM)),
        'router_w': nrm((D, N_EXPERTS), D ** -0.5),
        'router_b': nrm((N_EXPERTS,), 0.01),
        'moe_w_gate': nrm((DEPTH, N_EXPERTS, D, EXPERT_FF), D ** -0.5),
        'moe_w_up': nrm((DEPTH, N_EXPERTS, D, EXPERT_FF), D ** -0.5),
        'moe_w_down': nrm((DEPTH, N_EXPERTS, EXPERT_FF, D), EXPERT_FF ** -0.5),
        'shared_w_gate': nrm((DEPTH, D, SHARED_FF), D ** -0.5),
        'shared_w_up': nrm((DEPTH, D, SHARED_FF), D ** -0.5),
        'shared_w_down': nrm((DEPTH, SHARED_FF, D), SHARED_FF ** -0.5),
    }


def reference(x, c, ctx, c_ctx, ada_w, ada_b, norm1_g, norm2_g, ab_w_in, ab_w_out, rw_mu, rw_w0,
              rw_w2, rw_a0, rw_a2, rw_k_k, rw_k_a, rw_r_k, rw_g2, rw_ln_w, rw_ln_b, mla_g_qa,
              mla_w_q_up, mla_g_kva, mla_w_kv_up, mla_g_q, mla_g_k, gqa_w_in, gqa_w_out, gqa_g_q,
              gqa_g_k, router_w, router_b, moe_w_gate, moe_w_up, moe_w_down, shared_w_gate,
              shared_w_up, shared_w_down):
    seq = x.shape[1]
    rows = seq // GRID_W
    row = jnp.repeat(jnp.arange(rows, dtype=jnp.float32), GRID_W)
    col = jnp.tile(jnp.arange(GRID_W, dtype=jnp.float32), rows)
    xl, xc = x, ctx
    for l in range(DEPTH):
        need_ctx = l < DEPTH - 1
        i = l // 2
        sh1_l, sc1_l, ga1_l, sh2_l, sc2_l, ga2_l = ada_mods(c, ada_w[l], ada_b[l])
        sh1_c, sc1_c, ga1_c, sh2_c, sc2_c, ga2_c = ada_mods(c_ctx[None, :], ada_w[l], ada_b[l])
        hl = modulate(rmsnorm(xl, norm1_g[l]), sh1_l, sc1_l)
        hc = modulate(rmsnorm(xc, norm1_g[l]), sh1_c, sc1_c)
        if l % 2 == 0:
            prm = dict(w_in=ab_w_in[i], w_out=ab_w_out[i], mu=rw_mu[i], w0=rw_w0[i], w2=rw_w2[i],
                       a0=rw_a0[i], a2=rw_a2[i], k_k=rw_k_k[i], k_a=rw_k_a[i], r_k=rw_r_k[i],
                       g2=rw_g2[i], ln_w=rw_ln_w[i], ln_b=rw_ln_b[i], g_qa=mla_g_qa[i],
                       w_q_up=mla_w_q_up[i], g_kva=mla_g_kva[i], w_kv_up=mla_w_kv_up[i],
                       g_q=mla_g_q[i], g_k=mla_g_k[i])
            oc, ol = mixer_ab(hc, hl, row, col, need_ctx, prm)
        else:
            prm = dict(w_in=gqa_w_in[i], w_out=gqa_w_out[i], g_q=gqa_g_q[i], g_k=gqa_g_k[i])
            oc, ol = mixer_gqa(hc, hl, row, col, need_ctx, prm)
        moe_prm = dict(router_w=router_w, router_b=router_b, w_gate=moe_w_gate[l], w_up=moe_w_up[l],
                       w_down=moe_w_down[l], sh_gate=shared_w_gate[l], sh_up=shared_w_up[l],
                       sh_down=shared_w_down[l])
        xl = xl + ga1_l * ol
        hl = modulate(rmsnorm(xl, norm2_g[l]), sh2_l, sc2_l)
        xl = xl + ga2_l * moe_ffn(hl, moe_prm)
        if need_ctx:
            xc = xc + ga1_c * oc
            hc = modulate(rmsnorm(xc, norm2_g[l]), sh2_c, sc2_c)
            xc = xc + ga2_c * moe_ffn(hc, moe_prm)
    return xl
```

```python
import functools
import math

import jax
import jax.numpy as jnp
from jax import lax
from jax.experimental import pallas as pl
from jax.experimental.pallas import tpu as pltpu

F32 = jnp.float32
BF16 = jnp.bfloat16
HIGHEST = lax.Precision.HIGHEST

GRID_W = 64
RW_HEADS = 8
RW_HEAD_DIM = 64
RW_DIM = RW_HEADS * RW_HEAD_DIM
DECAY_LORA = 64
ICLR_LORA = 64
GATE_LORA = 128
RW_COLS = 3 * RW_DIM + 2 * DECAY_LORA + 2 * ICLR_LORA + GATE_LORA
LN_X_EPS = 64e-5
MLA_HEADS = 8
MLA_NOPE = 64
MLA_ROPE = 32
MLA_V = 64
MLA_QK = MLA_NOPE + MLA_ROPE
MLA_Q_RANK = 384
MLA_KV_RANK = 256
GQA_HEADS = 16
GQA_KV_HEADS = 4
GQA_HEAD_DIM = 64
N_EXPERTS = 16
N_GROUPS = 4
EXPERTS_PER_GROUP = N_EXPERTS // N_GROUPS
ROPE_THETA = 10000.0
NORM_EPS = 1e-6

LANES = 128
V7X_VMEM_BYTES = 64 * 1024 * 1024
VMEM_LIMIT = V7X_VMEM_BYTES - 8 * 1024 * 1024

TOKEN_BLOCK = 256
SCAN_CHUNK = 64
ATTN_BLOCK = 256


def _params(*sem):
    return pltpu.CompilerParams(dimension_semantics=sem, vmem_limit_bytes=VMEM_LIMIT)


def _bdot(a, b):
    return jnp.dot(a.astype(BF16), b.astype(BF16), preferred_element_type=F32)


def _bdot_nt(a, b):
    return lax.dot_general(a.astype(BF16), b.astype(BF16), (((1,), (1,)), ((), ())),
                           preferred_element_type=F32)


def _bdot_tn(a, b):
    return lax.dot_general(a.astype(BF16), b.astype(BF16), (((0,), (0,)), ((), ())),
                           preferred_element_type=F32)


def _sigmoid(x):
    return 1.0 / (1.0 + jnp.exp(-x))


def _silu(x):
    return x * _sigmoid(x)


def _softplus(x):
    return jnp.maximum(x, 0.0) + jnp.log(1.0 + jnp.exp(-jnp.abs(x)))


def _rms(x):
    return x * lax.rsqrt(jnp.mean(x * x, axis=-1, keepdims=True) + NORM_EPS)


def _row_mods(ml_ref, mc_ref, first_row, rows, n_ctx, idx):
    r = first_row + lax.broadcasted_iota(jnp.int32, (rows, 1), 0)
    return jnp.where(r < n_ctx, mc_ref[0, idx:idx + 1, :], ml_ref[0, idx:idx + 1, :])


def _ada_kernel(c_ref, w_ref, b_ref, o_ref):
    s = _silu(c_ref[...])
    o_ref[0] = jnp.dot(s, w_ref[0], preferred_element_type=F32, precision=HIGHEST) + b_ref[0]


def _ada_mods(cvec, ada_w, ada_b):
    depth, d, n = ada_w.shape
    rows = cvec.shape[0]
    bn = n // 4
    return pl.pallas_call(
        _ada_kernel,
        grid=(depth, n // bn),
        in_specs=[pl.BlockSpec((rows, d), lambda l, j: (0, 0)),
                  pl.BlockSpec((1, d, bn), lambda l, j: (l, 0, j)),
                  pl.BlockSpec((1, 1, bn), lambda l, j: (l, 0, j))],
        out_specs=pl.BlockSpec((1, rows, bn), lambda l, j: (l, 0, j)),
        out_shape=jax.ShapeDtypeStruct((depth, rows, n), F32),
        compiler_params=_params("parallel", "parallel"),
        name="ada_mods",
    )(cvec, ada_w, ada_b.reshape(depth, 1, n))


def _ln_mod_mm_kernel(x_ref, g_ref, ml_ref, mc_ref, *refs, n_w, bl, n_ctx):
    w_refs, o_refs = refs[:n_w], refs[n_w:]
    first = pl.program_id(1) * bl
    shift = _row_mods(ml_ref, mc_ref, first, bl, n_ctx, 0)
    scale = _row_mods(ml_ref, mc_ref, first, bl, n_ctx, 1)
    h = (_rms(x_ref[0]) * g_ref[...]) * (1.0 + scale) + shift
    hb = h.astype(BF16)
    for w_ref, o_ref in zip(w_refs, o_refs):
        o_ref[0] = jnp.dot(hb, w_ref[...], preferred_element_type=F32).astype(o_ref.dtype)


def _ln_mod_mm(x, g, mod_l, mod_c, ws, n_ctx):
    b, lt, d = x.shape
    bl = TOKEN_BLOCK
    kern = functools.partial(_ln_mod_mm_kernel, n_w=len(ws), bl=bl, n_ctx=n_ctx)
    return pl.pallas_call(
        kern,
        grid=(b, lt // bl),
        in_specs=[pl.BlockSpec((1, bl, d), lambda i, j: (i, j, 0)),
                  pl.BlockSpec((1, d), lambda i, j: (0, 0)),
                  pl.BlockSpec((1, 6, d), lambda i, j: (i, 0, 0)),
                  pl.BlockSpec((1, 6, d), lambda i, j: (0, 0, 0))]
        + [pl.BlockSpec(w.shape, lambda i, j: (0, 0)) for w in ws],
        out_specs=[pl.BlockSpec((1, bl, w.shape[1]), lambda i, j: (i, j, 0)) for w in ws],
        out_shape=[jax.ShapeDtypeStruct((b, lt, w.shape[1]), F32) for w in ws],
        compiler_params=_params("parallel", "parallel"),
        name="ln_mod_mm",
    )(x, g.reshape(1, d), mod_l, mod_c, *ws)


def _rwkv_prep_kernel(p_ref, prev_ref, next_ref, mu_ref, w0_ref, w2_ref, a0_ref, a2_ref, kk_ref,
                      ka_ref, rk_ref, g2_ref, r_ref, v_ref, na_ref, lw_ref, kd_ref, bd_ref,
                      gate_ref, bonus_ref, *, bl):
    p = p_ref[0]
    row = lax.broadcasted_iota(jnp.int32, p.shape, 0)
    prev = jnp.where(row == 0, prev_ref[0, 0], pltpu.roll(p, 1, 0))
    nxt = jnp.where(row == bl - 1, next_ref[0, 0], pltpu.roll(p, bl - 1, 0))
    pm = p + mu_ref[...] * (0.5 * (prev + nxt) - p)

    o = 3 * RW_DIM
    r = pm[:, 0:RW_DIM]
    k = pm[:, RW_DIM:2 * RW_DIM]
    v = pm[:, 2 * RW_DIM:o]
    w1 = (pm[:, o:o + DECAY_LORA], pm[:, o + DECAY_LORA:o + 2 * DECAY_LORA])
    o += 2 * DECAY_LORA
    a1 = (pm[:, o:o + ICLR_LORA], pm[:, o + ICLR_LORA:o + 2 * ICLR_LORA])
    o += 2 * ICLR_LORA
    g1 = pm[:, o:o + GATE_LORA]

    kk = k * kk_ref[...]
    lw, kd, ad = [], [], []
    for d in range(2):
        z = w0_ref[d:d + 1, :] + _bdot(jnp.tanh(w1[d]), w2_ref[d])
        lw.append(-jnp.exp(-_softplus(-z) - 0.5))
        a = _sigmoid(a0_ref[d:d + 1, :] + _bdot(a1[d], a2_ref[d]))
        ad.append(a)
        kd.append(k * (1.0 + (a - 1.0) * ka_ref[...]))
    gate = _bdot(_sigmoid(g1), g2_ref[...])
    rkk = r * (kd[0] + kd[1]) * rk_ref[...]

    for h in range(RW_HEADS):
        sl = slice(h * RW_HEAD_DIM, (h + 1) * RW_HEAD_DIM)
        kk_h = kk[:, sl]
        kk_h = kk_h * lax.rsqrt(jnp.sum(kk_h * kk_h, axis=-1, keepdims=True) + 1e-12)
        v_h = v[:, sl]
        r_ref[0, h] = r[:, sl]
        v_ref[0, h] = v_h
        na_ref[0, h] = -kk_h
        for d in range(2):
            lw_ref[d, 0, h] = lw[d][:, sl]
            kd_ref[d, 0, h] = kd[d][:, sl]
            bd_ref[d, 0, h] = kk_h * ad[d][:, sl]
        gate_ref[0, h] = gate[:, sl]
        bonus_ref[0, h] = jnp.sum(rkk[:, sl], axis=-1, keepdims=True) * v_h


def _rwkv_prep(p, n_ctx, mu, w0, w2, a0, a2, k_k, k_a, r_k, g2):
    b, lt, cols = p.shape
    bl = TOKEN_BLOCK
    nblk = lt // bl
    zero = jnp.zeros((b, 1, cols), F32)
    lastrows = p[:, bl - 1::bl]
    firstrows = p[:, ::bl]
    prev_edge = jnp.concatenate([zero, lastrows[:, :-1]], axis=1)
    next_edge = jnp.concatenate([firstrows[:, 1:], zero], axis=1)
    cb = n_ctx // bl
    blk = jnp.arange(nblk)[None, :, None]
    prev_edge = jnp.where(blk == cb, 0.0, prev_edge).reshape(b, nblk, 1, cols)
    next_edge = jnp.where(blk == cb - 1, 0.0, next_edge).reshape(b, nblk, 1, cols)

    hm = jax.ShapeDtypeStruct((b, RW_HEADS, lt, RW_HEAD_DIM), F32)
    hm2 = jax.ShapeDtypeStruct((2, b, RW_HEADS, lt, RW_HEAD_DIM), F32)
    hm_spec = pl.BlockSpec((1, RW_HEADS, bl, RW_HEAD_DIM), lambda i, j: (i, 0, j, 0))
    hm2_spec = pl.BlockSpec((2, 1, RW_HEADS, bl, RW_HEAD_DIM), lambda i, j: (0, i, 0, j, 0))
    full = lambda a: pl.BlockSpec(a.shape, lambda i, j: (0,) * a.ndim)
    consts = [mu.reshape(1, cols), w0, w2.astype(BF16), a0, a2.astype(BF16), k_k.reshape(1, RW_DIM),
              k_a.reshape(1, RW_DIM), r_k.reshape(1, RW_DIM), g2.astype(BF16)]
    return pl.pallas_call(
        functools.partial(_rwkv_prep_kernel, bl=bl),
        grid=(b, nblk),
        in_specs=[pl.BlockSpec((1, bl, cols), lambda i, j: (i, j, 0)),
                  pl.BlockSpec((1, 1, 1, cols), lambda i, j: (i, j, 0, 0)),
                  pl.BlockSpec((1, 1, 1, cols), lambda i, j: (i, j, 0, 0))]
        + [full(a) for a in consts],
        out_specs=[hm_spec, hm_spec, hm_spec, hm2_spec, hm2_spec, hm2_spec, hm_spec, hm_spec],
        out_shape=[hm, hm, hm, hm2, hm2, hm2, hm, hm],
        compiler_params=_params("parallel", "parallel"),
        name="rwkv_prep",
    )(p, prev_edge, next_edge, *consts)


def _wkv_kernel(r_ref, v_ref, a_ref, lw_ref, k_ref, b_ref, y_ref, st_ref, *, cn, chains):
    d = pl.program_id(0)

    @pl.when(pl.program_id(1) == 0)
    def _():
        st_ref[...] = jnp.zeros_like(st_ref)

    row = lax.broadcasted_iota(jnp.int32, (cn, cn), 0)
    col = lax.broadcasted_iota(jnp.int32, (cn, cn), 1)
    lead = (row - col) * jnp.where(d == 0, 1, -1)
    before = lead > 0
    upto = lead >= 0
    eye = (col == row).astype(F32)
    cum_mat = upto.astype(F32)
    hd = r_ref.shape[-1]
    eye_h = (lax.broadcasted_iota(jnp.int32, (hd, hd), 0)
             == lax.broadcasted_iota(jnp.int32, (hd, hd), 1)).astype(F32)
    n_double = int(math.log2(cn)) - 1

    def chain(n, carry):
        r, v, a = r_ref[n], v_ref[n], a_ref[n]
        lw, k, b = lw_ref[0, n], k_ref[0, n], b_ref[0, n]
        cum = jnp.dot(cum_mat, lw, preferred_element_type=F32, precision=HIGHEST)
        tot = jnp.sum(lw, axis=0, keepdims=True)
        at = a * jnp.exp(cum - lw)
        rt = r * jnp.exp(cum)
        einv = jnp.exp(-cum)
        bt, kt = b * einv, k * einv
        eend = jnp.exp(tot - cum)
        bh, kh = b * eend, k * eend

        a_ab = jnp.where(before, _bdot_nt(at, bt), 0.0)
        a_ak = jnp.where(before, _bdot_nt(at, kt), 0.0)
        a_rb = jnp.where(upto, _bdot_nt(rt, bt), 0.0)
        a_rk = jnp.where(upto, _bdot_nt(rt, kt), 0.0)

        x = a_ab
        t = eye + x
        for _ in range(n_double):
            x = _bdot(x, x)
            t = t + _bdot(t, x)
        w = _bdot(t, at)
        up = _bdot(t, _bdot(a_ak, v))
        qe = rt + _bdot(a_rb, w)
        yp = _bdot(a_rb, up) + _bdot(a_rk, v)
        mc = eye_h * jnp.exp(tot) + _bdot_tn(bh, w)
        nc = _bdot_tn(bh, up) + _bdot_tn(kh, v)

        st = st_ref[n]
        y_ref[0, n] = _bdot(qe, st) + yp
        st_ref[n] = _bdot(mc, st) + nc
        return carry

    lax.fori_loop(0, chains, chain, 0)


def _wkv_scan(r, v, na, lw, kd, bd, n_ctx):
    b, h, lt, hd = r.shape
    cn = SCAN_CHUNK
    chains = b * h
    n_all, n_c = lt // cn, n_ctx // cn

    def chunk(d, i):
        rev = jnp.where(i < n_c, n_c - 1 - i, n_all - 1 - (i - n_c))
        return jnp.where(d == 0, i, rev)

    shared = pl.BlockSpec((chains, cn, hd), lambda d, i: (0, chunk(d, i), 0))
    per_dir = pl.BlockSpec((1, chains, cn, hd), lambda d, i: (d, 0, chunk(d, i), 0))
    flat = lambda t: t.reshape(t.shape[:-4] + (chains, lt, hd))
    y = pl.pallas_call(
        functools.partial(_wkv_kernel, cn=cn, chains=chains),
        grid=(2, n_all),
        in_specs=[shared, shared, shared, per_dir, per_dir, per_dir],
        out_specs=per_dir,
        out_shape=jax.ShapeDtypeStruct((2, chains, lt, hd), F32),
        scratch_shapes=[pltpu.VMEM((chains, hd, hd), F32)],
        compiler_params=_params("parallel", "arbitrary"),
        name="wkv_scan",
    )(flat(r), flat(v), flat(na), flat(lw), flat(kd), flat(bd))
    return y.reshape(2, b, h, lt, hd)


def _rope(x, cos, sin, half):
    lane = lax.broadcasted_iota(jnp.int32, x.shape, 1)
    width = x.shape[1]
    partner = jnp.where(lane % (2 * half) < half, pltpu.roll(x, width - half, 1), pltpu.roll(x, half, 1))
    return x * cos + partner * sin


def _mla_qkv_kernel(p_ref, gqa_ref, gkva_ref, wq_ref, wk_ref, wv_ref, gq_ref, gk_ref, cos_ref, sin_ref,
                    q_ref, kt_ref, v_ref, *, scale):
    p = p_ref[0]
    q_a = p[:, 0:MLA_Q_RANK]
    kv_a = p[:, MLA_Q_RANK:MLA_Q_RANK + MLA_KV_RANK]
    k_rope = p[:, MLA_Q_RANK + MLA_KV_RANK:MLA_Q_RANK + MLA_KV_RANK + LANES]
    qa_n = (_rms(q_a) * gqa_ref[...]).astype(BF16)
    kva_n = (_rms(kv_a) * gkva_ref[...]).astype(BF16)
    q_all = jnp.dot(qa_n, wq_ref[...], preferred_element_type=F32)
    k_all = jnp.dot(kva_n, wk_ref[...], preferred_element_type=F32)
    v_all = jnp.dot(kva_n, wv_ref[...], preferred_element_type=F32)
    cos, sin = cos_ref[...], sin_ref[...]
    for h in range(MLA_HEADS):
        sl = slice(h * LANES, (h + 1) * LANES)
        q = q_all[:, sl]
        q = q * lax.rsqrt(jnp.sum(q * q, axis=-1, keepdims=True) / MLA_QK + NORM_EPS) * gq_ref[...]
        q = _rope(q, cos, sin, MLA_ROPE // 4)
        q_ref[0, h] = (q * scale).astype(q_ref.dtype)
        k = k_all[:, sl] + k_rope
        k = k * lax.rsqrt(jnp.sum(k * k, axis=-1, keepdims=True) / MLA_QK + NORM_EPS) * gk_ref[...]
        k = _rope(k, cos, sin, MLA_ROPE // 4)
        kt_ref[0, h, 0] = k.T.astype(kt_ref.dtype)
        v_ref[0, h] = v_all[:, h * LANES:h * LANES + MLA_V].astype(v_ref.dtype)


def _pad_heads(w, heads, width):
    k = w.shape[0]
    w = w.reshape(k, heads, width)
    return jnp.pad(w, ((0, 0), (0, 0), (0, LANES - width))).reshape(k, heads * LANES)


def _mla_qkv(p, g_qa, w_q_up, g_kva, w_kv_up, g_q, g_k, cos, sin):
    b, lt, cols = p.shape
    bl = ATTN_BLOCK
    wq = _pad_heads(w_q_up, MLA_HEADS, MLA_QK).astype(BF16)
    wkv = w_kv_up.reshape(MLA_KV_RANK, MLA_HEADS, MLA_NOPE + MLA_V)
    wk = _pad_heads(wkv[:, :, :MLA_NOPE].reshape(MLA_KV_RANK, -1), MLA_HEADS, MLA_NOPE).astype(BF16)
    wv = _pad_heads(wkv[:, :, MLA_NOPE:].reshape(MLA_KV_RANK, -1), MLA_HEADS, MLA_V).astype(BF16)
    pad = lambda g: jnp.pad(g, (0, LANES - MLA_QK)).reshape(1, LANES)
    consts = [g_qa.reshape(1, -1), g_kva.reshape(1, -1), wq, wk, wv, pad(g_q), pad(g_k)]
    full = lambda a: pl.BlockSpec(a.shape, lambda i, j: (0,) * a.ndim)
    tab = pl.BlockSpec((bl, LANES), lambda i, j: (j, 0))
    return pl.pallas_call(
        functools.partial(_mla_qkv_kernel, scale=MLA_QK ** -0.5),
        grid=(b, lt // bl),
        in_specs=[pl.BlockSpec((1, bl, cols), lambda i, j: (i, j, 0))] + [full(a) for a in consts] + [tab, tab],
        out_specs=[pl.BlockSpec((1, MLA_HEADS, bl, LANES), lambda i, j: (i, 0, j, 0)),
                   pl.BlockSpec((1, MLA_HEADS, 1, LANES, bl), lambda i, j: (i, 0, j, 0, 0)),
                   pl.BlockSpec((1, MLA_HEADS, bl, MLA_V), lambda i, j: (i, 0, j, 0))],
        out_shape=[jax.ShapeDtypeStruct((b, MLA_HEADS, lt, LANES), BF16),
                   jax.ShapeDtypeStruct((b, MLA_HEADS, lt // bl, LANES, bl), BF16),
                   jax.ShapeDtypeStruct((b, MLA_HEADS, lt, MLA_V), BF16)],
        compiler_params=_params("parallel", "parallel"),
        name="mla_qkv",
    )(p, *consts, cos, sin)


def _pair_norm_rope(x, g, cos, sin):
    lane = lax.broadcasted_iota(jnp.int32, x.shape, 1)
    lo = lane < GQA_HEAD_DIM
    xx = x * x
    ss_lo = jnp.sum(jnp.where(lo, xx, 0.0), axis=-1, keepdims=True)
    ss_hi = jnp.sum(jnp.where(lo, 0.0, xx), axis=-1, keepdims=True)
    rs = lax.rsqrt(jnp.where(lo, ss_lo, ss_hi) / GQA_HEAD_DIM + NORM_EPS)
    return _rope(x * rs * g, cos, sin, GQA_HEAD_DIM // 4)


def _gqa_qkv_kernel(p_ref, gq_ref, gk_ref, cos_ref, sin_ref, q_ref, kt_ref, v_ref, *, scale):
    p = p_ref[0]
    cos, sin = cos_ref[...], sin_ref[...]
    qw = GQA_HEADS * GQA_HEAD_DIM
    kw = GQA_KV_HEADS * GQA_HEAD_DIM
    for j in range(GQA_HEADS // 2):
        q = _pair_norm_rope(p[:, j * LANES:(j + 1) * LANES], gq_ref[...], cos, sin) * scale
        q_ref[0, 2 * j] = q[:, :GQA_HEAD_DIM].astype(q_ref.dtype)
        q_ref[0, 2 * j + 1] = q[:, GQA_HEAD_DIM:].astype(q_ref.dtype)
    for j in range(GQA_KV_HEADS // 2):
        k = _pair_norm_rope(p[:, qw + j * LANES:qw + (j + 1) * LANES], gk_ref[...], cos, sin)
        kt = k.T
        kt_ref[0, 2 * j, 0] = kt[:GQA_HEAD_DIM].astype(kt_ref.dtype)
        kt_ref[0, 2 * j + 1, 0] = kt[GQA_HEAD_DIM:].astype(kt_ref.dtype)
    for h in range(GQA_KV_HEADS):
        o = qw + kw + h * GQA_HEAD_DIM
        v_ref[0, h] = p[:, o:o + GQA_HEAD_DIM].astype(v_ref.dtype)


def _gqa_qkv(p, g_q, g_k, cos, sin):
    b, lt, cols = p.shape
    bl = ATTN_BLOCK
    hd = GQA_HEAD_DIM
    two = lambda g: jnp.concatenate([g, g]).reshape(1, LANES)
    tab = pl.BlockSpec((bl, LANES), lambda i, j: (j, 0))
    vec = pl.BlockSpec((1, LANES), lambda i, j: (0, 0))
    return pl.pallas_call(
        functools.partial(_gqa_qkv_kernel, scale=hd ** -0.5),
        grid=(b, lt // bl),
        in_specs=[pl.BlockSpec((1, bl, cols), lambda i, j: (i, j, 0)), vec, vec, tab, tab],
        out_specs=[pl.BlockSpec((1, GQA_HEADS, bl, hd), lambda i, j: (i, 0, j, 0)),
                   pl.BlockSpec((1, GQA_KV_HEADS, 1, hd, bl), lambda i, j: (i, 0, j, 0, 0)),
                   pl.BlockSpec((1, GQA_KV_HEADS, bl, hd), lambda i, j: (i, 0, j, 0))],
        out_shape=[jax.ShapeDtypeStruct((b, GQA_HEADS, lt, hd), BF16),
                   jax.ShapeDtypeStruct((b, GQA_KV_HEADS, lt // bl, hd, bl), BF16),
                   jax.ShapeDtypeStruct((b, GQA_KV_HEADS, lt, hd), BF16)],
        compiler_params=_params("parallel", "parallel"),
        name="gqa_qkv",
    )(p, two(g_q), two(g_k), cos, sin)


def _attn_kernel(q_ref, kt_ref, v_ref, o_ref, m_scr, l_scr, acc_scr, *, groups, bq, ctx_blocks, all_blocks):
    rows = groups * bq
    q = q_ref[0, 0].reshape(rows, q_ref.shape[-1])
    n_kv = jnp.where(pl.program_id(2) < ctx_blocks, ctx_blocks, all_blocks)
    m_scr[...] = jnp.full_like(m_scr, -jnp.inf)
    l_scr[...] = jnp.zeros_like(l_scr)
    acc_scr[...] = jnp.zeros_like(acc_scr)
    bk = kt_ref.shape[-1]

    def step(j, carry):
        s = jnp.dot(q, kt_ref[0, 0, j], preferred_element_type=F32)
        m_prev = m_scr[...]
        m_new = jnp.maximum(m_prev, jnp.max(s, axis=-1, keepdims=True))
        alpha = jnp.exp(m_prev - m_new)
        p = jnp.exp(s - m_new)
        l_scr[...] = alpha * l_scr[...] + jnp.sum(p, axis=-1, keepdims=True)
        vv = v_ref[0, 0, pl.ds(pl.multiple_of(j * bk, bk), bk), :]
        acc_scr[...] = alpha * acc_scr[...] + jnp.dot(p.astype(BF16), vv, preferred_element_type=F32)
        m_scr[...] = m_new
        return carry

    lax.fori_loop(0, n_kv, step, 0)
    o = acc_scr[...] / l_scr[...]
    o_ref[0, 0] = o.reshape(groups, bq, o.shape[-1]).astype(o_ref.dtype)


def _attention(q, kt, v, n_ctx):
    b, hq, lt, dk = q.shape
    hk, dv = v.shape[1], v.shape[-1]
    groups = hq // hk
    bq = ATTN_BLOCK
    nblk = lt // bq
    rows = groups * bq
    q5 = q.reshape(b, hk, groups, lt, dk)
    out = pl.pallas_call(
        functools.partial(_attn_kernel, groups=groups, bq=bq, ctx_blocks=n_ctx // bq, all_blocks=nblk),
        grid=(b, hk, nblk),
        in_specs=[pl.BlockSpec((1, 1, groups, bq, dk), lambda i, h, j: (i, h, 0, j, 0)),
                  pl.BlockSpec((1, 1) + kt.shape[2:], lambda i, h, j: (i, h, 0, 0, 0)),
                  pl.BlockSpec((1, 1, lt, dv), lambda i, h, j: (i, h, 0, 0))],
        out_specs=pl.BlockSpec((1, 1, groups, bq, dv), lambda i, h, j: (i, h, 0, j, 0)),
        out_shape=jax.ShapeDtypeStruct((b, hk, groups, lt, dv), BF16),
        scratch_shapes=[pltpu.VMEM((rows, 1), F32), pltpu.VMEM((rows, 1), F32), pltpu.VMEM((rows, dv), F32)],
        compiler_params=_params("parallel", "parallel", "arbitrary"),
        name="attention",
    )(q5, kt, v)
    return out.reshape(b, hq, lt, dv)


def _ab_out_kernel(x_ref, ml_ref, mc_ref, y_ref, bonus_ref, gate_ref, o_ref, lnw_ref, lnb_ref, wrw_ref,
                   wmla_ref, out_ref, *, bl, n_ctx):
    acc = jnp.zeros(out_ref.shape[1:], F32)
    for h in range(RW_HEADS):
        y = y_ref[0, 0, h] + y_ref[1, 0, h]
        mean = jnp.mean(y, axis=-1, keepdims=True)
        yc = y - mean
        var = jnp.mean(yc * yc, axis=-1, keepdims=True)
        yn = yc * lax.rsqrt(var + LN_X_EPS) * lnw_ref[h] + lnb_ref[h]
        rw = (yn + bonus_ref[0, h]) * gate_ref[0, h]
        acc += _bdot(rw, wrw_ref[h])
    for h in range(MLA_HEADS):
        acc += jnp.dot(o_ref[0, h], wmla_ref[h], preferred_element_type=F32)
    gate = _row_mods(ml_ref, mc_ref, pl.program_id(1) * bl, bl, n_ctx, 2)
    out_ref[0] = x_ref[0] + gate * acc


def _ab_out(x, mod_l, mod_c, y, bonus, gate, o, ln_w, ln_b, w_out, n_ctx):
    b, lt, d = x.shape
    bl = TOKEN_BLOCK
    hd = RW_HEAD_DIM
    wrw = w_out[:RW_DIM].reshape(RW_HEADS, hd, d).astype(BF16)
    wmla = w_out[RW_DIM:].reshape(MLA_HEADS, MLA_V, d).astype(BF16)
    xs = pl.BlockSpec((1, bl, d), lambda i, j: (i, j, 0))
    hm = pl.BlockSpec((1, RW_HEADS, bl, hd), lambda i, j: (i, 0, j, 0))
    full = lambda a: pl.BlockSpec(a.shape, lambda i, j: (0,) * a.ndim)
    lnw, lnb = ln_w.reshape(RW_HEADS, 1, hd), ln_b.reshape(RW_HEADS, 1, hd)
    return pl.pallas_call(
        functools.partial(_ab_out_kernel, bl=bl, n_ctx=n_ctx),
        grid=(b, lt // bl),
        in_specs=[xs, pl.BlockSpec((1, 6, d), lambda i, j: (i, 0, 0)), pl.BlockSpec((1, 6, d), lambda i, j: (0, 0, 0)),
                  pl.BlockSpec((2, 1, RW_HEADS, bl, hd), lambda i, j: (0, i, 0, j, 0)), hm, hm,
                  pl.BlockSpec((1, MLA_HEADS, bl, MLA_V), lambda i, j: (i, 0, j, 0)),
                  full(lnw), full(lnb), full(wrw), full(wmla)],
        out_specs=xs,
        out_shape=jax.ShapeDtypeStruct((b, lt, d), F32),
        compiler_params=_params("parallel", "parallel"),
        name="ab_out",
    )(x, mod_l, mod_c, y, bonus, gate, o, lnw, lnb, wrw, wmla)


def _gqa_out_kernel(x_ref, ml_ref, mc_ref, o_ref, w_ref, out_ref, *, bl, n_ctx):
    acc = jnp.zeros(out_ref.shape[1:], F32)
    for h in range(GQA_HEADS):
        acc += jnp.dot(o_ref[0, h], w_ref[h], preferred_element_type=F32)
    gate = _row_mods(ml_ref, mc_ref, pl.program_id(1) * bl, bl, n_ctx, 2)
    out_ref[0] = x_ref[0] + gate * acc


def _gqa_out(x, mod_l, mod_c, o, w_out, n_ctx):
    b, lt, d = x.shape
    bl = TOKEN_BLOCK
    hd = GQA_HEAD_DIM
    w = w_out.reshape(GQA_HEADS, hd, d).astype(BF16)
    xs = pl.BlockSpec((1, bl, d), lambda i, j: (i, j, 0))
    return pl.pallas_call(
        functools.partial(_gqa_out_kernel, bl=bl, n_ctx=n_ctx),
        grid=(b, lt // bl),
        in_specs=[xs, pl.BlockSpec((1, 6, d), lambda i, j: (i, 0, 0)), pl.BlockSpec((1, 6, d), lambda i, j: (0, 0, 0)),
                  pl.BlockSpec((1, GQA_HEADS, bl, hd), lambda i, j: (i, 0, j, 0)),
                  pl.BlockSpec(w.shape, lambda i, j: (0, 0, 0))],
        out_specs=xs,
        out_shape=jax.ShapeDtypeStruct((b, lt, d), F32),
        compiler_params=_params("parallel", "parallel"),
        name="gqa_out",
    )(x, mod_l, mod_c, o, w)


def _first_argmax(vals):
    best, idx = vals[0], jnp.zeros(vals[0].shape, jnp.int32)
    for i in range(1, len(vals)):
        better = vals[i] > best
        idx = jnp.where(better, i, idx)
        best = jnp.where(better, vals[i], best)
    return best, idx


def _pick(vals, idx):
    out = vals[0]
    for i in range(1, len(vals)):
        out = jnp.where(idx == i, vals[i], out)
    return out


def _route_kernel(x_ref, g_ref, ml_ref, mc_ref, rw_ref, rb_ref, t_ref, comb_ref, *, bl, n_ctx):
    first = pl.program_id(1) * bl
    shift = _row_mods(ml_ref, mc_ref, first, bl, n_ctx, 3)
    scale = _row_mods(ml_ref, mc_ref, first, bl, n_ctx, 4)
    t = (_rms(x_ref[0]) * g_ref[...]) * (1.0 + scale) + shift
    t_ref[0] = t.astype(t_ref.dtype)
    logits = jnp.dot(t, rw_ref[...], preferred_element_type=F32, precision=HIGHEST)
    lt = logits.T
    score = [_sigmoid(lt[e:e + 1]) for e in range(N_EXPERTS)]
    biased = [score[e] + rb_ref[e:e + 1, :] for e in range(N_EXPERTS)]
    epg = EXPERTS_PER_GROUP
    group_score = []
    for g in range(N_GROUPS):
        vals = biased[g * epg:(g + 1) * epg]
        pair = [vals[i] + vals[j] for i in range(epg) for j in range(i + 1, epg)]
        group_score.append(functools.reduce(jnp.maximum, pair))
    _, grp = _first_argmax(group_score)
    in_b = [_pick([biased[g * epg + j] for g in range(N_GROUPS)], grp) for j in range(epg)]
    in_s = [_pick([score[g * epg + j] for g in range(N_GROUPS)], grp) for j in range(epg)]
    _, loc1 = _first_argmax(in_b)
    _, loc2 = _first_argmax([jnp.where(loc1 == j, -jnp.inf, in_b[j]) for j in range(epg)])
    w1, w2 = _pick(in_s, loc1), _pick(in_s, loc2)
    wsum = w1 + w2
    w1, w2 = w1 / wsum, w2 / wsum
    e1, e2 = grp * epg + loc1, grp * epg + loc2
    sub = lax.broadcasted_iota(jnp.int32, (LANES, bl), 0)
    comb = jnp.zeros((LANES, bl), F32)
    for e in range(N_EXPERTS):
        c_e = jnp.where(e1 == e, w1, 0.0) + jnp.where(e2 == e, w2, 0.0)
        comb = jnp.where(sub == e, jnp.broadcast_to(c_e, (LANES, bl)), comb)
    comb_ref[0] = comb.T


def _moe_route(x, g, mod_l, mod_c, router_w, router_b, n_ctx):
    b, lt, d = x.shape
    bl = TOKEN_BLOCK
    rw = jnp.pad(router_w, ((0, 0), (0, LANES - N_EXPERTS)))
    xs = pl.BlockSpec((1, bl, d), lambda i, j: (i, j, 0))
    return pl.pallas_call(
        functools.partial(_route_kernel, bl=bl, n_ctx=n_ctx),
        grid=(b, lt // bl),
        in_specs=[xs, pl.BlockSpec((1, d), lambda i, j: (0, 0)),
                  pl.BlockSpec((1, 6, d), lambda i, j: (i, 0, 0)), pl.BlockSpec((1, 6, d), lambda i, j: (0, 0, 0)),
                  pl.BlockSpec(rw.shape, lambda i, j: (0, 0)), pl.BlockSpec((N_EXPERTS, 1), lambda i, j: (0, 0))],
        out_specs=[xs, pl.BlockSpec((1, bl, LANES), lambda i, j: (i, j, 0))],
        out_shape=[jax.ShapeDtypeStruct((b, lt, d), BF16), jax.ShapeDtypeStruct((b, lt, LANES), F32)],
        compiler_params=_params("parallel", "parallel"),
        name="moe_route",
    )(x, g.reshape(1, d), mod_l, mod_c, rw, router_b.reshape(N_EXPERTS, 1))


def _ffn(t, wg, wu, wd, cw):
    g = jnp.dot(t, wg, preferred_element_type=F32)
    u = jnp.dot(t, wu, preferred_element_type=F32)
    h = _silu(g) * u
    if cw is not None:
        h = h * cw
    return jnp.dot(h.astype(BF16), wd, preferred_element_type=F32)


def _experts_kernel(x_ref, t_ref, comb_ref, ml_ref, mc_ref, wg_ref, wu_ref, wd_ref, sg_ref, su_ref, sd_ref,
                    out_ref, acc_ref, *, bl, n_ctx):
    e = pl.program_id(2)
    t = t_ref[0]

    @pl.when(e == 0)
    def _():
        acc_ref[...] = _ffn(t, sg_ref[...], su_ref[...], sd_ref[...], None)

    lane = lax.broadcasted_iota(jnp.int32, comb_ref.shape[1:], 1)
    cw = jnp.sum(jnp.where(lane == e, comb_ref[0], 0.0), axis=-1, keepdims=True)
    acc_ref[...] += _ffn(t, wg_ref[0], wu_ref[0], wd_ref[0], cw)

    @pl.when(e == N_EXPERTS - 1)
    def _():
        gate = _row_mods(ml_ref, mc_ref, pl.program_id(1) * bl, bl, n_ctx, 5)
        out_ref[0] = x_ref[0] + gate * acc_ref[...]


def _moe_experts(x, t, comb, mod_l, mod_c, w_gate, w_up, w_down, sh_gate, sh_up, sh_down, n_ctx):
    b, lt, d = x.shape
    bl = TOKEN_BLOCK
    ff = w_gate.shape[-1]
    xs = pl.BlockSpec((1, bl, d), lambda i, j, e: (i, j, 0))
    full = lambda a: pl.BlockSpec(a.shape, lambda i, j, e: (0,) * a.ndim)
    ws = [w_gate.astype(BF16), w_up.astype(BF16), w_down.astype(BF16),
          sh_gate.astype(BF16), sh_up.astype(BF16), sh_down.astype(BF16)]
    return pl.pallas_call(
        functools.partial(_experts_kernel, bl=bl, n_ctx=n_ctx),
        grid=(b, lt // bl, N_EXPERTS),
        in_specs=[xs, xs, pl.BlockSpec((1, bl, LANES), lambda i, j, e: (i, j, 0)),
                  pl.BlockSpec((1, 6, d), lambda i, j, e: (i, 0, 0)), pl.BlockSpec((1, 6, d), lambda i, j, e: (0, 0, 0)),
                  pl.BlockSpec((1, d, ff), lambda i, j, e: (e, 0, 0)),
                  pl.BlockSpec((1, d, ff), lambda i, j, e: (e, 0, 0)),
                  pl.BlockSpec((1, ff, d), lambda i, j, e: (e, 0, 0)),
                  full(ws[3]), full(ws[4]), full(ws[5])],
        out_specs=xs,
        out_shape=jax.ShapeDtypeStruct((b, lt, d), F32),
        scratch_shapes=[pltpu.VMEM((bl, d), F32)],
        compiler_params=_params("parallel", "parallel", "arbitrary"),
        name="moe_experts",
    )(x, t, comb, mod_l, mod_c, *ws)


def _rope_tables(n_ctx, seq, rope_dims, lane_offset, repeat):
    half = rope_dims // 4
    t = jnp.arange(seq)
    rowp = (t // GRID_W).astype(F32)
    colp = (t % GRID_W).astype(F32)
    inv = ROPE_THETA ** (-jnp.arange(half, dtype=F32) / half)
    ar, ac = rowp[:, None] * inv[None, :], colp[:, None] * inv[None, :]
    cos = jnp.concatenate([jnp.cos(ar), jnp.cos(ar), jnp.cos(ac), jnp.cos(ac)], axis=1)
    sin = jnp.concatenate([-jnp.sin(ar), jnp.sin(ar), -jnp.sin(ac), jnp.sin(ac)], axis=1)
    width = LANES // repeat
    padl, padr = lane_offset, width - lane_offset - rope_dims
    cos = jnp.pad(cos, ((n_ctx, 0), (0, 0)), constant_values=1.0)
    sin = jnp.pad(sin, ((n_ctx, 0), (0, 0)))
    cos = jnp.pad(cos, ((0, 0), (padl, padr)), constant_values=1.0)
    sin = jnp.pad(sin, ((0, 0), (padl, padr)))
    return jnp.tile(cos, (1, repeat)), jnp.tile(sin, (1, repeat))


def kernel(x, c, ctx, c_ctx, ada_w, ada_b, norm1_g, norm2_g, ab_w_in, ab_w_out, rw_mu, rw_w0, rw_w2, rw_a0, rw_a2, rw_k_k, rw_k_a, rw_r_k, rw_g2, rw_ln_w, rw_ln_b, mla_g_qa, mla_w_q_up, mla_g_kva, mla_w_kv_up, mla_g_q, mla_g_k, gqa_w_in, gqa_w_out, gqa_g_q, gqa_g_k, router_w, router_b, moe_w_gate, moe_w_up, moe_w_down, shared_w_gate, shared_w_up, shared_w_down):
    b, seq, d = x.shape
    n_ctx = ctx.shape[1]
    depth = ada_w.shape[0]
    xs = jnp.concatenate([ctx, x], axis=1)

    cvec = jnp.concatenate([c, c_ctx[None, :]], axis=0)
    cvec = jnp.pad(cvec, ((0, 8 - (b + 1) % 8), (0, 0))) if (b + 1) % 8 else cvec
    mods = _ada_mods(cvec, ada_w, ada_b).reshape(depth, -1, 6, d)

    mla_cos, mla_sin = _rope_tables(n_ctx, seq, MLA_ROPE, MLA_NOPE, 1)
    gqa_cos, gqa_sin = _rope_tables(n_ctx, seq, GQA_HEAD_DIM, 0, 2)

    for l in range(depth):
        i = l // 2
        mod_l, mod_c = mods[l, :b], mods[l, b:b + 1]
        if l % 2 == 0:
            w_in = ab_w_in[i]
            w_rw = w_in[:, :RW_COLS].astype(BF16)
            w_m = w_in[:, RW_COLS:]
            w_mla = jnp.concatenate(
                [w_m[:, :MLA_Q_RANK + MLA_KV_RANK],
                 jnp.pad(w_m[:, MLA_Q_RANK + MLA_KV_RANK:], ((0, 0), (MLA_NOPE, LANES - MLA_NOPE - MLA_ROPE)))],
                axis=1).astype(BF16)
            p_rw, p_mla = _ln_mod_mm(xs, norm1_g[l], mod_l, mod_c, [w_rw, w_mla], n_ctx)
            r, v, na, lw, kd, bd, gate, bonus = _rwkv_prep(
                p_rw, n_ctx, rw_mu[i], rw_w0[i], rw_w2[i], rw_a0[i], rw_a2[i], rw_k_k[i], rw_k_a[i],
                rw_r_k[i].reshape(-1), rw_g2[i])
            y = _wkv_scan(r, v, na, lw, kd, bd, n_ctx)
            q, kt, vm = _mla_qkv(p_mla, mla_g_qa[i], mla_w_q_up[i], mla_g_kva[i], mla_w_kv_up[i],
                                 mla_g_q[i], mla_g_k[i], mla_cos, mla_sin)
            o = _attention(q, kt, vm, n_ctx)
            xs = _ab_out(xs, mod_l, mod_c, y, bonus, gate, o, rw_ln_w[i], rw_ln_b[i], ab_w_out[i], n_ctx)
        else:
            (p,) = _ln_mod_mm(xs, norm1_g[l], mod_l, mod_c, [gqa_w_in[i].astype(BF16)], n_ctx)
            q, kt, vg = _gqa_qkv(p, gqa_g_q[i], gqa_g_k[i], gqa_cos, gqa_sin)
            o = _attention(q, kt, vg, n_ctx)
            xs = _gqa_out(xs, mod_l, mod_c, o, gqa_w_out[i], n_ctx)
        t, comb = _moe_route(xs, norm2_g[l], mod_l, mod_c, router_w, router_b, n_ctx)
        xs = _moe_experts(xs, t, comb, mod_l, mod_c, moe_w_gate[l], moe_w_up[l], moe_w_down[l],
                          shared_w_gate[l], shared_w_up[l], shared_w_down[l], n_ctx)
    return xs[:, n_ctx:]
```

```python
import functools
import math

import jax
import jax.numpy as jnp
from jax import lax
from jax.experimental import pallas as pl
from jax.experimental.pallas import tpu as pltpu

F32 = jnp.float32
BF16 = jnp.bfloat16
HIGHEST = lax.Precision.HIGHEST

GRID_W = 64
RW_HEADS = 8
RW_HEAD_DIM = 64
RW_DIM = RW_HEADS * RW_HEAD_DIM
DECAY_LORA = 64
ICLR_LORA = 64
GATE_LORA = 128
RW_COLS = 3 * RW_DIM + 2 * DECAY_LORA + 2 * ICLR_LORA + GATE_LORA
LN_X_EPS = 64e-5
MLA_HEADS = 8
MLA_NOPE = 64
MLA_ROPE = 32
MLA_V = 64
MLA_QK = MLA_NOPE + MLA_ROPE
MLA_Q_RANK = 384
MLA_KV_RANK = 256
GQA_HEADS = 16
GQA_KV_HEADS = 4
GQA_HEAD_DIM = 64
N_EXPERTS = 16
N_GROUPS = 4
EXPERTS_PER_GROUP = N_EXPERTS // N_GROUPS
ROPE_THETA = 10000.0
NORM_EPS = 1e-6

LANES = 128
V7X_VMEM_BYTES = 64 * 1024 * 1024
VMEM_LIMIT = V7X_VMEM_BYTES - 8 * 1024 * 1024

TOKEN_BLOCK = 256
SCAN_CHUNK = 64
SCAN_INTERLEAVE = 8
ATTN_ROWS = 1024
ATTN_KEYS = 1408
MOE_ROWS = 768
LOG2E = math.log2(math.e)


def _row_block(lt, limit):
    return max(r for r in range(TOKEN_BLOCK, limit + 1, TOKEN_BLOCK) if lt % r == 0)


def _params(*sem):
    return pltpu.CompilerParams(dimension_semantics=sem, vmem_limit_bytes=VMEM_LIMIT)


def _bdot(a, b):
    return jnp.dot(a.astype(BF16), b.astype(BF16), preferred_element_type=F32)


def _batched(a, b, ca, cb):
    return lax.dot_general(a.astype(BF16), b.astype(BF16), (((ca,), (cb,)), ((0,), (0,))),
                           preferred_element_type=F32)


def _bmm(a, b):
    return _batched(a, b, 2, 1)


def _bmm_nt(a, b):
    return _batched(a, b, 2, 2)


def _bmm_tn(a, b):
    return _batched(a, b, 1, 1)


def _sigmoid(x):
    return 1.0 / (1.0 + jnp.exp(-x))


def _silu(x):
    return x * _sigmoid(x)


def _softplus(x):
    return jnp.maximum(x, 0.0) + jnp.log(1.0 + jnp.exp(-jnp.abs(x)))


def _rms(x):
    return x * lax.rsqrt(jnp.mean(x * x, axis=-1, keepdims=True) + NORM_EPS)


def _row_mods(ml_ref, mc_ref, first_row, rows, n_ctx, idx):
    r = first_row + lax.broadcasted_iota(jnp.int32, (rows, 1), 0)
    return jnp.where(r < n_ctx, mc_ref[0, idx:idx + 1, :], ml_ref[0, idx:idx + 1, :])


def _ada_kernel(c_ref, w_ref, b_ref, o_ref):
    s = _silu(c_ref[...])
    o_ref[0] = jnp.dot(s, w_ref[0], preferred_element_type=F32, precision=HIGHEST) + b_ref[0]


def _ada_mods(cvec, ada_w, ada_b):
    depth, d, n = ada_w.shape
    rows = cvec.shape[0]
    bn = n // 4
    return pl.pallas_call(
        _ada_kernel,
        grid=(depth, n // bn),
        in_specs=[pl.BlockSpec((rows, d), lambda l, j: (0, 0)),
                  pl.BlockSpec((1, d, bn), lambda l, j: (l, 0, j)),
                  pl.BlockSpec((1, 1, bn), lambda l, j: (l, 0, j))],
        out_specs=pl.BlockSpec((1, rows, bn), lambda l, j: (l, 0, j)),
        out_shape=jax.ShapeDtypeStruct((depth, rows, n), F32),
        compiler_params=_params("parallel", "parallel"),
        name="ada_mods",
    )(cvec, ada_w, ada_b.reshape(depth, 1, n))


def _ln_mod_mm_kernel(x_ref, g_ref, ml_ref, mc_ref, *refs, n_w, bl, n_ctx):
    w_refs, o_refs = refs[:n_w], refs[n_w:]
    first = pl.program_id(1) * bl
    shift = _row_mods(ml_ref, mc_ref, first, bl, n_ctx, 0)
    scale = _row_mods(ml_ref, mc_ref, first, bl, n_ctx, 1)
    h = (_rms(x_ref[0]) * g_ref[...]) * (1.0 + scale) + shift
    hb = h.astype(BF16)
    for w_ref, o_ref in zip(w_refs, o_refs):
        o_ref[0] = jnp.dot(hb, w_ref[...], preferred_element_type=F32).astype(o_ref.dtype)


def _ln_mod_mm(x, g, mod_l, mod_c, ws, n_ctx):
    b, lt, d = x.shape
    bl = TOKEN_BLOCK
    kern = functools.partial(_ln_mod_mm_kernel, n_w=len(ws), bl=bl, n_ctx=n_ctx)
    return pl.pallas_call(
        kern,
        grid=(b, lt // bl),
        in_specs=[pl.BlockSpec((1, bl, d), lambda i, j: (i, j, 0)),
                  pl.BlockSpec((1, d), lambda i, j: (0, 0)),
                  pl.BlockSpec((1, 6, d), lambda i, j: (i, 0, 0)),
                  pl.BlockSpec((1, 6, d), lambda i, j: (0, 0, 0))]
        + [pl.BlockSpec(w.shape, lambda i, j: (0, 0)) for w in ws],
        out_specs=[pl.BlockSpec((1, bl, w.shape[1]), lambda i, j: (i, j, 0)) for w in ws],
        out_shape=[jax.ShapeDtypeStruct((b, lt, w.shape[1]), F32) for w in ws],
        compiler_params=_params("parallel", "parallel"),
        name="ln_mod_mm",
    )(x, g.reshape(1, d), mod_l, mod_c, *ws)


def _rwkv_prep_kernel(p_ref, prev_ref, next_ref, mu_ref, w0_ref, w2_ref, a0_ref, a2_ref, kk_ref,
                      ka_ref, rk_ref, g2_ref, r_ref, v_ref, na_ref, lw_ref, kd_ref, bd_ref,
                      gate_ref, bonus_ref, *, bl):
    p = p_ref[0]
    row = lax.broadcasted_iota(jnp.int32, p.shape, 0)
    prev = jnp.where(row == 0, prev_ref[0, 0], pltpu.roll(p, 1, 0))
    nxt = jnp.where(row == bl - 1, next_ref[0, 0], pltpu.roll(p, bl - 1, 0))
    pm = p + mu_ref[...] * (0.5 * (prev + nxt) - p)

    o = 3 * RW_DIM
    r = pm[:, 0:RW_DIM]
    k = pm[:, RW_DIM:2 * RW_DIM]
    v = pm[:, 2 * RW_DIM:o]
    w1 = (pm[:, o:o + DECAY_LORA], pm[:, o + DECAY_LORA:o + 2 * DECAY_LORA])
    o += 2 * DECAY_LORA
    a1 = (pm[:, o:o + ICLR_LORA], pm[:, o + ICLR_LORA:o + 2 * ICLR_LORA])
    o += 2 * ICLR_LORA
    g1 = pm[:, o:o + GATE_LORA]

    kk = k * kk_ref[...]
    lw, kd, ad = [], [], []
    for d in range(2):
        z = w0_ref[d:d + 1, :] + _bdot(jnp.tanh(w1[d]), w2_ref[d])
        lw.append(-jnp.exp(-_softplus(-z) - 0.5))
        a = _sigmoid(a0_ref[d:d + 1, :] + _bdot(a1[d], a2_ref[d]))
        ad.append(a)
        kd.append(k * (1.0 + (a - 1.0) * ka_ref[...]))
    gate = _bdot(_sigmoid(g1), g2_ref[...])
    rkk = r * (kd[0] + kd[1]) * rk_ref[...]

    for h in range(RW_HEADS):
        sl = slice(h * RW_HEAD_DIM, (h + 1) * RW_HEAD_DIM)
        kk_h = kk[:, sl]
        kk_h = kk_h * lax.rsqrt(jnp.sum(kk_h * kk_h, axis=-1, keepdims=True) + 1e-12)
        v_h = v[:, sl]
        r_ref[0, h] = r[:, sl]
        v_ref[0, h] = v_h
        na_ref[0, h] = -kk_h
        for d in range(2):
            lw_ref[d, 0, h] = lw[d][:, sl]
            kd_ref[d, 0, h] = kd[d][:, sl]
            bd_ref[d, 0, h] = kk_h * ad[d][:, sl]
        gate_ref[0, h] = gate[:, sl]
        bonus_ref[0, h] = jnp.sum(rkk[:, sl], axis=-1, keepdims=True) * v_h


def _rwkv_prep(p, n_ctx, mu, w0, w2, a0, a2, k_k, k_a, r_k, g2):
    b, lt, cols = p.shape
    bl = TOKEN_BLOCK
    nblk = lt // bl
    zero = jnp.zeros((b, 1, cols), F32)
    lastrows = p[:, bl - 1::bl]
    firstrows = p[:, ::bl]
    prev_edge = jnp.concatenate([zero, lastrows[:, :-1]], axis=1)
    next_edge = jnp.concatenate([firstrows[:, 1:], zero], axis=1)
    cb = n_ctx // bl
    blk = jnp.arange(nblk)[None, :, None]
    prev_edge = jnp.where(blk == cb, 0.0, prev_edge).reshape(b, nblk, 1, cols)
    next_edge = jnp.where(blk == cb - 1, 0.0, next_edge).reshape(b, nblk, 1, cols)

    hm = jax.ShapeDtypeStruct((b, RW_HEADS, lt, RW_HEAD_DIM), F32)
    hm2 = jax.ShapeDtypeStruct((2, b, RW_HEADS, lt, RW_HEAD_DIM), F32)
    hm_spec = pl.BlockSpec((1, RW_HEADS, bl, RW_HEAD_DIM), lambda i, j: (i, 0, j, 0))
    hm2_spec = pl.BlockSpec((2, 1, RW_HEADS, bl, RW_HEAD_DIM), lambda i, j: (0, i, 0, j, 0))
    full = lambda a: pl.BlockSpec(a.shape, lambda i, j: (0,) * a.ndim)
    consts = [mu.reshape(1, cols), w0, w2.astype(BF16), a0, a2.astype(BF16), k_k.reshape(1, RW_DIM),
              k_a.reshape(1, RW_DIM), r_k.reshape(1, RW_DIM), g2.astype(BF16)]
    return pl.pallas_call(
        functools.partial(_rwkv_prep_kernel, bl=bl),
        grid=(b, nblk),
        in_specs=[pl.BlockSpec((1, bl, cols), lambda i, j: (i, j, 0)),
                  pl.BlockSpec((1, 1, 1, cols), lambda i, j: (i, j, 0, 0)),
                  pl.BlockSpec((1, 1, 1, cols), lambda i, j: (i, j, 0, 0))]
        + [full(a) for a in consts],
        out_specs=[hm_spec, hm_spec, hm_spec, hm2_spec, hm2_spec, hm2_spec, hm_spec, hm_spec],
        out_shape=[hm, hm, hm, hm2, hm2, hm2, hm, hm],
        compiler_params=_params("parallel", "parallel"),
        name="rwkv_prep",
    )(p, prev_edge, next_edge, *consts)


def _wkv_kernel(r_ref, v_ref, a_ref, lw_ref, k_ref, b_ref, y_ref, st_ref, *, cn, chains):
    d = pl.program_id(0)

    @pl.when(pl.program_id(1) == 0)
    def _():
        st_ref[...] = jnp.zeros_like(st_ref)

    row = lax.broadcasted_iota(jnp.int32, (cn, cn), 0)
    col = lax.broadcasted_iota(jnp.int32, (cn, cn), 1)
    lead = (row - col) * jnp.where(d == 0, 1, -1)
    before = lead > 0
    upto = lead >= 0
    eye = (col == row).astype(F32)
    cum_mat = upto.astype(F32)
    hd = r_ref.shape[-1]
    eye_h = (lax.broadcasted_iota(jnp.int32, (hd, hd), 0)
             == lax.broadcasted_iota(jnp.int32, (hd, hd), 1)).astype(F32)
    n_double = int(math.log2(cn)) - 1

    nb = SCAN_INTERLEAVE
    cum_b = jnp.broadcast_to(cum_mat, (nb, cn, cn))

    def group(g, carry):
        sl = pl.ds(g * nb, nb)
        r, v, a = r_ref[sl], v_ref[sl], a_ref[sl]
        lw, k, b = lw_ref[0, sl], k_ref[0, sl], b_ref[0, sl]
        cum = lax.dot_general(cum_b, lw, (((2,), (1,)), ((0,), (0,))), preferred_element_type=F32,
                              precision=HIGHEST)
        tot = jnp.sum(lw, axis=1, keepdims=True)
        at = a * jnp.exp(cum - lw)
        rt = r * jnp.exp(cum)
        einv = jnp.exp(-cum)
        bt, kt = b * einv, k * einv
        eend = jnp.exp(tot - cum)
        bh, kh = b * eend, k * eend

        a_ab = jnp.where(before, _bmm_nt(at, bt), 0.0)
        a_ak = jnp.where(before, _bmm_nt(at, kt), 0.0)
        a_rb = jnp.where(upto, _bmm_nt(rt, bt), 0.0)
        a_rk = jnp.where(upto, _bmm_nt(rt, kt), 0.0)

        t = eye + a_ab
        x = _bmm(a_ab, a_ab)
        for _ in range(n_double - 1):
            tx = _bmm(jnp.concatenate([t, x], axis=1), x)
            t = t + tx[:, :cn]
            x = tx[:, cn:]
        t = t + _bmm(t, x)
        w = _bmm(t, at)
        up = _bmm(t, _bmm(a_ak, v))
        qe = rt + _bmm(a_rb, w)
        yp = _bmm(a_rb, up) + _bmm(a_rk, v)
        mc = eye_h * jnp.exp(tot) + _bmm_tn(bh, w)
        nc = _bmm_tn(bh, up) + _bmm_tn(kh, v)

        st = st_ref[sl]
        y_ref[0, sl] = _bmm(qe, st) + yp
        st_ref[sl] = _bmm(mc, st) + nc
        return carry

    lax.fori_loop(0, chains // SCAN_INTERLEAVE, group, 0)


def _wkv_scan(r, v, na, lw, kd, bd, n_ctx):
    b, h, lt, hd = r.shape
    cn = SCAN_CHUNK
    chains = b * h
    n_all, n_c = lt // cn, n_ctx // cn

    def chunk(d, i):
        rev = jnp.where(i < n_c, n_c - 1 - i, n_all - 1 - (i - n_c))
        return jnp.where(d == 0, i, rev)

    shared = pl.BlockSpec((chains, cn, hd), lambda d, i: (0, chunk(d, i), 0))
    per_dir = pl.BlockSpec((1, chains, cn, hd), lambda d, i: (d, 0, chunk(d, i), 0))
    flat = lambda t: t.reshape(t.shape[:-4] + (chains, lt, hd))
    y = pl.pallas_call(
        functools.partial(_wkv_kernel, cn=cn, chains=chains),
        grid=(2, n_all),
        in_specs=[shared, shared, shared, per_dir, per_dir, per_dir],
        out_specs=per_dir,
        out_shape=jax.ShapeDtypeStruct((2, chains, lt, hd), F32),
        scratch_shapes=[pltpu.VMEM((chains, hd, hd), F32)],
        compiler_params=_params("parallel", "arbitrary"),
        name="wkv_scan",
    )(flat(r), flat(v), flat(na), flat(lw), flat(kd), flat(bd))
    return y.reshape(2, b, h, lt, hd)


def _rope(x, cos, sin, half):
    lane = lax.broadcasted_iota(jnp.int32, x.shape, 1)
    width = x.shape[1]
    partner = jnp.where(lane % (2 * half) < half, pltpu.roll(x, width - half, 1), pltpu.roll(x, half, 1))
    return x * cos + partner * sin


def _mla_qkv_kernel(p_ref, gqa_ref, gkva_ref, wq_ref, wk_ref, wv_ref, gq_ref, gk_ref, cos_ref, sin_ref,
                    q_ref, kt_ref, v_ref, *, scale):
    p = p_ref[0]
    q_a = p[:, 0:MLA_Q_RANK]
    kv_a = p[:, MLA_Q_RANK:MLA_Q_RANK + MLA_KV_RANK]
    k_rope = p[:, MLA_Q_RANK + MLA_KV_RANK:MLA_Q_RANK + MLA_KV_RANK + LANES]
    qa_n = (_rms(q_a) * gqa_ref[...]).astype(BF16)
    kva_n = (_rms(kv_a) * gkva_ref[...]).astype(BF16)
    q_all = jnp.dot(qa_n, wq_ref[...], preferred_element_type=F32)
    k_all = jnp.dot(kva_n, wk_ref[...], preferred_element_type=F32)
    v_all = jnp.dot(kva_n, wv_ref[...], preferred_element_type=F32)
    cos, sin = cos_ref[...], sin_ref[...]
    for h in range(MLA_HEADS):
        sl = slice(h * LANES, (h + 1) * LANES)
        q = q_all[:, sl]
        q = q * lax.rsqrt(jnp.sum(q * q, axis=-1, keepdims=True) / MLA_QK + NORM_EPS) * gq_ref[...]
        q = _rope(q, cos, sin, MLA_ROPE // 4)
        q_ref[0, h] = (q * scale).astype(q_ref.dtype)
        k = k_all[:, sl] + k_rope
        k = k * lax.rsqrt(jnp.sum(k * k, axis=-1, keepdims=True) / MLA_QK + NORM_EPS) * gk_ref[...]
        k = _rope(k, cos, sin, MLA_ROPE // 4)
        kt_ref[0, h] = k.T.astype(kt_ref.dtype)
        lane = lax.broadcasted_iota(jnp.int32, q.shape, 1)
        v_ref[0, h] = jnp.where(lane < MLA_V, v_all[:, sl], 1.0).astype(v_ref.dtype)


def _pad_heads(w, heads, width):
    k = w.shape[0]
    w = w.reshape(k, heads, width)
    return jnp.pad(w, ((0, 0), (0, 0), (0, LANES - width))).reshape(k, heads * LANES)


def _mla_qkv(p, g_qa, w_q_up, g_kva, w_kv_up, g_q, g_k, cos, sin):
    b, lt, cols = p.shape
    bl = TOKEN_BLOCK
    wq = _pad_heads(w_q_up, MLA_HEADS, MLA_QK).astype(BF16)
    wkv = w_kv_up.reshape(MLA_KV_RANK, MLA_HEADS, MLA_NOPE + MLA_V)
    wk = _pad_heads(wkv[:, :, :MLA_NOPE].reshape(MLA_KV_RANK, -1), MLA_HEADS, MLA_NOPE).astype(BF16)
    wv = _pad_heads(wkv[:, :, MLA_NOPE:].reshape(MLA_KV_RANK, -1), MLA_HEADS, MLA_V).astype(BF16)
    pad = lambda g: jnp.pad(g, (0, LANES - MLA_QK)).reshape(1, LANES)
    consts = [g_qa.reshape(1, -1), g_kva.reshape(1, -1), wq, wk, wv, pad(g_q), pad(g_k)]
    full = lambda a: pl.BlockSpec(a.shape, lambda i, j: (0,) * a.ndim)
    tab = pl.BlockSpec((bl, LANES), lambda i, j: (j, 0))
    return pl.pallas_call(
        functools.partial(_mla_qkv_kernel, scale=MLA_QK ** -0.5 * LOG2E),
        grid=(b, lt // bl),
        in_specs=[pl.BlockSpec((1, bl, cols), lambda i, j: (i, j, 0))] + [full(a) for a in consts] + [tab, tab],
        out_specs=[pl.BlockSpec((1, MLA_HEADS, bl, LANES), lambda i, j: (i, 0, j, 0)),
                   pl.BlockSpec((1, MLA_HEADS, LANES, bl), lambda i, j: (i, 0, 0, j)),
                   pl.BlockSpec((1, MLA_HEADS, bl, LANES), lambda i, j: (i, 0, j, 0))],
        out_shape=[jax.ShapeDtypeStruct((b, MLA_HEADS, lt, LANES), BF16),
                   jax.ShapeDtypeStruct((b, MLA_HEADS, LANES, lt), BF16),
                   jax.ShapeDtypeStruct((b, MLA_HEADS, lt, LANES), BF16)],
        compiler_params=_params("parallel", "parallel"),
        name="mla_qkv",
    )(p, *consts, cos, sin)


def _pair_norm_rope(x, g, cos, sin):
    lane = lax.broadcasted_iota(jnp.int32, x.shape, 1)
    lo = lane < GQA_HEAD_DIM
    xx = x * x
    ss_lo = jnp.sum(jnp.where(lo, xx, 0.0), axis=-1, keepdims=True)
    ss_hi = jnp.sum(jnp.where(lo, 0.0, xx), axis=-1, keepdims=True)
    rs = lax.rsqrt(jnp.where(lo, ss_lo, ss_hi) / GQA_HEAD_DIM + NORM_EPS)
    return _rope(x * rs * g, cos, sin, GQA_HEAD_DIM // 4)


def _gqa_qkv_kernel(p_ref, gq_ref, gk_ref, cos_ref, sin_ref, q_ref, kt_ref, v_ref, *, scale):
    p = p_ref[0]
    cos, sin = cos_ref[...], sin_ref[...]
    qw = GQA_HEADS * GQA_HEAD_DIM
    kw = GQA_KV_HEADS * GQA_HEAD_DIM
    for j in range(GQA_HEADS // 2):
        q = _pair_norm_rope(p[:, j * LANES:(j + 1) * LANES], gq_ref[...], cos, sin) * scale
        q_ref[0, 2 * j] = q[:, :GQA_HEAD_DIM].astype(q_ref.dtype)
        q_ref[0, 2 * j + 1] = q[:, GQA_HEAD_DIM:].astype(q_ref.dtype)
    for j in range(GQA_KV_HEADS // 2):
        k = _pair_norm_rope(p[:, qw + j * LANES:qw + (j + 1) * LANES], gk_ref[...], cos, sin)
        kt = k.T
        kt_ref[0, 2 * j] = kt[:GQA_HEAD_DIM].astype(kt_ref.dtype)
        kt_ref[0, 2 * j + 1] = kt[GQA_HEAD_DIM:].astype(kt_ref.dtype)
        v = p[:, qw + kw + j * LANES:qw + kw + (j + 1) * LANES]
        lo = lax.broadcasted_iota(jnp.int32, v.shape, 1) < GQA_HEAD_DIM
        v_ref[0, 2 * j] = jnp.where(lo, v, 1.0).astype(v_ref.dtype)
        v_ref[0, 2 * j + 1] = jnp.where(lo, pltpu.roll(v, GQA_HEAD_DIM, 1), 1.0).astype(v_ref.dtype)


def _gqa_qkv(p, g_q, g_k, cos, sin):
    b, lt, cols = p.shape
    bl = TOKEN_BLOCK
    hd = GQA_HEAD_DIM
    two = lambda g: jnp.concatenate([g, g]).reshape(1, LANES)
    tab = pl.BlockSpec((bl, LANES), lambda i, j: (j, 0))
    vec = pl.BlockSpec((1, LANES), lambda i, j: (0, 0))
    return pl.pallas_call(
        functools.partial(_gqa_qkv_kernel, scale=hd ** -0.5 * LOG2E),
        grid=(b, lt // bl),
        in_specs=[pl.BlockSpec((1, bl, cols), lambda i, j: (i, j, 0)), vec, vec, tab, tab],
        out_specs=[pl.BlockSpec((1, GQA_HEADS, bl, hd), lambda i, j: (i, 0, j, 0)),
                   pl.BlockSpec((1, GQA_KV_HEADS, hd, bl), lambda i, j: (i, 0, 0, j)),
                   pl.BlockSpec((1, GQA_KV_HEADS, bl, LANES), lambda i, j: (i, 0, j, 0))],
        out_shape=[jax.ShapeDtypeStruct((b, GQA_HEADS, lt, hd), BF16),
                   jax.ShapeDtypeStruct((b, GQA_KV_HEADS, hd, lt), BF16),
                   jax.ShapeDtypeStruct((b, GQA_KV_HEADS, lt, LANES), BF16)],
        compiler_params=_params("parallel", "parallel"),
        name="gqa_qkv",
    )(p, two(g_q), two(g_k), cos, sin)


def _attn_kernel(q_ref, kt_ref, v_ref, o_ref, m_scr, acc_scr, *, groups, bq, n_chunks):
    rows = groups * bq
    q = q_ref[0, 0].reshape(rows, q_ref.shape[-1])
    bk = kt_ref.shape[-1]
    dv = o_ref.shape[-1]
    m_scr[...] = jnp.full_like(m_scr, -jnp.inf)
    acc_scr[...] = jnp.zeros_like(acc_scr)

    def step(j, carry):
        s = jnp.dot(q, kt_ref[0, 0, j], preferred_element_type=F32)
        m_prev = m_scr[...]
        m_new = jnp.maximum(m_prev, jnp.max(s, axis=-1, keepdims=True))
        alpha = jnp.exp2(m_prev - m_new)
        p = jnp.exp2(s - pltpu.repeat(m_new, bk // LANES, axis=1))
        vv = v_ref[0, 0, pl.ds(pl.multiple_of(j * bk, bk), bk), :]
        acc_scr[...] = alpha * acc_scr[...] + jnp.dot(p.astype(BF16), vv, preferred_element_type=F32)
        m_scr[...] = m_new
        return carry

    lax.fori_loop(0, n_chunks, step, 0)
    acc = acc_scr[...]
    o = acc / pltpu.roll(acc, LANES - dv, 1)
    o_ref[0, 0] = o[:, :dv].reshape(groups, bq, dv).astype(o_ref.dtype)


def _attention_call(q, kt, v1, bk):
    b, hq, lq, dk = q.shape
    hk, lk = kt.shape[1], kt.shape[-1]
    dv = LANES // 2
    groups = hq // hk
    bq = min(ATTN_ROWS // groups, lq)
    rows = groups * bq
    n_chunks = lk // bk
    q5 = q.reshape(b, hk, groups, lq, dk)
    ktc = kt.reshape(b, hk, dk, n_chunks, bk).transpose(0, 1, 3, 2, 4)
    out = pl.pallas_call(
        functools.partial(_attn_kernel, groups=groups, bq=bq, n_chunks=n_chunks),
        grid=(b, hk, lq // bq),
        in_specs=[pl.BlockSpec((1, 1, groups, bq, dk), lambda i, h, j: (i, h, 0, j, 0)),
                  pl.BlockSpec((1, 1, n_chunks, dk, bk), lambda i, h, j: (i, h, 0, 0, 0)),
                  pl.BlockSpec((1, 1, lk, LANES), lambda i, h, j: (i, h, 0, 0))],
        out_specs=pl.BlockSpec((1, 1, groups, bq, dv), lambda i, h, j: (i, h, 0, j, 0)),
        out_shape=jax.ShapeDtypeStruct((b, hk, groups, lq, dv), BF16),
        scratch_shapes=[pltpu.VMEM((rows, LANES), F32), pltpu.VMEM((rows, LANES), F32)],
        compiler_params=_params("parallel", "parallel", "arbitrary"),
        name="attention",
    )(q5, ktc, v1)
    return out.reshape(b, hq, lq, dv)


def _key_chunk(lt):
    return max(c for c in range(LANES, ATTN_KEYS + 1, LANES) if lt % c == 0)


def _attention(q, kt, v1, n_ctx):
    o_c = _attention_call(q[:, :, :n_ctx], kt[..., :n_ctx], v1[:, :, :n_ctx], n_ctx)
    o_l = _attention_call(q[:, :, n_ctx:], kt, v1, _key_chunk(kt.shape[-1]))
    return jnp.concatenate([o_c, o_l], axis=2)


def _ab_out_kernel(x_ref, ml_ref, mc_ref, y_ref, bonus_ref, gate_ref, o_ref, lnw_ref, lnb_ref, wrw_ref,
                   wmla_ref, out_ref, *, bl, n_ctx):
    acc = jnp.zeros(out_ref.shape[1:], F32)
    for h in range(RW_HEADS):
        y = y_ref[0, 0, h] + y_ref[1, 0, h]
        mean = jnp.mean(y, axis=-1, keepdims=True)
        yc = y - mean
        var = jnp.mean(yc * yc, axis=-1, keepdims=True)
        yn = yc * lax.rsqrt(var + LN_X_EPS) * lnw_ref[h] + lnb_ref[h]
        rw = (yn + bonus_ref[0, h]) * gate_ref[0, h]
        acc += _bdot(rw, wrw_ref[h])
    for h in range(MLA_HEADS):
        acc += jnp.dot(o_ref[0, h], wmla_ref[h], preferred_element_type=F32)
    gate = _row_mods(ml_ref, mc_ref, pl.program_id(1) * bl, bl, n_ctx, 2)
    out_ref[0] = x_ref[0] + gate * acc


def _ab_out(x, mod_l, mod_c, y, bonus, gate, o, ln_w, ln_b, w_out, n_ctx):
    b, lt, d = x.shape
    bl = TOKEN_BLOCK
    hd = RW_HEAD_DIM
    wrw = w_out[:RW_DIM].reshape(RW_HEADS, hd, d).astype(BF16)
    wmla = w_out[RW_DIM:].reshape(MLA_HEADS, MLA_V, d).astype(BF16)
    xs = pl.BlockSpec((1, bl, d), lambda i, j: (i, j, 0))
    hm = pl.BlockSpec((1, RW_HEADS, bl, hd), lambda i, j: (i, 0, j, 0))
    full = lambda a: pl.BlockSpec(a.shape, lambda i, j: (0,) * a.ndim)
    lnw, lnb = ln_w.reshape(RW_HEADS, 1, hd), ln_b.reshape(RW_HEADS, 1, hd)
    return pl.pallas_call(
        functools.partial(_ab_out_kernel, bl=bl, n_ctx=n_ctx),
        grid=(b, lt // bl),
        in_specs=[xs, pl.BlockSpec((1, 6, d), lambda i, j: (i, 0, 0)), pl.BlockSpec((1, 6, d), lambda i, j: (0, 0, 0)),
                  pl.BlockSpec((2, 1, RW_HEADS, bl, hd), lambda i, j: (0, i, 0, j, 0)), hm, hm,
                  pl.BlockSpec((1, MLA_HEADS, bl, MLA_V), lambda i, j: (i, 0, j, 0)),
                  full(lnw), full(lnb), full(wrw), full(wmla)],
        out_specs=xs,
        out_shape=jax.ShapeDtypeStruct((b, lt, d), F32),
        compiler_params=_params("parallel", "parallel"),
        name="ab_out",
    )(x, mod_l, mod_c, y, bonus, gate, o, lnw, lnb, wrw, wmla)


def _gqa_out_kernel(x_ref, ml_ref, mc_ref, o_ref, w_ref, out_ref, *, bl, n_ctx):
    acc = jnp.zeros(out_ref.shape[1:], F32)
    for h in range(GQA_HEADS):
        acc += jnp.dot(o_ref[0, h], w_ref[h], preferred_element_type=F32)
    gate = _row_mods(ml_ref, mc_ref, pl.program_id(1) * bl, bl, n_ctx, 2)
    out_ref[0] = x_ref[0] + gate * acc


def _gqa_out(x, mod_l, mod_c, o, w_out, n_ctx):
    b, lt, d = x.shape
    bl = TOKEN_BLOCK
    hd = GQA_HEAD_DIM
    w = w_out.reshape(GQA_HEADS, hd, d).astype(BF16)
    xs = pl.BlockSpec((1, bl, d), lambda i, j: (i, j, 0))
    return pl.pallas_call(
        functools.partial(_gqa_out_kernel, bl=bl, n_ctx=n_ctx),
        grid=(b, lt // bl),
        in_specs=[xs, pl.BlockSpec((1, 6, d), lambda i, j: (i, 0, 0)), pl.BlockSpec((1, 6, d), lambda i, j: (0, 0, 0)),
                  pl.BlockSpec((1, GQA_HEADS, bl, hd), lambda i, j: (i, 0, j, 0)),
                  pl.BlockSpec(w.shape, lambda i, j: (0, 0, 0))],
        out_specs=xs,
        out_shape=jax.ShapeDtypeStruct((b, lt, d), F32),
        compiler_params=_params("parallel", "parallel"),
        name="gqa_out",
    )(x, mod_l, mod_c, o, w)


def _first_argmax(vals):
    best, idx = vals[0], jnp.zeros(vals[0].shape, jnp.int32)
    for i in range(1, len(vals)):
        better = vals[i] > best
        idx = jnp.where(better, i, idx)
        best = jnp.where(better, vals[i], best)
    return best, idx


def _pick(vals, idx):
    out = vals[0]
    for i in range(1, len(vals)):
        out = jnp.where(idx == i, vals[i], out)
    return out


def _route_kernel(x_ref, g_ref, ml_ref, mc_ref, rw_ref, rb_ref, t_ref, comb_ref, *, bl, n_ctx):
    first = pl.program_id(1) * bl
    shift = _row_mods(ml_ref, mc_ref, first, bl, n_ctx, 3)
    scale = _row_mods(ml_ref, mc_ref, first, bl, n_ctx, 4)
    t = (_rms(x_ref[0]) * g_ref[...]) * (1.0 + scale) + shift
    t_ref[0] = t.astype(t_ref.dtype)
    logits = jnp.dot(t, rw_ref[...], preferred_element_type=F32, precision=HIGHEST)
    lt = logits.T
    score = [_sigmoid(lt[e:e + 1]) for e in range(N_EXPERTS)]
    biased = [score[e] + rb_ref[e:e + 1, :] for e in range(N_EXPERTS)]
    epg = EXPERTS_PER_GROUP
    group_score = []
    for g in range(N_GROUPS):
        vals = biased[g * epg:(g + 1) * epg]
        pair = [vals[i] + vals[j] for i in range(epg) for j in range(i + 1, epg)]
        group_score.append(functools.reduce(jnp.maximum, pair))
    _, grp = _first_argmax(group_score)
    in_b = [_pick([biased[g * epg + j] for g in range(N_GROUPS)], grp) for j in range(epg)]
    in_s = [_pick([score[g * epg + j] for g in range(N_GROUPS)], grp) for j in range(epg)]
    _, loc1 = _first_argmax(in_b)
    _, loc2 = _first_argmax([jnp.where(loc1 == j, -jnp.inf, in_b[j]) for j in range(epg)])
    w1, w2 = _pick(in_s, loc1), _pick(in_s, loc2)
    wsum = w1 + w2
    w1, w2 = w1 / wsum, w2 / wsum
    e1, e2 = grp * epg + loc1, grp * epg + loc2
    sub = lax.broadcasted_iota(jnp.int32, (LANES, bl), 0)
    comb = jnp.zeros((LANES, bl), F32)
    for e in range(N_EXPERTS):
        c_e = jnp.where(e1 == e, w1, 0.0) + jnp.where(e2 == e, w2, 0.0)
        comb = jnp.where(sub == e, jnp.broadcast_to(c_e, (LANES, bl)), comb)
    comb_ref[0] = comb.T


def _moe_route(x, g, mod_l, mod_c, router_w, router_b, n_ctx):
    b, lt, d = x.shape
    bl = TOKEN_BLOCK
    rw = jnp.pad(router_w, ((0, 0), (0, LANES - N_EXPERTS)))
    xs = pl.BlockSpec((1, bl, d), lambda i, j: (i, j, 0))
    return pl.pallas_call(
        functools.partial(_route_kernel, bl=bl, n_ctx=n_ctx),
        grid=(b, lt // bl),
        in_specs=[xs, pl.BlockSpec((1, d), lambda i, j: (0, 0)),
                  pl.BlockSpec((1, 6, d), lambda i, j: (i, 0, 0)), pl.BlockSpec((1, 6, d), lambda i, j: (0, 0, 0)),
                  pl.BlockSpec(rw.shape, lambda i, j: (0, 0)), pl.BlockSpec((N_EXPERTS, 1), lambda i, j: (0, 0))],
        out_specs=[xs, pl.BlockSpec((1, bl, LANES), lambda i, j: (i, j, 0))],
        out_shape=[jax.ShapeDtypeStruct((b, lt, d), BF16), jax.ShapeDtypeStruct((b, lt, LANES), F32)],
        compiler_params=_params("parallel", "parallel"),
        name="moe_route",
    )(x, g.reshape(1, d), mod_l, mod_c, rw, router_b.reshape(N_EXPERTS, 1))


def _ffn(t, wg, wu, wd, cw):
    g = jnp.dot(t, wg, preferred_element_type=F32)
    u = jnp.dot(t, wu, preferred_element_type=F32)
    h = _silu(g) * u
    if cw is not None:
        h = h * cw
    return jnp.dot(h.astype(BF16), wd, preferred_element_type=F32)


def _experts_kernel(x_ref, t_ref, comb_ref, ml_ref, mc_ref, wg_ref, wu_ref, wd_ref, sg_ref, su_ref, sd_ref,
                    out_ref, acc_ref, *, bl, n_ctx):
    e = pl.program_id(2)
    t = t_ref[0]

    @pl.when(e == 0)
    def _():
        acc_ref[...] = _ffn(t, sg_ref[...], su_ref[...], sd_ref[...], None)

    lane = lax.broadcasted_iota(jnp.int32, comb_ref.shape[1:], 1)
    cw = jnp.sum(jnp.where(lane == e, comb_ref[0], 0.0), axis=-1, keepdims=True)
    acc_ref[...] += _ffn(t, wg_ref[0], wu_ref[0], wd_ref[0], cw)

    @pl.when(e == N_EXPERTS - 1)
    def _():
        gate = _row_mods(ml_ref, mc_ref, pl.program_id(1) * bl, bl, n_ctx, 5)
        out_ref[0] = x_ref[0] + gate * acc_ref[...]


def _moe_experts(x, t, comb, mod_l, mod_c, w_gate, w_up, w_down, sh_gate, sh_up, sh_down, n_ctx):
    b, lt, d = x.shape
    bl = _row_block(lt, MOE_ROWS)
    ff = w_gate.shape[-1]
    xs = pl.BlockSpec((1, bl, d), lambda i, j, e: (i, j, 0))
    full = lambda a: pl.BlockSpec(a.shape, lambda i, j, e: (0,) * a.ndim)
    ws = [w_gate.astype(BF16), w_up.astype(BF16), w_down.astype(BF16),
          sh_gate.astype(BF16), sh_up.astype(BF16), sh_down.astype(BF16)]
    return pl.pallas_call(
        functools.partial(_experts_kernel, bl=bl, n_ctx=n_ctx),
        grid=(b, lt // bl, N_EXPERTS),
        in_specs=[xs, xs, pl.BlockSpec((1, bl, LANES), lambda i, j, e: (i, j, 0)),
                  pl.BlockSpec((1, 6, d), lambda i, j, e: (i, 0, 0)), pl.BlockSpec((1, 6, d), lambda i, j, e: (0, 0, 0)),
                  pl.BlockSpec((1, d, ff), lambda i, j, e: (e, 0, 0)),
                  pl.BlockSpec((1, d, ff), lambda i, j, e: (e, 0, 0)),
                  pl.BlockSpec((1, ff, d), lambda i, j, e: (e, 0, 0)),
                  full(ws[3]), full(ws[4]), full(ws[5])],
        out_specs=xs,
        out_shape=jax.ShapeDtypeStruct((b, lt, d), F32),
        scratch_shapes=[pltpu.VMEM((bl, d), F32)],
        compiler_params=_params("parallel", "parallel", "arbitrary"),
        name="moe_experts",
    )(x, t, comb, mod_l, mod_c, *ws)


def _rope_tables(n_ctx, seq, rope_dims, lane_offset, repeat):
    half = rope_dims // 4
    t = jnp.arange(seq)
    rowp = (t // GRID_W).astype(F32)
    colp = (t % GRID_W).astype(F32)
    inv = ROPE_THETA ** (-jnp.arange(half, dtype=F32) / half)
    ar, ac = rowp[:, None] * inv[None, :], colp[:, None] * inv[None, :]
    cos = jnp.concatenate([jnp.cos(ar), jnp.cos(ar), jnp.cos(ac), jnp.cos(ac)], axis=1)
    sin = jnp.concatenate([-jnp.sin(ar), jnp.sin(ar), -jnp.sin(ac), jnp.sin(ac)], axis=1)
    width = LANES // repeat
    padl, padr = lane_offset, width - lane_offset - rope_dims
    cos = jnp.pad(cos, ((n_ctx, 0), (0, 0)), constant_values=1.0)
    sin = jnp.pad(sin, ((n_ctx, 0), (0, 0)))
    cos = jnp.pad(cos, ((0, 0), (padl, padr)), constant_values=1.0)
    sin = jnp.pad(sin, ((0, 0), (padl, padr)))
    return jnp.tile(cos, (1, repeat)), jnp.tile(sin, (1, repeat))


def kernel(x, c, ctx, c_ctx, ada_w, ada_b, norm1_g, norm2_g, ab_w_in, ab_w_out, rw_mu, rw_w0, rw_w2, rw_a0, rw_a2, rw_k_k, rw_k_a, rw_r_k, rw_g2, rw_ln_w, rw_ln_b, mla_g_qa, mla_w_q_up, mla_g_kva, mla_w_kv_up, mla_g_q, mla_g_k, gqa_w_in, gqa_w_out, gqa_g_q, gqa_g_k, router_w, router_b, moe_w_gate, moe_w_up, moe_w_down, shared_w_gate, shared_w_up, shared_w_down):
    b, seq, d = x.shape
    n_ctx = ctx.shape[1]
    depth = ada_w.shape[0]
    xs = jnp.concatenate([ctx, x], axis=1)

    cvec = jnp.concatenate([c, c_ctx[None, :]], axis=0)
    cvec = jnp.pad(cvec, ((0, 8 - (b + 1) % 8), (0, 0))) if (b + 1) % 8 else cvec
    mods = _ada_mods(cvec, ada_w, ada_b).reshape(depth, -1, 6, d)

    mla_cos, mla_sin = _rope_tables(n_ctx, seq, MLA_ROPE, MLA_NOPE, 1)
    gqa_cos, gqa_sin = _rope_tables(n_ctx, seq, GQA_HEAD_DIM, 0, 2)

    for l in range(depth):
        i = l // 2
        mod_l, mod_c = mods[l, :b], mods[l, b:b + 1]
        if l % 2 == 0:
            w_in = ab_w_in[i]
            w_rw = w_in[:, :RW_COLS].astype(BF16)
            w_m = w_in[:, RW_COLS:]
            w_mla = jnp.concatenate(
                [w_m[:, :MLA_Q_RANK + MLA_KV_RANK],
                 jnp.pad(w_m[:, MLA_Q_RANK + MLA_KV_RANK:], ((0, 0), (MLA_NOPE, LANES - MLA_NOPE - MLA_ROPE)))],
                axis=1).astype(BF16)
            p_rw, p_mla = _ln_mod_mm(xs, norm1_g[l], mod_l, mod_c, [w_rw, w_mla], n_ctx)
            r, v, na, lw, kd, bd, gate, bonus = _rwkv_prep(
                p_rw, n_ctx, rw_mu[i], rw_w0[i], rw_w2[i], rw_a0[i], rw_a2[i], rw_k_k[i], rw_k_a[i],
                rw_r_k[i].reshape(-1), rw_g2[i])
            y = _wkv_scan(r, v, na, lw, kd, bd, n_ctx)
            q, kt, vm = _mla_qkv(p_mla, mla_g_qa[i], mla_w_q_up[i], mla_g_kva[i], mla_w_kv_up[i],
                                 mla_g_q[i], mla_g_k[i], mla_cos, mla_sin)
            o = _attention(q, kt, vm, n_ctx)
            xs = _ab_out(xs, mod_l, mod_c, y, bonus, gate, o, rw_ln_w[i], rw_ln_b[i], ab_w_out[i], n_ctx)
        else:
            (p,) = _ln_mod_mm(xs, norm1_g[l], mod_l, mod_c, [gqa_w_in[i].astype(BF16)], n_ctx)
            q, kt, vg = _gqa_qkv(p, gqa_g_q[i], gqa_g_k[i], gqa_cos, gqa_sin)
            o = _attention(q, kt, vg, n_ctx)
            xs = _gqa_out(xs, mod_l, mod_c, o, gqa_w_out[i], n_ctx)
        t, comb = _moe_route(xs, norm2_g[l], mod_l, mod_c, router_w, router_b, n_ctx)
        xs = _moe_experts(xs, t, comb, mod_l, mod_c, moe_w_gate[l], moe_w_up[l], moe_w_down[l],
                          shared_w_gate[l], shared_w_up[l], shared_w_down[l], n_ctx)
    return xs[:, n_ctx:]
```

```python
import functools
import math

import jax
import jax.numpy as jnp
from jax import lax
from jax.experimental import pallas as pl
from jax.experimental.pallas import tpu as pltpu

F32 = jnp.float32
BF16 = jnp.bfloat16
HIGHEST = lax.Precision.HIGHEST

GRID_W = 64
RW_HEADS = 8
RW_HEAD_DIM = 64
RW_DIM = RW_HEADS * RW_HEAD_DIM
DECAY_LORA = 64
ICLR_LORA = 64
GATE_LORA = 128
RW_COLS = 3 * RW_DIM + 2 * DECAY_LORA + 2 * ICLR_LORA + GATE_LORA
LN_X_EPS = 64e-5
MLA_HEADS = 8
MLA_NOPE = 64
MLA_ROPE = 32
MLA_V = 64
MLA_QK = MLA_NOPE + MLA_ROPE
MLA_Q_RANK = 384
MLA_KV_RANK = 256
GQA_HEADS = 16
GQA_KV_HEADS = 4
GQA_HEAD_DIM = 64
N_EXPERTS = 16
N_GROUPS = 4
EXPERTS_PER_GROUP = N_EXPERTS // N_GROUPS
ROPE_THETA = 10000.0
NORM_EPS = 1e-6

LANES = 128
V7X_VMEM_BYTES = 64 * 1024 * 1024
VMEM_LIMIT = V7X_VMEM_BYTES - 8 * 1024 * 1024

TOKEN_BLOCK = 256
SCAN_CHUNK = 64
SCAN_INTERLEAVE = 16
ATTN_ROWS = 1024
ATTN_KEYS = 1408
MOE_ROWS = 768
MOE_EXPERTS_PER_STEP = 4
LOG2E = math.log2(math.e)


def _row_block(lt, limit):
    return max(r for r in range(TOKEN_BLOCK, limit + 1, TOKEN_BLOCK) if lt % r == 0)


def _params(*sem):
    return pltpu.CompilerParams(dimension_semantics=sem, vmem_limit_bytes=VMEM_LIMIT)


def _bdot(a, b):
    return jnp.dot(a.astype(BF16), b.astype(BF16), preferred_element_type=F32)


def _batched(a, b, ca, cb):
    return lax.dot_general(a.astype(BF16), b.astype(BF16), (((ca,), (cb,)), ((0,), (0,))),
                           preferred_element_type=F32)


def _bmm(a, b):
    return _batched(a, b, 2, 1)


def _bmm_nt(a, b):
    return _batched(a, b, 2, 2)


def _bmm_tn(a, b):
    return _batched(a, b, 1, 1)


def _sigmoid(x):
    return 1.0 / (1.0 + jnp.exp(-x))


def _silu(x):
    return x * _sigmoid(x)


def _softplus(x):
    return jnp.maximum(x, 0.0) + jnp.log(1.0 + jnp.exp(-jnp.abs(x)))


def _rms(x):
    return x * lax.rsqrt(jnp.mean(x * x, axis=-1, keepdims=True) + NORM_EPS)


def _row_mods(ml_ref, mc_ref, first_row, rows, n_ctx, idx):
    r = first_row + lax.broadcasted_iota(jnp.int32, (rows, 1), 0)
    return jnp.where(r < n_ctx, mc_ref[0, idx:idx + 1, :], ml_ref[0, idx:idx + 1, :])


def _ada_kernel(c_ref, w_ref, b_ref, o_ref):
    s = _silu(c_ref[...])
    o_ref[0] = jnp.dot(s, w_ref[0], preferred_element_type=F32, precision=HIGHEST) + b_ref[0]


def _ada_mods(cvec, ada_w, ada_b):
    depth, d, n = ada_w.shape
    rows = cvec.shape[0]
    bn = n // 4
    return pl.pallas_call(
        _ada_kernel,
        grid=(depth, n // bn),
        in_specs=[pl.BlockSpec((rows, d), lambda l, j: (0, 0)),
                  pl.BlockSpec((1, d, bn), lambda l, j: (l, 0, j)),
                  pl.BlockSpec((1, 1, bn), lambda l, j: (l, 0, j))],
        out_specs=pl.BlockSpec((1, rows, bn), lambda l, j: (l, 0, j)),
        out_shape=jax.ShapeDtypeStruct((depth, rows, n), F32),
        compiler_params=_params("parallel", "parallel"),
        name="ada_mods",
    )(cvec, ada_w, ada_b.reshape(depth, 1, n))


def _ln_mod_mm_kernel(x_ref, g_ref, ml_ref, mc_ref, *refs, n_w, bl, n_ctx):
    w_refs, o_refs = refs[:n_w], refs[n_w:]
    first = pl.program_id(1) * bl
    shift = _row_mods(ml_ref, mc_ref, first, bl, n_ctx, 0)
    scale = _row_mods(ml_ref, mc_ref, first, bl, n_ctx, 1)
    h = (_rms(x_ref[0]) * g_ref[...]) * (1.0 + scale) + shift
    hb = h.astype(BF16)
    for w_ref, o_ref in zip(w_refs, o_refs):
        o_ref[0] = jnp.dot(hb, w_ref[...], preferred_element_type=F32).astype(o_ref.dtype)


def _ln_mod_mm(x, g, mod_l, mod_c, ws, n_ctx):
    b, lt, d = x.shape
    bl = TOKEN_BLOCK
    kern = functools.partial(_ln_mod_mm_kernel, n_w=len(ws), bl=bl, n_ctx=n_ctx)
    return pl.pallas_call(
        kern,
        grid=(b, lt // bl),
        in_specs=[pl.BlockSpec((1, bl, d), lambda i, j: (i, j, 0)),
                  pl.BlockSpec((1, d), lambda i, j: (0, 0)),
                  pl.BlockSpec((1, 6, d), lambda i, j: (i, 0, 0)),
                  pl.BlockSpec((1, 6, d), lambda i, j: (0, 0, 0))]
        + [pl.BlockSpec(w.shape, lambda i, j: (0, 0)) for w in ws],
        out_specs=[pl.BlockSpec((1, bl, w.shape[1]), lambda i, j: (i, j, 0)) for w in ws],
        out_shape=[jax.ShapeDtypeStruct((b, lt, w.shape[1]), F32) for w in ws],
        compiler_params=_params("parallel", "parallel"),
        name="ln_mod_mm",
    )(x, g.reshape(1, d), mod_l, mod_c, *ws)


def _rwkv_prep_kernel(p_ref, prev_ref, next_ref, mu_ref, w0_ref, w2_ref, a0_ref, a2_ref, kk_ref,
                      ka_ref, rk_ref, g2_ref, r_ref, v_ref, na_ref, lw_ref, kd_ref, bd_ref,
                      gate_ref, bonus_ref, *, bl):
    p = p_ref[0]
    row = lax.broadcasted_iota(jnp.int32, p.shape, 0)
    prev = jnp.where(row == 0, prev_ref[0, 0], pltpu.roll(p, 1, 0))
    nxt = jnp.where(row == bl - 1, next_ref[0, 0], pltpu.roll(p, bl - 1, 0))
    pm = p + mu_ref[...] * (0.5 * (prev + nxt) - p)

    o = 3 * RW_DIM
    r = pm[:, 0:RW_DIM]
    k = pm[:, RW_DIM:2 * RW_DIM]
    v = pm[:, 2 * RW_DIM:o]
    w1 = (pm[:, o:o + DECAY_LORA], pm[:, o + DECAY_LORA:o + 2 * DECAY_LORA])
    o += 2 * DECAY_LORA
    a1 = (pm[:, o:o + ICLR_LORA], pm[:, o + ICLR_LORA:o + 2 * ICLR_LORA])
    o += 2 * ICLR_LORA
    g1 = pm[:, o:o + GATE_LORA]

    kk = k * kk_ref[...]
    lw, kd, ad = [], [], []
    for d in range(2):
        z = w0_ref[d:d + 1, :] + _bdot(jnp.tanh(w1[d]), w2_ref[d])
        lw.append(-jnp.exp(-_softplus(-z) - 0.5))
        a = _sigmoid(a0_ref[d:d + 1, :] + _bdot(a1[d], a2_ref[d]))
        ad.append(a)
        kd.append(k * (1.0 + (a - 1.0) * ka_ref[...]))
    gate = _bdot(_sigmoid(g1), g2_ref[...])
    rkk = r * (kd[0] + kd[1]) * rk_ref[...]

    for h in range(RW_HEADS):
        sl = slice(h * RW_HEAD_DIM, (h + 1) * RW_HEAD_DIM)
        kk_h = kk[:, sl]
        kk_h = kk_h * lax.rsqrt(jnp.sum(kk_h * kk_h, axis=-1, keepdims=True) + 1e-12)
        v_h = v[:, sl]
        r_ref[0, h] = r[:, sl]
        v_ref[0, h] = v_h
        na_ref[0, h] = -kk_h
        for d in range(2):
            lw_ref[d, 0, h] = lw[d][:, sl]
            kd_ref[d, 0, h] = kd[d][:, sl]
            bd_ref[d, 0, h] = kk_h * ad[d][:, sl]
        gate_ref[0, h] = gate[:, sl]
        bonus_ref[0, h] = jnp.sum(rkk[:, sl], axis=-1, keepdims=True) * v_h


def _rwkv_prep(p, n_ctx, mu, w0, w2, a0, a2, k_k, k_a, r_k, g2):
    b, lt, cols = p.shape
    bl = TOKEN_BLOCK
    nblk = lt // bl
    zero = jnp.zeros((b, 1, cols), F32)
    lastrows = p[:, bl - 1::bl]
    firstrows = p[:, ::bl]
    prev_edge = jnp.concatenate([zero, lastrows[:, :-1]], axis=1)
    next_edge = jnp.concatenate([firstrows[:, 1:], zero], axis=1)
    cb = n_ctx // bl
    blk = jnp.arange(nblk)[None, :, None]
    prev_edge = jnp.where(blk == cb, 0.0, prev_edge).reshape(b, nblk, 1, cols)
    next_edge = jnp.where(blk == cb - 1, 0.0, next_edge).reshape(b, nblk, 1, cols)

    hm = jax.ShapeDtypeStruct((b, RW_HEADS, lt, RW_HEAD_DIM), F32)
    hm2 = jax.ShapeDtypeStruct((2, b, RW_HEADS, lt, RW_HEAD_DIM), F32)
    hm_spec = pl.BlockSpec((1, RW_HEADS, bl, RW_HEAD_DIM), lambda i, j: (i, 0, j, 0))
    hm2_spec = pl.BlockSpec((2, 1, RW_HEADS, bl, RW_HEAD_DIM), lambda i, j: (0, i, 0, j, 0))
    full = lambda a: pl.BlockSpec(a.shape, lambda i, j: (0,) * a.ndim)
    consts = [mu.reshape(1, cols), w0, w2.astype(BF16), a0, a2.astype(BF16), k_k.reshape(1, RW_DIM),
              k_a.reshape(1, RW_DIM), r_k.reshape(1, RW_DIM), g2.astype(BF16)]
    return pl.pallas_call(
        functools.partial(_rwkv_prep_kernel, bl=bl),
        grid=(b, nblk),
        in_specs=[pl.BlockSpec((1, bl, cols), lambda i, j: (i, j, 0)),
                  pl.BlockSpec((1, 1, 1, cols), lambda i, j: (i, j, 0, 0)),
                  pl.BlockSpec((1, 1, 1, cols), lambda i, j: (i, j, 0, 0))]
        + [full(a) for a in consts],
        out_specs=[hm_spec, hm_spec, hm_spec, hm2_spec, hm2_spec, hm2_spec, hm_spec, hm_spec],
        out_shape=[hm, hm, hm, hm2, hm2, hm2, hm, hm],
        compiler_params=_params("parallel", "parallel"),
        name="rwkv_prep",
    )(p, prev_edge, next_edge, *consts)


def _wkv_kernel(r_ref, v_ref, a_ref, lw_ref, k_ref, b_ref, y_ref, st_ref, *, cn, chains):
    d = pl.program_id(0)

    @pl.when(pl.program_id(1) == 0)
    def _():
        st_ref[...] = jnp.zeros_like(st_ref)

    row = lax.broadcasted_iota(jnp.int32, (cn, cn), 0)
    col = lax.broadcasted_iota(jnp.int32, (cn, cn), 1)
    lead = (row - col) * jnp.where(d == 0, 1, -1)
    before = lead > 0
    upto = lead >= 0
    eye = (col == row).astype(F32)
    cum_mat = upto.astype(F32)
    hd = r_ref.shape[-1]
    eye_h = (lax.broadcasted_iota(jnp.int32, (hd, hd), 0)
             == lax.broadcasted_iota(jnp.int32, (hd, hd), 1)).astype(F32)
    n_double = int(math.log2(cn)) - 1

    nb = SCAN_INTERLEAVE
    cum_b = jnp.broadcast_to(cum_mat, (nb, cn, cn))

    def group(g, carry):
        sl = pl.ds(g * nb, nb)
        r, v, a = r_ref[sl], v_ref[sl], a_ref[sl]
        lw, k, b = lw_ref[0, sl], k_ref[0, sl], b_ref[0, sl]
        cum = lax.dot_general(cum_b, lw, (((2,), (1,)), ((0,), (0,))), preferred_element_type=F32,
                              precision=HIGHEST)
        tot = jnp.sum(lw, axis=1, keepdims=True)
        at = a * jnp.exp(cum - lw)
        rt = r * jnp.exp(cum)
        einv = jnp.exp(-cum)
        bt, kt = b * einv, k * einv
        eend = jnp.exp(tot - cum)
        bh, kh = b * eend, k * eend

        a_ab = jnp.where(before, _bmm_nt(at, bt), 0.0)
        a_ak = jnp.where(before, _bmm_nt(at, kt), 0.0)
        a_rb = jnp.where(upto, _bmm_nt(rt, bt), 0.0)
        a_rk = jnp.where(upto, _bmm_nt(rt, kt), 0.0)

        t = eye + a_ab
        x = _bmm(a_ab, a_ab)
        for _ in range(n_double - 1):
            tx = _bmm(jnp.concatenate([t, x], axis=1), x)
            t = t + tx[:, :cn]
            x = tx[:, cn:]
        t = t + _bmm(t, x)
        w = _bmm(t, at)
        up = _bmm(t, _bmm(a_ak, v))
        qe = rt + _bmm(a_rb, w)
        yp = _bmm(a_rb, up) + _bmm(a_rk, v)
        mc = eye_h * jnp.exp(tot) + _bmm_tn(bh, w)
        nc = _bmm_tn(bh, up) + _bmm_tn(kh, v)

        st = st_ref[sl]
        y_ref[0, sl] = _bmm(qe, st) + yp
        st_ref[sl] = _bmm(mc, st) + nc
        return carry

    lax.fori_loop(0, chains // SCAN_INTERLEAVE, group, 0)


def _wkv_scan(r, v, na, lw, kd, bd, n_ctx):
    b, h, lt, hd = r.shape
    cn = SCAN_CHUNK
    chains = b * h
    n_all, n_c = lt // cn, n_ctx // cn

    def chunk(d, i):
        rev = jnp.where(i < n_c, n_c - 1 - i, n_all - 1 - (i - n_c))
        return jnp.where(d == 0, i, rev)

    shared = pl.BlockSpec((chains, cn, hd), lambda d, i: (0, chunk(d, i), 0))
    per_dir = pl.BlockSpec((1, chains, cn, hd), lambda d, i: (d, 0, chunk(d, i), 0))
    flat = lambda t: t.reshape(t.shape[:-4] + (chains, lt, hd))
    y = pl.pallas_call(
        functools.partial(_wkv_kernel, cn=cn, chains=chains),
        grid=(2, n_all),
        in_specs=[shared, shared, shared, per_dir, per_dir, per_dir],
        out_specs=per_dir,
        out_shape=jax.ShapeDtypeStruct((2, chains, lt, hd), F32),
        scratch_shapes=[pltpu.VMEM((chains, hd, hd), F32)],
        compiler_params=_params("parallel", "arbitrary"),
        name="wkv_scan",
    )(flat(r), flat(v), flat(na), flat(lw), flat(kd), flat(bd))
    return y.reshape(2, b, h, lt, hd)


def _rope(x, cos, sin, half):
    lane = lax.broadcasted_iota(jnp.int32, x.shape, 1)
    width = x.shape[1]
    partner = jnp.where(lane % (2 * half) < half, pltpu.roll(x, width - half, 1), pltpu.roll(x, half, 1))
    return x * cos + partner * sin


def _mla_qkv_kernel(p_ref, gqa_ref, gkva_ref, wq_ref, wk_ref, wv_ref, gq_ref, gk_ref, cos_ref, sin_ref,
                    q_ref, kt_ref, v_ref, *, scale):
    p = p_ref[0]
    q_a = p[:, 0:MLA_Q_RANK]
    kv_a = p[:, MLA_Q_RANK:MLA_Q_RANK + MLA_KV_RANK]
    k_rope = p[:, MLA_Q_RANK + MLA_KV_RANK:MLA_Q_RANK + MLA_KV_RANK + LANES]
    qa_n = (_rms(q_a) * gqa_ref[...]).astype(BF16)
    kva_n = (_rms(kv_a) * gkva_ref[...]).astype(BF16)
    q_all = jnp.dot(qa_n, wq_ref[...], preferred_element_type=F32)
    k_all = jnp.dot(kva_n, wk_ref[...], preferred_element_type=F32)
    v_all = jnp.dot(kva_n, wv_ref[...], preferred_element_type=F32)
    cos, sin = cos_ref[...], sin_ref[...]
    for h in range(MLA_HEADS):
        sl = slice(h * LANES, (h + 1) * LANES)
        q = q_all[:, sl]
        q = q * lax.rsqrt(jnp.sum(q * q, axis=-1, keepdims=True) / MLA_QK + NORM_EPS) * gq_ref[...]
        q = _rope(q, cos, sin, MLA_ROPE // 4)
        q_ref[0, h] = (q * scale).astype(q_ref.dtype)
        k = k_all[:, sl] + k_rope
        k = k * lax.rsqrt(jnp.sum(k * k, axis=-1, keepdims=True) / MLA_QK + NORM_EPS) * gk_ref[...]
        k = _rope(k, cos, sin, MLA_ROPE // 4)
        kt_ref[0, h] = k.T.astype(kt_ref.dtype)
        lane = lax.broadcasted_iota(jnp.int32, q.shape, 1)
        v_ref[0, h] = jnp.where(lane < MLA_V, v_all[:, sl], 1.0).astype(v_ref.dtype)


def _pad_heads(w, heads, width):
    k = w.shape[0]
    w = w.reshape(k, heads, width)
    return jnp.pad(w, ((0, 0), (0, 0), (0, LANES - width))).reshape(k, heads * LANES)


def _mla_qkv(p, g_qa, w_q_up, g_kva, w_kv_up, g_q, g_k, cos, sin):
    b, lt, cols = p.shape
    bl = TOKEN_BLOCK
    wq = _pad_heads(w_q_up, MLA_HEADS, MLA_QK).astype(BF16)
    wkv = w_kv_up.reshape(MLA_KV_RANK, MLA_HEADS, MLA_NOPE + MLA_V)
    wk = _pad_heads(wkv[:, :, :MLA_NOPE].reshape(MLA_KV_RANK, -1), MLA_HEADS, MLA_NOPE).astype(BF16)
    wv = _pad_heads(wkv[:, :, MLA_NOPE:].reshape(MLA_KV_RANK, -1), MLA_HEADS, MLA_V).astype(BF16)
    pad = lambda g: jnp.pad(g, (0, LANES - MLA_QK)).reshape(1, LANES)
    consts = [g_qa.reshape(1, -1), g_kva.reshape(1, -1), wq, wk, wv, pad(g_q), pad(g_k)]
    full = lambda a: pl.BlockSpec(a.shape, lambda i, j: (0,) * a.ndim)
    tab = pl.BlockSpec((bl, LANES), lambda i, j: (j, 0))
    return pl.pallas_call(
        functools.partial(_mla_qkv_kernel, scale=MLA_QK ** -0.5 * LOG2E),
        grid=(b, lt // bl),
        in_specs=[pl.BlockSpec((1, bl, cols), lambda i, j: (i, j, 0))] + [full(a) for a in consts] + [tab, tab],
        out_specs=[pl.BlockSpec((1, MLA_HEADS, bl, LANES), lambda i, j: (i, 0, j, 0)),
                   pl.BlockSpec((1, MLA_HEADS, LANES, bl), lambda i, j: (i, 0, 0, j)),
                   pl.BlockSpec((1, MLA_HEADS, bl, LANES), lambda i, j: (i, 0, j, 0))],
        out_shape=[jax.ShapeDtypeStruct((b, MLA_HEADS, lt, LANES), BF16),
                   jax.ShapeDtypeStruct((b, MLA_HEADS, LANES, lt), BF16),
                   jax.ShapeDtypeStruct((b, MLA_HEADS, lt, LANES), BF16)],
        compiler_params=_params("parallel", "parallel"),
        name="mla_qkv",
    )(p, *consts, cos, sin)


def _pair_norm_rope(x, g, cos, sin):
    lane = lax.broadcasted_iota(jnp.int32, x.shape, 1)
    lo = lane < GQA_HEAD_DIM
    xx = x * x
    ss_lo = jnp.sum(jnp.where(lo, xx, 0.0), axis=-1, keepdims=True)
    ss_hi = jnp.sum(jnp.where(lo, 0.0, xx), axis=-1, keepdims=True)
    rs = lax.rsqrt(jnp.where(lo, ss_lo, ss_hi) / GQA_HEAD_DIM + NORM_EPS)
    return _rope(x * rs * g, cos, sin, GQA_HEAD_DIM // 4)


def _gqa_qkv_kernel(p_ref, gq_ref, gk_ref, cos_ref, sin_ref, q_ref, kt_ref, v_ref, *, scale):
    p = p_ref[0]
    cos, sin = cos_ref[...], sin_ref[...]
    qw = GQA_HEADS * GQA_HEAD_DIM
    kw = GQA_KV_HEADS * GQA_HEAD_DIM
    for j in range(GQA_HEADS // 2):
        q = _pair_norm_rope(p[:, j * LANES:(j + 1) * LANES], gq_ref[...], cos, sin) * scale
        q_ref[0, 2 * j] = q[:, :GQA_HEAD_DIM].astype(q_ref.dtype)
        q_ref[0, 2 * j + 1] = q[:, GQA_HEAD_DIM:].astype(q_ref.dtype)
    for j in range(GQA_KV_HEADS // 2):
        k = _pair_norm_rope(p[:, qw + j * LANES:qw + (j + 1) * LANES], gk_ref[...], cos, sin)
        kt = k.T
        kt_ref[0, 2 * j] = kt[:GQA_HEAD_DIM].astype(kt_ref.dtype)
        kt_ref[0, 2 * j + 1] = kt[GQA_HEAD_DIM:].astype(kt_ref.dtype)
        v = p[:, qw + kw + j * LANES:qw + kw + (j + 1) * LANES]
        lo = lax.broadcasted_iota(jnp.int32, v.shape, 1) < GQA_HEAD_DIM
        v_ref[0, 2 * j] = jnp.where(lo, v, 1.0).astype(v_ref.dtype)
        v_ref[0, 2 * j + 1] = jnp.where(lo, pltpu.roll(v, GQA_HEAD_DIM, 1), 1.0).astype(v_ref.dtype)


def _gqa_qkv(p, g_q, g_k, cos, sin):
    b, lt, cols = p.shape
    bl = TOKEN_BLOCK
    hd = GQA_HEAD_DIM
    two = lambda g: jnp.concatenate([g, g]).reshape(1, LANES)
    tab = pl.BlockSpec((bl, LANES), lambda i, j: (j, 0))
    vec = pl.BlockSpec((1, LANES), lambda i, j: (0, 0))
    return pl.pallas_call(
        functools.partial(_gqa_qkv_kernel, scale=hd ** -0.5 * LOG2E),
        grid=(b, lt // bl),
        in_specs=[pl.BlockSpec((1, bl, cols), lambda i, j: (i, j, 0)), vec, vec, tab, tab],
        out_specs=[pl.BlockSpec((1, GQA_HEADS, bl, hd), lambda i, j: (i, 0, j, 0)),
                   pl.BlockSpec((1, GQA_KV_HEADS, hd, bl), lambda i, j: (i, 0, 0, j)),
                   pl.BlockSpec((1, GQA_KV_HEADS, bl, LANES), lambda i, j: (i, 0, j, 0))],
        out_shape=[jax.ShapeDtypeStruct((b, GQA_HEADS, lt, hd), BF16),
                   jax.ShapeDtypeStruct((b, GQA_KV_HEADS, hd, lt), BF16),
                   jax.ShapeDtypeStruct((b, GQA_KV_HEADS, lt, LANES), BF16)],
        compiler_params=_params("parallel", "parallel"),
        name="gqa_qkv",
    )(p, two(g_q), two(g_k), cos, sin)


def _attn_kernel(q_ref, kt_ref, v_ref, o_ref, m_scr, acc_scr, *, groups, bq, n_chunks):
    rows = groups * bq
    q = q_ref[0, 0].reshape(rows, q_ref.shape[-1])
    bk = kt_ref.shape[-1]
    dv = o_ref.shape[-1]
    m_scr[...] = jnp.full_like(m_scr, -jnp.inf)
    acc_scr[...] = jnp.zeros_like(acc_scr)

    def step(j, carry):
        s = jnp.dot(q, kt_ref[0, 0, j], preferred_element_type=F32)
        m_prev = m_scr[...]
        m_new = jnp.maximum(m_prev, jnp.max(s, axis=-1, keepdims=True))
        alpha = jnp.exp2(m_prev - m_new)
        p = jnp.exp2(s - pltpu.repeat(m_new, bk // LANES, axis=1))
        vv = v_ref[0, 0, j * bk:(j + 1) * bk, :]
        acc_scr[...] = alpha * acc_scr[...] + jnp.dot(p.astype(BF16), vv, preferred_element_type=F32)
        m_scr[...] = m_new
        return carry

    for j in range(n_chunks):
        step(j, 0)
    acc = acc_scr[...]
    o = acc / pltpu.roll(acc, LANES - dv, 1)
    o_ref[0, 0] = o[:, :dv].reshape(groups, bq, dv).astype(o_ref.dtype)


def _attention_call(q, kt, v1, bk):
    b, hq, lq, dk = q.shape
    hk, lk = kt.shape[1], kt.shape[-1]
    dv = LANES // 2
    groups = hq // hk
    bq = min(ATTN_ROWS // groups, lq)
    rows = groups * bq
    n_chunks = lk // bk
    q5 = q.reshape(b, hk, groups, lq, dk)
    ktc = kt.reshape(b, hk, dk, n_chunks, bk).transpose(0, 1, 3, 2, 4)
    out = pl.pallas_call(
        functools.partial(_attn_kernel, groups=groups, bq=bq, n_chunks=n_chunks),
        grid=(b, hk, lq // bq),
        in_specs=[pl.BlockSpec((1, 1, groups, bq, dk), lambda i, h, j: (i, h, 0, j, 0)),
                  pl.BlockSpec((1, 1, n_chunks, dk, bk), lambda i, h, j: (i, h, 0, 0, 0)),
                  pl.BlockSpec((1, 1, lk, LANES), lambda i, h, j: (i, h, 0, 0))],
        out_specs=pl.BlockSpec((1, 1, groups, bq, dv), lambda i, h, j: (i, h, 0, j, 0)),
        out_shape=jax.ShapeDtypeStruct((b, hk, groups, lq, dv), BF16),
        scratch_shapes=[pltpu.VMEM((rows, LANES), F32), pltpu.VMEM((rows, LANES), F32)],
        compiler_params=_params("parallel", "parallel", "arbitrary"),
        name="attention",
    )(q5, ktc, v1)
    return out.reshape(b, hq, lq, dv)


def _key_chunk(lt):
    return max(c for c in range(LANES, ATTN_KEYS + 1, LANES) if lt % c == 0)


def _attention(q, kt, v1, n_ctx):
    o_c = _attention_call(q[:, :, :n_ctx], kt[..., :n_ctx], v1[:, :, :n_ctx], n_ctx)
    o_l = _attention_call(q[:, :, n_ctx:], kt, v1, _key_chunk(kt.shape[-1]))
    return jnp.concatenate([o_c, o_l], axis=2)


def _ab_out_kernel(x_ref, ml_ref, mc_ref, y_ref, bonus_ref, gate_ref, o_ref, lnw_ref, lnb_ref, wrw_ref,
                   wmla_ref, out_ref, *, bl, n_ctx):
    rw = []
    for h in range(RW_HEADS):
        y = y_ref[0, 0, h] + y_ref[1, 0, h]
        mean = jnp.mean(y, axis=-1, keepdims=True)
        yc = y - mean
        var = jnp.mean(yc * yc, axis=-1, keepdims=True)
        yn = yc * lax.rsqrt(var + LN_X_EPS) * lnw_ref[h] + lnb_ref[h]
        rw.append((yn + bonus_ref[0, h]) * gate_ref[0, h])
    rw = jnp.concatenate(rw, axis=-1).astype(BF16)
    o = jnp.concatenate([o_ref[0, h] for h in range(MLA_HEADS)], axis=-1)
    acc = (jnp.dot(rw, wrw_ref[...], preferred_element_type=F32)
           + jnp.dot(o, wmla_ref[...], preferred_element_type=F32))
    gate = _row_mods(ml_ref, mc_ref, pl.program_id(1) * bl, bl, n_ctx, 2)
    out_ref[0] = x_ref[0] + gate * acc


def _ab_out(x, mod_l, mod_c, y, bonus, gate, o, ln_w, ln_b, w_out, n_ctx):
    b, lt, d = x.shape
    bl = TOKEN_BLOCK
    hd = RW_HEAD_DIM
    wrw = w_out[:RW_DIM].astype(BF16)
    wmla = w_out[RW_DIM:].astype(BF16)
    xs = pl.BlockSpec((1, bl, d), lambda i, j: (i, j, 0))
    hm = pl.BlockSpec((1, RW_HEADS, bl, hd), lambda i, j: (i, 0, j, 0))
    full = lambda a: pl.BlockSpec(a.shape, lambda i, j: (0,) * a.ndim)
    lnw, lnb = ln_w.reshape(RW_HEADS, 1, hd), ln_b.reshape(RW_HEADS, 1, hd)
    return pl.pallas_call(
        functools.partial(_ab_out_kernel, bl=bl, n_ctx=n_ctx),
        grid=(b, lt // bl),
        in_specs=[xs, pl.BlockSpec((1, 6, d), lambda i, j: (i, 0, 0)), pl.BlockSpec((1, 6, d), lambda i, j: (0, 0, 0)),
                  pl.BlockSpec((2, 1, RW_HEADS, bl, hd), lambda i, j: (0, i, 0, j, 0)), hm, hm,
                  pl.BlockSpec((1, MLA_HEADS, bl, MLA_V), lambda i, j: (i, 0, j, 0)),
                  full(lnw), full(lnb), full(wrw), full(wmla)],
        out_specs=xs,
        out_shape=jax.ShapeDtypeStruct((b, lt, d), F32),
        compiler_params=_params("parallel", "parallel"),
        name="ab_out",
    )(x, mod_l, mod_c, y, bonus, gate, o, lnw, lnb, wrw, wmla)


def _gqa_out_kernel(x_ref, ml_ref, mc_ref, o_ref, w_ref, out_ref, *, bl, n_ctx):
    o = jnp.concatenate([o_ref[0, h] for h in range(GQA_HEADS)], axis=-1)
    acc = jnp.dot(o, w_ref[...], preferred_element_type=F32)
    gate = _row_mods(ml_ref, mc_ref, pl.program_id(1) * bl, bl, n_ctx, 2)
    out_ref[0] = x_ref[0] + gate * acc


def _gqa_out(x, mod_l, mod_c, o, w_out, n_ctx):
    b, lt, d = x.shape
    bl = TOKEN_BLOCK
    hd = GQA_HEAD_DIM
    w = w_out.astype(BF16)
    xs = pl.BlockSpec((1, bl, d), lambda i, j: (i, j, 0))
    return pl.pallas_call(
        functools.partial(_gqa_out_kernel, bl=bl, n_ctx=n_ctx),
        grid=(b, lt // bl),
        in_specs=[xs, pl.BlockSpec((1, 6, d), lambda i, j: (i, 0, 0)), pl.BlockSpec((1, 6, d), lambda i, j: (0, 0, 0)),
                  pl.BlockSpec((1, GQA_HEADS, bl, hd), lambda i, j: (i, 0, j, 0)),
                  pl.BlockSpec(w.shape, lambda i, j: (0, 0))],
        out_specs=xs,
        out_shape=jax.ShapeDtypeStruct((b, lt, d), F32),
        compiler_params=_params("parallel", "parallel"),
        name="gqa_out",
    )(x, mod_l, mod_c, o, w)


def _first_argmax(vals):
    best, idx = vals[0], jnp.zeros(vals[0].shape, jnp.int32)
    for i in range(1, len(vals)):
        better = vals[i] > best
        idx = jnp.where(better, i, idx)
        best = jnp.where(better, vals[i], best)
    return best, idx


def _pick(vals, idx):
    out = vals[0]
    for i in range(1, len(vals)):
        out = jnp.where(idx == i, vals[i], out)
    return out


def _route_kernel(x_ref, g_ref, ml_ref, mc_ref, rw_ref, rb_ref, t_ref, comb_ref, *, bl, n_ctx):
    first = pl.program_id(1) * bl
    shift = _row_mods(ml_ref, mc_ref, first, bl, n_ctx, 3)
    scale = _row_mods(ml_ref, mc_ref, first, bl, n_ctx, 4)
    t = (_rms(x_ref[0]) * g_ref[...]) * (1.0 + scale) + shift
    t_ref[0] = t.astype(t_ref.dtype)
    logits = jnp.dot(t, rw_ref[...], preferred_element_type=F32, precision=HIGHEST)
    lt = logits.T
    score = [_sigmoid(lt[e:e + 1]) for e in range(N_EXPERTS)]
    biased = [score[e] + rb_ref[e:e + 1, :] for e in range(N_EXPERTS)]
    epg = EXPERTS_PER_GROUP
    group_score = []
    for g in range(N_GROUPS):
        vals = biased[g * epg:(g + 1) * epg]
        pair = [vals[i] + vals[j] for i in range(epg) for j in range(i + 1, epg)]
        group_score.append(functools.reduce(jnp.maximum, pair))
    _, grp = _first_argmax(group_score)
    in_b = [_pick([biased[g * epg + j] for g in range(N_GROUPS)], grp) for j in range(epg)]
    in_s = [_pick([score[g * epg + j] for g in range(N_GROUPS)], grp) for j in range(epg)]
    _, loc1 = _first_argmax(in_b)
    _, loc2 = _first_argmax([jnp.where(loc1 == j, -jnp.inf, in_b[j]) for j in range(epg)])
    w1, w2 = _pick(in_s, loc1), _pick(in_s, loc2)
    wsum = w1 + w2
    w1, w2 = w1 / wsum, w2 / wsum
    e1, e2 = grp * epg + loc1, grp * epg + loc2
    sub = lax.broadcasted_iota(jnp.int32, (LANES, bl), 0)
    comb = jnp.zeros((LANES, bl), F32)
    for e in range(N_EXPERTS):
        c_e = jnp.where(e1 == e, w1, 0.0) + jnp.where(e2 == e, w2, 0.0)
        comb = jnp.where(sub == e, jnp.broadcast_to(c_e, (LANES, bl)), comb)
    comb_ref[0] = comb.T


def _moe_route(x, g, mod_l, mod_c, router_w, router_b, n_ctx):
    b, lt, d = x.shape
    bl = TOKEN_BLOCK
    rw = jnp.pad(router_w, ((0, 0), (0, LANES - N_EXPERTS)))
    xs = pl.BlockSpec((1, bl, d), lambda i, j: (i, j, 0))
    return pl.pallas_call(
        functools.partial(_route_kernel, bl=bl, n_ctx=n_ctx),
        grid=(b, lt // bl),
        in_specs=[xs, pl.BlockSpec((1, d), lambda i, j: (0, 0)),
                  pl.BlockSpec((1, 6, d), lambda i, j: (i, 0, 0)), pl.BlockSpec((1, 6, d), lambda i, j: (0, 0, 0)),
                  pl.BlockSpec(rw.shape, lambda i, j: (0, 0)), pl.BlockSpec((N_EXPERTS, 1), lambda i, j: (0, 0))],
        out_specs=[xs, pl.BlockSpec((1, bl, LANES), lambda i, j: (i, j, 0))],
        out_shape=[jax.ShapeDtypeStruct((b, lt, d), BF16), jax.ShapeDtypeStruct((b, lt, LANES), F32)],
        compiler_params=_params("parallel", "parallel"),
        name="moe_route",
    )(x, g.reshape(1, d), mod_l, mod_c, rw, router_b.reshape(N_EXPERTS, 1))


def _ffn(t, wg, wu, wd, cw):
    g = jnp.dot(t, wg, preferred_element_type=F32)
    u = jnp.dot(t, wu, preferred_element_type=F32)
    h = _silu(g) * u
    if cw is not None:
        h = h * cw
    return jnp.dot(h.astype(BF16), wd, preferred_element_type=F32)


def _experts_kernel(x_ref, t_ref, comb_ref, ml_ref, mc_ref, wg_ref, wu_ref, wd_ref, sg_ref, su_ref, sd_ref,
                    out_ref, acc_ref, *, bl, n_ctx):
    e = pl.program_id(2)
    t = t_ref[0]

    @pl.when(e == 0)
    def _():
        acc_ref[...] = _ffn(t, sg_ref[...], su_ref[...], sd_ref[...], None)

    lane = lax.broadcasted_iota(jnp.int32, comb_ref.shape[1:], 1)
    comb = comb_ref[0]
    hs = []
    for i in range(MOE_EXPERTS_PER_STEP):
        cw = jnp.sum(jnp.where(lane == e * MOE_EXPERTS_PER_STEP + i, comb, 0.0), axis=-1, keepdims=True)
        g = jnp.dot(t, wg_ref[i], preferred_element_type=F32)
        u = jnp.dot(t, wu_ref[i], preferred_element_type=F32)
        hs.append((_silu(g) * u * cw).astype(BF16))
    h = jnp.concatenate(hs, axis=-1)
    wd = wd_ref[...]
    acc_ref[...] += jnp.dot(h, wd.reshape(wd.shape[0] * wd.shape[1], wd.shape[2]), preferred_element_type=F32)

    @pl.when(e == pl.num_programs(2) - 1)
    def _():
        gate = _row_mods(ml_ref, mc_ref, pl.program_id(1) * bl, bl, n_ctx, 5)
        out_ref[0] = x_ref[0] + gate * acc_ref[...]


def _moe_experts(x, t, comb, mod_l, mod_c, w_gate, w_up, w_down, sh_gate, sh_up, sh_down, n_ctx):
    b, lt, d = x.shape
    bl = _row_block(lt, MOE_ROWS)
    ff = w_gate.shape[-1]
    xs = pl.BlockSpec((1, bl, d), lambda i, j, e: (i, j, 0))
    full = lambda a: pl.BlockSpec(a.shape, lambda i, j, e: (0,) * a.ndim)
    ws = [w_gate.astype(BF16), w_up.astype(BF16), w_down.astype(BF16),
          sh_gate.astype(BF16), sh_up.astype(BF16), sh_down.astype(BF16)]
    return pl.pallas_call(
        functools.partial(_experts_kernel, bl=bl, n_ctx=n_ctx),
        grid=(b, lt // bl, N_EXPERTS // MOE_EXPERTS_PER_STEP),
        in_specs=[xs, xs, pl.BlockSpec((1, bl, LANES), lambda i, j, e: (i, j, 0)),
                  pl.BlockSpec((1, 6, d), lambda i, j, e: (i, 0, 0)), pl.BlockSpec((1, 6, d), lambda i, j, e: (0, 0, 0)),
                  pl.BlockSpec((MOE_EXPERTS_PER_STEP, d, ff), lambda i, j, e: (e, 0, 0)),
                  pl.BlockSpec((MOE_EXPERTS_PER_STEP, d, ff), lambda i, j, e: (e, 0, 0)),
                  pl.BlockSpec((MOE_EXPERTS_PER_STEP, ff, d), lambda i, j, e: (e, 0, 0)),
                  full(ws[3]), full(ws[4]), full(ws[5])],
        out_specs=xs,
        out_shape=jax.ShapeDtypeStruct((b, lt, d), F32),
        scratch_shapes=[pltpu.VMEM((bl, d), F32)],
        compiler_params=_params("parallel", "parallel", "arbitrary"),
        name="moe_experts",
    )(x, t, comb, mod_l, mod_c, *ws)


def _rope_tables(n_ctx, seq, rope_dims, lane_offset, repeat):
    half = rope_dims // 4
    t = jnp.arange(seq)
    rowp = (t // GRID_W).astype(F32)
    colp = (t % GRID_W).astype(F32)
    inv = ROPE_THETA ** (-jnp.arange(half, dtype=F32) / half)
    ar, ac = rowp[:, None] * inv[None, :], colp[:, None] * inv[None, :]
    cos = jnp.concatenate([jnp.cos(ar), jnp.cos(ar), jnp.cos(ac), jnp.cos(ac)], axis=1)
    sin = jnp.concatenate([-jnp.sin(ar), jnp.sin(ar), -jnp.sin(ac), jnp.sin(ac)], axis=1)
    width = LANES // repeat
    padl, padr = lane_offset, width - lane_offset - rope_dims
    cos = jnp.pad(cos, ((n_ctx, 0), (0, 0)), constant_values=1.0)
    sin = jnp.pad(sin, ((n_ctx, 0), (0, 0)))
    cos = jnp.pad(cos, ((0, 0), (padl, padr)), constant_values=1.0)
    sin = jnp.pad(sin, ((0, 0), (padl, padr)))
    return jnp.tile(cos, (1, repeat)), jnp.tile(sin, (1, repeat))


def kernel(x, c, ctx, c_ctx, ada_w, ada_b, norm1_g, norm2_g, ab_w_in, ab_w_out, rw_mu, rw_w0, rw_w2, rw_a0, rw_a2, rw_k_k, rw_k_a, rw_r_k, rw_g2, rw_ln_w, rw_ln_b, mla_g_qa, mla_w_q_up, mla_g_kva, mla_w_kv_up, mla_g_q, mla_g_k, gqa_w_in, gqa_w_out, gqa_g_q, gqa_g_k, router_w, router_b, moe_w_gate, moe_w_up, moe_w_down, shared_w_gate, shared_w_up, shared_w_down):
    b, seq, d = x.shape
    n_ctx = ctx.shape[1]
    depth = ada_w.shape[0]
    xs = jnp.concatenate([ctx, x], axis=1)

    cvec = jnp.concatenate([c, c_ctx[None, :]], axis=0)
    cvec = jnp.pad(cvec, ((0, 8 - (b + 1) % 8), (0, 0))) if (b + 1) % 8 else cvec
    mods = _ada_mods(cvec, ada_w, ada_b).reshape(depth, -1, 6, d)

    mla_cos, mla_sin = _rope_tables(n_ctx, seq, MLA_ROPE, MLA_NOPE, 1)
    gqa_cos, gqa_sin = _rope_tables(n_ctx, seq, GQA_HEAD_DIM, 0, 2)

    for l in range(depth):
        i = l // 2
        mod_l, mod_c = mods[l, :b], mods[l, b:b + 1]
        if l % 2 == 0:
            w_in = ab_w_in[i]
            w_rw = w_in[:, :RW_COLS].astype(BF16)
            w_m = w_in[:, RW_COLS:]
            w_mla = jnp.concatenate(
                [w_m[:, :MLA_Q_RANK + MLA_KV_RANK],
                 jnp.pad(w_m[:, MLA_Q_RANK + MLA_KV_RANK:], ((0, 0), (MLA_NOPE, LANES - MLA_NOPE - MLA_ROPE)))],
                axis=1).astype(BF16)
            p_rw, p_mla = _ln_mod_mm(xs, norm1_g[l], mod_l, mod_c, [w_rw, w_mla], n_ctx)
            r, v, na, lw, kd, bd, gate, bonus = _rwkv_prep(
                p_rw, n_ctx, rw_mu[i], rw_w0[i], rw_w2[i], rw_a0[i], rw_a2[i], rw_k_k[i], rw_k_a[i],
                rw_r_k[i].reshape(-1), rw_g2[i])
            y = _wkv_scan(r, v, na, lw, kd, bd, n_ctx)
            q, kt, vm = _mla_qkv(p_mla, mla_g_qa[i], mla_w_q_up[i], mla_g_kva[i], mla_w_kv_up[i],
                                 mla_g_q[i], mla_g_k[i], mla_cos, mla_sin)
            o = _attention(q, kt, vm, n_ctx)
            xs = _ab_out(xs, mod_l, mod_c, y, bonus, gate, o, rw_ln_w[i], rw_ln_b[i], ab_w_out[i], n_ctx)
        else:
            (p,) = _ln_mod_mm(xs, norm1_g[l], mod_l, mod_c, [gqa_w_in[i].astype(BF16)], n_ctx)
            q, kt, vg = _gqa_qkv(p, gqa_g_q[i], gqa_g_k[i], gqa_cos, gqa_sin)
            o = _attention(q, kt, vg, n_ctx)
            xs = _gqa_out(xs, mod_l, mod_c, o, gqa_w_out[i], n_ctx)
        t, comb = _moe_route(xs, norm2_g[l], mod_l, mod_c, router_w, router_b, n_ctx)
        xs = _moe_experts(xs, t, comb, mod_l, mod_c, moe_w_gate[l], moe_w_up[l], moe_w_down[l],
                          shared_w_gate[l], shared_w_up[l], shared_w_down[l], n_ctx)
    return xs[:, n_ctx:]
```

```python
import functools
import math

import jax
import jax.numpy as jnp
from jax import lax
from jax.experimental import pallas as pl
from jax.experimental.pallas import tpu as pltpu

F32 = jnp.float32
BF16 = jnp.bfloat16
HIGHEST = lax.Precision.HIGHEST

GRID_W = 64
RW_HEADS = 8
RW_HEAD_DIM = 64
RW_DIM = RW_HEADS * RW_HEAD_DIM
DECAY_LORA = 64
ICLR_LORA = 64
GATE_LORA = 128
RW_COLS = 3 * RW_DIM + 2 * DECAY_LORA + 2 * ICLR_LORA + GATE_LORA
LN_X_EPS = 64e-5
MLA_HEADS = 8
MLA_NOPE = 64
MLA_ROPE = 32
MLA_V = 64
MLA_QK = MLA_NOPE + MLA_ROPE
MLA_Q_RANK = 384
MLA_KV_RANK = 256
GQA_HEADS = 16
GQA_KV_HEADS = 4
GQA_HEAD_DIM = 64
N_EXPERTS = 16
N_GROUPS = 4
EXPERTS_PER_GROUP = N_EXPERTS // N_GROUPS
ROPE_THETA = 10000.0
NORM_EPS = 1e-6

LANES = 128
V7X_VMEM_BYTES = 64 * 1024 * 1024
VMEM_LIMIT = V7X_VMEM_BYTES - 8 * 1024 * 1024

TOKEN_BLOCK = 256
SCAN_CHUNK = 64
SCAN_INTERLEAVE = 16
SCAN_CHUNKS_PER_STEP = 2
ATTN_ROWS = 1024
ATTN_KEYS = 2816
MOE_ROWS = 1024
MOE_EXPERTS_PER_STEP = 4
LOG2E = math.log2(math.e)


def _row_block(lt, limit):
    return max(r for r in range(TOKEN_BLOCK, limit + 1, TOKEN_BLOCK) if lt % r == 0)


def _params(*sem):
    return pltpu.CompilerParams(dimension_semantics=sem, vmem_limit_bytes=VMEM_LIMIT)


def _bdot(a, b):
    return jnp.dot(a.astype(BF16), b.astype(BF16), preferred_element_type=F32)


def _batched(a, b, ca, cb):
    return lax.dot_general(a.astype(BF16), b.astype(BF16), (((ca,), (cb,)), ((0,), (0,))),
                           preferred_element_type=F32)


def _bmm(a, b):
    return _batched(a, b, 2, 1)


def _bmm_nt(a, b):
    return _batched(a, b, 2, 2)


def _bmm_tn(a, b):
    return _batched(a, b, 1, 1)


def _sigmoid(x):
    return 1.0 / (1.0 + jnp.exp(-x))


def _silu(x):
    return x * _sigmoid(x)


def _softplus(x):
    return jnp.maximum(x, 0.0) + jnp.log(1.0 + jnp.exp(-jnp.abs(x)))


def _rms(x):
    return x * lax.rsqrt(jnp.mean(x * x, axis=-1, keepdims=True) + NORM_EPS)


def _row_mods(ml_ref, mc_ref, first_row, rows, n_lat, idx):
    r = first_row + lax.broadcasted_iota(jnp.int32, (rows, 1), 0)
    return jnp.where(r >= n_lat, mc_ref[0, idx:idx + 1, :], ml_ref[0, idx:idx + 1, :])


def _ada_kernel(c_ref, w_ref, b_ref, o_ref):
    s = _silu(c_ref[...])
    o_ref[0] = jnp.dot(s, w_ref[0], preferred_element_type=F32, precision=HIGHEST) + b_ref[0]


def _ada_mods(cvec, ada_w, ada_b):
    depth, d, n = ada_w.shape
    rows = cvec.shape[0]
    bn = n // 4
    return pl.pallas_call(
        _ada_kernel,
        grid=(depth, n // bn),
        in_specs=[pl.BlockSpec((rows, d), lambda l, j: (0, 0)),
                  pl.BlockSpec((1, d, bn), lambda l, j: (l, 0, j)),
                  pl.BlockSpec((1, 1, bn), lambda l, j: (l, 0, j))],
        out_specs=pl.BlockSpec((1, rows, bn), lambda l, j: (l, 0, j)),
        out_shape=jax.ShapeDtypeStruct((depth, rows, n), F32),
        compiler_params=_params("parallel", "parallel"),
        name="ada_mods",
    )(cvec, ada_w, ada_b.reshape(depth, 1, n))


def _ln_mod_mm_kernel(x_ref, g_ref, ml_ref, mc_ref, *refs, n_w, bl, n_lat):
    w_refs, o_refs = refs[:n_w], refs[n_w:]
    first = pl.program_id(1) * bl
    shift = _row_mods(ml_ref, mc_ref, first, bl, n_lat, 0)
    scale = _row_mods(ml_ref, mc_ref, first, bl, n_lat, 1)
    h = (_rms(x_ref[0]) * g_ref[...]) * (1.0 + scale) + shift
    hb = h.astype(BF16)
    for w_ref, o_ref in zip(w_refs, o_refs):
        o_ref[0] = jnp.dot(hb, w_ref[...], preferred_element_type=F32).astype(o_ref.dtype)


def _ln_mod_mm(x, g, mod_l, mod_c, ws, n_lat):
    b, lt, d = x.shape
    bl = TOKEN_BLOCK
    kern = functools.partial(_ln_mod_mm_kernel, n_w=len(ws), bl=bl, n_lat=n_lat)
    return pl.pallas_call(
        kern,
        grid=(b, lt // bl),
        in_specs=[pl.BlockSpec((1, bl, d), lambda i, j: (i, j, 0)),
                  pl.BlockSpec((1, d), lambda i, j: (0, 0)),
                  pl.BlockSpec((1, 6, d), lambda i, j: (i, 0, 0)),
                  pl.BlockSpec((1, 6, d), lambda i, j: (0, 0, 0))]
        + [pl.BlockSpec(w.shape, lambda i, j: (0, 0)) for w in ws],
        out_specs=[pl.BlockSpec((1, bl, w.shape[1]), lambda i, j: (i, j, 0)) for w in ws],
        out_shape=[jax.ShapeDtypeStruct((b, lt, w.shape[1]), F32) for w in ws],
        compiler_params=_params("parallel", "parallel"),
        name="ln_mod_mm",
    )(x, g.reshape(1, d), mod_l, mod_c, *ws)


def _rwkv_prep_kernel(p_ref, prev_ref, next_ref, mu_ref, w0_ref, w2_ref, a0_ref, a2_ref, kk_ref,
                      ka_ref, rk_ref, g2_ref, r_ref, v_ref, na_ref, lw_ref, kd_ref, bd_ref,
                      gate_ref, bonus_ref, *, bl):
    p = p_ref[0]
    row = lax.broadcasted_iota(jnp.int32, p.shape, 0)
    prev = jnp.where(row == 0, prev_ref[0, 0], pltpu.roll(p, 1, 0))
    nxt = jnp.where(row == bl - 1, next_ref[0, 0], pltpu.roll(p, bl - 1, 0))
    pm = p + mu_ref[...] * (0.5 * (prev + nxt) - p)

    o = 3 * RW_DIM
    r = pm[:, 0:RW_DIM]
    k = pm[:, RW_DIM:2 * RW_DIM]
    v = pm[:, 2 * RW_DIM:o]
    w1 = (pm[:, o:o + DECAY_LORA], pm[:, o + DECAY_LORA:o + 2 * DECAY_LORA])
    o += 2 * DECAY_LORA
    a1 = (pm[:, o:o + ICLR_LORA], pm[:, o + ICLR_LORA:o + 2 * ICLR_LORA])
    o += 2 * ICLR_LORA
    g1 = pm[:, o:o + GATE_LORA]

    kk = k * kk_ref[...]
    lw, kd, ad = [], [], []
    for d in range(2):
        z = w0_ref[d:d + 1, :] + _bdot(jnp.tanh(w1[d]), w2_ref[d])
        lw.append(-jnp.exp(-_softplus(-z) - 0.5))
        a = _sigmoid(a0_ref[d:d + 1, :] + _bdot(a1[d], a2_ref[d]))
        ad.append(a)
        kd.append(k * (1.0 + (a - 1.0) * ka_ref[...]))
    gate = _bdot(_sigmoid(g1), g2_ref[...])
    rkk = r * (kd[0] + kd[1]) * rk_ref[...]

    for h in range(RW_HEADS):
        sl = slice(h * RW_HEAD_DIM, (h + 1) * RW_HEAD_DIM)
        kk_h = kk[:, sl]
        kk_h = kk_h * lax.rsqrt(jnp.sum(kk_h * kk_h, axis=-1, keepdims=True) + 1e-12)
        v_h = v[:, sl]
        r_ref[0, h] = r[:, sl]
        v_ref[0, h] = v_h
        na_ref[0, h] = -kk_h
        for d in range(2):
            lw_ref[d, 0, h] = lw[d][:, sl]
            kd_ref[d, 0, h] = kd[d][:, sl]
            bd_ref[d, 0, h] = kk_h * ad[d][:, sl]
        gate_ref[0, h] = gate[:, sl]
        bonus_ref[0, h] = jnp.sum(rkk[:, sl], axis=-1, keepdims=True) * v_h


def _rwkv_prep(p, n_lat, mu, w0, w2, a0, a2, k_k, k_a, r_k, g2):
    b, lt, cols = p.shape
    bl = TOKEN_BLOCK
    nblk = lt // bl
    zero = jnp.zeros((b, 1, cols), F32)
    lastrows = p[:, bl - 1::bl]
    firstrows = p[:, ::bl]
    prev_edge = jnp.concatenate([zero, lastrows[:, :-1]], axis=1)
    next_edge = jnp.concatenate([firstrows[:, 1:], zero], axis=1)
    cb = n_lat // bl
    blk = jnp.arange(nblk)[None, :, None]
    prev_edge = jnp.where(blk == cb, 0.0, prev_edge).reshape(b, nblk, 1, cols)
    next_edge = jnp.where(blk == cb - 1, 0.0, next_edge).reshape(b, nblk, 1, cols)

    hm = jax.ShapeDtypeStruct((b, RW_HEADS, lt, RW_HEAD_DIM), F32)
    hm2 = jax.ShapeDtypeStruct((2, b, RW_HEADS, lt, RW_HEAD_DIM), F32)
    hm_spec = pl.BlockSpec((1, RW_HEADS, bl, RW_HEAD_DIM), lambda i, j: (i, 0, j, 0))
    hm2_spec = pl.BlockSpec((2, 1, RW_HEADS, bl, RW_HEAD_DIM), lambda i, j: (0, i, 0, j, 0))
    full = lambda a: pl.BlockSpec(a.shape, lambda i, j: (0,) * a.ndim)
    consts = [mu.reshape(1, cols), w0, w2.astype(BF16), a0, a2.astype(BF16), k_k.reshape(1, RW_DIM),
              k_a.reshape(1, RW_DIM), r_k.reshape(1, RW_DIM), g2.astype(BF16)]
    return pl.pallas_call(
        functools.partial(_rwkv_prep_kernel, bl=bl),
        grid=(b, nblk),
        in_specs=[pl.BlockSpec((1, bl, cols), lambda i, j: (i, j, 0)),
                  pl.BlockSpec((1, 1, 1, cols), lambda i, j: (i, j, 0, 0)),
                  pl.BlockSpec((1, 1, 1, cols), lambda i, j: (i, j, 0, 0))]
        + [full(a) for a in consts],
        out_specs=[hm_spec, hm_spec, hm_spec, hm2_spec, hm2_spec, hm2_spec, hm_spec, hm_spec],
        out_shape=[hm, hm, hm, hm2, hm2, hm2, hm, hm],
        compiler_params=_params("parallel", "parallel"),
        name="rwkv_prep",
    )(p, prev_edge, next_edge, *consts)


def _wkv_kernel(r_ref, v_ref, a_ref, lw_ref, k_ref, b_ref, y_ref, st_ref, *, cn, chains):
    d = pl.program_id(0)

    @pl.when(pl.program_id(1) == 0)
    def _():
        st_ref[...] = jnp.zeros_like(st_ref)

    row = lax.broadcasted_iota(jnp.int32, (cn, cn), 0)
    col = lax.broadcasted_iota(jnp.int32, (cn, cn), 1)
    lead = (row - col) * jnp.where(d == 0, 1, -1)
    before = lead > 0
    upto = lead >= 0
    eye = (col == row).astype(F32)
    cum_mat = upto.astype(F32)
    hd = r_ref.shape[-1]
    eye_h = (lax.broadcasted_iota(jnp.int32, (hd, hd), 0)
             == lax.broadcasted_iota(jnp.int32, (hd, hd), 1)).astype(F32)
    n_double = int(math.log2(cn)) - 1

    nb = SCAN_INTERLEAVE
    cum_b = jnp.broadcast_to(cum_mat, (nb, cn, cn))

    def chunk(sl, rows, st):
        r, v, a = r_ref[sl, rows], v_ref[sl, rows], a_ref[sl, rows]
        lw, k, b = lw_ref[0, sl, rows], k_ref[0, sl, rows], b_ref[0, sl, rows]
        lw_hi = lw.astype(BF16)
        lw_lo = (lw - lw_hi.astype(F32)).astype(BF16)
        cum2 = _bmm(cum_b, jnp.concatenate([lw_hi, lw_lo], axis=-1))
        cum = cum2[..., :hd] + cum2[..., hd:]
        tot = jnp.sum(lw, axis=1, keepdims=True)
        at = a * jnp.exp(cum - lw)
        rt = r * jnp.exp(cum)
        einv = jnp.exp(-cum)
        bt, kt = b * einv, k * einv
        eend = jnp.exp(tot - cum)
        bh, kh = b * eend, k * eend

        a_ab = jnp.where(before, _bmm_nt(at, bt), 0.0)
        a_ak = jnp.where(before, _bmm_nt(at, kt), 0.0)
        a_rb = jnp.where(upto, _bmm_nt(rt, bt), 0.0)
        a_rk = jnp.where(upto, _bmm_nt(rt, kt), 0.0)

        t = eye + a_ab
        x = _bmm(a_ab, a_ab)
        for _ in range(n_double - 1):
            tx = _bmm(jnp.concatenate([t, x], axis=1), x)
            t = t + tx[:, :cn]
            x = tx[:, cn:]
        t = t + _bmm(t, x)
        w = _bmm(t, at)
        up = _bmm(t, _bmm(a_ak, v))
        qe = rt + _bmm(a_rb, w)
        yp = _bmm(a_rb, up) + _bmm(a_rk, v)
        mc = eye_h * jnp.exp(tot) + _bmm_tn(bh, w)
        nc = _bmm_tn(bh, up) + _bmm_tn(kh, v)

        y_ref[0, sl, rows] = _bmm(qe, st) + yp
        return _bmm(mc, st) + nc

    n_sub = r_ref.shape[1] // cn

    def group(g, carry):
        sl = pl.ds(g * nb, nb)
        st = st_ref[sl]
        for s in range(n_sub):
            sub = jnp.where(d == 0, s, n_sub - 1 - s)
            st = chunk(sl, pl.ds(pl.multiple_of(sub * cn, cn), cn), st)
        st_ref[sl] = st
        return carry

    lax.fori_loop(0, chains // SCAN_INTERLEAVE, group, 0)


def _wkv_scan(r, v, na, lw, kd, bd, n_lat):
    b, h, lt, hd = r.shape
    cn = SCAN_CHUNK
    chains = b * h
    rows = cn * SCAN_CHUNKS_PER_STEP
    assert n_lat % rows == 0 and lt % rows == 0
    n_all, n_l = lt // rows, n_lat // rows
    n_c = n_all - n_l

    def chunk(d, i):
        fwd = jnp.where(i < n_c, n_l + i, i - n_c)
        rev = jnp.where(i < n_c, n_all - 1 - i, n_l - 1 - (i - n_c))
        return jnp.where(d == 0, fwd, rev)

    shared = pl.BlockSpec((chains, rows, hd), lambda d, i: (0, chunk(d, i), 0))
    per_dir = pl.BlockSpec((1, chains, rows, hd), lambda d, i: (d, 0, chunk(d, i), 0))
    flat = lambda t: t.reshape(t.shape[:-4] + (chains, lt, hd))
    y = pl.pallas_call(
        functools.partial(_wkv_kernel, cn=cn, chains=chains),
        grid=(2, n_all),
        in_specs=[shared, shared, shared, per_dir, per_dir, per_dir],
        out_specs=per_dir,
        out_shape=jax.ShapeDtypeStruct((2, chains, lt, hd), F32),
        scratch_shapes=[pltpu.VMEM((chains, hd, hd), F32)],
        compiler_params=_params("parallel", "arbitrary"),
        name="wkv_scan",
    )(flat(r), flat(v), flat(na), flat(lw), flat(kd), flat(bd))
    return y.reshape(2, b, h, lt, hd)


def _rope(x, cos, sin, half):
    lane = lax.broadcasted_iota(jnp.int32, x.shape, 1)
    width = x.shape[1]
    partner = jnp.where(lane % (2 * half) < half, pltpu.roll(x, width - half, 1), pltpu.roll(x, half, 1))
    return x * cos + partner * sin


def _mla_qkv_kernel(p_ref, gqa_ref, gkva_ref, wq_ref, wk_ref, wv_ref, gq_ref, gk_ref, cos_ref, sin_ref,
                    q_ref, kt_ref, v_ref, *, scale):
    p = p_ref[0]
    q_a = p[:, 0:MLA_Q_RANK]
    kv_a = p[:, MLA_Q_RANK:MLA_Q_RANK + MLA_KV_RANK]
    k_rope = p[:, MLA_Q_RANK + MLA_KV_RANK:MLA_Q_RANK + MLA_KV_RANK + LANES]
    qa_n = (_rms(q_a) * gqa_ref[...]).astype(BF16)
    kva_n = (_rms(kv_a) * gkva_ref[...]).astype(BF16)
    q_all = jnp.dot(qa_n, wq_ref[...], preferred_element_type=F32)
    k_all = jnp.dot(kva_n, wk_ref[...], preferred_element_type=F32)
    v_all = jnp.dot(kva_n, wv_ref[...], preferred_element_type=F32)
    cos, sin = cos_ref[...], sin_ref[...]
    for h in range(MLA_HEADS):
        sl = slice(h * LANES, (h + 1) * LANES)
        q = q_all[:, sl]
        q = q * lax.rsqrt(jnp.sum(q * q, axis=-1, keepdims=True) / MLA_QK + NORM_EPS) * gq_ref[...]
        q = _rope(q, cos, sin, MLA_ROPE // 4)
        q_ref[0, h] = (q * scale).astype(q_ref.dtype)
        k = k_all[:, sl] + k_rope
        k = k * lax.rsqrt(jnp.sum(k * k, axis=-1, keepdims=True) / MLA_QK + NORM_EPS) * gk_ref[...]
        k = _rope(k, cos, sin, MLA_ROPE // 4)
        kt_ref[0, h] = k.T.astype(kt_ref.dtype)
        lane = lax.broadcasted_iota(jnp.int32, q.shape, 1)
        v_ref[0, h] = jnp.where(lane < MLA_V, v_all[:, sl], 1.0).astype(v_ref.dtype)


def _pad_heads(w, heads, width):
    k = w.shape[0]
    w = w.reshape(k, heads, width)
    return jnp.pad(w, ((0, 0), (0, 0), (0, LANES - width))).reshape(k, heads * LANES)


def _mla_qkv(p, g_qa, w_q_up, g_kva, w_kv_up, g_q, g_k, cos, sin):
    b, lt, cols = p.shape
    bl = TOKEN_BLOCK
    wq = _pad_heads(w_q_up, MLA_HEADS, MLA_QK).astype(BF16)
    wkv = w_kv_up.reshape(MLA_KV_RANK, MLA_HEADS, MLA_NOPE + MLA_V)
    wk = _pad_heads(wkv[:, :, :MLA_NOPE].reshape(MLA_KV_RANK, -1), MLA_HEADS, MLA_NOPE).astype(BF16)
    wv = _pad_heads(wkv[:, :, MLA_NOPE:].reshape(MLA_KV_RANK, -1), MLA_HEADS, MLA_V).astype(BF16)
    pad = lambda g: jnp.pad(g, (0, LANES - MLA_QK)).reshape(1, LANES)
    consts = [g_qa.reshape(1, -1), g_kva.reshape(1, -1), wq, wk, wv, pad(g_q), pad(g_k)]
    full = lambda a: pl.BlockSpec(a.shape, lambda i, j: (0,) * a.ndim)
    tab = pl.BlockSpec((bl, LANES), lambda i, j: (j, 0))
    return pl.pallas_call(
        functools.partial(_mla_qkv_kernel, scale=MLA_QK ** -0.5 * LOG2E),
        grid=(b, lt // bl),
        in_specs=[pl.BlockSpec((1, bl, cols), lambda i, j: (i, j, 0))] + [full(a) for a in consts] + [tab, tab],
        out_specs=[pl.BlockSpec((1, MLA_HEADS, bl, LANES), lambda i, j: (i, 0, j, 0)),
                   pl.BlockSpec((1, MLA_HEADS, LANES, bl), lambda i, j: (i, 0, 0, j)),
                   pl.BlockSpec((1, MLA_HEADS, bl, LANES), lambda i, j: (i, 0, j, 0))],
        out_shape=[jax.ShapeDtypeStruct((b, MLA_HEADS, lt, LANES), BF16),
                   jax.ShapeDtypeStruct((b, MLA_HEADS, LANES, lt), BF16),
                   jax.ShapeDtypeStruct((b, MLA_HEADS, lt, LANES), BF16)],
        compiler_params=_params("parallel", "parallel"),
        name="mla_qkv",
    )(p, *consts, cos, sin)


def _pair_norm_rope(x, g, cos, sin):
    lane = lax.broadcasted_iota(jnp.int32, x.shape, 1)
    lo = lane < GQA_HEAD_DIM
    xx = x * x
    ss_lo = jnp.sum(jnp.where(lo, xx, 0.0), axis=-1, keepdims=True)
    ss_hi = jnp.sum(jnp.where(lo, 0.0, xx), axis=-1, keepdims=True)
    rs = lax.rsqrt(jnp.where(lo, ss_lo, ss_hi) / GQA_HEAD_DIM + NORM_EPS)
    return _rope(x * rs * g, cos, sin, GQA_HEAD_DIM // 4)


def _gqa_qkv_kernel(p_ref, gq_ref, gk_ref, cos_ref, sin_ref, q_ref, kt_ref, v_ref, *, scale):
    p = p_ref[0]
    cos, sin = cos_ref[...], sin_ref[...]
    qw = GQA_HEADS * GQA_HEAD_DIM
    kw = GQA_KV_HEADS * GQA_HEAD_DIM
    for j in range(GQA_HEADS // 2):
        q = _pair_norm_rope(p[:, j * LANES:(j + 1) * LANES], gq_ref[...], cos, sin) * scale
        q_ref[0, 2 * j] = q[:, :GQA_HEAD_DIM].astype(q_ref.dtype)
        q_ref[0, 2 * j + 1] = q[:, GQA_HEAD_DIM:].astype(q_ref.dtype)
    for j in range(GQA_KV_HEADS // 2):
        k = _pair_norm_rope(p[:, qw + j * LANES:qw + (j + 1) * LANES], gk_ref[...], cos, sin)
        kt = k.T
        kt_ref[0, 2 * j] = kt[:GQA_HEAD_DIM].astype(kt_ref.dtype)
        kt_ref[0, 2 * j + 1] = kt[GQA_HEAD_DIM:].astype(kt_ref.dtype)
        v = p[:, qw + kw + j * LANES:qw + kw + (j + 1) * LANES]
        lo = lax.broadcasted_iota(jnp.int32, v.shape, 1) < GQA_HEAD_DIM
        v_ref[0, 2 * j] = jnp.where(lo, v, 1.0).astype(v_ref.dtype)
        v_ref[0, 2 * j + 1] = jnp.where(lo, pltpu.roll(v, GQA_HEAD_DIM, 1), 1.0).astype(v_ref.dtype)


def _gqa_qkv(p, g_q, g_k, cos, sin):
    b, lt, cols = p.shape
    bl = TOKEN_BLOCK
    hd = GQA_HEAD_DIM
    two = lambda g: jnp.concatenate([g, g]).reshape(1, LANES)
    tab = pl.BlockSpec((bl, LANES), lambda i, j: (j, 0))
    vec = pl.BlockSpec((1, LANES), lambda i, j: (0, 0))
    return pl.pallas_call(
        functools.partial(_gqa_qkv_kernel, scale=hd ** -0.5 * LOG2E),
        grid=(b, lt // bl),
        in_specs=[pl.BlockSpec((1, bl, cols), lambda i, j: (i, j, 0)), vec, vec, tab, tab],
        out_specs=[pl.BlockSpec((1, GQA_HEADS, bl, hd), lambda i, j: (i, 0, j, 0)),
                   pl.BlockSpec((1, GQA_KV_HEADS, hd, bl), lambda i, j: (i, 0, 0, j)),
                   pl.BlockSpec((1, GQA_KV_HEADS, bl, LANES), lambda i, j: (i, 0, j, 0))],
        out_shape=[jax.ShapeDtypeStruct((b, GQA_HEADS, lt, hd), BF16),
                   jax.ShapeDtypeStruct((b, GQA_KV_HEADS, hd, lt), BF16),
                   jax.ShapeDtypeStruct((b, GQA_KV_HEADS, lt, LANES), BF16)],
        compiler_params=_params("parallel", "parallel"),
        name="gqa_qkv",
    )(p, two(g_q), two(g_k), cos, sin)


def _attn_kernel(q_ref, kt_ref, v_ref, o_ref, m_scr, acc_scr, *, groups, bq, n_chunks):
    rows = groups * bq
    q = q_ref[0, 0].reshape(rows, q_ref.shape[-1])
    bk = kt_ref.shape[-1]
    dv = o_ref.shape[-1]
    m_scr[...] = jnp.full_like(m_scr, -jnp.inf)
    acc_scr[...] = jnp.zeros_like(acc_scr)

    def step(j, carry):
        s = jnp.dot(q, kt_ref[0, 0, j], preferred_element_type=F32)
        m_prev = m_scr[...]
        m_new = jnp.maximum(m_prev, jnp.max(s, axis=-1, keepdims=True))
        alpha = jnp.exp2(m_prev - m_new)
        p = jnp.exp2(s - pltpu.repeat(m_new, bk // LANES, axis=1))
        vv = v_ref[0, 0, j * bk:(j + 1) * bk, :]
        acc_scr[...] = alpha * acc_scr[...] + jnp.dot(p.astype(BF16), vv, preferred_element_type=F32)
        m_scr[...] = m_new
        return carry

    for j in range(n_chunks):
        step(j, 0)
    acc = acc_scr[...]
    o = acc / pltpu.roll(acc, LANES - dv, 1)
    o_ref[0, 0] = o[:, :dv].reshape(groups, bq, dv).astype(o_ref.dtype)


def _attention_call(q, q_start, lq, kt, v1, bk):
    b, hq, lt, dk = q.shape
    hk, lk = kt.shape[1], kt.shape[-1]
    dv = LANES // 2
    groups = hq // hk
    bq = min(ATTN_ROWS // groups, lq)
    rows = groups * bq
    n_chunks = lk // bk
    first = q_start // bq
    q5 = q.reshape(b, hk, groups, lt, dk)
    ktc = kt.reshape(b, hk, dk, n_chunks, bk).transpose(0, 1, 3, 2, 4)
    out = pl.pallas_call(
        functools.partial(_attn_kernel, groups=groups, bq=bq, n_chunks=n_chunks),
        grid=(b, hk, lq // bq),
        in_specs=[pl.BlockSpec((1, 1, groups, bq, dk), lambda i, h, j: (i, h, 0, j + first, 0)),
                  pl.BlockSpec((1, 1, n_chunks, dk, bk), lambda i, h, j: (i, h, 0, 0, 0)),
                  pl.BlockSpec((1, 1, lk, LANES), lambda i, h, j: (i, h, 0, 0))],
        out_specs=pl.BlockSpec((1, 1, groups, bq, dv), lambda i, h, j: (i, h, 0, j, 0)),
        out_shape=jax.ShapeDtypeStruct((b, hk, groups, lq, dv), BF16),
        scratch_shapes=[pltpu.VMEM((rows, LANES), F32), pltpu.VMEM((rows, LANES), F32)],
        compiler_params=_params("parallel", "parallel", "arbitrary"),
        name="attention",
    )(q5, ktc, v1)
    return out.reshape(b, hq, lq, dv)


def _key_chunk(lt):
    return max(c for c in range(LANES, ATTN_KEYS + 1, LANES) if lt % c == 0)


def _attention(q, kt, v1, n_lat):
    lt = kt.shape[-1]
    o_l = _attention_call(q, 0, n_lat, kt, v1, _key_chunk(lt))
    o_c = _attention_call(q, n_lat, lt - n_lat, kt[..., n_lat:], v1[:, :, n_lat:], lt - n_lat)
    return jnp.concatenate([o_l, o_c], axis=2)


def _ab_out_kernel(x_ref, ml_ref, mc_ref, y_ref, bonus_ref, gate_ref, o_ref, lnw_ref, lnb_ref, wrw_ref,
                   wmla_ref, out_ref, *, bl, n_lat):
    rw = []
    for h in range(RW_HEADS):
        y = y_ref[0, 0, h] + y_ref[1, 0, h]
        mean = jnp.mean(y, axis=-1, keepdims=True)
        yc = y - mean
        var = jnp.mean(yc * yc, axis=-1, keepdims=True)
        yn = yc * lax.rsqrt(var + LN_X_EPS) * lnw_ref[h] + lnb_ref[h]
        rw.append((yn + bonus_ref[0, h]) * gate_ref[0, h])
    rw = jnp.concatenate(rw, axis=-1).astype(BF16)
    o = jnp.concatenate([o_ref[0, h] for h in range(MLA_HEADS)], axis=-1)
    acc = (jnp.dot(rw, wrw_ref[...], preferred_element_type=F32)
           + jnp.dot(o, wmla_ref[...], preferred_element_type=F32))
    gate = _row_mods(ml_ref, mc_ref, pl.program_id(1) * bl, bl, n_lat, 2)
    out_ref[0] = x_ref[0] + gate * acc


def _ab_out(x, mod_l, mod_c, y, bonus, gate, o, ln_w, ln_b, w_out, n_lat, lt):
    b, _, d = x.shape
    bl = TOKEN_BLOCK
    hd = RW_HEAD_DIM
    wrw = w_out[:RW_DIM].astype(BF16)
    wmla = w_out[RW_DIM:].astype(BF16)
    xs = pl.BlockSpec((1, bl, d), lambda i, j: (i, j, 0))
    hm = pl.BlockSpec((1, RW_HEADS, bl, hd), lambda i, j: (i, 0, j, 0))
    full = lambda a: pl.BlockSpec(a.shape, lambda i, j: (0,) * a.ndim)
    lnw, lnb = ln_w.reshape(RW_HEADS, 1, hd), ln_b.reshape(RW_HEADS, 1, hd)
    return pl.pallas_call(
        functools.partial(_ab_out_kernel, bl=bl, n_lat=n_lat),
        grid=(b, lt // bl),
        in_specs=[xs, pl.BlockSpec((1, 6, d), lambda i, j: (i, 0, 0)), pl.BlockSpec((1, 6, d), lambda i, j: (0, 0, 0)),
                  pl.BlockSpec((2, 1, RW_HEADS, bl, hd), lambda i, j: (0, i, 0, j, 0)), hm, hm,
                  pl.BlockSpec((1, MLA_HEADS, bl, MLA_V), lambda i, j: (i, 0, j, 0)),
                  full(lnw), full(lnb), full(wrw), full(wmla)],
        out_specs=xs,
        out_shape=jax.ShapeDtypeStruct((b, lt, d), F32),
        compiler_params=_params("parallel", "parallel"),
        name="ab_out",
    )(x, mod_l, mod_c, y, bonus, gate, o, lnw, lnb, wrw, wmla)


def _gqa_out_kernel(x_ref, ml_ref, mc_ref, o_ref, w_ref, out_ref, *, bl, n_lat):
    o = jnp.concatenate([o_ref[0, h] for h in range(GQA_HEADS)], axis=-1)
    acc = jnp.dot(o, w_ref[...], preferred_element_type=F32)
    gate = _row_mods(ml_ref, mc_ref, pl.program_id(1) * bl, bl, n_lat, 2)
    out_ref[0] = x_ref[0] + gate * acc


def _gqa_out(x, mod_l, mod_c, o, w_out, n_lat, lt):
    b, _, d = x.shape
    bl = TOKEN_BLOCK
    hd = GQA_HEAD_DIM
    w = w_out.astype(BF16)
    xs = pl.BlockSpec((1, bl, d), lambda i, j: (i, j, 0))
    return pl.pallas_call(
        functools.partial(_gqa_out_kernel, bl=bl, n_lat=n_lat),
        grid=(b, lt // bl),
        in_specs=[xs, pl.BlockSpec((1, 6, d), lambda i, j: (i, 0, 0)), pl.BlockSpec((1, 6, d), lambda i, j: (0, 0, 0)),
                  pl.BlockSpec((1, GQA_HEADS, bl, hd), lambda i, j: (i, 0, j, 0)),
                  pl.BlockSpec(w.shape, lambda i, j: (0, 0))],
        out_specs=xs,
        out_shape=jax.ShapeDtypeStruct((b, lt, d), F32),
        compiler_params=_params("parallel", "parallel"),
        name="gqa_out",
    )(x, mod_l, mod_c, o, w)


def _first_argmax(vals):
    best, idx = vals[0], jnp.zeros(vals[0].shape, jnp.int32)
    for i in range(1, len(vals)):
        better = vals[i] > best
        idx = jnp.where(better, i, idx)
        best = jnp.where(better, vals[i], best)
    return best, idx


def _pick(vals, idx):
    out = vals[0]
    for i in range(1, len(vals)):
        out = jnp.where(idx == i, vals[i], out)
    return out


def _route_kernel(x_ref, g_ref, ml_ref, mc_ref, rw_ref, rb_ref, t_ref, comb_ref, *, bl, n_lat):
    first = pl.program_id(1) * bl
    shift = _row_mods(ml_ref, mc_ref, first, bl, n_lat, 3)
    scale = _row_mods(ml_ref, mc_ref, first, bl, n_lat, 4)
    t = (_rms(x_ref[0]) * g_ref[...]) * (1.0 + scale) + shift
    t_ref[0] = t.astype(t_ref.dtype)
    logits = jnp.dot(t, rw_ref[...], preferred_element_type=F32, precision=HIGHEST)
    lt = logits.T
    score = [_sigmoid(lt[e:e + 1]) for e in range(N_EXPERTS)]
    biased = [score[e] + rb_ref[e:e + 1, :] for e in range(N_EXPERTS)]
    epg = EXPERTS_PER_GROUP
    group_score = []
    for g in range(N_GROUPS):
        vals = biased[g * epg:(g + 1) * epg]
        pair = [vals[i] + vals[j] for i in range(epg) for j in range(i + 1, epg)]
        group_score.append(functools.reduce(jnp.maximum, pair))
    _, grp = _first_argmax(group_score)
    in_b = [_pick([biased[g * epg + j] for g in range(N_GROUPS)], grp) for j in range(epg)]
    in_s = [_pick([score[g * epg + j] for g in range(N_GROUPS)], grp) for j in range(epg)]
    _, loc1 = _first_argmax(in_b)
    _, loc2 = _first_argmax([jnp.where(loc1 == j, -jnp.inf, in_b[j]) for j in range(epg)])
    w1, w2 = _pick(in_s, loc1), _pick(in_s, loc2)
    wsum = w1 + w2
    w1, w2 = w1 / wsum, w2 / wsum
    e1, e2 = grp * epg + loc1, grp * epg + loc2
    sub = lax.broadcasted_iota(jnp.int32, (LANES, bl), 0)
    comb = jnp.zeros((LANES, bl), F32)
    for e in range(N_EXPERTS):
        c_e = jnp.where(e1 == e, w1, 0.0) + jnp.where(e2 == e, w2, 0.0)
        comb = jnp.where(sub == e, jnp.broadcast_to(c_e, (LANES, bl)), comb)
    comb_ref[0] = comb.T


def _moe_route(x, g, mod_l, mod_c, router_w, router_b, n_lat):
    b, lt, d = x.shape
    bl = TOKEN_BLOCK
    rw = jnp.pad(router_w, ((0, 0), (0, LANES - N_EXPERTS)))
    xs = pl.BlockSpec((1, bl, d), lambda i, j: (i, j, 0))
    return pl.pallas_call(
        functools.partial(_route_kernel, bl=bl, n_lat=n_lat),
        grid=(b, lt // bl),
        in_specs=[xs, pl.BlockSpec((1, d), lambda i, j: (0, 0)),
                  pl.BlockSpec((1, 6, d), lambda i, j: (i, 0, 0)), pl.BlockSpec((1, 6, d), lambda i, j: (0, 0, 0)),
                  pl.BlockSpec(rw.shape, lambda i, j: (0, 0)), pl.BlockSpec((N_EXPERTS, 1), lambda i, j: (0, 0))],
        out_specs=[xs, pl.BlockSpec((1, bl, LANES), lambda i, j: (i, j, 0))],
        out_shape=[jax.ShapeDtypeStruct((b, lt, d), BF16), jax.ShapeDtypeStruct((b, lt, LANES), F32)],
        compiler_params=_params("parallel", "parallel"),
        name="moe_route",
    )(x, g.reshape(1, d), mod_l, mod_c, rw, router_b.reshape(N_EXPERTS, 1))


def _ffn(t, wg, wu, wd, cw):
    g = jnp.dot(t, wg, preferred_element_type=F32)
    u = jnp.dot(t, wu, preferred_element_type=F32)
    h = _silu(g) * u
    if cw is not None:
        h = h * cw
    return jnp.dot(h.astype(BF16), wd, preferred_element_type=F32)


def _experts_kernel(x_ref, t_ref, comb_ref, ml_ref, mc_ref, wg_ref, wu_ref, wd_ref, sg_ref, su_ref, sd_ref,
                    out_ref, acc_ref, *, bl, n_lat):
    e = pl.program_id(2)
    t = t_ref[0]

    @pl.when(e == 0)
    def _():
        acc_ref[...] = _ffn(t, sg_ref[...], su_ref[...], sd_ref[...], None)

    lane = lax.broadcasted_iota(jnp.int32, comb_ref.shape[1:], 1)
    comb = comb_ref[0]
    hs = []
    for i in range(MOE_EXPERTS_PER_STEP):
        cw = jnp.sum(jnp.where(lane == e * MOE_EXPERTS_PER_STEP + i, comb, 0.0), axis=-1, keepdims=True)
        g = jnp.dot(t, wg_ref[i], preferred_element_type=F32)
        u = jnp.dot(t, wu_ref[i], preferred_element_type=F32)
        hs.append((_silu(g) * u * cw).astype(BF16))
    h = jnp.concatenate(hs, axis=-1)
    wd = wd_ref[...]
    acc_ref[...] += jnp.dot(h, wd.reshape(wd.shape[0] * wd.shape[1], wd.shape[2]), preferred_element_type=F32)

    @pl.when(e == pl.num_programs(2) - 1)
    def _():
        gate = _row_mods(ml_ref, mc_ref, pl.program_id(1) * bl, bl, n_lat, 5)
        out_ref[0] = x_ref[0] + gate * acc_ref[...]


def _moe_experts(x, t, comb, mod_l, mod_c, w_gate, w_up, w_down, sh_gate, sh_up, sh_down, n_lat):
    b, lt, d = x.shape
    bl = _row_block(lt, MOE_ROWS)
    ff = w_gate.shape[-1]
    xs = pl.BlockSpec((1, bl, d), lambda i, j, e: (i, j, 0))
    full = lambda a: pl.BlockSpec(a.shape, lambda i, j, e: (0,) * a.ndim)
    ws = [w_gate.astype(BF16), w_up.astype(BF16), w_down.astype(BF16),
          sh_gate.astype(BF16), sh_up.astype(BF16), sh_down.astype(BF16)]
    return pl.pallas_call(
        functools.partial(_experts_kernel, bl=bl, n_lat=n_lat),
        grid=(b, lt // bl, N_EXPERTS // MOE_EXPERTS_PER_STEP),
        in_specs=[xs, xs, pl.BlockSpec((1, bl, LANES), lambda i, j, e: (i, j, 0)),
                  pl.BlockSpec((1, 6, d), lambda i, j, e: (i, 0, 0)), pl.BlockSpec((1, 6, d), lambda i, j, e: (0, 0, 0)),
                  pl.BlockSpec((MOE_EXPERTS_PER_STEP, d, ff), lambda i, j, e: (e, 0, 0)),
                  pl.BlockSpec((MOE_EXPERTS_PER_STEP, d, ff), lambda i, j, e: (e, 0, 0)),
                  pl.BlockSpec((MOE_EXPERTS_PER_STEP, ff, d), lambda i, j, e: (e, 0, 0)),
                  full(ws[3]), full(ws[4]), full(ws[5])],
        out_specs=xs,
        out_shape=jax.ShapeDtypeStruct((b, lt, d), F32),
        scratch_shapes=[pltpu.VMEM((bl, d), F32)],
        compiler_params=_params("parallel", "parallel", "arbitrary"),
        name="moe_experts",
    )(x, t, comb, mod_l, mod_c, *ws)


def _rope_tables(seq, n_ctx, rope_dims, lane_offset, repeat):
    half = rope_dims // 4
    t = jnp.arange(seq)
    rowp = (t // GRID_W).astype(F32)
    colp = (t % GRID_W).astype(F32)
    inv = ROPE_THETA ** (-jnp.arange(half, dtype=F32) / half)
    ar, ac = rowp[:, None] * inv[None, :], colp[:, None] * inv[None, :]
    cos = jnp.concatenate([jnp.cos(ar), jnp.cos(ar), jnp.cos(ac), jnp.cos(ac)], axis=1)
    sin = jnp.concatenate([-jnp.sin(ar), jnp.sin(ar), -jnp.sin(ac), jnp.sin(ac)], axis=1)
    width = LANES // repeat
    padl, padr = lane_offset, width - lane_offset - rope_dims
    cos = jnp.pad(cos, ((0, n_ctx), (0, 0)), constant_values=1.0)
    sin = jnp.pad(sin, ((0, n_ctx), (0, 0)))
    cos = jnp.pad(cos, ((0, 0), (padl, padr)), constant_values=1.0)
    sin = jnp.pad(sin, ((0, 0), (padl, padr)))
    return jnp.tile(cos, (1, repeat)), jnp.tile(sin, (1, repeat))


def kernel(x, c, ctx, c_ctx, ada_w, ada_b, norm1_g, norm2_g, ab_w_in, ab_w_out, rw_mu, rw_w0, rw_w2, rw_a0, rw_a2, rw_k_k, rw_k_a, rw_r_k, rw_g2, rw_ln_w, rw_ln_b, mla_g_qa, mla_w_q_up, mla_g_kva, mla_w_kv_up, mla_g_q, mla_g_k, gqa_w_in, gqa_w_out, gqa_g_q, gqa_g_k, router_w, router_b, moe_w_gate, moe_w_up, moe_w_down, shared_w_gate, shared_w_up, shared_w_down):
    b, n_lat, d = x.shape
    n_ctx = ctx.shape[1]
    depth = ada_w.shape[0]
    xs = jnp.concatenate([x, ctx], axis=1)

    cvec = jnp.concatenate([c, c_ctx[None, :]], axis=0)
    cvec = jnp.pad(cvec, ((0, 8 - (b + 1) % 8), (0, 0))) if (b + 1) % 8 else cvec
    mods = _ada_mods(cvec, ada_w, ada_b).reshape(depth, -1, 6, d)

    mla_cos, mla_sin = _rope_tables(n_lat, n_ctx, MLA_ROPE, MLA_NOPE, 1)
    gqa_cos, gqa_sin = _rope_tables(n_lat, n_ctx, GQA_HEAD_DIM, 0, 2)

    for l in range(depth):
        i = l // 2
        mod_l, mod_c = mods[l, :b], mods[l, b:b + 1]
        rows = n_lat if l == depth - 1 else n_lat + n_ctx
        if l % 2 == 0:
            w_in = ab_w_in[i]
            w_rw = w_in[:, :RW_COLS].astype(BF16)
            w_m = w_in[:, RW_COLS:]
            w_mla = jnp.concatenate(
                [w_m[:, :MLA_Q_RANK + MLA_KV_RANK],
                 jnp.pad(w_m[:, MLA_Q_RANK + MLA_KV_RANK:], ((0, 0), (MLA_NOPE, LANES - MLA_NOPE - MLA_ROPE)))],
                axis=1).astype(BF16)
            p_rw, p_mla = _ln_mod_mm(xs, norm1_g[l], mod_l, mod_c, [w_rw, w_mla], n_lat)
            r, v, na, lw, kd, bd, gate, bonus = _rwkv_prep(
                p_rw, n_lat, rw_mu[i], rw_w0[i], rw_w2[i], rw_a0[i], rw_a2[i], rw_k_k[i], rw_k_a[i],
                rw_r_k[i].reshape(-1), rw_g2[i])
            y = _wkv_scan(r, v, na, lw, kd, bd, n_lat)
            q, kt, vm = _mla_qkv(p_mla, mla_g_qa[i], mla_w_q_up[i], mla_g_kva[i], mla_w_kv_up[i],
                                 mla_g_q[i], mla_g_k[i], mla_cos, mla_sin)
            o = _attention(q, kt, vm, n_lat)
            xs = _ab_out(xs, mod_l, mod_c, y, bonus, gate, o, rw_ln_w[i], rw_ln_b[i], ab_w_out[i], n_lat, rows)
        else:
            (p,) = _ln_mod_mm(xs, norm1_g[l], mod_l, mod_c, [gqa_w_in[i].astype(BF16)], n_lat)
            q, kt, vg = _gqa_qkv(p, gqa_g_q[i], gqa_g_k[i], gqa_cos, gqa_sin)
            o = _attention(q, kt, vg, n_lat)
            xs = _gqa_out(xs, mod_l, mod_c, o, gqa_w_out[i], n_lat, rows)
        t, comb = _moe_route(xs, norm2_g[l], mod_l, mod_c, router_w, router_b, n_lat)
        xs = _moe_experts(xs, t, comb, mod_l, mod_c, moe_w_gate[l], moe_w_up[l], moe_w_down[l],
                          shared_w_gate[l], shared_w_up[l], shared_w_down[l], n_lat)
    return xs[:, :n_lat]
```

```python
import functools
import math

import jax
import jax.numpy as jnp
from jax import lax
from jax.experimental import pallas as pl
from jax.experimental.pallas import tpu as pltpu

F32 = jnp.float32
BF16 = jnp.bfloat16
HIGHEST = lax.Precision.HIGHEST

GRID_W = 64
RW_HEADS = 8
RW_HEAD_DIM = 64
RW_DIM = RW_HEADS * RW_HEAD_DIM
DECAY_LORA = 64
ICLR_LORA = 64
GATE_LORA = 128
RW_COLS = 3 * RW_DIM + 2 * DECAY_LORA + 2 * ICLR_LORA + GATE_LORA
LN_X_EPS = 64e-5
MLA_HEADS = 8
MLA_NOPE = 64
MLA_ROPE = 32
MLA_V = 64
MLA_QK = MLA_NOPE + MLA_ROPE
MLA_Q_RANK = 384
MLA_KV_RANK = 256
GQA_HEADS = 16
GQA_KV_HEADS = 4
GQA_HEAD_DIM = 64
N_EXPERTS = 16
N_GROUPS = 4
EXPERTS_PER_GROUP = N_EXPERTS // N_GROUPS
ROPE_THETA = 10000.0
NORM_EPS = 1e-6

LANES = 128
V7X_VMEM_BYTES = 64 * 1024 * 1024
VMEM_LIMIT = V7X_VMEM_BYTES - 8 * 1024 * 1024

TOKEN_BLOCK = 256
HALO = 8
SCAN_CHUNK = 64
SCAN_INTERLEAVE = 16
SCAN_CHUNKS_PER_STEP = 2
ATTN_ROWS = 1024
ATTN_KEYS = 2816
MOE_ROWS = 768
MOE_CAP = 256
LOG2E = math.log2(math.e)


def _row_block(lt, limit):
    return max(r for r in range(TOKEN_BLOCK, limit + 1, TOKEN_BLOCK) if lt % r == 0)


def _key_chunk(lt):
    return max(c for c in range(TOKEN_BLOCK, ATTN_KEYS + 1, TOKEN_BLOCK) if lt % c == 0)


def _kt_spec(heads, dk, bl, bk):
    per = bk // bl
    return pl.BlockSpec((1, heads, 1, dk, bl), lambda i, j: (i, 0, j // per, 0, j % per))


def _params(*sem):
    return pltpu.CompilerParams(dimension_semantics=sem, vmem_limit_bytes=VMEM_LIMIT)


def _bdot(a, b):
    return jnp.dot(a.astype(BF16), b.astype(BF16), preferred_element_type=F32)


def _batched(a, b, ca, cb):
    return lax.dot_general(a.astype(BF16), b.astype(BF16), (((ca,), (cb,)), ((0,), (0,))),
                           preferred_element_type=F32)


def _bmm(a, b):
    return _batched(a, b, 2, 1)


def _bmm_nt(a, b):
    return _batched(a, b, 2, 2)


def _bmm_tn(a, b):
    return _batched(a, b, 1, 1)


def _sigmoid(x):
    return 1.0 / (1.0 + jnp.exp(-x))


def _silu(x):
    return x * _sigmoid(x)


def _softplus(x):
    return jnp.maximum(x, 0.0) + jnp.log(1.0 + jnp.exp(-jnp.abs(x)))


def _rms(x):
    return x * lax.rsqrt(jnp.mean(x * x, axis=-1, keepdims=True) + NORM_EPS)


def _row_mods(ml_ref, mc_ref, first_row, rows, n_lat, idx):
    r = first_row + lax.broadcasted_iota(jnp.int32, (rows, 1), 0)
    return jnp.where(r >= n_lat, mc_ref[0, idx:idx + 1, :], ml_ref[0, idx:idx + 1, :])


def _ada_kernel(c_ref, w_ref, b_ref, o_ref):
    s = _silu(c_ref[...])
    o_ref[0] = jnp.dot(s, w_ref[0], preferred_element_type=F32, precision=HIGHEST) + b_ref[0]


def _ada_mods(cvec, ada_w, ada_b):
    depth, d, n = ada_w.shape
    rows = cvec.shape[0]
    bn = n // 4
    return pl.pallas_call(
        _ada_kernel,
        grid=(depth, n // bn),
        in_specs=[pl.BlockSpec((rows, d), lambda l, j: (0, 0)),
                  pl.BlockSpec((1, d, bn), lambda l, j: (l, 0, j)),
                  pl.BlockSpec((1, 1, bn), lambda l, j: (l, 0, j))],
        out_specs=pl.BlockSpec((1, rows, bn), lambda l, j: (l, 0, j)),
        out_shape=jax.ShapeDtypeStruct((depth, rows, n), F32),
        compiler_params=_params("parallel", "parallel"),
        name="ada_mods",
    )(cvec, ada_w, ada_b.reshape(depth, 1, n))


def _ln_mod_mm_kernel(x_ref, g_ref, ml_ref, mc_ref, *refs, n_w, bl, n_lat):
    w_refs, o_refs = refs[:n_w], refs[n_w:]
    first = pl.program_id(1) * bl
    shift = _row_mods(ml_ref, mc_ref, first, bl, n_lat, 0)
    scale = _row_mods(ml_ref, mc_ref, first, bl, n_lat, 1)
    h = (_rms(x_ref[0]) * g_ref[...]) * (1.0 + scale) + shift
    hb = h.astype(BF16)
    for w_ref, o_ref in zip(w_refs, o_refs):
        o_ref[0] = jnp.dot(hb, w_ref[...], preferred_element_type=F32).astype(o_ref.dtype)


def _ln_mod_mm(x, g, mod_l, mod_c, ws, n_lat):
    b, lt, d = x.shape
    bl = TOKEN_BLOCK
    kern = functools.partial(_ln_mod_mm_kernel, n_w=len(ws), bl=bl, n_lat=n_lat)
    return pl.pallas_call(
        kern,
        grid=(b, lt // bl),
        in_specs=[pl.BlockSpec((1, bl, d), lambda i, j: (i, j, 0)),
                  pl.BlockSpec((1, d), lambda i, j: (0, 0)),
                  pl.BlockSpec((1, 6, d), lambda i, j: (i, 0, 0)),
                  pl.BlockSpec((1, 6, d), lambda i, j: (0, 0, 0))]
        + [pl.BlockSpec(w.shape, lambda i, j: (0, 0)) for w in ws],
        out_specs=[pl.BlockSpec((1, bl, w.shape[1]), lambda i, j: (i, j, 0)) for w in ws],
        out_shape=[jax.ShapeDtypeStruct((b, lt, w.shape[1]), F32) for w in ws],
        compiler_params=_params("parallel", "parallel"),
        name="ln_mod_mm",
    )(x, g.reshape(1, d), mod_l, mod_c, *ws)


def _rwkv_prep_kernel(p_ref, prev_ref, next_ref, mu_ref, w0_ref, w2_ref, a0_ref, a2_ref, kk_ref,
                      ka_ref, rk_ref, g2_ref, r_ref, v_ref, na_ref, lw_ref, kd_ref, bd_ref,
                      gate_ref, bonus_ref, *, bl, ctx_block):
    p = p_ref[0]
    j = pl.program_id(1)
    starts = jnp.logical_or(j == 0, j == ctx_block)
    ends = jnp.logical_or(j == ctx_block - 1, j == pl.num_programs(1) - 1)
    before = jnp.where(starts, 0.0, prev_ref[0, HALO - 1:HALO, :])
    after = jnp.where(ends, 0.0, next_ref[0, 0:1, :])
    row = lax.broadcasted_iota(jnp.int32, p.shape, 0)
    prev = jnp.where(row == 0, before, pltpu.roll(p, 1, 0))
    nxt = jnp.where(row == bl - 1, after, pltpu.roll(p, bl - 1, 0))
    pm = p + mu_ref[...] * (0.5 * (prev + nxt) - p)

    o = 3 * RW_DIM
    r = pm[:, 0:RW_DIM]
    k = pm[:, RW_DIM:2 * RW_DIM]
    v = pm[:, 2 * RW_DIM:o]
    w1 = (pm[:, o:o + DECAY_LORA], pm[:, o + DECAY_LORA:o + 2 * DECAY_LORA])
    o += 2 * DECAY_LORA
    a1 = (pm[:, o:o + ICLR_LORA], pm[:, o + ICLR_LORA:o + 2 * ICLR_LORA])
    o += 2 * ICLR_LORA
    g1 = pm[:, o:o + GATE_LORA]

    kk = k * kk_ref[...]
    lw, kd, ad = [], [], []
    for d in range(2):
        z = w0_ref[d:d + 1, :] + _bdot(jnp.tanh(w1[d]), w2_ref[d])
        lw.append(-jnp.exp(-_softplus(-z) - 0.5))
        a = _sigmoid(a0_ref[d:d + 1, :] + _bdot(a1[d], a2_ref[d]))
        ad.append(a)
        kd.append(k * (1.0 + (a - 1.0) * ka_ref[...]))
    gate = _bdot(_sigmoid(g1), g2_ref[...])
    rkk = r * (kd[0] + kd[1]) * rk_ref[...]

    for h in range(RW_HEADS):
        sl = slice(h * RW_HEAD_DIM, (h + 1) * RW_HEAD_DIM)
        kk_h = kk[:, sl]
        kk_h = kk_h * lax.rsqrt(jnp.sum(kk_h * kk_h, axis=-1, keepdims=True) + 1e-12)
        v_h = v[:, sl]
        r_ref[0, h] = r[:, sl]
        v_ref[0, h] = v_h
        na_ref[0, h] = -kk_h
        for d in range(2):
            lw_ref[d, 0, h] = lw[d][:, sl]
            kd_ref[d, 0, h] = kd[d][:, sl]
            bd_ref[d, 0, h] = kk_h * ad[d][:, sl]
        gate_ref[0, h] = gate[:, sl]
        bonus_ref[0, h] = jnp.sum(rkk[:, sl], axis=-1, keepdims=True) * v_h


def _rwkv_prep(p, n_lat, mu, w0, w2, a0, a2, k_k, k_a, r_k, g2):
    b, lt, cols = p.shape
    bl = TOKEN_BLOCK
    nblk = lt // bl
    per = bl // HALO
    last_slab = lt // HALO - 1
    prev_spec = pl.BlockSpec((1, HALO, cols), lambda i, j: (i, jnp.maximum(j * per - 1, 0), 0))
    next_spec = pl.BlockSpec((1, HALO, cols), lambda i, j: (i, jnp.minimum((j + 1) * per, last_slab), 0))

    hm = jax.ShapeDtypeStruct((b, RW_HEADS, lt, RW_HEAD_DIM), F32)
    hm2 = jax.ShapeDtypeStruct((2, b, RW_HEADS, lt, RW_HEAD_DIM), F32)
    hm_spec = pl.BlockSpec((1, RW_HEADS, bl, RW_HEAD_DIM), lambda i, j: (i, 0, j, 0))
    hm2_spec = pl.BlockSpec((2, 1, RW_HEADS, bl, RW_HEAD_DIM), lambda i, j: (0, i, 0, j, 0))
    full = lambda a: pl.BlockSpec(a.shape, lambda i, j: (0,) * a.ndim)
    consts = [mu.reshape(1, cols), w0, w2.astype(BF16), a0, a2.astype(BF16), k_k.reshape(1, RW_DIM),
              k_a.reshape(1, RW_DIM), r_k.reshape(1, RW_DIM), g2.astype(BF16)]
    return pl.pallas_call(
        functools.partial(_rwkv_prep_kernel, bl=bl, ctx_block=n_lat // bl),
        grid=(b, nblk),
        in_specs=[pl.BlockSpec((1, bl, cols), lambda i, j: (i, j, 0)), prev_spec, next_spec]
        + [full(a) for a in consts],
        out_specs=[hm_spec, hm_spec, hm_spec, hm2_spec, hm2_spec, hm2_spec, hm_spec, hm_spec],
        out_shape=[hm, hm, hm, hm2, hm2, hm2, hm, hm],
        compiler_params=_params("parallel", "parallel"),
        name="rwkv_prep",
    )(p, p, p, *consts)


def _wkv_kernel(r_ref, v_ref, a_ref, lw_ref, k_ref, b_ref, y_ref, st_ref, *, cn, chains):
    d = pl.program_id(0)

    @pl.when(pl.program_id(1) == 0)
    def _():
        st_ref[...] = jnp.zeros_like(st_ref)

    row = lax.broadcasted_iota(jnp.int32, (cn, cn), 0)
    col = lax.broadcasted_iota(jnp.int32, (cn, cn), 1)
    lead = (row - col) * jnp.where(d == 0, 1, -1)
    before = lead > 0
    upto = lead >= 0
    eye = (col == row).astype(F32)
    cum_mat = upto.astype(F32)
    hd = r_ref.shape[-1]
    eye_h = (lax.broadcasted_iota(jnp.int32, (hd, hd), 0)
             == lax.broadcasted_iota(jnp.int32, (hd, hd), 1)).astype(F32)
    n_double = int(math.log2(cn)) - 1

    nb = SCAN_INTERLEAVE
    cum_b = jnp.broadcast_to(cum_mat, (nb, cn, cn))

    def chunk(sl, rows, st):
        r, v, a = r_ref[sl, rows], v_ref[sl, rows], a_ref[sl, rows]
        lw, k, b = lw_ref[0, sl, rows], k_ref[0, sl, rows], b_ref[0, sl, rows]
        lw_hi = lw.astype(BF16)
        lw_lo = (lw - lw_hi.astype(F32)).astype(BF16)
        cum2 = _bmm(cum_b, jnp.concatenate([lw_hi, lw_lo], axis=-1))
        cum = cum2[..., :hd] + cum2[..., hd:]
        tot = jnp.sum(lw, axis=1, keepdims=True)
        at = a * jnp.exp(cum - lw)
        rt = r * jnp.exp(cum)
        einv = jnp.exp(-cum)
        bt, kt = b * einv, k * einv
        eend = jnp.exp(tot - cum)
        bh, kh = b * eend, k * eend

        a_ab = jnp.where(before, _bmm_nt(at, bt), 0.0)
        a_ak = jnp.where(before, _bmm_nt(at, kt), 0.0)
        a_rb = jnp.where(upto, _bmm_nt(rt, bt), 0.0)
        a_rk = jnp.where(upto, _bmm_nt(rt, kt), 0.0)

        t = eye + a_ab
        x = _bmm(a_ab, a_ab)
        for _ in range(n_double - 1):
            tx = _bmm(jnp.concatenate([t, x], axis=1), x)
            t = t + tx[:, :cn]
            x = tx[:, cn:]
        t = t + _bmm(t, x)
        w = _bmm(t, at)
        up = _bmm(t, _bmm(a_ak, v))
        qe = rt + _bmm(a_rb, w)
        yp = _bmm(a_rb, up) + _bmm(a_rk, v)
        mc = eye_h * jnp.exp(tot) + _bmm_tn(bh, w)
        nc = _bmm_tn(bh, up) + _bmm_tn(kh, v)

        y_ref[0, sl, rows] = _bmm(qe, st) + yp
        return _bmm(mc, st) + nc

    n_sub = r_ref.shape[1] // cn

    def group(g, carry):
        sl = pl.ds(g * nb, nb)
        st = st_ref[sl]
        for s in range(n_sub):
            sub = jnp.where(d == 0, s, n_sub - 1 - s)
            st = chunk(sl, pl.ds(pl.multiple_of(sub * cn, cn), cn), st)
        st_ref[sl] = st
        return carry

    lax.fori_loop(0, chains // SCAN_INTERLEAVE, group, 0)


def _wkv_scan(r, v, na, lw, kd, bd, n_lat):
    b, h, lt, hd = r.shape
    cn = SCAN_CHUNK
    chains = b * h
    rows = cn * SCAN_CHUNKS_PER_STEP
    assert n_lat % rows == 0 and lt % rows == 0
    n_all, n_l = lt // rows, n_lat // rows
    n_c = n_all - n_l

    def chunk(d, i):
        fwd = jnp.where(i < n_c, n_l + i, i - n_c)
        rev = jnp.where(i < n_c, n_all - 1 - i, n_l - 1 - (i - n_c))
        return jnp.where(d == 0, fwd, rev)

    shared = pl.BlockSpec((chains, rows, hd), lambda d, i: (0, chunk(d, i), 0))
    per_dir = pl.BlockSpec((1, chains, rows, hd), lambda d, i: (d, 0, chunk(d, i), 0))
    flat = lambda t: t.reshape(t.shape[:-4] + (chains, lt, hd))
    y = pl.pallas_call(
        functools.partial(_wkv_kernel, cn=cn, chains=chains),
        grid=(2, n_all),
        in_specs=[shared, shared, shared, per_dir, per_dir, per_dir],
        out_specs=per_dir,
        out_shape=jax.ShapeDtypeStruct((2, chains, lt, hd), F32),
        scratch_shapes=[pltpu.VMEM((chains, hd, hd), F32)],
        compiler_params=_params("parallel", "arbitrary"),
        name="wkv_scan",
    )(flat(r), flat(v), flat(na), flat(lw), flat(kd), flat(bd))
    return y.reshape(2, b, h, lt, hd)


def _rope(x, cos, sin, half):
    lane = lax.broadcasted_iota(jnp.int32, x.shape, 1)
    width = x.shape[1]
    partner = jnp.where(lane % (2 * half) < half, pltpu.roll(x, width - half, 1), pltpu.roll(x, half, 1))
    return x * cos + partner * sin


def _mla_qkv_kernel(p_ref, gqa_ref, gkva_ref, wq_ref, wk_ref, wv_ref, gq_ref, gk_ref, cos_ref, sin_ref,
                    q_ref, kt_ref, v_ref, *, scale):
    p = p_ref[0]
    q_a = p[:, 0:MLA_Q_RANK]
    kv_a = p[:, MLA_Q_RANK:MLA_Q_RANK + MLA_KV_RANK]
    k_rope = p[:, MLA_Q_RANK + MLA_KV_RANK:MLA_Q_RANK + MLA_KV_RANK + LANES]
    qa_n = (_rms(q_a) * gqa_ref[...]).astype(BF16)
    kva_n = (_rms(kv_a) * gkva_ref[...]).astype(BF16)
    q_all = jnp.dot(qa_n, wq_ref[...], preferred_element_type=F32)
    k_all = jnp.dot(kva_n, wk_ref[...], preferred_element_type=F32)
    v_all = jnp.dot(kva_n, wv_ref[...], preferred_element_type=F32)
    cos, sin = cos_ref[...], sin_ref[...]
    for h in range(MLA_HEADS):
        sl = slice(h * LANES, (h + 1) * LANES)
        q = q_all[:, sl]
        q = q * lax.rsqrt(jnp.sum(q * q, axis=-1, keepdims=True) / MLA_QK + NORM_EPS) * gq_ref[...]
        q = _rope(q, cos, sin, MLA_ROPE // 4)
        q_ref[0, h] = (q * scale).astype(q_ref.dtype)
        k = k_all[:, sl] + k_rope
        k = k * lax.rsqrt(jnp.sum(k * k, axis=-1, keepdims=True) / MLA_QK + NORM_EPS) * gk_ref[...]
        k = _rope(k, cos, sin, MLA_ROPE // 4)
        kt_ref[0, h, 0] = k.T.astype(kt_ref.dtype)
        lane = lax.broadcasted_iota(jnp.int32, q.shape, 1)
        v_ref[0, h] = jnp.where(lane < MLA_V, v_all[:, sl], 1.0).astype(v_ref.dtype)


def _pad_heads(w, heads, width):
    k = w.shape[0]
    w = w.reshape(k, heads, width)
    return jnp.pad(w, ((0, 0), (0, 0), (0, LANES - width))).reshape(k, heads * LANES)


def _mla_qkv(p, g_qa, w_q_up, g_kva, w_kv_up, g_q, g_k, cos, sin):
    b, lt, cols = p.shape
    bl = TOKEN_BLOCK
    bk = _key_chunk(lt)
    wq = _pad_heads(w_q_up, MLA_HEADS, MLA_QK).astype(BF16)
    wkv = w_kv_up.reshape(MLA_KV_RANK, MLA_HEADS, MLA_NOPE + MLA_V)
    wk = _pad_heads(wkv[:, :, :MLA_NOPE].reshape(MLA_KV_RANK, -1), MLA_HEADS, MLA_NOPE).astype(BF16)
    wv = _pad_heads(wkv[:, :, MLA_NOPE:].reshape(MLA_KV_RANK, -1), MLA_HEADS, MLA_V).astype(BF16)
    pad = lambda g: jnp.pad(g, (0, LANES - MLA_QK)).reshape(1, LANES)
    consts = [g_qa.reshape(1, -1), g_kva.reshape(1, -1), wq, wk, wv, pad(g_q), pad(g_k)]
    full = lambda a: pl.BlockSpec(a.shape, lambda i, j: (0,) * a.ndim)
    tab = pl.BlockSpec((bl, LANES), lambda i, j: (j, 0))
    return pl.pallas_call(
        functools.partial(_mla_qkv_kernel, scale=MLA_QK ** -0.5 * LOG2E),
        grid=(b, lt // bl),
        in_specs=[pl.BlockSpec((1, bl, cols), lambda i, j: (i, j, 0))] + [full(a) for a in consts] + [tab, tab],
        out_specs=[pl.BlockSpec((1, MLA_HEADS, bl, LANES), lambda i, j: (i, 0, j, 0)),
                   _kt_spec(MLA_HEADS, LANES, bl, bk),
                   pl.BlockSpec((1, MLA_HEADS, bl, LANES), lambda i, j: (i, 0, j, 0))],
        out_shape=[jax.ShapeDtypeStruct((b, MLA_HEADS, lt, LANES), BF16),
                   jax.ShapeDtypeStruct((b, MLA_HEADS, lt // bk, LANES, bk), BF16),
                   jax.ShapeDtypeStruct((b, MLA_HEADS, lt, LANES), BF16)],
        compiler_params=_params("parallel", "parallel"),
        name="mla_qkv",
    )(p, *consts, cos, sin)


def _pair_norm_rope(x, g, cos, sin):
    lane = lax.broadcasted_iota(jnp.int32, x.shape, 1)
    lo = lane < GQA_HEAD_DIM
    xx = x * x
    ss_lo = jnp.sum(jnp.where(lo, xx, 0.0), axis=-1, keepdims=True)
    ss_hi = jnp.sum(jnp.where(lo, 0.0, xx), axis=-1, keepdims=True)
    rs = lax.rsqrt(jnp.where(lo, ss_lo, ss_hi) / GQA_HEAD_DIM + NORM_EPS)
    return _rope(x * rs * g, cos, sin, GQA_HEAD_DIM // 4)


def _gqa_qkv_kernel(p_ref, gq_ref, gk_ref, cos_ref, sin_ref, q_ref, kt_ref, v_ref, *, scale):
    p = p_ref[0]
    cos, sin = cos_ref[...], sin_ref[...]
    qw = GQA_HEADS * GQA_HEAD_DIM
    kw = GQA_KV_HEADS * GQA_HEAD_DIM
    for j in range(GQA_HEADS // 2):
        q = _pair_norm_rope(p[:, j * LANES:(j + 1) * LANES], gq_ref[...], cos, sin) * scale
        q_ref[0, 2 * j] = q[:, :GQA_HEAD_DIM].astype(q_ref.dtype)
        q_ref[0, 2 * j + 1] = q[:, GQA_HEAD_DIM:].astype(q_ref.dtype)
    for j in range(GQA_KV_HEADS // 2):
        k = _pair_norm_rope(p[:, qw + j * LANES:qw + (j + 1) * LANES], gk_ref[...], cos, sin)
        kt = k.T
        kt_ref[0, 2 * j, 0] = kt[:GQA_HEAD_DIM].astype(kt_ref.dtype)
        kt_ref[0, 2 * j + 1, 0] = kt[GQA_HEAD_DIM:].astype(kt_ref.dtype)
        v = p[:, qw + kw + j * LANES:qw + kw + (j + 1) * LANES]
        lo = lax.broadcasted_iota(jnp.int32, v.shape, 1) < GQA_HEAD_DIM
        v_ref[0, 2 * j] = jnp.where(lo, v, 1.0).astype(v_ref.dtype)
        v_ref[0, 2 * j + 1] = jnp.where(lo, pltpu.roll(v, GQA_HEAD_DIM, 1), 1.0).astype(v_ref.dtype)


def _gqa_qkv(p, g_q, g_k, cos, sin):
    b, lt, cols = p.shape
    bl = TOKEN_BLOCK
    bk = _key_chunk(lt)
    hd = GQA_HEAD_DIM
    two = lambda g: jnp.concatenate([g, g]).reshape(1, LANES)
    tab = pl.BlockSpec((bl, LANES), lambda i, j: (j, 0))
    vec = pl.BlockSpec((1, LANES), lambda i, j: (0, 0))
    return pl.pallas_call(
        functools.partial(_gqa_qkv_kernel, scale=hd ** -0.5 * LOG2E),
        grid=(b, lt // bl),
        in_specs=[pl.BlockSpec((1, bl, cols), lambda i, j: (i, j, 0)), vec, vec, tab, tab],
        out_specs=[pl.BlockSpec((1, GQA_HEADS, bl, hd), lambda i, j: (i, 0, j, 0)),
                   _kt_spec(GQA_KV_HEADS, hd, bl, bk),
                   pl.BlockSpec((1, GQA_KV_HEADS, bl, LANES), lambda i, j: (i, 0, j, 0))],
        out_shape=[jax.ShapeDtypeStruct((b, GQA_HEADS, lt, hd), BF16),
                   jax.ShapeDtypeStruct((b, GQA_KV_HEADS, lt // bk, hd, bk), BF16),
                   jax.ShapeDtypeStruct((b, GQA_KV_HEADS, lt, LANES), BF16)],
        compiler_params=_params("parallel", "parallel"),
        name="gqa_qkv",
    )(p, two(g_q), two(g_k), cos, sin)


def _attn_kernel(q_ref, kt_ref, v_ref, o_ref, m_scr, acc_scr, *, groups, bq, n_chunks):
    rows = groups * bq
    q = q_ref[0, 0].reshape(rows, q_ref.shape[-1])
    bk = kt_ref.shape[-1]
    dv = o_ref.shape[-1]
    m_scr[...] = jnp.full_like(m_scr, -jnp.inf)
    acc_scr[...] = jnp.zeros_like(acc_scr)

    def step(j, carry):
        s = jnp.dot(q, kt_ref[0, 0, j], preferred_element_type=F32)
        m_prev = m_scr[...]
        m_new = jnp.maximum(m_prev, jnp.max(s, axis=-1, keepdims=True))
        alpha = jnp.exp2(m_prev - m_new)
        p = jnp.exp2(s - pltpu.repeat(m_new, bk // LANES, axis=1))
        vv = v_ref[0, 0, j * bk:(j + 1) * bk, :]
        acc_scr[...] = alpha * acc_scr[...] + jnp.dot(p.astype(BF16), vv, preferred_element_type=F32)
        m_scr[...] = m_new
        return carry

    for j in range(n_chunks):
        step(j, 0)
    acc = acc_scr[...]
    o = acc / pltpu.roll(acc, LANES - dv, 1)
    o_ref[0, 0] = o[:, :dv].reshape(groups, bq, dv).astype(o_ref.dtype)


def _attention_call(q, q_start, lq, ktc, v1):
    b, hq, lt, dk = q.shape
    hk, n_chunks, bk = ktc.shape[1], ktc.shape[2], ktc.shape[-1]
    lk = n_chunks * bk
    dv = LANES // 2
    groups = hq // hk
    bq = min(ATTN_ROWS // groups, lq)
    rows = groups * bq
    first = q_start // bq
    q5 = q.reshape(b, hk, groups, lt, dk)
    out = pl.pallas_call(
        functools.partial(_attn_kernel, groups=groups, bq=bq, n_chunks=n_chunks),
        grid=(b, hk, lq // bq),
        in_specs=[pl.BlockSpec((1, 1, groups, bq, dk), lambda i, h, j: (i, h, 0, j + first, 0)),
                  pl.BlockSpec((1, 1, n_chunks, dk, bk), lambda i, h, j: (i, h, 0, 0, 0)),
                  pl.BlockSpec((1, 1, lk, LANES), lambda i, h, j: (i, h, 0, 0))],
        out_specs=pl.BlockSpec((1, 1, groups, bq, dv), lambda i, h, j: (i, h, 0, j, 0)),
        out_shape=jax.ShapeDtypeStruct((b, hk, groups, lq, dv), BF16),
        scratch_shapes=[pltpu.VMEM((rows, LANES), F32), pltpu.VMEM((rows, LANES), F32)],
        compiler_params=_params("parallel", "parallel", "arbitrary"),
        name="attention",
    )(q5, ktc, v1)
    return out.reshape(b, hq, lq, dv)


def _attention(q, ktc, v1, n_lat):
    lt = q.shape[2]
    n_ctx = lt - n_lat
    o_l = _attention_call(q, 0, n_lat, ktc, v1)
    o_c = _attention_call(q, n_lat, n_ctx, ktc[:, :, -1:, :, -n_ctx:], v1[:, :, n_lat:])
    return jnp.concatenate([o_l, o_c], axis=2)


def _ab_out_kernel(x_ref, ml_ref, mc_ref, y_ref, bonus_ref, gate_ref, o_ref, lnw_ref, lnb_ref, wrw_ref,
                   wmla_ref, out_ref, *, bl, n_lat):
    rw = []
    for h in range(RW_HEADS):
        y = y_ref[0, 0, h] + y_ref[1, 0, h]
        mean = jnp.mean(y, axis=-1, keepdims=True)
        yc = y - mean
        var = jnp.mean(yc * yc, axis=-1, keepdims=True)
        yn = yc * lax.rsqrt(var + LN_X_EPS) * lnw_ref[h] + lnb_ref[h]
        rw.append((yn + bonus_ref[0, h]) * gate_ref[0, h])
    rw = jnp.concatenate(rw, axis=-1).astype(BF16)
    o = jnp.concatenate([o_ref[0, h] for h in range(MLA_HEADS)], axis=-1)
    acc = (jnp.dot(rw, wrw_ref[...], preferred_element_type=F32)
           + jnp.dot(o, wmla_ref[...], preferred_element_type=F32))
    gate = _row_mods(ml_ref, mc_ref, pl.program_id(1) * bl, bl, n_lat, 2)
    out_ref[0] = x_ref[0] + gate * acc


def _ab_out(x, mod_l, mod_c, y, bonus, gate, o, ln_w, ln_b, w_out, n_lat, lt):
    b, _, d = x.shape
    bl = TOKEN_BLOCK
    hd = RW_HEAD_DIM
    wrw = w_out[:RW_DIM].astype(BF16)
    wmla = w_out[RW_DIM:].astype(BF16)
    xs = pl.BlockSpec((1, bl, d), lambda i, j: (i, j, 0))
    hm = pl.BlockSpec((1, RW_HEADS, bl, hd), lambda i, j: (i, 0, j, 0))
    full = lambda a: pl.BlockSpec(a.shape, lambda i, j: (0,) * a.ndim)
    lnw, lnb = ln_w.reshape(RW_HEADS, 1, hd), ln_b.reshape(RW_HEADS, 1, hd)
    return pl.pallas_call(
        functools.partial(_ab_out_kernel, bl=bl, n_lat=n_lat),
        grid=(b, lt // bl),
        in_specs=[xs, pl.BlockSpec((1, 6, d), lambda i, j: (i, 0, 0)), pl.BlockSpec((1, 6, d), lambda i, j: (0, 0, 0)),
                  pl.BlockSpec((2, 1, RW_HEADS, bl, hd), lambda i, j: (0, i, 0, j, 0)), hm, hm,
                  pl.BlockSpec((1, MLA_HEADS, bl, MLA_V), lambda i, j: (i, 0, j, 0)),
                  full(lnw), full(lnb), full(wrw), full(wmla)],
        out_specs=xs,
        out_shape=jax.ShapeDtypeStruct((b, lt, d), F32),
        compiler_params=_params("parallel", "parallel"),
        name="ab_out",
    )(x, mod_l, mod_c, y, bonus, gate, o, lnw, lnb, wrw, wmla)


def _gqa_out_kernel(x_ref, ml_ref, mc_ref, o_ref, w_ref, out_ref, *, bl, n_lat):
    o = jnp.concatenate([o_ref[0, h] for h in range(GQA_HEADS)], axis=-1)
    acc = jnp.dot(o, w_ref[...], preferred_element_type=F32)
    gate = _row_mods(ml_ref, mc_ref, pl.program_id(1) * bl, bl, n_lat, 2)
    out_ref[0] = x_ref[0] + gate * acc


def _gqa_out(x, mod_l, mod_c, o, w_out, n_lat, lt):
    b, _, d = x.shape
    bl = TOKEN_BLOCK
    hd = GQA_HEAD_DIM
    w = w_out.astype(BF16)
    xs = pl.BlockSpec((1, bl, d), lambda i, j: (i, j, 0))
    return pl.pallas_call(
        functools.partial(_gqa_out_kernel, bl=bl, n_lat=n_lat),
        grid=(b, lt // bl),
        in_specs=[xs, pl.BlockSpec((1, 6, d), lambda i, j: (i, 0, 0)), pl.BlockSpec((1, 6, d), lambda i, j: (0, 0, 0)),
                  pl.BlockSpec((1, GQA_HEADS, bl, hd), lambda i, j: (i, 0, j, 0)),
                  pl.BlockSpec(w.shape, lambda i, j: (0, 0))],
        out_specs=xs,
        out_shape=jax.ShapeDtypeStruct((b, lt, d), F32),
        compiler_params=_params("parallel", "parallel"),
        name="gqa_out",
    )(x, mod_l, mod_c, o, w)


def _first_argmax(vals):
    best, idx = vals[0], jnp.zeros(vals[0].shape, jnp.int32)
    for i in range(1, len(vals)):
        better = vals[i] > best
        idx = jnp.where(better, i, idx)
        best = jnp.where(better, vals[i], best)
    return best, idx


def _pick(vals, idx):
    out = vals[0]
    for i in range(1, len(vals)):
        out = jnp.where(idx == i, vals[i], out)
    return out


def _route_kernel(x_ref, g_ref, ml_ref, mc_ref, rw_ref, rb_ref, t_ref, comb_ref, grp_ref, *, bl, n_lat):
    first = pl.program_id(1) * bl
    shift = _row_mods(ml_ref, mc_ref, first, bl, n_lat, 3)
    scale = _row_mods(ml_ref, mc_ref, first, bl, n_lat, 4)
    t = (_rms(x_ref[0]) * g_ref[...]) * (1.0 + scale) + shift
    t_ref[0] = t.astype(t_ref.dtype)
    logits = jnp.dot(t, rw_ref[...], preferred_element_type=F32, precision=HIGHEST)
    lt = logits.T
    score = [_sigmoid(lt[e:e + 1]) for e in range(N_EXPERTS)]
    biased = [score[e] + rb_ref[e:e + 1, :] for e in range(N_EXPERTS)]
    epg = EXPERTS_PER_GROUP
    group_score = []
    for g in range(N_GROUPS):
        vals = biased[g * epg:(g + 1) * epg]
        pair = [vals[i] + vals[j] for i in range(epg) for j in range(i + 1, epg)]
        group_score.append(functools.reduce(jnp.maximum, pair))
    _, grp = _first_argmax(group_score)
    in_b = [_pick([biased[g * epg + j] for g in range(N_GROUPS)], grp) for j in range(epg)]
    in_s = [_pick([score[g * epg + j] for g in range(N_GROUPS)], grp) for j in range(epg)]
    _, loc1 = _first_argmax(in_b)
    _, loc2 = _first_argmax([jnp.where(loc1 == j, -jnp.inf, in_b[j]) for j in range(epg)])
    w1, w2 = _pick(in_s, loc1), _pick(in_s, loc2)
    wsum = w1 + w2
    w1, w2 = w1 / wsum, w2 / wsum
    e1, e2 = grp * epg + loc1, grp * epg + loc2
    sub = lax.broadcasted_iota(jnp.int32, (LANES, bl), 0)
    comb = jnp.zeros((LANES, bl), F32)
    for e in range(N_EXPERTS):
        c_e = jnp.where(e1 == e, w1, 0.0) + jnp.where(e2 == e, w2, 0.0)
        c_hi = c_e.astype(BF16).astype(F32)
        comb = jnp.where(sub == e, jnp.broadcast_to(c_hi, (LANES, bl)), comb)
        comb = jnp.where(sub == N_EXPERTS + e, jnp.broadcast_to(c_e - c_hi, (LANES, bl)), comb)
    comb_ref[0] = comb.T.astype(comb_ref.dtype)
    grp_ref[0] = grp


def _moe_route(x, g, mod_l, mod_c, router_w, router_b, n_lat):
    b, lt, d = x.shape
    bl = TOKEN_BLOCK
    rw = jnp.pad(router_w, ((0, 0), (0, LANES - N_EXPERTS)))
    xs = pl.BlockSpec((1, bl, d), lambda i, j: (i, j, 0))
    return pl.pallas_call(
        functools.partial(_route_kernel, bl=bl, n_lat=n_lat),
        grid=(b, lt // bl),
        in_specs=[xs, pl.BlockSpec((1, d), lambda i, j: (0, 0)),
                  pl.BlockSpec((1, 6, d), lambda i, j: (i, 0, 0)), pl.BlockSpec((1, 6, d), lambda i, j: (0, 0, 0)),
                  pl.BlockSpec(rw.shape, lambda i, j: (0, 0)), pl.BlockSpec((N_EXPERTS, 1), lambda i, j: (0, 0))],
        out_specs=[xs, pl.BlockSpec((1, bl, LANES), lambda i, j: (i, j, 0)),
                   pl.BlockSpec((1, 1, bl), lambda i, j: (i, 0, j))],
        out_shape=[jax.ShapeDtypeStruct((b, lt, d), BF16), jax.ShapeDtypeStruct((b, lt, LANES), BF16),
                   jax.ShapeDtypeStruct((b, 1, lt), jnp.int32)],
        compiler_params=_params("parallel", "parallel"),
        name="moe_route",
    )(x, g.reshape(1, d), mod_l, mod_c, rw, router_b.reshape(N_EXPERTS, 1))


def _ffn(t, wg, wu, wd, cw):
    g = jnp.dot(t, wg, preferred_element_type=F32)
    u = jnp.dot(t, wu, preferred_element_type=F32)
    h = _silu(g) * u
    if cw is not None:
        h = h * cw
    return jnp.dot(h.astype(BF16), wd, preferred_element_type=F32)


def _experts_kernel(cnt_ref, x_ref, t_ref, comb_ref, grp_ref, tri_ref, ml_ref, mc_ref, wg_ref, wu_ref, wd_ref,
                    sg_ref, su_ref, sd_ref, out_ref, acc_ref, row_ref, col_ref, *, bl, n_lat):
    g = pl.program_id(2)
    t = t_ref[0]

    @pl.when(g == 0)
    def _():
        acc_ref[...] = _ffn(t, sg_ref[...], su_ref[...], sd_ref[...], None)
        grp = grp_ref[0]
        gid = lax.broadcasted_iota(jnp.int32, (2 * HALO, bl), 0)
        member = jnp.where(grp == gid, 1.0, 0.0)
        before = jnp.dot(member.astype(BF16), tri_ref[...], preferred_element_type=F32)
        slot = jnp.sum(member * before, axis=0, keepdims=True)
        sub = lax.broadcasted_iota(jnp.int32, (LANES, bl), 0)
        info = jnp.where(sub == 0, jnp.broadcast_to(slot, (LANES, bl)),
                         jnp.where(sub == 1, jnp.broadcast_to(grp.astype(F32), (LANES, bl)), 0.0))
        row_ref[...] = info[:HALO]
        col_ref[...] = info.T

    gf = g.astype(F32)
    slot_r, in_r = row_ref[0:1, :], jnp.where(row_ref[1:2, :] == gf, 1.0, 0.0)
    slot_c, in_c = col_ref[:, 0:1], jnp.where(col_ref[:, 1:2] == gf, 1.0, 0.0)
    count = cnt_ref[(pl.program_id(0) * pl.num_programs(1) + pl.program_id(1)) * N_GROUPS + g]
    wd = wd_ref[...]
    wd = wd.reshape(wd.shape[0] * wd.shape[1], wd.shape[2])
    lane = lax.broadcasted_iota(jnp.int32, (MOE_CAP, LANES), 1)

    def one_pass(k, carry):
        base = (k * MOE_CAP).astype(F32)
        want_r = base + lax.broadcasted_iota(jnp.int32, (MOE_CAP, bl), 0).astype(F32)
        want_c = base + lax.broadcasted_iota(jnp.int32, (bl, MOE_CAP), 1).astype(F32)
        gather = jnp.where(slot_r == want_r, in_r, 0.0).astype(BF16)
        scatter = jnp.where(slot_c == want_c, in_c, 0.0).astype(BF16)
        tg = jnp.dot(gather, t, preferred_element_type=F32).astype(BF16)
        cg = jnp.dot(gather, comb_ref[0], preferred_element_type=F32)
        hs = []
        for i in range(EXPERTS_PER_GROUP):
            e = g * EXPERTS_PER_GROUP + i
            cw = jnp.sum(jnp.where(lane % N_EXPERTS == e, cg, 0.0), axis=-1, keepdims=True)
            gate = jnp.dot(tg, wg_ref[i], preferred_element_type=F32)
            up = jnp.dot(tg, wu_ref[i], preferred_element_type=F32)
            hs.append((_silu(gate) * up * cw).astype(BF16))
        yg = jnp.dot(jnp.concatenate(hs, axis=-1), wd, preferred_element_type=F32)
        acc_ref[...] += jnp.dot(scatter, yg.astype(BF16), preferred_element_type=F32)
        return carry

    lax.fori_loop(0, (count + MOE_CAP - 1) // MOE_CAP, one_pass, 0)

    @pl.when(g == pl.num_programs(2) - 1)
    def _():
        gate = _row_mods(ml_ref, mc_ref, pl.program_id(1) * bl, bl, n_lat, 5)
        out_ref[0] = x_ref[0] + gate * acc_ref[...]


def _moe_experts(x, t, comb, grp, mod_l, mod_c, w_gate, w_up, w_down, sh_gate, sh_up, sh_down, n_lat):
    b, lt, d = x.shape
    bl = _row_block(lt, MOE_ROWS)
    nblk = lt // bl
    ff = w_gate.shape[-1]
    epg = EXPERTS_PER_GROUP
    counts = jnp.sum(grp.reshape(b, nblk, bl, 1) == jnp.arange(N_GROUPS), axis=2, dtype=jnp.int32).reshape(-1)
    tri = jnp.triu(jnp.ones((bl, bl), BF16), 1)
    xs = pl.BlockSpec((1, bl, d), lambda i, j, g, c: (i, j, 0))
    full = lambda a: pl.BlockSpec(a.shape, lambda i, j, g, c: (0,) * a.ndim)
    ws = [w_gate.astype(BF16), w_up.astype(BF16), w_down.astype(BF16),
          sh_gate.astype(BF16), sh_up.astype(BF16), sh_down.astype(BF16)]
    grid_spec = pltpu.PrefetchScalarGridSpec(
        num_scalar_prefetch=1,
        grid=(b, nblk, N_GROUPS),
        in_specs=[xs, xs, pl.BlockSpec((1, bl, LANES), lambda i, j, g, c: (i, j, 0)),
                  pl.BlockSpec((1, 1, bl), lambda i, j, g, c: (i, 0, j)), full(tri),
                  pl.BlockSpec((1, 6, d), lambda i, j, g, c: (i, 0, 0)),
                  pl.BlockSpec((1, 6, d), lambda i, j, g, c: (0, 0, 0)),
                  pl.BlockSpec((epg, d, ff), lambda i, j, g, c: (g, 0, 0)),
                  pl.BlockSpec((epg, d, ff), lambda i, j, g, c: (g, 0, 0)),
                  pl.BlockSpec((epg, ff, d), lambda i, j, g, c: (g, 0, 0)),
                  full(ws[3]), full(ws[4]), full(ws[5])],
        out_specs=xs,
        scratch_shapes=[pltpu.VMEM((bl, d), F32), pltpu.VMEM((HALO, bl), F32), pltpu.VMEM((bl, LANES), F32)],
    )
    return pl.pallas_call(
        functools.partial(_experts_kernel, bl=bl, n_lat=n_lat),
        grid_spec=grid_spec,
        out_shape=jax.ShapeDtypeStruct((b, lt, d), F32),
        compiler_params=_params("parallel", "parallel", "arbitrary"),
        name="moe_experts",
    )(counts, x, t, comb, grp, tri, mod_l, mod_c, *ws)


def _rope_tables(seq, n_ctx, rope_dims, lane_offset, repeat):
    half = rope_dims // 4
    t = jnp.arange(seq)
    rowp = (t // GRID_W).astype(F32)
    colp = (t % GRID_W).astype(F32)
    inv = ROPE_THETA ** (-jnp.arange(half, dtype=F32) / half)
    ar, ac = rowp[:, None] * inv[None, :], colp[:, None] * inv[None, :]
    cos = jnp.concatenate([jnp.cos(ar), jnp.cos(ar), jnp.cos(ac), jnp.cos(ac)], axis=1)
    sin = jnp.concatenate([-jnp.sin(ar), jnp.sin(ar), -jnp.sin(ac), jnp.sin(ac)], axis=1)
    width = LANES // repeat
    padl, padr = lane_offset, width - lane_offset - rope_dims
    cos = jnp.pad(cos, ((0, n_ctx), (0, 0)), constant_values=1.0)
    sin = jnp.pad(sin, ((0, n_ctx), (0, 0)))
    cos = jnp.pad(cos, ((0, 0), (padl, padr)), constant_values=1.0)
    sin = jnp.pad(sin, ((0, 0), (padl, padr)))
    return jnp.tile(cos, (1, repeat)), jnp.tile(sin, (1, repeat))


def kernel(x, c, ctx, c_ctx, ada_w, ada_b, norm1_g, norm2_g, ab_w_in, ab_w_out, rw_mu, rw_w0, rw_w2, rw_a0, rw_a2, rw_k_k, rw_k_a, rw_r_k, rw_g2, rw_ln_w, rw_ln_b, mla_g_qa, mla_w_q_up, mla_g_kva, mla_w_kv_up, mla_g_q, mla_g_k, gqa_w_in, gqa_w_out, gqa_g_q, gqa_g_k, router_w, router_b, moe_w_gate, moe_w_up, moe_w_down, shared_w_gate, shared_w_up, shared_w_down):
    b, n_lat, d = x.shape
    n_ctx = ctx.shape[1]
    depth = ada_w.shape[0]
    xs = jnp.concatenate([x, ctx], axis=1)

    cvec = jnp.concatenate([c, c_ctx[None, :]], axis=0)
    cvec = jnp.pad(cvec, ((0, 8 - (b + 1) % 8), (0, 0))) if (b + 1) % 8 else cvec
    mods = _ada_mods(cvec, ada_w, ada_b).reshape(depth, -1, 6, d)

    mla_cos, mla_sin = _rope_tables(n_lat, n_ctx, MLA_ROPE, MLA_NOPE, 1)
    gqa_cos, gqa_sin = _rope_tables(n_lat, n_ctx, GQA_HEAD_DIM, 0, 2)

    for l in range(depth):
        i = l // 2
        mod_l, mod_c = mods[l, :b], mods[l, b:b + 1]
        rows = n_lat if l == depth - 1 else n_lat + n_ctx
        if l % 2 == 0:
            w_in = ab_w_in[i]
            w_rw = w_in[:, :RW_COLS].astype(BF16)
            w_m = w_in[:, RW_COLS:]
            w_mla = jnp.concatenate(
                [w_m[:, :MLA_Q_RANK + MLA_KV_RANK],
                 jnp.pad(w_m[:, MLA_Q_RANK + MLA_KV_RANK:], ((0, 0), (MLA_NOPE, LANES - MLA_NOPE - MLA_ROPE)))],
                axis=1).astype(BF16)
            p_rw, p_mla = _ln_mod_mm(xs, norm1_g[l], mod_l, mod_c, [w_rw, w_mla], n_lat)
            r, v, na, lw, kd, bd, gate, bonus = _rwkv_prep(
                p_rw, n_lat, rw_mu[i], rw_w0[i], rw_w2[i], rw_a0[i], rw_a2[i], rw_k_k[i], rw_k_a[i],
                rw_r_k[i].reshape(-1), rw_g2[i])
            y = _wkv_scan(r, v, na, lw, kd, bd, n_lat)
            q, kt, vm = _mla_qkv(p_mla, mla_g_qa[i], mla_w_q_up[i], mla_g_kva[i], mla_w_kv_up[i],
                                 mla_g_q[i], mla_g_k[i], mla_cos, mla_sin)
            o = _attention(q, kt, vm, n_lat)
            xs = _ab_out(xs, mod_l, mod_c, y, bonus, gate, o, rw_ln_w[i], rw_ln_b[i], ab_w_out[i], n_lat, rows)
        else:
            (p,) = _ln_mod_mm(xs, norm1_g[l], mod_l, mod_c, [gqa_w_in[i].astype(BF16)], n_lat)
            q, kt, vg = _gqa_qkv(p, gqa_g_q[i], gqa_g_k[i], gqa_cos, gqa_sin)
            o = _attention(q, kt, vg, n_lat)
            xs = _gqa_out(xs, mod_l, mod_c, o, gqa_w_out[i], n_lat, rows)
        t, comb, grp = _moe_route(xs, norm2_g[l], mod_l, mod_c, router_w, router_b, n_lat)
        xs = _moe_experts(xs, t, comb, grp, mod_l, mod_c, moe_w_gate[l], moe_w_up[l], moe_w_down[l],
                          shared_w_gate[l], shared_w_up[l], shared_w_down[l], n_lat)
    return xs[:, :n_lat]
```

```python
import functools
import math

import jax
import jax.numpy as jnp
from jax import lax
from jax.experimental import pallas as pl
from jax.experimental.pallas import tpu as pltpu

F32 = jnp.float32
BF16 = jnp.bfloat16
HIGHEST = lax.Precision.HIGHEST

GRID_W = 64
RW_HEADS = 8
RW_HEAD_DIM = 64
RW_DIM = RW_HEADS * RW_HEAD_DIM
DECAY_LORA = 64
ICLR_LORA = 64
GATE_LORA = 128
RW_COLS = 3 * RW_DIM + 2 * DECAY_LORA + 2 * ICLR_LORA + GATE_LORA
LN_X_EPS = 64e-5
MLA_HEADS = 8
MLA_NOPE = 64
MLA_ROPE = 32
MLA_V = 64
MLA_QK = MLA_NOPE + MLA_ROPE
MLA_Q_RANK = 384
MLA_KV_RANK = 256
GQA_HEADS = 16
GQA_KV_HEADS = 4
GQA_HEAD_DIM = 64
N_EXPERTS = 16
N_GROUPS = 4
EXPERTS_PER_GROUP = N_EXPERTS // N_GROUPS
ROPE_THETA = 10000.0
NORM_EPS = 1e-6

LANES = 128
V7X_VMEM_BYTES = 64 * 1024 * 1024
VMEM_LIMIT = V7X_VMEM_BYTES - 8 * 1024 * 1024

TOKEN_BLOCK = 256
HALO = 8
SCAN_CHUNK = 64
SCAN_INTERLEAVE = 16
SCAN_CHUNKS_PER_STEP = 2
ATTN_ROWS = 1024
ATTN_KEYS = 2816
MOE_ROWS = 768
MOE_CAP = 256
LOG2E = math.log2(math.e)


def _row_block(lt, limit):
    return max(r for r in range(TOKEN_BLOCK, limit + 1, TOKEN_BLOCK) if lt % r == 0)


def _key_chunk(lt):
    return max(c for c in range(TOKEN_BLOCK, ATTN_KEYS + 1, TOKEN_BLOCK) if lt % c == 0)


def _kt_spec(heads, dk, bl, bk):
    per = bk // bl
    return pl.BlockSpec((1, heads, 1, dk, bl), lambda i, j: (i, 0, j // per, 0, j % per))


def _params(*sem):
    return pltpu.CompilerParams(dimension_semantics=sem, vmem_limit_bytes=VMEM_LIMIT)


def _bdot(a, b):
    return jnp.dot(a.astype(BF16), b.astype(BF16), preferred_element_type=F32)


def _batched(a, b, ca, cb):
    return lax.dot_general(a.astype(BF16), b.astype(BF16), (((ca,), (cb,)), ((0,), (0,))),
                           preferred_element_type=F32)


def _bmm(a, b):
    return _batched(a, b, 2, 1)


def _bmm_nt(a, b):
    return _batched(a, b, 2, 2)


def _bmm_tn(a, b):
    return _batched(a, b, 1, 1)


def _sigmoid(x):
    return 1.0 / (1.0 + jnp.exp(-x))


def _silu(x):
    return x * _sigmoid(x)


def _softplus(x):
    return jnp.maximum(x, 0.0) + jnp.log(1.0 + jnp.exp(-jnp.abs(x)))


def _rms(x):
    return x * lax.rsqrt(jnp.mean(x * x, axis=-1, keepdims=True) + NORM_EPS)


def _row_mods(ml_ref, mc_ref, first_row, rows, n_lat, idx):
    r = first_row + lax.broadcasted_iota(jnp.int32, (rows, 1), 0)
    return jnp.where(r >= n_lat, mc_ref[0, idx:idx + 1, :], ml_ref[0, idx:idx + 1, :])


def _ada_kernel(c_ref, w_ref, b_ref, o_ref):
    s = _silu(c_ref[...])
    o_ref[0] = jnp.dot(s, w_ref[0], preferred_element_type=F32, precision=HIGHEST) + b_ref[0]


def _ada_mods(cvec, ada_w, ada_b):
    depth, d, n = ada_w.shape
    rows = cvec.shape[0]
    bn = n // 4
    return pl.pallas_call(
        _ada_kernel,
        grid=(depth, n // bn),
        in_specs=[pl.BlockSpec((rows, d), lambda l, j: (0, 0)),
                  pl.BlockSpec((1, d, bn), lambda l, j: (l, 0, j)),
                  pl.BlockSpec((1, 1, bn), lambda l, j: (l, 0, j))],
        out_specs=pl.BlockSpec((1, rows, bn), lambda l, j: (l, 0, j)),
        out_shape=jax.ShapeDtypeStruct((depth, rows, n), F32),
        compiler_params=_params("parallel", "parallel"),
        name="ada_mods",
    )(cvec, ada_w, ada_b.reshape(depth, 1, n))


def _ln_mod(x_ref, g_ref, ml_ref, mc_ref, bl, n_lat):
    first = pl.program_id(1) * bl
    shift = _row_mods(ml_ref, mc_ref, first, bl, n_lat, 0)
    scale = _row_mods(ml_ref, mc_ref, first, bl, n_lat, 1)
    h = (_rms(x_ref[0]) * g_ref[...]) * (1.0 + scale) + shift
    return h.astype(BF16)


def _ln_mod_specs(d, bl):
    return [pl.BlockSpec((1, bl, d), lambda i, j: (i, j, 0)), pl.BlockSpec((1, d), lambda i, j: (0, 0)),
            pl.BlockSpec((1, 6, d), lambda i, j: (i, 0, 0)), pl.BlockSpec((1, 6, d), lambda i, j: (0, 0, 0))]


def _ln_mod_mm_kernel(x_ref, g_ref, ml_ref, mc_ref, *refs, n_w, bl, n_lat):
    w_refs, o_refs = refs[:n_w], refs[n_w:]
    hb = _ln_mod(x_ref, g_ref, ml_ref, mc_ref, bl, n_lat)
    for w_ref, o_ref in zip(w_refs, o_refs):
        o_ref[0] = jnp.dot(hb, w_ref[...], preferred_element_type=F32).astype(o_ref.dtype)


def _ln_mod_mm(x, g, mod_l, mod_c, ws, n_lat):
    b, lt, d = x.shape
    bl = TOKEN_BLOCK
    kern = functools.partial(_ln_mod_mm_kernel, n_w=len(ws), bl=bl, n_lat=n_lat)
    return pl.pallas_call(
        kern,
        grid=(b, lt // bl),
        in_specs=[pl.BlockSpec((1, bl, d), lambda i, j: (i, j, 0)),
                  pl.BlockSpec((1, d), lambda i, j: (0, 0)),
                  pl.BlockSpec((1, 6, d), lambda i, j: (i, 0, 0)),
                  pl.BlockSpec((1, 6, d), lambda i, j: (0, 0, 0))]
        + [pl.BlockSpec(w.shape, lambda i, j: (0, 0)) for w in ws],
        out_specs=[pl.BlockSpec((1, bl, w.shape[1]), lambda i, j: (i, j, 0)) for w in ws],
        out_shape=[jax.ShapeDtypeStruct((b, lt, w.shape[1]), F32) for w in ws],
        compiler_params=_params("parallel", "parallel"),
        name="ln_mod_mm",
    )(x, g.reshape(1, d), mod_l, mod_c, *ws)


def _rwkv_prep_kernel(p_ref, prev_ref, next_ref, mu_ref, w0_ref, w2_ref, a0_ref, a2_ref, kk_ref,
                      ka_ref, rk_ref, g2_ref, r_ref, v_ref, na_ref, lw_ref, kd_ref, bd_ref,
                      gate_ref, bonus_ref, *, bl, ctx_block):
    p = p_ref[0]
    j = pl.program_id(1)
    starts = jnp.logical_or(j == 0, j == ctx_block)
    ends = jnp.logical_or(j == ctx_block - 1, j == pl.num_programs(1) - 1)
    before = jnp.where(starts, 0.0, prev_ref[0, HALO - 1:HALO, :])
    after = jnp.where(ends, 0.0, next_ref[0, 0:1, :])
    row = lax.broadcasted_iota(jnp.int32, p.shape, 0)
    prev = jnp.where(row == 0, before, pltpu.roll(p, 1, 0))
    nxt = jnp.where(row == bl - 1, after, pltpu.roll(p, bl - 1, 0))
    pm = p + mu_ref[...] * (0.5 * (prev + nxt) - p)

    o = 3 * RW_DIM
    r = pm[:, 0:RW_DIM]
    k = pm[:, RW_DIM:2 * RW_DIM]
    v = pm[:, 2 * RW_DIM:o]
    w1 = (pm[:, o:o + DECAY_LORA], pm[:, o + DECAY_LORA:o + 2 * DECAY_LORA])
    o += 2 * DECAY_LORA
    a1 = (pm[:, o:o + ICLR_LORA], pm[:, o + ICLR_LORA:o + 2 * ICLR_LORA])
    o += 2 * ICLR_LORA
    g1 = pm[:, o:o + GATE_LORA]

    kk = k * kk_ref[...]
    lw, kd, ad = [], [], []
    for d in range(2):
        z = w0_ref[d:d + 1, :] + _bdot(jnp.tanh(w1[d]), w2_ref[d])
        lw.append(-jnp.exp(-_softplus(-z) - 0.5))
        a = _sigmoid(a0_ref[d:d + 1, :] + _bdot(a1[d], a2_ref[d]))
        ad.append(a)
        kd.append(k * (1.0 + (a - 1.0) * ka_ref[...]))
    gate = _bdot(_sigmoid(g1), g2_ref[...])
    rkk = r * (kd[0] + kd[1]) * rk_ref[...]

    for h in range(RW_HEADS):
        sl = slice(h * RW_HEAD_DIM, (h + 1) * RW_HEAD_DIM)
        kk_h = kk[:, sl]
        kk_h = kk_h * lax.rsqrt(jnp.sum(kk_h * kk_h, axis=-1, keepdims=True) + 1e-12)
        v_h = v[:, sl]
        r_ref[0, h] = r[:, sl]
        v_ref[0, h] = v_h
        na_ref[0, h] = -kk_h
        for d in range(2):
            lw_ref[d, 0, h] = lw[d][:, sl]
            kd_ref[d, 0, h] = kd[d][:, sl]
            bd_ref[d, 0, h] = kk_h * ad[d][:, sl]
        gate_ref[0, h] = gate[:, sl]
        bonus_ref[0, h] = jnp.sum(rkk[:, sl], axis=-1, keepdims=True) * v_h


def _rwkv_prep(p, n_lat, mu, w0, w2, a0, a2, k_k, k_a, r_k, g2):
    b, lt, cols = p.shape
    bl = TOKEN_BLOCK
    nblk = lt // bl
    per = bl // HALO
    last_slab = lt // HALO - 1
    prev_spec = pl.BlockSpec((1, HALO, cols), lambda i, j: (i, jnp.maximum(j * per - 1, 0), 0))
    next_spec = pl.BlockSpec((1, HALO, cols), lambda i, j: (i, jnp.minimum((j + 1) * per, last_slab), 0))

    hm = jax.ShapeDtypeStruct((b, RW_HEADS, lt, RW_HEAD_DIM), F32)
    hm2 = jax.ShapeDtypeStruct((2, b, RW_HEADS, lt, RW_HEAD_DIM), F32)
    hm_spec = pl.BlockSpec((1, RW_HEADS, bl, RW_HEAD_DIM), lambda i, j: (i, 0, j, 0))
    hm2_spec = pl.BlockSpec((2, 1, RW_HEADS, bl, RW_HEAD_DIM), lambda i, j: (0, i, 0, j, 0))
    full = lambda a: pl.BlockSpec(a.shape, lambda i, j: (0,) * a.ndim)
    consts = [mu.reshape(1, cols), w0, w2.astype(BF16), a0, a2.astype(BF16), k_k.reshape(1, RW_DIM),
              k_a.reshape(1, RW_DIM), r_k.reshape(1, RW_DIM), g2.astype(BF16)]
    return pl.pallas_call(
        functools.partial(_rwkv_prep_kernel, bl=bl, ctx_block=n_lat // bl),
        grid=(b, nblk),
        in_specs=[pl.BlockSpec((1, bl, cols), lambda i, j: (i, j, 0)), prev_spec, next_spec]
        + [full(a) for a in consts],
        out_specs=[hm_spec, hm_spec, hm_spec, hm2_spec, hm2_spec, hm2_spec, hm_spec, hm_spec],
        out_shape=[hm, hm, hm, hm2, hm2, hm2, hm, hm],
        compiler_params=_params("parallel", "parallel"),
        name="rwkv_prep",
    )(p, p, p, *consts)


def _wkv_kernel(r_ref, v_ref, a_ref, lw_ref, k_ref, b_ref, y_ref, st_ref, *, cn, chains):
    d = pl.program_id(0)

    @pl.when(pl.program_id(1) == 0)
    def _():
        st_ref[...] = jnp.zeros_like(st_ref)

    row = lax.broadcasted_iota(jnp.int32, (cn, cn), 0)
    col = lax.broadcasted_iota(jnp.int32, (cn, cn), 1)
    lead = (row - col) * jnp.where(d == 0, 1, -1)
    upto = lead >= 0
    row2 = lax.broadcasted_iota(jnp.int32, (2 * cn, 2 * cn), 0)
    col2 = lax.broadcasted_iota(jnp.int32, (2 * cn, 2 * cn), 1)
    lead2 = (row2 % cn - col2 % cn) * jnp.where(d == 0, 1, -1)
    mask4 = lead2 >= jnp.where(row2 < cn, 1, 0)
    eye = (col == row).astype(F32)
    cum_mat = upto.astype(F32)
    hd = r_ref.shape[-1]
    eye_h = (lax.broadcasted_iota(jnp.int32, (hd, hd), 0)
             == lax.broadcasted_iota(jnp.int32, (hd, hd), 1)).astype(F32)
    n_double = int(math.log2(cn)) - 1

    nb = SCAN_INTERLEAVE
    cum_b = jnp.broadcast_to(cum_mat, (nb, cn, cn))

    def chunk(sl, rows, st):
        r, v, a = r_ref[sl, rows], v_ref[sl, rows], a_ref[sl, rows]
        lw, k, b = lw_ref[0, sl, rows], k_ref[0, sl, rows], b_ref[0, sl, rows]
        lw_hi = lw.astype(BF16)
        lw_lo = (lw - lw_hi.astype(F32)).astype(BF16)
        cum2 = _bmm(cum_b, jnp.concatenate([lw_hi, lw_lo], axis=-1))
        cum = cum2[..., :hd] + cum2[..., hd:]
        tot = jnp.sum(lw, axis=1, keepdims=True)
        at = a * jnp.exp(cum - lw)
        rt = r * jnp.exp(cum)
        einv = jnp.exp(-cum)
        bt, kt = b * einv, k * einv
        eend = jnp.exp(tot - cum)
        bh, kh = b * eend, k * eend

        amat = _bmm_nt(jnp.concatenate([at, rt], axis=1), jnp.concatenate([bt, kt], axis=1))
        amat = jnp.where(mask4, amat, 0.0)
        a_ab = amat[:, :cn, :cn]

        t = eye + a_ab
        x = _bmm(a_ab, a_ab)
        for _ in range(n_double - 1):
            tx = _bmm(jnp.concatenate([t, x], axis=1), x)
            t = t + tx[:, :cn]
            x = tx[:, cn:]
        t = t + _bmm(t, x)

        zero = jnp.zeros_like(v)
        av = _bmm(amat[:, :cn], jnp.concatenate([zero, v], axis=1))
        wu = _bmm(t, jnp.concatenate([at, av], axis=-1))
        rhs = jnp.concatenate([wu, jnp.concatenate([zero, v], axis=-1)], axis=1)
        qy = _bmm(amat[:, cn:], rhs)
        mn = _bmm_tn(jnp.concatenate([bh, kh], axis=1), rhs)
        qe = rt + qy[..., :hd]
        mc = eye_h * jnp.exp(tot) + mn[..., :hd]

        out = _bmm(jnp.concatenate([qe, mc], axis=1), st)
        y_ref[0, sl, rows] = out[:, :cn] + qy[..., hd:]
        return out[:, cn:] + mn[..., hd:]

    n_sub = r_ref.shape[1] // cn

    def group(g, carry):
        sl = pl.ds(g * nb, nb)
        st = st_ref[sl]
        for s in range(n_sub):
            sub = jnp.where(d == 0, s, n_sub - 1 - s)
            st = chunk(sl, pl.ds(pl.multiple_of(sub * cn, cn), cn), st)
        st_ref[sl] = st
        return carry

    lax.fori_loop(0, chains // SCAN_INTERLEAVE, group, 0)


def _wkv_scan(r, v, na, lw, kd, bd, n_lat):
    b, h, lt, hd = r.shape
    cn = SCAN_CHUNK
    chains = b * h
    rows = cn * SCAN_CHUNKS_PER_STEP
    assert n_lat % rows == 0 and lt % rows == 0
    n_all, n_l = lt // rows, n_lat // rows
    n_c = n_all - n_l

    def chunk(d, i):
        fwd = jnp.where(i < n_c, n_l + i, i - n_c)
        rev = jnp.where(i < n_c, n_all - 1 - i, n_l - 1 - (i - n_c))
        return jnp.where(d == 0, fwd, rev)

    shared = pl.BlockSpec((chains, rows, hd), lambda d, i: (0, chunk(d, i), 0))
    per_dir = pl.BlockSpec((1, chains, rows, hd), lambda d, i: (d, 0, chunk(d, i), 0))
    flat = lambda t: t.reshape(t.shape[:-4] + (chains, lt, hd))
    y = pl.pallas_call(
        functools.partial(_wkv_kernel, cn=cn, chains=chains),
        grid=(2, n_all),
        in_specs=[shared, shared, shared, per_dir, per_dir, per_dir],
        out_specs=per_dir,
        out_shape=jax.ShapeDtypeStruct((2, chains, lt, hd), F32),
        scratch_shapes=[pltpu.VMEM((chains, hd, hd), F32)],
        compiler_params=_params("parallel", "arbitrary"),
        name="wkv_scan",
    )(flat(r), flat(v), flat(na), flat(lw), flat(kd), flat(bd))
    return y.reshape(2, b, h, lt, hd)


def _rope(x, cos, sin, half):
    lane = lax.broadcasted_iota(jnp.int32, x.shape, 1)
    width = x.shape[1]
    partner = jnp.where(lane % (2 * half) < half, pltpu.roll(x, width - half, 1), pltpu.roll(x, half, 1))
    return x * cos + partner * sin


def _mla_qkv_kernel(p_ref, gqa_ref, gkva_ref, wq_ref, wk_ref, wv_ref, gq_ref, gk_ref, cos_ref, sin_ref,
                    q_ref, kt_ref, v_ref, *, scale):
    p = p_ref[0]
    q_a = p[:, 0:MLA_Q_RANK]
    kv_a = p[:, MLA_Q_RANK:MLA_Q_RANK + MLA_KV_RANK]
    k_rope = p[:, MLA_Q_RANK + MLA_KV_RANK:MLA_Q_RANK + MLA_KV_RANK + LANES]
    qa_n = (_rms(q_a) * gqa_ref[...]).astype(BF16)
    kva_n = (_rms(kv_a) * gkva_ref[...]).astype(BF16)
    q_all = jnp.dot(qa_n, wq_ref[...], preferred_element_type=F32)
    k_all = jnp.dot(kva_n, wk_ref[...], preferred_element_type=F32)
    v_all = jnp.dot(kva_n, wv_ref[...], preferred_element_type=F32)
    cos, sin = cos_ref[...], sin_ref[...]
    for h in range(MLA_HEADS):
        sl = slice(h * LANES, (h + 1) * LANES)
        q = q_all[:, sl]
        q = q * lax.rsqrt(jnp.sum(q * q, axis=-1, keepdims=True) / MLA_QK + NORM_EPS) * gq_ref[...]
        q = _rope(q, cos, sin, MLA_ROPE // 4)
        q_ref[0, h] = (q * scale).astype(q_ref.dtype)
        k = k_all[:, sl] + k_rope
        k = k * lax.rsqrt(jnp.sum(k * k, axis=-1, keepdims=True) / MLA_QK + NORM_EPS) * gk_ref[...]
        k = _rope(k, cos, sin, MLA_ROPE // 4)
        kt_ref[0, h, 0] = k.T.astype(kt_ref.dtype)
        lane = lax.broadcasted_iota(jnp.int32, q.shape, 1)
        v_ref[0, h] = jnp.where(lane < MLA_V, v_all[:, sl], 1.0).astype(v_ref.dtype)


def _pad_heads(w, heads, width):
    k = w.shape[0]
    w = w.reshape(k, heads, width)
    return jnp.pad(w, ((0, 0), (0, 0), (0, LANES - width))).reshape(k, heads * LANES)


def _mla_qkv(p, g_qa, w_q_up, g_kva, w_kv_up, g_q, g_k, cos, sin):
    b, lt, cols = p.shape
    bl = TOKEN_BLOCK
    bk = _key_chunk(lt)
    wq = _pad_heads(w_q_up, MLA_HEADS, MLA_QK).astype(BF16)
    wkv = w_kv_up.reshape(MLA_KV_RANK, MLA_HEADS, MLA_NOPE + MLA_V)
    wk = _pad_heads(wkv[:, :, :MLA_NOPE].reshape(MLA_KV_RANK, -1), MLA_HEADS, MLA_NOPE).astype(BF16)
    wv = _pad_heads(wkv[:, :, MLA_NOPE:].reshape(MLA_KV_RANK, -1), MLA_HEADS, MLA_V).astype(BF16)
    pad = lambda g: jnp.pad(g, (0, LANES - MLA_QK)).reshape(1, LANES)
    consts = [g_qa.reshape(1, -1), g_kva.reshape(1, -1), wq, wk, wv, pad(g_q), pad(g_k)]
    full = lambda a: pl.BlockSpec(a.shape, lambda i, j: (0,) * a.ndim)
    tab = pl.BlockSpec((bl, LANES), lambda i, j: (j, 0))
    return pl.pallas_call(
        functools.partial(_mla_qkv_kernel, scale=MLA_QK ** -0.5 * LOG2E),
        grid=(b, lt // bl),
        in_specs=[pl.BlockSpec((1, bl, cols), lambda i, j: (i, j, 0))] + [full(a) for a in consts] + [tab, tab],
        out_specs=[pl.BlockSpec((1, MLA_HEADS, bl, LANES), lambda i, j: (i, 0, j, 0)),
                   _kt_spec(MLA_HEADS, LANES, bl, bk),
                   pl.BlockSpec((1, MLA_HEADS, bl, LANES), lambda i, j: (i, 0, j, 0))],
        out_shape=[jax.ShapeDtypeStruct((b, MLA_HEADS, lt, LANES), BF16),
                   jax.ShapeDtypeStruct((b, MLA_HEADS, lt // bk, LANES, bk), BF16),
                   jax.ShapeDtypeStruct((b, MLA_HEADS, lt, LANES), BF16)],
        compiler_params=_params("parallel", "parallel"),
        name="mla_qkv",
    )(p, *consts, cos, sin)


def _pair_norm_rope(x, g, cos, sin):
    lane = lax.broadcasted_iota(jnp.int32, x.shape, 1)
    lo = lane < GQA_HEAD_DIM
    xx = x * x
    ss_lo = jnp.sum(jnp.where(lo, xx, 0.0), axis=-1, keepdims=True)
    ss_hi = jnp.sum(jnp.where(lo, 0.0, xx), axis=-1, keepdims=True)
    rs = lax.rsqrt(jnp.where(lo, ss_lo, ss_hi) / GQA_HEAD_DIM + NORM_EPS)
    return _rope(x * rs * g, cos, sin, GQA_HEAD_DIM // 4)


def _gqa_qkv_kernel(x_ref, g_ref, ml_ref, mc_ref, win_ref, gq_ref, gk_ref, cos_ref, sin_ref, q_ref, kt_ref,
                    v_ref, *, scale, bl, n_lat):
    p = jnp.dot(_ln_mod(x_ref, g_ref, ml_ref, mc_ref, bl, n_lat), win_ref[...], preferred_element_type=F32)
    cos, sin = cos_ref[...], sin_ref[...]
    qw = GQA_HEADS * GQA_HEAD_DIM
    kw = GQA_KV_HEADS * GQA_HEAD_DIM
    for j in range(GQA_HEADS // 2):
        q = _pair_norm_rope(p[:, j * LANES:(j + 1) * LANES], gq_ref[...], cos, sin) * scale
        q_ref[0, 2 * j] = q[:, :GQA_HEAD_DIM].astype(q_ref.dtype)
        q_ref[0, 2 * j + 1] = q[:, GQA_HEAD_DIM:].astype(q_ref.dtype)
    for j in range(GQA_KV_HEADS // 2):
        k = _pair_norm_rope(p[:, qw + j * LANES:qw + (j + 1) * LANES], gk_ref[...], cos, sin)
        kt = k.T
        kt_ref[0, 2 * j, 0] = kt[:GQA_HEAD_DIM].astype(kt_ref.dtype)
        kt_ref[0, 2 * j + 1, 0] = kt[GQA_HEAD_DIM:].astype(kt_ref.dtype)
        v = p[:, qw + kw + j * LANES:qw + kw + (j + 1) * LANES]
        lo = lax.broadcasted_iota(jnp.int32, v.shape, 1) < GQA_HEAD_DIM
        v_ref[0, 2 * j] = jnp.where(lo, v, 1.0).astype(v_ref.dtype)
        v_ref[0, 2 * j + 1] = jnp.where(lo, pltpu.roll(v, GQA_HEAD_DIM, 1), 1.0).astype(v_ref.dtype)


def _gqa_qkv(x, g, mod_l, mod_c, w_in, n_lat, g_q, g_k, cos, sin):
    b, lt, d = x.shape
    bl = TOKEN_BLOCK
    bk = _key_chunk(lt)
    hd = GQA_HEAD_DIM
    two = lambda t: jnp.concatenate([t, t]).reshape(1, LANES)
    tab = pl.BlockSpec((bl, LANES), lambda i, j: (j, 0))
    vec = pl.BlockSpec((1, LANES), lambda i, j: (0, 0))
    return pl.pallas_call(
        functools.partial(_gqa_qkv_kernel, scale=hd ** -0.5 * LOG2E, bl=bl, n_lat=n_lat),
        grid=(b, lt // bl),
        in_specs=_ln_mod_specs(d, bl) + [pl.BlockSpec(w_in.shape, lambda i, j: (0, 0)), vec, vec, tab, tab],
        out_specs=[pl.BlockSpec((1, GQA_HEADS, bl, hd), lambda i, j: (i, 0, j, 0)),
                   _kt_spec(GQA_KV_HEADS, hd, bl, bk),
                   pl.BlockSpec((1, GQA_KV_HEADS, bl, LANES), lambda i, j: (i, 0, j, 0))],
        out_shape=[jax.ShapeDtypeStruct((b, GQA_HEADS, lt, hd), BF16),
                   jax.ShapeDtypeStruct((b, GQA_KV_HEADS, lt // bk, hd, bk), BF16),
                   jax.ShapeDtypeStruct((b, GQA_KV_HEADS, lt, LANES), BF16)],
        compiler_params=_params("parallel", "parallel"),
        name="gqa_qkv",
    )(x, g.reshape(1, d), mod_l, mod_c, w_in, two(g_q), two(g_k), cos, sin)


def _attn_kernel(q_ref, kt_ref, v_ref, o_ref, m_scr, acc_scr, *, groups, bq, n_chunks):
    rows = groups * bq
    q = q_ref[0, 0].reshape(rows, q_ref.shape[-1])
    bk = kt_ref.shape[-1]
    dv = o_ref.shape[-1]
    m_scr[...] = jnp.full_like(m_scr, -jnp.inf)
    acc_scr[...] = jnp.zeros_like(acc_scr)

    def step(j, carry):
        s = jnp.dot(q, kt_ref[0, 0, j], preferred_element_type=F32)
        m_prev = m_scr[...]
        m_new = jnp.maximum(m_prev, jnp.max(s, axis=-1, keepdims=True))
        alpha = jnp.exp2(m_prev - m_new)
        p = jnp.exp2(s - pltpu.repeat(m_new, bk // LANES, axis=1))
        vv = v_ref[0, 0, j * bk:(j + 1) * bk, :]
        acc_scr[...] = alpha * acc_scr[...] + jnp.dot(p.astype(BF16), vv, preferred_element_type=F32)
        m_scr[...] = m_new
        return carry

    for j in range(n_chunks):
        step(j, 0)
    acc = acc_scr[...]
    o = acc / pltpu.roll(acc, LANES - dv, 1)
    o_ref[0, 0] = o[:, :dv].reshape(groups, bq, dv).astype(o_ref.dtype)


def _attention_call(q, q_start, lq, ktc, v1):
    b, hq, lt, dk = q.shape
    hk, n_chunks, bk = ktc.shape[1], ktc.shape[2], ktc.shape[-1]
    lk = n_chunks * bk
    dv = LANES // 2
    groups = hq // hk
    bq = min(ATTN_ROWS // groups, lq)
    rows = groups * bq
    first = q_start // bq
    q5 = q.reshape(b, hk, groups, lt, dk)
    out = pl.pallas_call(
        functools.partial(_attn_kernel, groups=groups, bq=bq, n_chunks=n_chunks),
        grid=(b, hk, lq // bq),
        in_specs=[pl.BlockSpec((1, 1, groups, bq, dk), lambda i, h, j: (i, h, 0, j + first, 0)),
                  pl.BlockSpec((1, 1, n_chunks, dk, bk), lambda i, h, j: (i, h, 0, 0, 0)),
                  pl.BlockSpec((1, 1, lk, LANES), lambda i, h, j: (i, h, 0, 0))],
        out_specs=pl.BlockSpec((1, 1, groups, bq, dv), lambda i, h, j: (i, h, 0, j, 0)),
        out_shape=jax.ShapeDtypeStruct((b, hk, groups, lq, dv), BF16),
        scratch_shapes=[pltpu.VMEM((rows, LANES), F32), pltpu.VMEM((rows, LANES), F32)],
        compiler_params=_params("parallel", "parallel", "arbitrary"),
        name="attention",
    )(q5, ktc, v1)
    return out.reshape(b, hq, lq, dv)


def _attention(q, ktc, v1, n_lat):
    lt = q.shape[2]
    n_ctx = lt - n_lat
    o_l = _attention_call(q, 0, n_lat, ktc, v1)
    o_c = _attention_call(q, n_lat, n_ctx, ktc[:, :, -1:, :, -n_ctx:], v1[:, :, n_lat:])
    return jnp.concatenate([o_l, o_c], axis=2)


def _ab_out_kernel(x_ref, ml_ref, mc_ref, y_ref, bonus_ref, gate_ref, o_ref, lnw_ref, lnb_ref, wrw_ref,
                   wmla_ref, out_ref, *, bl, n_lat):
    rw = []
    for h in range(RW_HEADS):
        y = y_ref[0, 0, h] + y_ref[1, 0, h]
        mean = jnp.mean(y, axis=-1, keepdims=True)
        yc = y - mean
        var = jnp.mean(yc * yc, axis=-1, keepdims=True)
        yn = yc * lax.rsqrt(var + LN_X_EPS) * lnw_ref[h] + lnb_ref[h]
        rw.append((yn + bonus_ref[0, h]) * gate_ref[0, h])
    rw = jnp.concatenate(rw, axis=-1).astype(BF16)
    o = jnp.concatenate([o_ref[0, h] for h in range(MLA_HEADS)], axis=-1)
    acc = (jnp.dot(rw, wrw_ref[...], preferred_element_type=F32)
           + jnp.dot(o, wmla_ref[...], preferred_element_type=F32))
    gate = _row_mods(ml_ref, mc_ref, pl.program_id(1) * bl, bl, n_lat, 2)
    out_ref[0] = x_ref[0] + gate * acc


def _ab_out(x, mod_l, mod_c, y, bonus, gate, o, ln_w, ln_b, w_out, n_lat, lt):
    b, _, d = x.shape
    bl = TOKEN_BLOCK
    hd = RW_HEAD_DIM
    wrw = w_out[:RW_DIM].astype(BF16)
    wmla = w_out[RW_DIM:].astype(BF16)
    xs = pl.BlockSpec((1, bl, d), lambda i, j: (i, j, 0))
    hm = pl.BlockSpec((1, RW_HEADS, bl, hd), lambda i, j: (i, 0, j, 0))
    full = lambda a: pl.BlockSpec(a.shape, lambda i, j: (0,) * a.ndim)
    lnw, lnb = ln_w.reshape(RW_HEADS, 1, hd), ln_b.reshape(RW_HEADS, 1, hd)
    return pl.pallas_call(
        functools.partial(_ab_out_kernel, bl=bl, n_lat=n_lat),
        grid=(b, lt // bl),
        in_specs=[xs, pl.BlockSpec((1, 6, d), lambda i, j: (i, 0, 0)), pl.BlockSpec((1, 6, d), lambda i, j: (0, 0, 0)),
                  pl.BlockSpec((2, 1, RW_HEADS, bl, hd), lambda i, j: (0, i, 0, j, 0)), hm, hm,
                  pl.BlockSpec((1, MLA_HEADS, bl, MLA_V), lambda i, j: (i, 0, j, 0)),
                  full(lnw), full(lnb), full(wrw), full(wmla)],
        out_specs=xs,
        out_shape=jax.ShapeDtypeStruct((b, lt, d), F32),
        compiler_params=_params("parallel", "parallel"),
        name="ab_out",
    )(x, mod_l, mod_c, y, bonus, gate, o, lnw, lnb, wrw, wmla)


def _gqa_out_kernel(x_ref, ml_ref, mc_ref, o_ref, w_ref, out_ref, *, bl, n_lat):
    o = jnp.concatenate([o_ref[0, h] for h in range(GQA_HEADS)], axis=-1)
    acc = jnp.dot(o, w_ref[...], preferred_element_type=F32)
    gate = _row_mods(ml_ref, mc_ref, pl.program_id(1) * bl, bl, n_lat, 2)
    out_ref[0] = x_ref[0] + gate * acc


def _gqa_out(x, mod_l, mod_c, o, w_out, n_lat, lt):
    b, _, d = x.shape
    bl = TOKEN_BLOCK
    hd = GQA_HEAD_DIM
    w = w_out.astype(BF16)
    xs = pl.BlockSpec((1, bl, d), lambda i, j: (i, j, 0))
    return pl.pallas_call(
        functools.partial(_gqa_out_kernel, bl=bl, n_lat=n_lat),
        grid=(b, lt // bl),
        in_specs=[xs, pl.BlockSpec((1, 6, d), lambda i, j: (i, 0, 0)), pl.BlockSpec((1, 6, d), lambda i, j: (0, 0, 0)),
                  pl.BlockSpec((1, GQA_HEADS, bl, hd), lambda i, j: (i, 0, j, 0)),
                  pl.BlockSpec(w.shape, lambda i, j: (0, 0))],
        out_specs=xs,
        out_shape=jax.ShapeDtypeStruct((b, lt, d), F32),
        compiler_params=_params("parallel", "parallel"),
        name="gqa_out",
    )(x, mod_l, mod_c, o, w)


def _first_argmax(vals):
    best, idx = vals[0], jnp.zeros(vals[0].shape, jnp.int32)
    for i in range(1, len(vals)):
        better = vals[i] > best
        idx = jnp.where(better, i, idx)
        best = jnp.where(better, vals[i], best)
    return best, idx


def _pick(vals, idx):
    out = vals[0]
    for i in range(1, len(vals)):
        out = jnp.where(idx == i, vals[i], out)
    return out


def _route_kernel(x_ref, g_ref, ml_ref, mc_ref, rw_ref, rb_ref, t_ref, comb_ref, grp_ref, *, bl, n_lat):
    first = pl.program_id(1) * bl
    shift = _row_mods(ml_ref, mc_ref, first, bl, n_lat, 3)
    scale = _row_mods(ml_ref, mc_ref, first, bl, n_lat, 4)
    t = (_rms(x_ref[0]) * g_ref[...]) * (1.0 + scale) + shift
    t_ref[0] = t.astype(t_ref.dtype)
    logits = jnp.dot(t, rw_ref[...], preferred_element_type=F32, precision=HIGHEST)
    lt = logits.T
    score = [_sigmoid(lt[e:e + 1]) for e in range(N_EXPERTS)]
    biased = [score[e] + rb_ref[e:e + 1, :] for e in range(N_EXPERTS)]
    epg = EXPERTS_PER_GROUP
    group_score = []
    for g in range(N_GROUPS):
        vals = biased[g * epg:(g + 1) * epg]
        pair = [vals[i] + vals[j] for i in range(epg) for j in range(i + 1, epg)]
        group_score.append(functools.reduce(jnp.maximum, pair))
    _, grp = _first_argmax(group_score)
    in_b = [_pick([biased[g * epg + j] for g in range(N_GROUPS)], grp) for j in range(epg)]
    in_s = [_pick([score[g * epg + j] for g in range(N_GROUPS)], grp) for j in range(epg)]
    _, loc1 = _first_argmax(in_b)
    _, loc2 = _first_argmax([jnp.where(loc1 == j, -jnp.inf, in_b[j]) for j in range(epg)])
    w1, w2 = _pick(in_s, loc1), _pick(in_s, loc2)
    wsum = w1 + w2
    w1, w2 = w1 / wsum, w2 / wsum
    e1, e2 = grp * epg + loc1, grp * epg + loc2
    sub = lax.broadcasted_iota(jnp.int32, (LANES, bl), 0)
    comb = jnp.zeros((LANES, bl), F32)
    for e in range(N_EXPERTS):
        c_e = jnp.where(e1 == e, w1, 0.0) + jnp.where(e2 == e, w2, 0.0)
        c_hi = c_e.astype(BF16).astype(F32)
        comb = jnp.where(sub == e, jnp.broadcast_to(c_hi, (LANES, bl)), comb)
        comb = jnp.where(sub == N_EXPERTS + e, jnp.broadcast_to(c_e - c_hi, (LANES, bl)), comb)
    comb_ref[0] = comb.T.astype(comb_ref.dtype)
    grp_ref[0] = grp


def _moe_route(x, g, mod_l, mod_c, router_w, router_b, n_lat):
    b, lt, d = x.shape
    bl = TOKEN_BLOCK
    rw = jnp.pad(router_w, ((0, 0), (0, LANES - N_EXPERTS)))
    xs = pl.BlockSpec((1, bl, d), lambda i, j: (i, j, 0))
    return pl.pallas_call(
        functools.partial(_route_kernel, bl=bl, n_lat=n_lat),
        grid=(b, lt // bl),
        in_specs=[xs, pl.BlockSpec((1, d), lambda i, j: (0, 0)),
                  pl.BlockSpec((1, 6, d), lambda i, j: (i, 0, 0)), pl.BlockSpec((1, 6, d), lambda i, j: (0, 0, 0)),
                  pl.BlockSpec(rw.shape, lambda i, j: (0, 0)), pl.BlockSpec((N_EXPERTS, 1), lambda i, j: (0, 0))],
        out_specs=[xs, pl.BlockSpec((1, bl, LANES), lambda i, j: (i, j, 0)),
                   pl.BlockSpec((1, 1, bl), lambda i, j: (i, 0, j))],
        out_shape=[jax.ShapeDtypeStruct((b, lt, d), BF16), jax.ShapeDtypeStruct((b, lt, LANES), BF16),
                   jax.ShapeDtypeStruct((b, 1, lt), jnp.int32)],
        compiler_params=_params("parallel", "parallel"),
        name="moe_route",
    )(x, g.reshape(1, d), mod_l, mod_c, rw, router_b.reshape(N_EXPERTS, 1))


def _ffn(t, wg, wu, wd, cw):
    g = jnp.dot(t, wg, preferred_element_type=F32)
    u = jnp.dot(t, wu, preferred_element_type=F32)
    h = _silu(g) * u
    if cw is not None:
        h = h * cw
    return jnp.dot(h.astype(BF16), wd, preferred_element_type=F32)


def _experts_kernel(cnt_ref, x_ref, t_ref, comb_ref, grp_ref, tri_ref, ml_ref, mc_ref, wg_ref, wu_ref, wd_ref,
                    sg_ref, su_ref, sd_ref, out_ref, acc_ref, row_ref, col_ref, *, bl, n_lat):
    g = pl.program_id(2)
    t = t_ref[0]

    @pl.when(g == 0)
    def _():
        acc_ref[...] = _ffn(t, sg_ref[...], su_ref[...], sd_ref[...], None)
        grp = grp_ref[0]
        gid = lax.broadcasted_iota(jnp.int32, (2 * HALO, bl), 0)
        member = jnp.where(grp == gid, 1.0, 0.0)
        before = jnp.dot(member.astype(BF16), tri_ref[...], preferred_element_type=F32)
        slot = jnp.sum(member * before, axis=0, keepdims=True)
        sub = lax.broadcasted_iota(jnp.int32, (LANES, bl), 0)
        info = jnp.where(sub == 0, jnp.broadcast_to(slot, (LANES, bl)),
                         jnp.where(sub == 1, jnp.broadcast_to(grp.astype(F32), (LANES, bl)), 0.0))
        row_ref[...] = info[:HALO]
        col_ref[...] = info.T

    gf = g.astype(F32)
    slot_r, in_r = row_ref[0:1, :], jnp.where(row_ref[1:2, :] == gf, 1.0, 0.0)
    slot_c, in_c = col_ref[:, 0:1], jnp.where(col_ref[:, 1:2] == gf, 1.0, 0.0)
    count = cnt_ref[(pl.program_id(0) * pl.num_programs(1) + pl.program_id(1)) * N_GROUPS + g]
    wd = wd_ref[...]
    wd = wd.reshape(wd.shape[0] * wd.shape[1], wd.shape[2])
    lane = lax.broadcasted_iota(jnp.int32, (MOE_CAP, LANES), 1)

    def one_pass(k, carry):
        base = (k * MOE_CAP).astype(F32)
        want_r = base + lax.broadcasted_iota(jnp.int32, (MOE_CAP, bl), 0).astype(F32)
        want_c = base + lax.broadcasted_iota(jnp.int32, (bl, MOE_CAP), 1).astype(F32)
        gather = jnp.where(slot_r == want_r, in_r, 0.0).astype(BF16)
        scatter = jnp.where(slot_c == want_c, in_c, 0.0).astype(BF16)
        tg = jnp.dot(gather, t, preferred_element_type=F32).astype(BF16)
        cg = jnp.dot(gather, comb_ref[0], preferred_element_type=F32)
        hs = []
        for i in range(EXPERTS_PER_GROUP):
            e = g * EXPERTS_PER_GROUP + i
            cw = jnp.sum(jnp.where(lane % N_EXPERTS == e, cg, 0.0), axis=-1, keepdims=True)
            gate = jnp.dot(tg, wg_ref[i], preferred_element_type=F32)
            up = jnp.dot(tg, wu_ref[i], preferred_element_type=F32)
            hs.append((_silu(gate) * up * cw).astype(BF16))
        yg = jnp.dot(jnp.concatenate(hs, axis=-1), wd, preferred_element_type=F32)
        acc_ref[...] += jnp.dot(scatter, yg.astype(BF16), preferred_element_type=F32)
        return carry

    lax.fori_loop(0, (count + MOE_CAP - 1) // MOE_CAP, one_pass, 0)

    @pl.when(g == pl.num_programs(2) - 1)
    def _():
        gate = _row_mods(ml_ref, mc_ref, pl.program_id(1) * bl, bl, n_lat, 5)
        out_ref[0] = x_ref[0] + gate * acc_ref[...]


def _moe_experts(x, t, comb, grp, mod_l, mod_c, w_gate, w_up, w_down, sh_gate, sh_up, sh_down, n_lat):
    b, lt, d = x.shape
    bl = _row_block(lt, MOE_ROWS)
    nblk = lt // bl
    ff = w_gate.shape[-1]
    epg = EXPERTS_PER_GROUP
    counts = jnp.sum(grp.reshape(b, nblk, bl, 1) == jnp.arange(N_GROUPS), axis=2, dtype=jnp.int32).reshape(-1)
    tri = jnp.triu(jnp.ones((bl, bl), BF16), 1)
    xs = pl.BlockSpec((1, bl, d), lambda i, j, g, c: (i, j, 0))
    full = lambda a: pl.BlockSpec(a.shape, lambda i, j, g, c: (0,) * a.ndim)
    ws = [w_gate.astype(BF16), w_up.astype(BF16), w_down.astype(BF16),
          sh_gate.astype(BF16), sh_up.astype(BF16), sh_down.astype(BF16)]
    grid_spec = pltpu.PrefetchScalarGridSpec(
        num_scalar_prefetch=1,
        grid=(b, nblk, N_GROUPS),
        in_specs=[xs, xs, pl.BlockSpec((1, bl, LANES), lambda i, j, g, c: (i, j, 0)),
                  pl.BlockSpec((1, 1, bl), lambda i, j, g, c: (i, 0, j)), full(tri),
                  pl.BlockSpec((1, 6, d), lambda i, j, g, c: (i, 0, 0)),
                  pl.BlockSpec((1, 6, d), lambda i, j, g, c: (0, 0, 0)),
                  pl.BlockSpec((epg, d, ff), lambda i, j, g, c: (g, 0, 0)),
                  pl.BlockSpec((epg, d, ff), lambda i, j, g, c: (g, 0, 0)),
                  pl.BlockSpec((epg, ff, d), lambda i, j, g, c: (g, 0, 0)),
                  full(ws[3]), full(ws[4]), full(ws[5])],
        out_specs=xs,
        scratch_shapes=[pltpu.VMEM((bl, d), F32), pltpu.VMEM((HALO, bl), F32), pltpu.VMEM((bl, LANES), F32)],
    )
    return pl.pallas_call(
        functools.partial(_experts_kernel, bl=bl, n_lat=n_lat),
        grid_spec=grid_spec,
        out_shape=jax.ShapeDtypeStruct((b, lt, d), F32),
        compiler_params=_params("parallel", "parallel", "arbitrary"),
        name="moe_experts",
    )(counts, x, t, comb, grp, tri, mod_l, mod_c, *ws)


def _rope_tables(seq, n_ctx, rope_dims, lane_offset, repeat):
    half = rope_dims // 4
    t = jnp.arange(seq)
    rowp = (t // GRID_W).astype(F32)
    colp = (t % GRID_W).astype(F32)
    inv = ROPE_THETA ** (-jnp.arange(half, dtype=F32) / half)
    ar, ac = rowp[:, None] * inv[None, :], colp[:, None] * inv[None, :]
    cos = jnp.concatenate([jnp.cos(ar), jnp.cos(ar), jnp.cos(ac), jnp.cos(ac)], axis=1)
    sin = jnp.concatenate([-jnp.sin(ar), jnp.sin(ar), -jnp.sin(ac), jnp.sin(ac)], axis=1)
    width = LANES // repeat
    padl, padr = lane_offset, width - lane_offset - rope_dims
    cos = jnp.pad(cos, ((0, n_ctx), (0, 0)), constant_values=1.0)
    sin = jnp.pad(sin, ((0, n_ctx), (0, 0)))
    cos = jnp.pad(cos, ((0, 0), (padl, padr)), constant_values=1.0)
    sin = jnp.pad(sin, ((0, 0), (padl, padr)))
    return jnp.tile(cos, (1, repeat)), jnp.tile(sin, (1, repeat))


def kernel(x, c, ctx, c_ctx, ada_w, ada_b, norm1_g, norm2_g, ab_w_in, ab_w_out, rw_mu, rw_w0, rw_w2, rw_a0, rw_a2, rw_k_k, rw_k_a, rw_r_k, rw_g2, rw_ln_w, rw_ln_b, mla_g_qa, mla_w_q_up, mla_g_kva, mla_w_kv_up, mla_g_q, mla_g_k, gqa_w_in, gqa_w_out, gqa_g_q, gqa_g_k, router_w, router_b, moe_w_gate, moe_w_up, moe_w_down, shared_w_gate, shared_w_up, shared_w_down):
    b, n_lat, d = x.shape
    n_ctx = ctx.shape[1]
    depth = ada_w.shape[0]
    xs = jnp.concatenate([x, ctx], axis=1)

    cvec = jnp.concatenate([c, c_ctx[None, :]], axis=0)
    cvec = jnp.pad(cvec, ((0, 8 - (b + 1) % 8), (0, 0))) if (b + 1) % 8 else cvec
    mods = _ada_mods(cvec, ada_w, ada_b).reshape(depth, -1, 6, d)

    mla_cos, mla_sin = _rope_tables(n_lat, n_ctx, MLA_ROPE, MLA_NOPE, 1)
    gqa_cos, gqa_sin = _rope_tables(n_lat, n_ctx, GQA_HEAD_DIM, 0, 2)

    for l in range(depth):
        i = l // 2
        mod_l, mod_c = mods[l, :b], mods[l, b:b + 1]
        rows = n_lat + n_ctx
        if l % 2 == 0:
            w_in = ab_w_in[i]
            w_rw = w_in[:, :RW_COLS].astype(BF16)
            w_m = w_in[:, RW_COLS:]
            w_mla = jnp.concatenate(
                [w_m[:, :MLA_Q_RANK + MLA_KV_RANK],
                 jnp.pad(w_m[:, MLA_Q_RANK + MLA_KV_RANK:], ((0, 0), (MLA_NOPE, LANES - MLA_NOPE - MLA_ROPE)))],
                axis=1).astype(BF16)
            p_rw, p_mla = _ln_mod_mm(xs, norm1_g[l], mod_l, mod_c, [w_rw, w_mla], n_lat)
            r, v, na, lw, kd, bd, gate, bonus = _rwkv_prep(
                p_rw, n_lat, rw_mu[i], rw_w0[i], rw_w2[i], rw_a0[i], rw_a2[i], rw_k_k[i], rw_k_a[i],
                rw_r_k[i].reshape(-1), rw_g2[i])
            y = _wkv_scan(r, v, na, lw, kd, bd, n_lat)
            q, kt, vm = _mla_qkv(p_mla, mla_g_qa[i], mla_w_q_up[i], mla_g_kva[i], mla_w_kv_up[i],
                                 mla_g_q[i], mla_g_k[i], mla_cos, mla_sin)
            o = _attention(q, kt, vm, n_lat)
            xs = _ab_out(xs, mod_l, mod_c, y, bonus, gate, o, rw_ln_w[i], rw_ln_b[i], ab_w_out[i], n_lat, rows)
        else:
            q, kt, vg = _gqa_qkv(xs, norm1_g[l], mod_l, mod_c, gqa_w_in[i].astype(BF16), n_lat,
                                 gqa_g_q[i], gqa_g_k[i], gqa_cos, gqa_sin)
            o = _attention(q, kt, vg, n_lat)
            xs = _gqa_out(xs, mod_l, mod_c, o, gqa_w_out[i], n_lat, rows)
        t, comb, grp = _moe_route(xs, norm2_g[l], mod_l, mod_c, router_w, router_b, n_lat)
        xs = _moe_experts(xs, t, comb, grp, mod_l, mod_c, moe_w_gate[l], moe_w_up[l], moe_w_down[l],
                          shared_w_gate[l], shared_w_up[l], shared_w_down[l], n_lat)
    return xs[:, :n_lat]
```

```python
import functools
import math

import jax
import jax.numpy as jnp
from jax import lax
from jax.experimental import pallas as pl
from jax.experimental.pallas import tpu as pltpu

F32 = jnp.float32
BF16 = jnp.bfloat16
HIGHEST = lax.Precision.HIGHEST

GRID_W = 64
RW_HEADS = 8
RW_HEAD_DIM = 64
RW_DIM = RW_HEADS * RW_HEAD_DIM
DECAY_LORA = 64
ICLR_LORA = 64
GATE_LORA = 128
RW_COLS = 3 * RW_DIM + 2 * DECAY_LORA + 2 * ICLR_LORA + GATE_LORA
LN_X_EPS = 64e-5
MLA_HEADS = 8
MLA_NOPE = 64
MLA_ROPE = 32
MLA_V = 64
MLA_QK = MLA_NOPE + MLA_ROPE
MLA_Q_RANK = 384
MLA_KV_RANK = 256
GQA_HEADS = 16
GQA_KV_HEADS = 4
GQA_HEAD_DIM = 64
N_EXPERTS = 16
N_GROUPS = 4
EXPERTS_PER_GROUP = N_EXPERTS // N_GROUPS
ROPE_THETA = 10000.0
NORM_EPS = 1e-6

LANES = 128
V7X_VMEM_BYTES = 64 * 1024 * 1024
VMEM_LIMIT = V7X_VMEM_BYTES - 8 * 1024 * 1024

TOKEN_BLOCK = 256
HALO = 8
SCAN_CHUNK = 64
SCAN_INTERLEAVE = 16
SCAN_CHUNKS_PER_STEP = 2
ATTN_ROWS = 1024
ATTN_KEYS = 2816
WIDE_ROWS = 768
MOE_CAP = 256
LOG2E = math.log2(math.e)


def _row_block(lt, limit):
    return max(r for r in range(TOKEN_BLOCK, limit + 1, TOKEN_BLOCK) if lt % r == 0)


def _key_chunk(lt):
    return max(c for c in range(TOKEN_BLOCK, ATTN_KEYS + 1, TOKEN_BLOCK) if lt % c == 0)


def _kt_spec(heads, dk, bl, bk):
    per = bk // bl
    return pl.BlockSpec((1, heads, 1, dk, bl), lambda i, j: (i, 0, j // per, 0, j % per))


def _params(*sem):
    return pltpu.CompilerParams(dimension_semantics=sem, vmem_limit_bytes=VMEM_LIMIT)


def _bdot(a, b):
    return jnp.dot(a.astype(BF16), b.astype(BF16), preferred_element_type=F32)


def _batched(a, b, ca, cb):
    return lax.dot_general(a.astype(BF16), b.astype(BF16), (((ca,), (cb,)), ((0,), (0,))),
                           preferred_element_type=F32)


def _bmm(a, b):
    return _batched(a, b, 2, 1)


def _bmm_nt(a, b):
    return _batched(a, b, 2, 2)


def _bmm_tn(a, b):
    return _batched(a, b, 1, 1)


def _sigmoid(x):
    return 1.0 / (1.0 + jnp.exp(-x))


def _silu(x):
    return x * _sigmoid(x)


def _softplus(x):
    return jnp.maximum(x, 0.0) + jnp.log(1.0 + jnp.exp(-jnp.abs(x)))


def _rms(x):
    return x * lax.rsqrt(jnp.mean(x * x, axis=-1, keepdims=True) + NORM_EPS)


def _row_mods(ml_ref, mc_ref, first_row, rows, n_lat, idx):
    r = first_row + lax.broadcasted_iota(jnp.int32, (rows, 1), 0)
    return jnp.where(r >= n_lat, mc_ref[0, idx:idx + 1, :], ml_ref[0, idx:idx + 1, :])


def _ada_kernel(c_ref, w_ref, b_ref, o_ref):
    s = _silu(c_ref[...])
    o_ref[0] = jnp.dot(s, w_ref[0], preferred_element_type=F32, precision=HIGHEST) + b_ref[0]


def _ada_mods(cvec, ada_w, ada_b):
    depth, d, n = ada_w.shape
    rows = cvec.shape[0]
    bn = n // 4
    return pl.pallas_call(
        _ada_kernel,
        grid=(depth, n // bn),
        in_specs=[pl.BlockSpec((rows, d), lambda l, j: (0, 0)),
                  pl.BlockSpec((1, d, bn), lambda l, j: (l, 0, j)),
                  pl.BlockSpec((1, 1, bn), lambda l, j: (l, 0, j))],
        out_specs=pl.BlockSpec((1, rows, bn), lambda l, j: (l, 0, j)),
        out_shape=jax.ShapeDtypeStruct((depth, rows, n), F32),
        compiler_params=_params("parallel", "parallel"),
        name="ada_mods",
    )(cvec, ada_w, ada_b.reshape(depth, 1, n))


def _ln_mod(x_ref, g_ref, ml_ref, mc_ref, bl, n_lat):
    first = pl.program_id(1) * bl
    shift = _row_mods(ml_ref, mc_ref, first, bl, n_lat, 0)
    scale = _row_mods(ml_ref, mc_ref, first, bl, n_lat, 1)
    h = (_rms(x_ref[0]) * g_ref[...]) * (1.0 + scale) + shift
    return h.astype(BF16)


def _ln_mod_specs(d, bl):
    return [pl.BlockSpec((1, bl, d), lambda i, j: (i, j, 0)), pl.BlockSpec((1, d), lambda i, j: (0, 0)),
            pl.BlockSpec((1, 6, d), lambda i, j: (i, 0, 0)), pl.BlockSpec((1, 6, d), lambda i, j: (0, 0, 0))]


def _ln_mod_mm_kernel(x_ref, g_ref, ml_ref, mc_ref, *refs, n_w, bl, n_lat):
    w_refs, o_refs = refs[:n_w], refs[n_w:]
    hb = _ln_mod(x_ref, g_ref, ml_ref, mc_ref, bl, n_lat)
    for w_ref, o_ref in zip(w_refs, o_refs):
        o_ref[0] = jnp.dot(hb, w_ref[...], preferred_element_type=F32).astype(o_ref.dtype)


def _ln_mod_mm(x, g, mod_l, mod_c, ws, n_lat):
    b, lt, d = x.shape
    bl = _row_block(lt, WIDE_ROWS)
    kern = functools.partial(_ln_mod_mm_kernel, n_w=len(ws), bl=bl, n_lat=n_lat)
    return pl.pallas_call(
        kern,
        grid=(b, lt // bl),
        in_specs=[pl.BlockSpec((1, bl, d), lambda i, j: (i, j, 0)),
                  pl.BlockSpec((1, d), lambda i, j: (0, 0)),
                  pl.BlockSpec((1, 6, d), lambda i, j: (i, 0, 0)),
                  pl.BlockSpec((1, 6, d), lambda i, j: (0, 0, 0))]
        + [pl.BlockSpec(w.shape, lambda i, j: (0, 0)) for w in ws],
        out_specs=[pl.BlockSpec((1, bl, w.shape[1]), lambda i, j: (i, j, 0)) for w in ws],
        out_shape=[jax.ShapeDtypeStruct((b, lt, w.shape[1]), F32) for w in ws],
        compiler_params=_params("parallel", "parallel"),
        name="ln_mod_mm",
    )(x, g.reshape(1, d), mod_l, mod_c, *ws)


def _rwkv_prep_kernel(p_ref, prev_ref, next_ref, mu_ref, w0_ref, w2_ref, a0_ref, a2_ref, kk_ref,
                      ka_ref, rk_ref, g2_ref, r_ref, v_ref, na_ref, lw_ref, kd_ref, bd_ref,
                      gate_ref, bonus_ref, *, bl, ctx_block):
    p = p_ref[0]
    j = pl.program_id(1)
    starts = jnp.logical_or(j == 0, j == ctx_block)
    ends = jnp.logical_or(j == ctx_block - 1, j == pl.num_programs(1) - 1)
    before = jnp.where(starts, 0.0, prev_ref[0, HALO - 1:HALO, :])
    after = jnp.where(ends, 0.0, next_ref[0, 0:1, :])
    row = lax.broadcasted_iota(jnp.int32, p.shape, 0)
    prev = jnp.where(row == 0, before, pltpu.roll(p, 1, 0))
    nxt = jnp.where(row == bl - 1, after, pltpu.roll(p, bl - 1, 0))
    pm = p + mu_ref[...] * (0.5 * (prev + nxt) - p)

    o = 3 * RW_DIM
    r = pm[:, 0:RW_DIM]
    k = pm[:, RW_DIM:2 * RW_DIM]
    v = pm[:, 2 * RW_DIM:o]
    w1 = (pm[:, o:o + DECAY_LORA], pm[:, o + DECAY_LORA:o + 2 * DECAY_LORA])
    o += 2 * DECAY_LORA
    a1 = (pm[:, o:o + ICLR_LORA], pm[:, o + ICLR_LORA:o + 2 * ICLR_LORA])
    o += 2 * ICLR_LORA
    g1 = pm[:, o:o + GATE_LORA]

    kk = k * kk_ref[...]
    lw, kd, ad = [], [], []
    for d in range(2):
        z = w0_ref[d:d + 1, :] + _bdot(jnp.tanh(w1[d]), w2_ref[d])
        lw.append(-jnp.exp(-_softplus(-z) - 0.5))
        a = _sigmoid(a0_ref[d:d + 1, :] + _bdot(a1[d], a2_ref[d]))
        ad.append(a)
        kd.append(k * (1.0 + (a - 1.0) * ka_ref[...]))
    gate = _bdot(_sigmoid(g1), g2_ref[...])
    rkk = r * (kd[0] + kd[1]) * rk_ref[...]

    for h in range(RW_HEADS):
        sl = slice(h * RW_HEAD_DIM, (h + 1) * RW_HEAD_DIM)
        kk_h = kk[:, sl]
        kk_h = kk_h * lax.rsqrt(jnp.sum(kk_h * kk_h, axis=-1, keepdims=True) + 1e-12)
        v_h = v[:, sl]
        r_ref[0, h] = r[:, sl]
        v_ref[0, h] = v_h
        na_ref[0, h] = -kk_h
        for d in range(2):
            lw_ref[d, 0, h] = lw[d][:, sl]
            kd_ref[d, 0, h] = kd[d][:, sl]
            bd_ref[d, 0, h] = kk_h * ad[d][:, sl]
        gate_ref[0, h] = gate[:, sl]
        bonus_ref[0, h] = jnp.sum(rkk[:, sl], axis=-1, keepdims=True) * v_h


def _rwkv_prep(p, n_lat, mu, w0, w2, a0, a2, k_k, k_a, r_k, g2):
    b, lt, cols = p.shape
    bl = TOKEN_BLOCK
    nblk = lt // bl
    per = bl // HALO
    last_slab = lt // HALO - 1
    prev_spec = pl.BlockSpec((1, HALO, cols), lambda i, j: (i, jnp.maximum(j * per - 1, 0), 0))
    next_spec = pl.BlockSpec((1, HALO, cols), lambda i, j: (i, jnp.minimum((j + 1) * per, last_slab), 0))

    hm = jax.ShapeDtypeStruct((b, RW_HEADS, lt, RW_HEAD_DIM), F32)
    hm2 = jax.ShapeDtypeStruct((2, b, RW_HEADS, lt, RW_HEAD_DIM), F32)
    hm_spec = pl.BlockSpec((1, RW_HEADS, bl, RW_HEAD_DIM), lambda i, j: (i, 0, j, 0))
    hm2_spec = pl.BlockSpec((2, 1, RW_HEADS, bl, RW_HEAD_DIM), lambda i, j: (0, i, 0, j, 0))
    full = lambda a: pl.BlockSpec(a.shape, lambda i, j: (0,) * a.ndim)
    consts = [mu.reshape(1, cols), w0, w2.astype(BF16), a0, a2.astype(BF16), k_k.reshape(1, RW_DIM),
              k_a.reshape(1, RW_DIM), r_k.reshape(1, RW_DIM), g2.astype(BF16)]
    return pl.pallas_call(
        functools.partial(_rwkv_prep_kernel, bl=bl, ctx_block=n_lat // bl),
        grid=(b, nblk),
        in_specs=[pl.BlockSpec((1, bl, cols), lambda i, j: (i, j, 0)), prev_spec, next_spec]
        + [full(a) for a in consts],
        out_specs=[hm_spec, hm_spec, hm_spec, hm2_spec, hm2_spec, hm2_spec, hm_spec, hm_spec],
        out_shape=[hm, hm, hm, hm2, hm2, hm2, hm, hm],
        compiler_params=_params("parallel", "parallel"),
        name="rwkv_prep",
    )(p, p, p, *consts)


def _wkv_kernel(r_ref, v_ref, a_ref, lw_ref, k_ref, b_ref, y_ref, st_ref, *, cn, chains):
    d = pl.program_id(0)

    @pl.when(pl.program_id(1) == 0)
    def _():
        st_ref[...] = jnp.zeros_like(st_ref)

    row = lax.broadcasted_iota(jnp.int32, (cn, cn), 0)
    col = lax.broadcasted_iota(jnp.int32, (cn, cn), 1)
    lead = (row - col) * jnp.where(d == 0, 1, -1)
    upto = lead >= 0
    row2 = lax.broadcasted_iota(jnp.int32, (2 * cn, 2 * cn), 0)
    col2 = lax.broadcasted_iota(jnp.int32, (2 * cn, 2 * cn), 1)
    lead2 = (row2 % cn - col2 % cn) * jnp.where(d == 0, 1, -1)
    mask4 = lead2 >= jnp.where(row2 < cn, 1, 0)
    eye = (col == row).astype(F32)
    cum_mat = upto.astype(F32)
    hd = r_ref.shape[-1]
    eye_h = (lax.broadcasted_iota(jnp.int32, (hd, hd), 0)
             == lax.broadcasted_iota(jnp.int32, (hd, hd), 1)).astype(F32)
    n_double = int(math.log2(cn)) - 1

    nb = SCAN_INTERLEAVE
    cum_b = jnp.broadcast_to(cum_mat, (nb, cn, cn))

    def chunk(sl, rows, st):
        r, v, a = r_ref[sl, rows], v_ref[sl, rows], a_ref[sl, rows]
        lw, k, b = lw_ref[0, sl, rows], k_ref[0, sl, rows], b_ref[0, sl, rows]
        lw_hi = lw.astype(BF16)
        lw_lo = (lw - lw_hi.astype(F32)).astype(BF16)
        cum2 = _bmm(cum_b, jnp.concatenate([lw_hi, lw_lo], axis=-1))
        cum = cum2[..., :hd] + cum2[..., hd:]
        tot = jnp.sum(lw, axis=1, keepdims=True)
        at = a * jnp.exp(cum - lw)
        rt = r * jnp.exp(cum)
        einv = jnp.exp(-cum)
        bt, kt = b * einv, k * einv
        eend = jnp.exp(tot - cum)
        bh, kh = b * eend, k * eend

        amat = _bmm_nt(jnp.concatenate([at, rt], axis=1), jnp.concatenate([bt, kt], axis=1))
        amat = jnp.where(mask4, amat, 0.0)
        a_ab = amat[:, :cn, :cn]

        t = eye + a_ab
        x = _bmm(a_ab, a_ab)
        for _ in range(n_double - 1):
            tx = _bmm(jnp.concatenate([t, x], axis=1), x)
            t = t + tx[:, :cn]
            x = tx[:, cn:]
        t = t + _bmm(t, x)

        zero = jnp.zeros_like(v)
        av = _bmm(amat[:, :cn], jnp.concatenate([zero, v], axis=1))
        wu = _bmm(t, jnp.concatenate([at, av], axis=-1))
        rhs = jnp.concatenate([wu, jnp.concatenate([zero, v], axis=-1)], axis=1)
        qy = _bmm(amat[:, cn:], rhs)
        mn = _bmm_tn(jnp.concatenate([bh, kh], axis=1), rhs)
        qe = rt + qy[..., :hd]
        mc = eye_h * jnp.exp(tot) + mn[..., :hd]

        out = _bmm(jnp.concatenate([qe, mc], axis=1), st)
        y_ref[0, sl, rows] = out[:, :cn] + qy[..., hd:]
        return out[:, cn:] + mn[..., hd:]

    n_sub = r_ref.shape[1] // cn

    def group(g, carry):
        sl = pl.ds(g * nb, nb)
        st = st_ref[sl]
        for s in range(n_sub):
            sub = jnp.where(d == 0, s, n_sub - 1 - s)
            st = chunk(sl, pl.ds(pl.multiple_of(sub * cn, cn), cn), st)
        st_ref[sl] = st
        return carry

    lax.fori_loop(0, chains // SCAN_INTERLEAVE, group, 0)


def _wkv_scan(r, v, na, lw, kd, bd, n_lat):
    b, h, lt, hd = r.shape
    cn = SCAN_CHUNK
    chains = b * h
    rows = cn * SCAN_CHUNKS_PER_STEP
    assert n_lat % rows == 0 and lt % rows == 0
    n_all, n_l = lt // rows, n_lat // rows
    n_c = n_all - n_l

    def chunk(d, i):
        fwd = jnp.where(i < n_c, n_l + i, i - n_c)
        rev = jnp.where(i < n_c, n_all - 1 - i, n_l - 1 - (i - n_c))
        return jnp.where(d == 0, fwd, rev)

    shared = pl.BlockSpec((chains, rows, hd), lambda d, i: (0, chunk(d, i), 0))
    per_dir = pl.BlockSpec((1, chains, rows, hd), lambda d, i: (d, 0, chunk(d, i), 0))
    flat = lambda t: t.reshape(t.shape[:-4] + (chains, lt, hd))
    y = pl.pallas_call(
        functools.partial(_wkv_kernel, cn=cn, chains=chains),
        grid=(2, n_all),
        in_specs=[shared, shared, shared, per_dir, per_dir, per_dir],
        out_specs=per_dir,
        out_shape=jax.ShapeDtypeStruct((2, chains, lt, hd), F32),
        scratch_shapes=[pltpu.VMEM((chains, hd, hd), F32)],
        compiler_params=_params("parallel", "arbitrary"),
        name="wkv_scan",
    )(flat(r), flat(v), flat(na), flat(lw), flat(kd), flat(bd))
    return y.reshape(2, b, h, lt, hd)


def _rope(x, cos, sin, half):
    lane = lax.broadcasted_iota(jnp.int32, x.shape, 1)
    width = x.shape[1]
    partner = jnp.where(lane % (2 * half) < half, pltpu.roll(x, width - half, 1), pltpu.roll(x, half, 1))
    return x * cos + partner * sin


def _mla_qkv_kernel(p_ref, gqa_ref, gkva_ref, wq_ref, wk_ref, wv_ref, gq_ref, gk_ref, cos_ref, sin_ref,
                    q_ref, kt_ref, v_ref, *, scale):
    p = p_ref[0]
    q_a = p[:, 0:MLA_Q_RANK]
    kv_a = p[:, MLA_Q_RANK:MLA_Q_RANK + MLA_KV_RANK]
    k_rope = p[:, MLA_Q_RANK + MLA_KV_RANK:MLA_Q_RANK + MLA_KV_RANK + LANES]
    qa_n = (_rms(q_a) * gqa_ref[...]).astype(BF16)
    kva_n = (_rms(kv_a) * gkva_ref[...]).astype(BF16)
    q_all = jnp.dot(qa_n, wq_ref[...], preferred_element_type=F32)
    k_all = jnp.dot(kva_n, wk_ref[...], preferred_element_type=F32)
    v_all = jnp.dot(kva_n, wv_ref[...], preferred_element_type=F32)
    cos, sin = cos_ref[...], sin_ref[...]
    for h in range(MLA_HEADS):
        sl = slice(h * LANES, (h + 1) * LANES)
        q = q_all[:, sl]
        q = q * lax.rsqrt(jnp.sum(q * q, axis=-1, keepdims=True) / MLA_QK + NORM_EPS) * gq_ref[...]
        q = _rope(q, cos, sin, MLA_ROPE // 4)
        q_ref[0, h] = (q * scale).astype(q_ref.dtype)
        k = k_all[:, sl] + k_rope
        k = k * lax.rsqrt(jnp.sum(k * k, axis=-1, keepdims=True) / MLA_QK + NORM_EPS) * gk_ref[...]
        k = _rope(k, cos, sin, MLA_ROPE // 4)
        kt_ref[0, h, 0] = k.T.astype(kt_ref.dtype)
        lane = lax.broadcasted_iota(jnp.int32, q.shape, 1)
        v_ref[0, h] = jnp.where(lane < MLA_V, v_all[:, sl], 1.0).astype(v_ref.dtype)


def _pad_heads(w, heads, width):
    k = w.shape[0]
    w = w.reshape(k, heads, width)
    return jnp.pad(w, ((0, 0), (0, 0), (0, LANES - width))).reshape(k, heads * LANES)


def _mla_qkv(p, g_qa, w_q_up, g_kva, w_kv_up, g_q, g_k, cos, sin):
    b, lt, cols = p.shape
    bl = TOKEN_BLOCK
    bk = _key_chunk(lt)
    wq = _pad_heads(w_q_up, MLA_HEADS, MLA_QK).astype(BF16)
    wkv = w_kv_up.reshape(MLA_KV_RANK, MLA_HEADS, MLA_NOPE + MLA_V)
    wk = _pad_heads(wkv[:, :, :MLA_NOPE].reshape(MLA_KV_RANK, -1), MLA_HEADS, MLA_NOPE).astype(BF16)
    wv = _pad_heads(wkv[:, :, MLA_NOPE:].reshape(MLA_KV_RANK, -1), MLA_HEADS, MLA_V).astype(BF16)
    pad = lambda g: jnp.pad(g, (0, LANES - MLA_QK)).reshape(1, LANES)
    consts = [g_qa.reshape(1, -1), g_kva.reshape(1, -1), wq, wk, wv, pad(g_q), pad(g_k)]
    full = lambda a: pl.BlockSpec(a.shape, lambda i, j: (0,) * a.ndim)
    tab = pl.BlockSpec((bl, LANES), lambda i, j: (j, 0))
    return pl.pallas_call(
        functools.partial(_mla_qkv_kernel, scale=MLA_QK ** -0.5 * LOG2E),
        grid=(b, lt // bl),
        in_specs=[pl.BlockSpec((1, bl, cols), lambda i, j: (i, j, 0))] + [full(a) for a in consts] + [tab, tab],
        out_specs=[pl.BlockSpec((1, MLA_HEADS, bl, LANES), lambda i, j: (i, 0, j, 0)),
                   _kt_spec(MLA_HEADS, LANES, bl, bk),
                   pl.BlockSpec((1, MLA_HEADS, bl, LANES), lambda i, j: (i, 0, j, 0))],
        out_shape=[jax.ShapeDtypeStruct((b, MLA_HEADS, lt, LANES), BF16),
                   jax.ShapeDtypeStruct((b, MLA_HEADS, lt // bk, LANES, bk), BF16),
                   jax.ShapeDtypeStruct((b, MLA_HEADS, lt, LANES), BF16)],
        compiler_params=_params("parallel", "parallel"),
        name="mla_qkv",
    )(p, *consts, cos, sin)


def _pair_norm_rope(x, g, cos, sin):
    lane = lax.broadcasted_iota(jnp.int32, x.shape, 1)
    lo = lane < GQA_HEAD_DIM
    xx = x * x
    ss_lo = jnp.sum(jnp.where(lo, xx, 0.0), axis=-1, keepdims=True)
    ss_hi = jnp.sum(jnp.where(lo, 0.0, xx), axis=-1, keepdims=True)
    rs = lax.rsqrt(jnp.where(lo, ss_lo, ss_hi) / GQA_HEAD_DIM + NORM_EPS)
    return _rope(x * rs * g, cos, sin, GQA_HEAD_DIM // 4)


def _gqa_qkv_kernel(x_ref, g_ref, ml_ref, mc_ref, win_ref, gq_ref, gk_ref, cos_ref, sin_ref, q_ref, kt_ref,
                    v_ref, *, scale, bl, n_lat):
    p = jnp.dot(_ln_mod(x_ref, g_ref, ml_ref, mc_ref, bl, n_lat), win_ref[...], preferred_element_type=F32)
    cos, sin = cos_ref[...], sin_ref[...]
    qw = GQA_HEADS * GQA_HEAD_DIM
    kw = GQA_KV_HEADS * GQA_HEAD_DIM
    for j in range(GQA_HEADS // 2):
        q = _pair_norm_rope(p[:, j * LANES:(j + 1) * LANES], gq_ref[...], cos, sin) * scale
        q_ref[0, 2 * j] = q[:, :GQA_HEAD_DIM].astype(q_ref.dtype)
        q_ref[0, 2 * j + 1] = q[:, GQA_HEAD_DIM:].astype(q_ref.dtype)
    for j in range(GQA_KV_HEADS // 2):
        k = _pair_norm_rope(p[:, qw + j * LANES:qw + (j + 1) * LANES], gk_ref[...], cos, sin)
        kt = k.T
        kt_ref[0, 2 * j, 0] = kt[:GQA_HEAD_DIM].astype(kt_ref.dtype)
        kt_ref[0, 2 * j + 1, 0] = kt[GQA_HEAD_DIM:].astype(kt_ref.dtype)
        v = p[:, qw + kw + j * LANES:qw + kw + (j + 1) * LANES]
        lo = lax.broadcasted_iota(jnp.int32, v.shape, 1) < GQA_HEAD_DIM
        v_ref[0, 2 * j] = jnp.where(lo, v, 1.0).astype(v_ref.dtype)
        v_ref[0, 2 * j + 1] = jnp.where(lo, pltpu.roll(v, GQA_HEAD_DIM, 1), 1.0).astype(v_ref.dtype)


def _gqa_qkv(x, g, mod_l, mod_c, w_in, n_lat, g_q, g_k, cos, sin):
    b, lt, d = x.shape
    bl = TOKEN_BLOCK
    bk = _key_chunk(lt)
    hd = GQA_HEAD_DIM
    two = lambda t: jnp.concatenate([t, t]).reshape(1, LANES)
    tab = pl.BlockSpec((bl, LANES), lambda i, j: (j, 0))
    vec = pl.BlockSpec((1, LANES), lambda i, j: (0, 0))
    return pl.pallas_call(
        functools.partial(_gqa_qkv_kernel, scale=hd ** -0.5 * LOG2E, bl=bl, n_lat=n_lat),
        grid=(b, lt // bl),
        in_specs=_ln_mod_specs(d, bl) + [pl.BlockSpec(w_in.shape, lambda i, j: (0, 0)), vec, vec, tab, tab],
        out_specs=[pl.BlockSpec((1, GQA_HEADS, bl, hd), lambda i, j: (i, 0, j, 0)),
                   _kt_spec(GQA_KV_HEADS, hd, bl, bk),
                   pl.BlockSpec((1, GQA_KV_HEADS, bl, LANES), lambda i, j: (i, 0, j, 0))],
        out_shape=[jax.ShapeDtypeStruct((b, GQA_HEADS, lt, hd), BF16),
                   jax.ShapeDtypeStruct((b, GQA_KV_HEADS, lt // bk, hd, bk), BF16),
                   jax.ShapeDtypeStruct((b, GQA_KV_HEADS, lt, LANES), BF16)],
        compiler_params=_params("parallel", "parallel"),
        name="gqa_qkv",
    )(x, g.reshape(1, d), mod_l, mod_c, w_in, two(g_q), two(g_k), cos, sin)


def _attn_kernel(q_ref, kt_ref, v_ref, o_ref, m_scr, acc_scr, *, groups, bq, n_chunks):
    rows = groups * bq
    q = q_ref[0, 0].reshape(rows, q_ref.shape[-1])
    bk = kt_ref.shape[-1]
    dv = o_ref.shape[-1]
    m_scr[...] = jnp.full_like(m_scr, -jnp.inf)
    acc_scr[...] = jnp.zeros_like(acc_scr)

    def step(j, carry):
        s = jnp.dot(q, kt_ref[0, 0, j], preferred_element_type=F32)
        m_prev = m_scr[...]
        m_new = jnp.maximum(m_prev, jnp.max(s, axis=-1, keepdims=True))
        alpha = jnp.exp2(m_prev - m_new)
        p = jnp.exp2(s - pltpu.repeat(m_new, bk // LANES, axis=1))
        vv = v_ref[0, 0, j * bk:(j + 1) * bk, :]
        acc_scr[...] = alpha * acc_scr[...] + jnp.dot(p.astype(BF16), vv, preferred_element_type=F32)
        m_scr[...] = m_new
        return carry

    for j in range(n_chunks):
        step(j, 0)
    acc = acc_scr[...]
    o = acc / pltpu.roll(acc, LANES - dv, 1)
    o_ref[0, 0] = o[:, :dv].reshape(groups, bq, dv).astype(o_ref.dtype)


def _attention_call(q, q_start, lq, ktc, v1):
    b, hq, lt, dk = q.shape
    hk, n_chunks, bk = ktc.shape[1], ktc.shape[2], ktc.shape[-1]
    lk = n_chunks * bk
    dv = LANES // 2
    groups = hq // hk
    bq = min(ATTN_ROWS // groups, lq)
    rows = groups * bq
    first = q_start // bq
    q5 = q.reshape(b, hk, groups, lt, dk)
    out = pl.pallas_call(
        functools.partial(_attn_kernel, groups=groups, bq=bq, n_chunks=n_chunks),
        grid=(b, hk, lq // bq),
        in_specs=[pl.BlockSpec((1, 1, groups, bq, dk), lambda i, h, j: (i, h, 0, j + first, 0)),
                  pl.BlockSpec((1, 1, n_chunks, dk, bk), lambda i, h, j: (i, h, 0, 0, 0)),
                  pl.BlockSpec((1, 1, lk, LANES), lambda i, h, j: (i, h, 0, 0))],
        out_specs=pl.BlockSpec((1, 1, groups, bq, dv), lambda i, h, j: (i, h, 0, j, 0)),
        out_shape=jax.ShapeDtypeStruct((b, hk, groups, lq, dv), BF16),
        scratch_shapes=[pltpu.VMEM((rows, LANES), F32), pltpu.VMEM((rows, LANES), F32)],
        compiler_params=_params("parallel", "parallel", "arbitrary"),
        name="attention",
    )(q5, ktc, v1)
    return out.reshape(b, hq, lq, dv)


def _attention(q, ktc, v1, n_lat):
    lt = q.shape[2]
    n_ctx = lt - n_lat
    o_l = _attention_call(q, 0, n_lat, ktc, v1)
    o_c = _attention_call(q, n_lat, n_ctx, ktc[:, :, -1:, :, -n_ctx:], v1[:, :, n_lat:])
    return o_l, o_c


def _attn_out_specs(heads, bl, n_lat):
    n_lb = n_lat // bl
    return [pl.BlockSpec((1, heads, bl, LANES // 2), lambda i, j: (i, 0, jnp.minimum(j, n_lb - 1), 0)),
            pl.BlockSpec((1, heads, bl, LANES // 2), lambda i, j: (i, 0, jnp.maximum(j - n_lb, 0), 0))]


def _heads_of(ol_ref, oc_ref, heads, bl, n_lat):
    is_ctx = pl.program_id(1) >= n_lat // bl
    return jnp.concatenate([jnp.where(is_ctx, oc_ref[0, h], ol_ref[0, h]) for h in range(heads)], axis=-1)


def _ab_out_kernel(x_ref, ml_ref, mc_ref, y_ref, bonus_ref, gate_ref, ol_ref, oc_ref, lnw_ref, lnb_ref, wrw_ref,
                   wmla_ref, out_ref, *, bl, n_lat):
    rw = []
    for h in range(RW_HEADS):
        y = y_ref[0, 0, h] + y_ref[1, 0, h]
        mean = jnp.mean(y, axis=-1, keepdims=True)
        yc = y - mean
        var = jnp.mean(yc * yc, axis=-1, keepdims=True)
        yn = yc * lax.rsqrt(var + LN_X_EPS) * lnw_ref[h] + lnb_ref[h]
        rw.append((yn + bonus_ref[0, h]) * gate_ref[0, h])
    rw = jnp.concatenate(rw, axis=-1).astype(BF16)
    o = _heads_of(ol_ref, oc_ref, MLA_HEADS, bl, n_lat)
    acc = (jnp.dot(rw, wrw_ref[...], preferred_element_type=F32)
           + jnp.dot(o, wmla_ref[...], preferred_element_type=F32))
    gate = _row_mods(ml_ref, mc_ref, pl.program_id(1) * bl, bl, n_lat, 2)
    out_ref[0] = x_ref[0] + gate * acc


def _ab_out(x, mod_l, mod_c, y, bonus, gate, o, ln_w, ln_b, w_out, n_lat, lt):
    b, _, d = x.shape
    bl = TOKEN_BLOCK
    hd = RW_HEAD_DIM
    wrw = w_out[:RW_DIM].astype(BF16)
    wmla = w_out[RW_DIM:].astype(BF16)
    xs = pl.BlockSpec((1, bl, d), lambda i, j: (i, j, 0))
    hm = pl.BlockSpec((1, RW_HEADS, bl, hd), lambda i, j: (i, 0, j, 0))
    full = lambda a: pl.BlockSpec(a.shape, lambda i, j: (0,) * a.ndim)
    lnw, lnb = ln_w.reshape(RW_HEADS, 1, hd), ln_b.reshape(RW_HEADS, 1, hd)
    return pl.pallas_call(
        functools.partial(_ab_out_kernel, bl=bl, n_lat=n_lat),
        grid=(b, lt // bl),
        in_specs=[xs, pl.BlockSpec((1, 6, d), lambda i, j: (i, 0, 0)), pl.BlockSpec((1, 6, d), lambda i, j: (0, 0, 0)),
                  pl.BlockSpec((2, 1, RW_HEADS, bl, hd), lambda i, j: (0, i, 0, j, 0)), hm, hm]
        + _attn_out_specs(MLA_HEADS, bl, n_lat) + [full(lnw), full(lnb), full(wrw), full(wmla)],
        out_specs=xs,
        out_shape=jax.ShapeDtypeStruct((b, lt, d), F32),
        compiler_params=_params("parallel", "parallel"),
        name="ab_out",
    )(x, mod_l, mod_c, y, bonus, gate, *o, lnw, lnb, wrw, wmla)


def _gqa_out_kernel(x_ref, ml_ref, mc_ref, ol_ref, oc_ref, w_ref, out_ref, *, bl, n_lat):
    o = _heads_of(ol_ref, oc_ref, GQA_HEADS, bl, n_lat)
    acc = jnp.dot(o, w_ref[...], preferred_element_type=F32)
    gate = _row_mods(ml_ref, mc_ref, pl.program_id(1) * bl, bl, n_lat, 2)
    out_ref[0] = x_ref[0] + gate * acc


def _gqa_out(x, mod_l, mod_c, o, w_out, n_lat, lt):
    b, _, d = x.shape
    bl = TOKEN_BLOCK
    hd = GQA_HEAD_DIM
    w = w_out.astype(BF16)
    xs = pl.BlockSpec((1, bl, d), lambda i, j: (i, j, 0))
    return pl.pallas_call(
        functools.partial(_gqa_out_kernel, bl=bl, n_lat=n_lat),
        grid=(b, lt // bl),
        in_specs=[xs, pl.BlockSpec((1, 6, d), lambda i, j: (i, 0, 0)), pl.BlockSpec((1, 6, d), lambda i, j: (0, 0, 0))]
        + _attn_out_specs(GQA_HEADS, bl, n_lat) + [pl.BlockSpec(w.shape, lambda i, j: (0, 0))],
        out_specs=xs,
        out_shape=jax.ShapeDtypeStruct((b, lt, d), F32),
        compiler_params=_params("parallel", "parallel"),
        name="gqa_out",
    )(x, mod_l, mod_c, *o, w)


def _first_argmax(vals):
    best, idx = vals[0], jnp.zeros(vals[0].shape, jnp.int32)
    for i in range(1, len(vals)):
        better = vals[i] > best
        idx = jnp.where(better, i, idx)
        best = jnp.where(better, vals[i], best)
    return best, idx


def _pick(vals, idx):
    out = vals[0]
    for i in range(1, len(vals)):
        out = jnp.where(idx == i, vals[i], out)
    return out


def _route_kernel(x_ref, g_ref, ml_ref, mc_ref, rwh_ref, rwl_ref, rb_ref, t_ref, comb_ref, grp_ref, *, bl, n_lat):
    first = pl.program_id(1) * bl
    shift = _row_mods(ml_ref, mc_ref, first, bl, n_lat, 3)
    scale = _row_mods(ml_ref, mc_ref, first, bl, n_lat, 4)
    t = (_rms(x_ref[0]) * g_ref[...]) * (1.0 + scale) + shift
    t_ref[0] = t.astype(t_ref.dtype)
    t_hi = t.astype(BF16)
    t_lo = (t - t_hi.astype(F32)).astype(BF16)
    logits = (jnp.dot(t_hi, rwh_ref[...], preferred_element_type=F32)
              + jnp.dot(t_lo, rwh_ref[...], preferred_element_type=F32)
              + jnp.dot(t_hi, rwl_ref[...], preferred_element_type=F32))
    lt = logits.T
    score = [_sigmoid(lt[e:e + 1]) for e in range(N_EXPERTS)]
    biased = [score[e] + rb_ref[e:e + 1, :] for e in range(N_EXPERTS)]
    epg = EXPERTS_PER_GROUP
    group_score = []
    for g in range(N_GROUPS):
        vals = biased[g * epg:(g + 1) * epg]
        pair = [vals[i] + vals[j] for i in range(epg) for j in range(i + 1, epg)]
        group_score.append(functools.reduce(jnp.maximum, pair))
    _, grp = _first_argmax(group_score)
    in_b = [_pick([biased[g * epg + j] for g in range(N_GROUPS)], grp) for j in range(epg)]
    in_s = [_pick([score[g * epg + j] for g in range(N_GROUPS)], grp) for j in range(epg)]
    _, loc1 = _first_argmax(in_b)
    _, loc2 = _first_argmax([jnp.where(loc1 == j, -jnp.inf, in_b[j]) for j in range(epg)])
    w1, w2 = _pick(in_s, loc1), _pick(in_s, loc2)
    wsum = w1 + w2
    w1, w2 = w1 / wsum, w2 / wsum
    e1, e2 = grp * epg + loc1, grp * epg + loc2
    sub = lax.broadcasted_iota(jnp.int32, (LANES, bl), 0)
    comb = jnp.zeros((LANES, bl), F32)
    for e in range(N_EXPERTS):
        c_e = jnp.where(e1 == e, w1, 0.0) + jnp.where(e2 == e, w2, 0.0)
        c_hi = c_e.astype(BF16).astype(F32)
        comb = jnp.where(sub == e, jnp.broadcast_to(c_hi, (LANES, bl)), comb)
        comb = jnp.where(sub == N_EXPERTS + e, jnp.broadcast_to(c_e - c_hi, (LANES, bl)), comb)
    comb_ref[0] = comb.T.astype(comb_ref.dtype)
    grp_ref[0] = grp


def _moe_route(x, g, mod_l, mod_c, router_w, router_b, n_lat):
    b, lt, d = x.shape
    bl = TOKEN_BLOCK
    rw = jnp.pad(router_w, ((0, 0), (0, LANES - N_EXPERTS)))
    rw_hi = rw.astype(BF16)
    rw_lo = (rw - rw_hi.astype(F32)).astype(BF16)
    xs = pl.BlockSpec((1, bl, d), lambda i, j: (i, j, 0))
    return pl.pallas_call(
        functools.partial(_route_kernel, bl=bl, n_lat=n_lat),
        grid=(b, lt // bl),
        in_specs=[xs, pl.BlockSpec((1, d), lambda i, j: (0, 0)),
                  pl.BlockSpec((1, 6, d), lambda i, j: (i, 0, 0)), pl.BlockSpec((1, 6, d), lambda i, j: (0, 0, 0)),
                  pl.BlockSpec(rw.shape, lambda i, j: (0, 0)), pl.BlockSpec(rw.shape, lambda i, j: (0, 0)),
                  pl.BlockSpec((N_EXPERTS, 1), lambda i, j: (0, 0))],
        out_specs=[xs, pl.BlockSpec((1, bl, LANES), lambda i, j: (i, j, 0)),
                   pl.BlockSpec((1, 1, bl), lambda i, j: (i, 0, j))],
        out_shape=[jax.ShapeDtypeStruct((b, lt, d), BF16), jax.ShapeDtypeStruct((b, lt, LANES), BF16),
                   jax.ShapeDtypeStruct((b, 1, lt), jnp.int32)],
        compiler_params=_params("parallel", "parallel"),
        name="moe_route",
    )(x, g.reshape(1, d), mod_l, mod_c, rw_hi, rw_lo, router_b.reshape(N_EXPERTS, 1))


def _ffn(t, wg, wu, wd, cw):
    g = jnp.dot(t, wg, preferred_element_type=F32)
    u = jnp.dot(t, wu, preferred_element_type=F32)
    h = _silu(g) * u
    if cw is not None:
        h = h * cw
    return jnp.dot(h.astype(BF16), wd, preferred_element_type=F32)


def _experts_kernel(cnt_ref, x_ref, t_ref, comb_ref, grp_ref, tri_ref, ml_ref, mc_ref, wg_ref, wu_ref, wd_ref,
                    sg_ref, su_ref, sd_ref, out_ref, acc_ref, row_ref, col_ref, *, bl, n_lat):
    g = pl.program_id(2)
    t = t_ref[0]

    @pl.when(g == 0)
    def _():
        acc_ref[...] = _ffn(t, sg_ref[...], su_ref[...], sd_ref[...], None)
        grp = grp_ref[0]
        gid = lax.broadcasted_iota(jnp.int32, (2 * HALO, bl), 0)
        member = jnp.where(grp == gid, 1.0, 0.0)
        before = jnp.dot(member.astype(BF16), tri_ref[...], preferred_element_type=F32)
        slot = jnp.sum(member * before, axis=0, keepdims=True)
        sub = lax.broadcasted_iota(jnp.int32, (LANES, bl), 0)
        info = jnp.where(sub == 0, jnp.broadcast_to(slot, (LANES, bl)),
                         jnp.where(sub == 1, jnp.broadcast_to(grp.astype(F32), (LANES, bl)), 0.0))
        row_ref[...] = info[:HALO]
        col_ref[...] = info.T

    gf = g.astype(F32)
    slot_r, in_r = row_ref[0:1, :], jnp.where(row_ref[1:2, :] == gf, 1.0, 0.0)
    slot_c, in_c = col_ref[:, 0:1], jnp.where(col_ref[:, 1:2] == gf, 1.0, 0.0)
    count = cnt_ref[(pl.program_id(0) * pl.num_programs(1) + pl.program_id(1)) * N_GROUPS + g]
    wd = wd_ref[...]
    wd = wd.reshape(wd.shape[0] * wd.shape[1], wd.shape[2])
    lane = lax.broadcasted_iota(jnp.int32, (MOE_CAP, LANES), 1)

    def one_pass(k, carry):
        base = (k * MOE_CAP).astype(F32)
        want_r = base + lax.broadcasted_iota(jnp.int32, (MOE_CAP, bl), 0).astype(F32)
        want_c = base + lax.broadcasted_iota(jnp.int32, (bl, MOE_CAP), 1).astype(F32)
        gather = jnp.where(slot_r == want_r, in_r, 0.0).astype(BF16)
        scatter = jnp.where(slot_c == want_c, in_c, 0.0).astype(BF16)
        tg = jnp.dot(gather, t, preferred_element_type=F32).astype(BF16)
        cg = jnp.dot(gather, comb_ref[0], preferred_element_type=F32)
        hs = []
        for i in range(EXPERTS_PER_GROUP):
            e = g * EXPERTS_PER_GROUP + i
            cw = jnp.sum(jnp.where(lane % N_EXPERTS == e, cg, 0.0), axis=-1, keepdims=True)
            gate = jnp.dot(tg, wg_ref[i], preferred_element_type=F32)
            up = jnp.dot(tg, wu_ref[i], preferred_element_type=F32)
            hs.append((_silu(gate) * up * cw).astype(BF16))
        yg = jnp.dot(jnp.concatenate(hs, axis=-1), wd, preferred_element_type=F32)
        acc_ref[...] += jnp.dot(scatter, yg.astype(BF16), preferred_element_type=F32)
        return carry

    lax.fori_loop(0, (count + MOE_CAP - 1) // MOE_CAP, one_pass, 0)

    @pl.when(g == pl.num_programs(2) - 1)
    def _():
        gate = _row_mods(ml_ref, mc_ref, pl.program_id(1) * bl, bl, n_lat, 5)
        out_ref[0] = x_ref[0] + gate * acc_ref[...]


def _moe_experts(x, t, comb, grp, mod_l, mod_c, w_gate, w_up, w_down, sh_gate, sh_up, sh_down, n_lat):
    b, lt, d = x.shape
    bl = _row_block(lt, WIDE_ROWS)
    nblk = lt // bl
    ff = w_gate.shape[-1]
    epg = EXPERTS_PER_GROUP
    counts = jnp.sum(grp.reshape(b, nblk, bl, 1) == jnp.arange(N_GROUPS), axis=2, dtype=jnp.int32).reshape(-1)
    tri = jnp.triu(jnp.ones((bl, bl), BF16), 1)
    xs = pl.BlockSpec((1, bl, d), lambda i, j, g, c: (i, j, 0))
    full = lambda a: pl.BlockSpec(a.shape, lambda i, j, g, c: (0,) * a.ndim)
    ws = [w_gate.astype(BF16), w_up.astype(BF16), w_down.astype(BF16),
          sh_gate.astype(BF16), sh_up.astype(BF16), sh_down.astype(BF16)]
    grid_spec = pltpu.PrefetchScalarGridSpec(
        num_scalar_prefetch=1,
        grid=(b, nblk, N_GROUPS),
        in_specs=[xs, xs, pl.BlockSpec((1, bl, LANES), lambda i, j, g, c: (i, j, 0)),
                  pl.BlockSpec((1, 1, bl), lambda i, j, g, c: (i, 0, j)), full(tri),
                  pl.BlockSpec((1, 6, d), lambda i, j, g, c: (i, 0, 0)),
                  pl.BlockSpec((1, 6, d), lambda i, j, g, c: (0, 0, 0)),
                  pl.BlockSpec((epg, d, ff), lambda i, j, g, c: (g, 0, 0)),
                  pl.BlockSpec((epg, d, ff), lambda i, j, g, c: (g, 0, 0)),
                  pl.BlockSpec((epg, ff, d), lambda i, j, g, c: (g, 0, 0)),
                  full(ws[3]), full(ws[4]), full(ws[5])],
        out_specs=xs,
        scratch_shapes=[pltpu.VMEM((bl, d), F32), pltpu.VMEM((HALO, bl), F32), pltpu.VMEM((bl, LANES), F32)],
    )
    return pl.pallas_call(
        functools.partial(_experts_kernel, bl=bl, n_lat=n_lat),
        grid_spec=grid_spec,
        out_shape=jax.ShapeDtypeStruct((b, lt, d), F32),
        compiler_params=_params("parallel", "parallel", "arbitrary"),
        name="moe_experts",
    )(counts, x, t, comb, grp, tri, mod_l, mod_c, *ws)


def _rope_tables(seq, n_ctx, rope_dims, lane_offset, repeat):
    half = rope_dims // 4
    t = jnp.arange(seq)
    rowp = (t // GRID_W).astype(F32)
    colp = (t % GRID_W).astype(F32)
    inv = ROPE_THETA ** (-jnp.arange(half, dtype=F32) / half)
    ar, ac = rowp[:, None] * inv[None, :], colp[:, None] * inv[None, :]
    cos = jnp.concatenate([jnp.cos(ar), jnp.cos(ar), jnp.cos(ac), jnp.cos(ac)], axis=1)
    sin = jnp.concatenate([-jnp.sin(ar), jnp.sin(ar), -jnp.sin(ac), jnp.sin(ac)], axis=1)
    width = LANES // repeat
    padl, padr = lane_offset, width - lane_offset - rope_dims
    cos = jnp.pad(cos, ((0, n_ctx), (0, 0)), constant_values=1.0)
    sin = jnp.pad(sin, ((0, n_ctx), (0, 0)))
    cos = jnp.pad(cos, ((0, 0), (padl, padr)), constant_values=1.0)
    sin = jnp.pad(sin, ((0, 0), (padl, padr)))
    return jnp.tile(cos, (1, repeat)), jnp.tile(sin, (1, repeat))


def kernel(x, c, ctx, c_ctx, ada_w, ada_b, norm1_g, norm2_g, ab_w_in, ab_w_out, rw_mu, rw_w0, rw_w2, rw_a0, rw_a2, rw_k_k, rw_k_a, rw_r_k, rw_g2, rw_ln_w, rw_ln_b, mla_g_qa, mla_w_q_up, mla_g_kva, mla_w_kv_up, mla_g_q, mla_g_k, gqa_w_in, gqa_w_out, gqa_g_q, gqa_g_k, router_w, router_b, moe_w_gate, moe_w_up, moe_w_down, shared_w_gate, shared_w_up, shared_w_down):
    b, n_lat, d = x.shape
    n_ctx = ctx.shape[1]
    depth = ada_w.shape[0]
    xs = jnp.concatenate([x, ctx], axis=1)

    cvec = jnp.concatenate([c, c_ctx[None, :]], axis=0)
    cvec = jnp.pad(cvec, ((0, 8 - (b + 1) % 8), (0, 0))) if (b + 1) % 8 else cvec
    mods = _ada_mods(cvec, ada_w, ada_b).reshape(depth, -1, 6, d)

    mla_cos, mla_sin = _rope_tables(n_lat, n_ctx, MLA_ROPE, MLA_NOPE, 1)
    gqa_cos, gqa_sin = _rope_tables(n_lat, n_ctx, GQA_HEAD_DIM, 0, 2)

    for l in range(depth):
        i = l // 2
        mod_l, mod_c = mods[l, :b], mods[l, b:b + 1]
        rows = n_lat + n_ctx
        if l % 2 == 0:
            w_in = ab_w_in[i]
            w_rw = w_in[:, :RW_COLS].astype(BF16)
            w_m = w_in[:, RW_COLS:]
            w_mla = jnp.concatenate(
                [w_m[:, :MLA_Q_RANK + MLA_KV_RANK],
                 jnp.pad(w_m[:, MLA_Q_RANK + MLA_KV_RANK:], ((0, 0), (MLA_NOPE, LANES - MLA_NOPE - MLA_ROPE)))],
                axis=1).astype(BF16)
            p_rw, p_mla = _ln_mod_mm(xs, norm1_g[l], mod_l, mod_c, [w_rw, w_mla], n_lat)
            r, v, na, lw, kd, bd, gate, bonus = _rwkv_prep(
                p_rw, n_lat, rw_mu[i], rw_w0[i], rw_w2[i], rw_a0[i], rw_a2[i], rw_k_k[i], rw_k_a[i],
                rw_r_k[i].reshape(-1), rw_g2[i])
            y = _wkv_scan(r, v, na, lw, kd, bd, n_lat)
            q, kt, vm = _mla_qkv(p_mla, mla_g_qa[i], mla_w_q_up[i], mla_g_kva[i], mla_w_kv_up[i],
                                 mla_g_q[i], mla_g_k[i], mla_cos, mla_sin)
            o = _attention(q, kt, vm, n_lat)
            xs = _ab_out(xs, mod_l, mod_c, y, bonus, gate, o, rw_ln_w[i], rw_ln_b[i], ab_w_out[i], n_lat, rows)
        else:
            q, kt, vg = _gqa_qkv(xs, norm1_g[l], mod_l, mod_c, gqa_w_in[i].astype(BF16), n_lat,
                                 gqa_g_q[i], gqa_g_k[i], gqa_cos, gqa_sin)
            o = _attention(q, kt, vg, n_lat)
            xs = _gqa_out(xs, mod_l, mod_c, o, gqa_w_out[i], n_lat, rows)
        t, comb, grp = _moe_route(xs, norm2_g[l], mod_l, mod_c, router_w, router_b, n_lat)
        xs = _moe_experts(xs, t, comb, grp, mod_l, mod_c, moe_w_gate[l], moe_w_up[l], moe_w_down[l],
                          shared_w_gate[l], shared_w_up[l], shared_w_down[l], n_lat)
    return xs[:, :n_lat]
```

```python
import functools
import math

import jax
import jax.numpy as jnp
from jax import lax
from jax.experimental import pallas as pl
from jax.experimental.pallas import tpu as pltpu

F32 = jnp.float32
BF16 = jnp.bfloat16
HIGHEST = lax.Precision.HIGHEST

GRID_W = 64
RW_HEADS = 8
RW_HEAD_DIM = 64
RW_DIM = RW_HEADS * RW_HEAD_DIM
DECAY_LORA = 64
ICLR_LORA = 64
GATE_LORA = 128
RW_COLS = 3 * RW_DIM + 2 * DECAY_LORA + 2 * ICLR_LORA + GATE_LORA
LN_X_EPS = 64e-5
MLA_HEADS = 8
MLA_NOPE = 64
MLA_ROPE = 32
MLA_V = 64
MLA_QK = MLA_NOPE + MLA_ROPE
MLA_Q_RANK = 384
MLA_KV_RANK = 256
GQA_HEADS = 16
GQA_KV_HEADS = 4
GQA_HEAD_DIM = 64
N_EXPERTS = 16
N_GROUPS = 4
EXPERTS_PER_GROUP = N_EXPERTS // N_GROUPS
ROPE_THETA = 10000.0
NORM_EPS = 1e-6

LANES = 128
V7X_VMEM_BYTES = 64 * 1024 * 1024
VMEM_LIMIT = V7X_VMEM_BYTES - 8 * 1024 * 1024

TOKEN_BLOCK = 256
HALO = 8
SCAN_CHUNK = 64
SCAN_INTERLEAVE = 16
SCAN_CHUNKS_PER_STEP = 4
ATTN_ROWS = 1024
ATTN_KEYS = 2816
WIDE_ROWS = 768
MOE_CAP = 256
LOG2E = math.log2(math.e)


def _row_block(lt, limit):
    return max(r for r in range(TOKEN_BLOCK, limit + 1, TOKEN_BLOCK) if lt % r == 0)


def _key_chunk(lt):
    return max(c for c in range(TOKEN_BLOCK, ATTN_KEYS + 1, TOKEN_BLOCK) if lt % c == 0)


def _kt_spec(heads, dk, bl, bk):
    per = bk // bl
    return pl.BlockSpec((1, heads, 1, dk, bl), lambda i, j: (i, 0, j // per, 0, j % per))


def _params(*sem):
    return pltpu.CompilerParams(dimension_semantics=sem, vmem_limit_bytes=VMEM_LIMIT)


def _bdot(a, b):
    return jnp.dot(a.astype(BF16), b.astype(BF16), preferred_element_type=F32)


def _batched(a, b, ca, cb):
    return lax.dot_general(a.astype(BF16), b.astype(BF16), (((ca,), (cb,)), ((0,), (0,))),
                           preferred_element_type=F32)


def _bmm(a, b):
    return _batched(a, b, 2, 1)


def _bmm_nt(a, b):
    return _batched(a, b, 2, 2)


def _bmm_tn(a, b):
    return _batched(a, b, 1, 1)


def _sigmoid(x):
    return 1.0 / (1.0 + jnp.exp(-x))


def _silu(x):
    return x * _sigmoid(x)


def _softplus(x):
    return jnp.maximum(x, 0.0) + jnp.log(1.0 + jnp.exp(-jnp.abs(x)))


def _rms(x):
    return x * lax.rsqrt(jnp.mean(x * x, axis=-1, keepdims=True) + NORM_EPS)


def _row_mods(ml_ref, mc_ref, first_row, rows, n_lat, idx):
    r = first_row + lax.broadcasted_iota(jnp.int32, (rows, 1), 0)
    return jnp.where(r >= n_lat, mc_ref[0, idx:idx + 1, :], ml_ref[0, idx:idx + 1, :])


def _ada_kernel(c_ref, w_ref, b_ref, o_ref):
    s = _silu(c_ref[...])
    o_ref[0] = jnp.dot(s, w_ref[0], preferred_element_type=F32, precision=HIGHEST) + b_ref[0]


def _ada_mods(cvec, ada_w, ada_b):
    depth, d, n = ada_w.shape
    rows = cvec.shape[0]
    bn = n // 4
    return pl.pallas_call(
        _ada_kernel,
        grid=(depth, n // bn),
        in_specs=[pl.BlockSpec((rows, d), lambda l, j: (0, 0)),
                  pl.BlockSpec((1, d, bn), lambda l, j: (l, 0, j)),
                  pl.BlockSpec((1, 1, bn), lambda l, j: (l, 0, j))],
        out_specs=pl.BlockSpec((1, rows, bn), lambda l, j: (l, 0, j)),
        out_shape=jax.ShapeDtypeStruct((depth, rows, n), F32),
        compiler_params=_params("parallel", "parallel"),
        name="ada_mods",
    )(cvec, ada_w, ada_b.reshape(depth, 1, n))


def _ln_mod(x_ref, g_ref, ml_ref, mc_ref, bl, n_lat):
    first = pl.program_id(1) * bl
    shift = _row_mods(ml_ref, mc_ref, first, bl, n_lat, 0)
    scale = _row_mods(ml_ref, mc_ref, first, bl, n_lat, 1)
    h = (_rms(x_ref[0]) * g_ref[...]) * (1.0 + scale) + shift
    return h.astype(BF16)


def _ln_mod_specs(d, bl):
    return [pl.BlockSpec((1, bl, d), lambda i, j: (i, j, 0)), pl.BlockSpec((1, d), lambda i, j: (0, 0)),
            pl.BlockSpec((1, 6, d), lambda i, j: (i, 0, 0)), pl.BlockSpec((1, 6, d), lambda i, j: (0, 0, 0))]


def _ln_mod_mm_kernel(x_ref, g_ref, ml_ref, mc_ref, *refs, n_w, bl, n_lat):
    w_refs, o_refs = refs[:n_w], refs[n_w:]
    hb = _ln_mod(x_ref, g_ref, ml_ref, mc_ref, bl, n_lat)
    for w_ref, o_ref in zip(w_refs, o_refs):
        o_ref[0] = jnp.dot(hb, w_ref[...], preferred_element_type=F32).astype(o_ref.dtype)


def _ln_mod_mm(x, g, mod_l, mod_c, ws, n_lat):
    b, lt, d = x.shape
    bl = _row_block(lt, WIDE_ROWS)
    kern = functools.partial(_ln_mod_mm_kernel, n_w=len(ws), bl=bl, n_lat=n_lat)
    return pl.pallas_call(
        kern,
        grid=(b, lt // bl),
        in_specs=[pl.BlockSpec((1, bl, d), lambda i, j: (i, j, 0)),
                  pl.BlockSpec((1, d), lambda i, j: (0, 0)),
                  pl.BlockSpec((1, 6, d), lambda i, j: (i, 0, 0)),
                  pl.BlockSpec((1, 6, d), lambda i, j: (0, 0, 0))]
        + [pl.BlockSpec(w.shape, lambda i, j: (0, 0)) for w in ws],
        out_specs=[pl.BlockSpec((1, bl, w.shape[1]), lambda i, j: (i, j, 0)) for w in ws],
        out_shape=[jax.ShapeDtypeStruct((b, lt, w.shape[1]), F32) for w in ws],
        compiler_params=_params("parallel", "parallel"),
        name="ln_mod_mm",
    )(x, g.reshape(1, d), mod_l, mod_c, *ws)


def _rwkv_prep_kernel(p_ref, prev_ref, next_ref, mu_ref, w0_ref, w2_ref, a0_ref, a2_ref, kk_ref,
                      ka_ref, rk_ref, g2_ref, r_ref, v_ref, na_ref, lw_ref, kd_ref, bd_ref,
                      gate_ref, bonus_ref, *, bl, ctx_block):
    p = p_ref[0]
    j = pl.program_id(1)
    starts = jnp.logical_or(j == 0, j == ctx_block)
    ends = jnp.logical_or(j == ctx_block - 1, j == pl.num_programs(1) - 1)
    before = jnp.where(starts, 0.0, prev_ref[0, HALO - 1:HALO, :])
    after = jnp.where(ends, 0.0, next_ref[0, 0:1, :])
    row = lax.broadcasted_iota(jnp.int32, p.shape, 0)
    prev = jnp.where(row == 0, before, pltpu.roll(p, 1, 0))
    nxt = jnp.where(row == bl - 1, after, pltpu.roll(p, bl - 1, 0))
    pm = p + mu_ref[...] * (0.5 * (prev + nxt) - p)

    o = 3 * RW_DIM
    r = pm[:, 0:RW_DIM]
    k = pm[:, RW_DIM:2 * RW_DIM]
    v = pm[:, 2 * RW_DIM:o]
    w1 = (pm[:, o:o + DECAY_LORA], pm[:, o + DECAY_LORA:o + 2 * DECAY_LORA])
    o += 2 * DECAY_LORA
    a1 = (pm[:, o:o + ICLR_LORA], pm[:, o + ICLR_LORA:o + 2 * ICLR_LORA])
    o += 2 * ICLR_LORA
    g1 = pm[:, o:o + GATE_LORA]

    kk = k * kk_ref[...]
    lw, kd, ad = [], [], []
    for d in range(2):
        z = w0_ref[d:d + 1, :] + _bdot(jnp.tanh(w1[d]), w2_ref[d])
        lw.append(-jnp.exp(-_softplus(-z) - 0.5))
        a = _sigmoid(a0_ref[d:d + 1, :] + _bdot(a1[d], a2_ref[d]))
        ad.append(a)
        kd.append(k * (1.0 + (a - 1.0) * ka_ref[...]))
    gate = _bdot(_sigmoid(g1), g2_ref[...])
    rkk = r * (kd[0] + kd[1]) * rk_ref[...]

    for h in range(RW_HEADS):
        sl = slice(h * RW_HEAD_DIM, (h + 1) * RW_HEAD_DIM)
        kk_h = kk[:, sl]
        kk_h = kk_h * lax.rsqrt(jnp.sum(kk_h * kk_h, axis=-1, keepdims=True) + 1e-12)
        v_h = v[:, sl]
        r_ref[0, h] = r[:, sl]
        v_ref[0, h] = v_h
        na_ref[0, h] = -kk_h
        for d in range(2):
            lw_ref[d, 0, h] = lw[d][:, sl]
            kd_ref[d, 0, h] = kd[d][:, sl]
            bd_ref[d, 0, h] = kk_h * ad[d][:, sl]
        gate_ref[0, h] = gate[:, sl]
        bonus_ref[0, h] = jnp.sum(rkk[:, sl], axis=-1, keepdims=True) * v_h


def _rwkv_prep(p, n_lat, mu, w0, w2, a0, a2, k_k, k_a, r_k, g2):
    b, lt, cols = p.shape
    bl = TOKEN_BLOCK
    nblk = lt // bl
    per = bl // HALO
    last_slab = lt // HALO - 1
    prev_spec = pl.BlockSpec((1, HALO, cols), lambda i, j: (i, jnp.maximum(j * per - 1, 0), 0))
    next_spec = pl.BlockSpec((1, HALO, cols), lambda i, j: (i, jnp.minimum((j + 1) * per, last_slab), 0))

    hm = jax.ShapeDtypeStruct((b, RW_HEADS, lt, RW_HEAD_DIM), F32)
    hm2 = jax.ShapeDtypeStruct((2, b, RW_HEADS, lt, RW_HEAD_DIM), F32)
    hm_spec = pl.BlockSpec((1, RW_HEADS, bl, RW_HEAD_DIM), lambda i, j: (i, 0, j, 0))
    hm2_spec = pl.BlockSpec((2, 1, RW_HEADS, bl, RW_HEAD_DIM), lambda i, j: (0, i, 0, j, 0))
    full = lambda a: pl.BlockSpec(a.shape, lambda i, j: (0,) * a.ndim)
    consts = [mu.reshape(1, cols), w0, w2.astype(BF16), a0, a2.astype(BF16), k_k.reshape(1, RW_DIM),
              k_a.reshape(1, RW_DIM), r_k.reshape(1, RW_DIM), g2.astype(BF16)]
    return pl.pallas_call(
        functools.partial(_rwkv_prep_kernel, bl=bl, ctx_block=n_lat // bl),
        grid=(b, nblk),
        in_specs=[pl.BlockSpec((1, bl, cols), lambda i, j: (i, j, 0)), prev_spec, next_spec]
        + [full(a) for a in consts],
        out_specs=[hm_spec, hm_spec, hm_spec, hm2_spec, hm2_spec, hm2_spec, hm_spec, hm_spec],
        out_shape=[hm, hm, hm, hm2, hm2, hm2, hm, hm],
        compiler_params=_params("parallel", "parallel"),
        name="rwkv_prep",
    )(p, p, p, *consts)


def _wkv_kernel(r_ref, v_ref, a_ref, lw_ref, k_ref, b_ref, y_ref, st_ref, *, cn, chains):
    d = pl.program_id(0)

    @pl.when(pl.program_id(1) == 0)
    def _():
        st_ref[...] = jnp.zeros_like(st_ref)

    row = lax.broadcasted_iota(jnp.int32, (cn, cn), 0)
    col = lax.broadcasted_iota(jnp.int32, (cn, cn), 1)
    lead = (row - col) * jnp.where(d == 0, 1, -1)
    upto = lead >= 0
    row2 = lax.broadcasted_iota(jnp.int32, (2 * cn, 2 * cn), 0)
    col2 = lax.broadcasted_iota(jnp.int32, (2 * cn, 2 * cn), 1)
    lead2 = (row2 % cn - col2 % cn) * jnp.where(d == 0, 1, -1)
    mask4 = lead2 >= jnp.where(row2 < cn, 1, 0)
    eye = (col == row).astype(F32)
    cum_mat = upto.astype(F32)
    hd = r_ref.shape[-1]
    eye_h = (lax.broadcasted_iota(jnp.int32, (hd, hd), 0)
             == lax.broadcasted_iota(jnp.int32, (hd, hd), 1)).astype(F32)
    n_double = int(math.log2(cn)) - 1

    nb = SCAN_INTERLEAVE
    cum_b = jnp.broadcast_to(cum_mat, (nb, cn, cn))

    def chunk(sl, rows, st):
        r, v, a = r_ref[sl, rows], v_ref[sl, rows], a_ref[sl, rows]
        lw, k, b = lw_ref[0, sl, rows], k_ref[0, sl, rows], b_ref[0, sl, rows]
        lw_hi = lw.astype(BF16)
        lw_lo = (lw - lw_hi.astype(F32)).astype(BF16)
        cum2 = _bmm(cum_b, jnp.concatenate([lw_hi, lw_lo], axis=-1))
        cum = cum2[..., :hd] + cum2[..., hd:]
        tot = jnp.sum(lw, axis=1, keepdims=True)
        at = a * jnp.exp(cum - lw)
        rt = r * jnp.exp(cum)
        einv = jnp.exp(-cum)
        bt, kt = b * einv, k * einv
        eend = jnp.exp(tot - cum)
        bh, kh = b * eend, k * eend

        amat = _bmm_nt(jnp.concatenate([at, rt], axis=1), jnp.concatenate([bt, kt], axis=1))
        amat = jnp.where(mask4, amat, 0.0)
        a_ab = amat[:, :cn, :cn]

        t = eye + a_ab
        x = _bmm(a_ab, a_ab)
        for _ in range(n_double - 1):
            tx = _bmm(jnp.concatenate([t, x], axis=1), x)
            t = t + tx[:, :cn]
            x = tx[:, cn:]
        t = t + _bmm(t, x)

        zero = jnp.zeros_like(v)
        av = _bmm(amat[:, :cn], jnp.concatenate([zero, v], axis=1))
        wu = _bmm(t, jnp.concatenate([at, av], axis=-1))
        rhs = jnp.concatenate([wu, jnp.concatenate([zero, v], axis=-1)], axis=1)
        qy = _bmm(amat[:, cn:], rhs)
        mn = _bmm_tn(jnp.concatenate([bh, kh], axis=1), rhs)
        qe = rt + qy[..., :hd]
        mc = eye_h * jnp.exp(tot) + mn[..., :hd]

        out = _bmm(jnp.concatenate([qe, mc], axis=1), st)
        y_ref[0, sl, rows] = out[:, :cn] + qy[..., hd:]
        return out[:, cn:] + mn[..., hd:]

    n_sub = r_ref.shape[1] // cn

    def group(g, carry):
        sl = pl.ds(g * nb, nb)
        st = st_ref[sl]
        for s in range(n_sub):
            sub = jnp.where(d == 0, s, n_sub - 1 - s)
            st = chunk(sl, pl.ds(pl.multiple_of(sub * cn, cn), cn), st)
        st_ref[sl] = st
        return carry

    lax.fori_loop(0, chains // SCAN_INTERLEAVE, group, 0)


def _wkv_scan(r, v, na, lw, kd, bd, n_lat):
    b, h, lt, hd = r.shape
    cn = SCAN_CHUNK
    chains = b * h
    rows = cn * SCAN_CHUNKS_PER_STEP
    assert n_lat % rows == 0 and lt % rows == 0
    n_all, n_l = lt // rows, n_lat // rows
    n_c = n_all - n_l

    def chunk(d, i):
        fwd = jnp.where(i < n_c, n_l + i, i - n_c)
        rev = jnp.where(i < n_c, n_all - 1 - i, n_l - 1 - (i - n_c))
        return jnp.where(d == 0, fwd, rev)

    shared = pl.BlockSpec((chains, rows, hd), lambda d, i: (0, chunk(d, i), 0))
    per_dir = pl.BlockSpec((1, chains, rows, hd), lambda d, i: (d, 0, chunk(d, i), 0))
    flat = lambda t: t.reshape(t.shape[:-4] + (chains, lt, hd))
    y = pl.pallas_call(
        functools.partial(_wkv_kernel, cn=cn, chains=chains),
        grid=(2, n_all),
        in_specs=[shared, shared, shared, per_dir, per_dir, per_dir],
        out_specs=per_dir,
        out_shape=jax.ShapeDtypeStruct((2, chains, lt, hd), F32),
        scratch_shapes=[pltpu.VMEM((chains, hd, hd), F32)],
        compiler_params=_params("parallel", "arbitrary"),
        name="wkv_scan",
    )(flat(r), flat(v), flat(na), flat(lw), flat(kd), flat(bd))
    return y.reshape(2, b, h, lt, hd)


def _rope(x, cos, sin, half):
    lane = lax.broadcasted_iota(jnp.int32, x.shape, 1)
    width = x.shape[1]
    partner = jnp.where(lane % (2 * half) < half, pltpu.roll(x, width - half, 1), pltpu.roll(x, half, 1))
    return x * cos + partner * sin


def _mla_qkv_kernel(p_ref, gqa_ref, gkva_ref, wq_ref, wk_ref, wv_ref, gq_ref, gk_ref, cos_ref, sin_ref,
                    q_ref, kt_ref, v_ref, *, scale):
    p = p_ref[0]
    q_a = p[:, 0:MLA_Q_RANK]
    kv_a = p[:, MLA_Q_RANK:MLA_Q_RANK + MLA_KV_RANK]
    k_rope = p[:, MLA_Q_RANK + MLA_KV_RANK:MLA_Q_RANK + MLA_KV_RANK + LANES]
    qa_n = (_rms(q_a) * gqa_ref[...]).astype(BF16)
    kva_n = (_rms(kv_a) * gkva_ref[...]).astype(BF16)
    q_all = jnp.dot(qa_n, wq_ref[...], preferred_element_type=F32)
    k_all = jnp.dot(kva_n, wk_ref[...], preferred_element_type=F32)
    v_all = jnp.dot(kva_n, wv_ref[...], preferred_element_type=F32)
    cos, sin = cos_ref[...], sin_ref[...]
    for h in range(MLA_HEADS):
        sl = slice(h * LANES, (h + 1) * LANES)
        q = q_all[:, sl]
        q = q * lax.rsqrt(jnp.sum(q * q, axis=-1, keepdims=True) / MLA_QK + NORM_EPS) * gq_ref[...]
        q = _rope(q, cos, sin, MLA_ROPE // 4)
        q_ref[0, h] = (q * scale).astype(q_ref.dtype)
        k = k_all[:, sl] + k_rope
        k = k * lax.rsqrt(jnp.sum(k * k, axis=-1, keepdims=True) / MLA_QK + NORM_EPS) * gk_ref[...]
        k = _rope(k, cos, sin, MLA_ROPE // 4)
        kt_ref[0, h, 0] = k.T.astype(kt_ref.dtype)
        lane = lax.broadcasted_iota(jnp.int32, q.shape, 1)
        v_ref[0, h] = jnp.where(lane < MLA_V, v_all[:, sl], 1.0).astype(v_ref.dtype)


def _pad_heads(w, heads, width):
    k = w.shape[0]
    w = w.reshape(k, heads, width)
    return jnp.pad(w, ((0, 0), (0, 0), (0, LANES - width))).reshape(k, heads * LANES)


def _mla_qkv(p, g_qa, w_q_up, g_kva, w_kv_up, g_q, g_k, cos, sin):
    b, lt, cols = p.shape
    bl = TOKEN_BLOCK
    bk = _key_chunk(lt)
    wq = _pad_heads(w_q_up, MLA_HEADS, MLA_QK).astype(BF16)
    wkv = w_kv_up.reshape(MLA_KV_RANK, MLA_HEADS, MLA_NOPE + MLA_V)
    wk = _pad_heads(wkv[:, :, :MLA_NOPE].reshape(MLA_KV_RANK, -1), MLA_HEADS, MLA_NOPE).astype(BF16)
    wv = _pad_heads(wkv[:, :, MLA_NOPE:].reshape(MLA_KV_RANK, -1), MLA_HEADS, MLA_V).astype(BF16)
    pad = lambda g: jnp.pad(g, (0, LANES - MLA_QK)).reshape(1, LANES)
    consts = [g_qa.reshape(1, -1), g_kva.reshape(1, -1), wq, wk, wv, pad(g_q), pad(g_k)]
    full = lambda a: pl.BlockSpec(a.shape, lambda i, j: (0,) * a.ndim)
    tab = pl.BlockSpec((bl, LANES), lambda i, j: (j, 0))
    return pl.pallas_call(
        functools.partial(_mla_qkv_kernel, scale=MLA_QK ** -0.5 * LOG2E),
        grid=(b, lt // bl),
        in_specs=[pl.BlockSpec((1, bl, cols), lambda i, j: (i, j, 0))] + [full(a) for a in consts] + [tab, tab],
        out_specs=[pl.BlockSpec((1, MLA_HEADS, bl, LANES), lambda i, j: (i, 0, j, 0)),
                   _kt_spec(MLA_HEADS, LANES, bl, bk),
                   pl.BlockSpec((1, MLA_HEADS, bl, LANES), lambda i, j: (i, 0, j, 0))],
        out_shape=[jax.ShapeDtypeStruct((b, MLA_HEADS, lt, LANES), BF16),
                   jax.ShapeDtypeStruct((b, MLA_HEADS, lt // bk, LANES, bk), BF16),
                   jax.ShapeDtypeStruct((b, MLA_HEADS, lt, LANES), BF16)],
        compiler_params=_params("parallel", "parallel"),
        name="mla_qkv",
    )(p, *consts, cos, sin)


def _pair_norm_rope(x, g, cos, sin):
    lane = lax.broadcasted_iota(jnp.int32, x.shape, 1)
    lo = lane < GQA_HEAD_DIM
    xx = x * x
    ss_lo = jnp.sum(jnp.where(lo, xx, 0.0), axis=-1, keepdims=True)
    ss_hi = jnp.sum(jnp.where(lo, 0.0, xx), axis=-1, keepdims=True)
    rs = lax.rsqrt(jnp.where(lo, ss_lo, ss_hi) / GQA_HEAD_DIM + NORM_EPS)
    return _rope(x * rs * g, cos, sin, GQA_HEAD_DIM // 4)


def _gqa_qkv_kernel(x_ref, g_ref, ml_ref, mc_ref, win_ref, gq_ref, gk_ref, cos_ref, sin_ref, q_ref, kt_ref,
                    v_ref, *, scale, bl, n_lat):
    p = jnp.dot(_ln_mod(x_ref, g_ref, ml_ref, mc_ref, bl, n_lat), win_ref[...], preferred_element_type=F32)
    cos, sin = cos_ref[...], sin_ref[...]
    qw = GQA_HEADS * GQA_HEAD_DIM
    kw = GQA_KV_HEADS * GQA_HEAD_DIM
    for j in range(GQA_HEADS // 2):
        q = _pair_norm_rope(p[:, j * LANES:(j + 1) * LANES], gq_ref[...], cos, sin) * scale
        q_ref[0, 2 * j] = q[:, :GQA_HEAD_DIM].astype(q_ref.dtype)
        q_ref[0, 2 * j + 1] = q[:, GQA_HEAD_DIM:].astype(q_ref.dtype)
    for j in range(GQA_KV_HEADS // 2):
        k = _pair_norm_rope(p[:, qw + j * LANES:qw + (j + 1) * LANES], gk_ref[...], cos, sin)
        kt = k.T
        kt_ref[0, 2 * j, 0] = kt[:GQA_HEAD_DIM].astype(kt_ref.dtype)
        kt_ref[0, 2 * j + 1, 0] = kt[GQA_HEAD_DIM:].astype(kt_ref.dtype)
        v = p[:, qw + kw + j * LANES:qw + kw + (j + 1) * LANES]
        lo = lax.broadcasted_iota(jnp.int32, v.shape, 1) < GQA_HEAD_DIM
        v_ref[0, 2 * j] = jnp.where(lo, v, 1.0).astype(v_ref.dtype)
        v_ref[0, 2 * j + 1] = jnp.where(lo, pltpu.roll(v, GQA_HEAD_DIM, 1), 1.0).astype(v_ref.dtype)


def _gqa_qkv(x, g, mod_l, mod_c, w_in, n_lat, g_q, g_k, cos, sin):
    b, lt, d = x.shape
    bl = TOKEN_BLOCK
    bk = _key_chunk(lt)
    hd = GQA_HEAD_DIM
    two = lambda t: jnp.concatenate([t, t]).reshape(1, LANES)
    tab = pl.BlockSpec((bl, LANES), lambda i, j: (j, 0))
    vec = pl.BlockSpec((1, LANES), lambda i, j: (0, 0))
    return pl.pallas_call(
        functools.partial(_gqa_qkv_kernel, scale=hd ** -0.5 * LOG2E, bl=bl, n_lat=n_lat),
        grid=(b, lt // bl),
        in_specs=_ln_mod_specs(d, bl) + [pl.BlockSpec(w_in.shape, lambda i, j: (0, 0)), vec, vec, tab, tab],
        out_specs=[pl.BlockSpec((1, GQA_HEADS, bl, hd), lambda i, j: (i, 0, j, 0)),
                   _kt_spec(GQA_KV_HEADS, hd, bl, bk),
                   pl.BlockSpec((1, GQA_KV_HEADS, bl, LANES), lambda i, j: (i, 0, j, 0))],
        out_shape=[jax.ShapeDtypeStruct((b, GQA_HEADS, lt, hd), BF16),
                   jax.ShapeDtypeStruct((b, GQA_KV_HEADS, lt // bk, hd, bk), BF16),
                   jax.ShapeDtypeStruct((b, GQA_KV_HEADS, lt, LANES), BF16)],
        compiler_params=_params("parallel", "parallel"),
        name="gqa_qkv",
    )(x, g.reshape(1, d), mod_l, mod_c, w_in, two(g_q), two(g_k), cos, sin)


def _attn_kernel(q_ref, kt_ref, v_ref, o_ref, m_scr, acc_scr, *, groups, bq, n_chunks):
    rows = groups * bq
    q = q_ref[0, 0].reshape(rows, q_ref.shape[-1])
    bk = kt_ref.shape[-1]
    dv = o_ref.shape[-1]
    m_scr[...] = jnp.full_like(m_scr, -jnp.inf)
    acc_scr[...] = jnp.zeros_like(acc_scr)

    def step(j, carry):
        s = jnp.dot(q, kt_ref[0, 0, j], preferred_element_type=F32)
        m_prev = m_scr[...]
        m_new = jnp.maximum(m_prev, jnp.max(s, axis=-1, keepdims=True))
        alpha = jnp.exp2(m_prev - m_new)
        p = jnp.exp2(s - pltpu.repeat(m_new, bk // LANES, axis=1))
        vv = v_ref[0, 0, j * bk:(j + 1) * bk, :]
        acc_scr[...] = alpha * acc_scr[...] + jnp.dot(p.astype(BF16), vv, preferred_element_type=F32)
        m_scr[...] = m_new
        return carry

    for j in range(n_chunks):
        step(j, 0)
    acc = acc_scr[...]
    o = acc / pltpu.roll(acc, LANES - dv, 1)
    o_ref[0, 0] = o[:, :dv].reshape(groups, bq, dv).astype(o_ref.dtype)


def _attention_call(q, q_start, lq, ktc, v1):
    b, hq, lt, dk = q.shape
    hk, n_chunks, bk = ktc.shape[1], ktc.shape[2], ktc.shape[-1]
    lk = n_chunks * bk
    dv = LANES // 2
    groups = hq // hk
    bq = min(ATTN_ROWS // groups, lq)
    rows = groups * bq
    first = q_start // bq
    q5 = q.reshape(b, hk, groups, lt, dk)
    out = pl.pallas_call(
        functools.partial(_attn_kernel, groups=groups, bq=bq, n_chunks=n_chunks),
        grid=(b, hk, lq // bq),
        in_specs=[pl.BlockSpec((1, 1, groups, bq, dk), lambda i, h, j: (i, h, 0, j + first, 0)),
                  pl.BlockSpec((1, 1, n_chunks, dk, bk), lambda i, h, j: (i, h, 0, 0, 0)),
                  pl.BlockSpec((1, 1, lk, LANES), lambda i, h, j: (i, h, 0, 0))],
        out_specs=pl.BlockSpec((1, 1, groups, bq, dv), lambda i, h, j: (i, h, 0, j, 0)),
        out_shape=jax.ShapeDtypeStruct((b, hk, groups, lq, dv), BF16),
        scratch_shapes=[pltpu.VMEM((rows, LANES), F32), pltpu.VMEM((rows, LANES), F32)],
        compiler_params=_params("parallel", "parallel", "arbitrary"),
        name="attention",
    )(q5, ktc, v1)
    return out.reshape(b, hq, lq, dv)


def _attention(q, ktc, v1, n_lat):
    lt = q.shape[2]
    n_ctx = lt - n_lat
    o_l = _attention_call(q, 0, n_lat, ktc, v1)
    o_c = _attention_call(q, n_lat, n_ctx, ktc[:, :, -1:, :, -n_ctx:], v1[:, :, n_lat:])
    return o_l, o_c


def _attn_out_specs(heads, bl, n_lat):
    n_lb = n_lat // bl
    return [pl.BlockSpec((1, heads, bl, LANES // 2), lambda i, j: (i, 0, jnp.minimum(j, n_lb - 1), 0)),
            pl.BlockSpec((1, heads, bl, LANES // 2), lambda i, j: (i, 0, jnp.maximum(j - n_lb, 0), 0))]


def _heads_of(ol_ref, oc_ref, heads, bl, n_lat):
    is_ctx = pl.program_id(1) >= n_lat // bl
    return jnp.concatenate([jnp.where(is_ctx, oc_ref[0, h], ol_ref[0, h]) for h in range(heads)], axis=-1)


def _ab_out_kernel(x_ref, ml_ref, mc_ref, y_ref, bonus_ref, gate_ref, ol_ref, oc_ref, lnw_ref, lnb_ref, wrw_ref,
                   wmla_ref, out_ref, *, bl, n_lat):
    rw = []
    for h in range(RW_HEADS):
        y = y_ref[0, 0, h] + y_ref[1, 0, h]
        mean = jnp.mean(y, axis=-1, keepdims=True)
        yc = y - mean
        var = jnp.mean(yc * yc, axis=-1, keepdims=True)
        yn = yc * lax.rsqrt(var + LN_X_EPS) * lnw_ref[h] + lnb_ref[h]
        rw.append((yn + bonus_ref[0, h]) * gate_ref[0, h])
    rw = jnp.concatenate(rw, axis=-1).astype(BF16)
    o = _heads_of(ol_ref, oc_ref, MLA_HEADS, bl, n_lat)
    acc = (jnp.dot(rw, wrw_ref[...], preferred_element_type=F32)
           + jnp.dot(o, wmla_ref[...], preferred_element_type=F32))
    gate = _row_mods(ml_ref, mc_ref, pl.program_id(1) * bl, bl, n_lat, 2)
    out_ref[0] = x_ref[0] + gate * acc


def _ab_out(x, mod_l, mod_c, y, bonus, gate, o, ln_w, ln_b, w_out, n_lat, lt):
    b, _, d = x.shape
    bl = TOKEN_BLOCK
    hd = RW_HEAD_DIM
    wrw = w_out[:RW_DIM].astype(BF16)
    wmla = w_out[RW_DIM:].astype(BF16)
    xs = pl.BlockSpec((1, bl, d), lambda i, j: (i, j, 0))
    hm = pl.BlockSpec((1, RW_HEADS, bl, hd), lambda i, j: (i, 0, j, 0))
    full = lambda a: pl.BlockSpec(a.shape, lambda i, j: (0,) * a.ndim)
    lnw, lnb = ln_w.reshape(RW_HEADS, 1, hd), ln_b.reshape(RW_HEADS, 1, hd)
    return pl.pallas_call(
        functools.partial(_ab_out_kernel, bl=bl, n_lat=n_lat),
        grid=(b, lt // bl),
        in_specs=[xs, pl.BlockSpec((1, 6, d), lambda i, j: (i, 0, 0)), pl.BlockSpec((1, 6, d), lambda i, j: (0, 0, 0)),
                  pl.BlockSpec((2, 1, RW_HEADS, bl, hd), lambda i, j: (0, i, 0, j, 0)), hm, hm]
        + _attn_out_specs(MLA_HEADS, bl, n_lat) + [full(lnw), full(lnb), full(wrw), full(wmla)],
        out_specs=xs,
        out_shape=jax.ShapeDtypeStruct((b, lt, d), F32),
        compiler_params=_params("parallel", "parallel"),
        name="ab_out",
    )(x, mod_l, mod_c, y, bonus, gate, *o, lnw, lnb, wrw, wmla)


def _gqa_out_kernel(x_ref, ml_ref, mc_ref, ol_ref, oc_ref, w_ref, out_ref, *, bl, n_lat):
    o = _heads_of(ol_ref, oc_ref, GQA_HEADS, bl, n_lat)
    acc = jnp.dot(o, w_ref[...], preferred_element_type=F32)
    gate = _row_mods(ml_ref, mc_ref, pl.program_id(1) * bl, bl, n_lat, 2)
    out_ref[0] = x_ref[0] + gate * acc


def _gqa_out(x, mod_l, mod_c, o, w_out, n_lat, lt):
    b, _, d = x.shape
    bl = TOKEN_BLOCK
    hd = GQA_HEAD_DIM
    w = w_out.astype(BF16)
    xs = pl.BlockSpec((1, bl, d), lambda i, j: (i, j, 0))
    return pl.pallas_call(
        functools.partial(_gqa_out_kernel, bl=bl, n_lat=n_lat),
        grid=(b, lt // bl),
        in_specs=[xs, pl.BlockSpec((1, 6, d), lambda i, j: (i, 0, 0)), pl.BlockSpec((1, 6, d), lambda i, j: (0, 0, 0))]
        + _attn_out_specs(GQA_HEADS, bl, n_lat) + [pl.BlockSpec(w.shape, lambda i, j: (0, 0))],
        out_specs=xs,
        out_shape=jax.ShapeDtypeStruct((b, lt, d), F32),
        compiler_params=_params("parallel", "parallel"),
        name="gqa_out",
    )(x, mod_l, mod_c, *o, w)


def _first_argmax(vals):
    best, idx = vals[0], jnp.zeros(vals[0].shape, jnp.int32)
    for i in range(1, len(vals)):
        better = vals[i] > best
        idx = jnp.where(better, i, idx)
        best = jnp.where(better, vals[i], best)
    return best, idx


def _pick(vals, idx):
    out = vals[0]
    for i in range(1, len(vals)):
        out = jnp.where(idx == i, vals[i], out)
    return out


def _route_kernel(x_ref, g_ref, ml_ref, mc_ref, rwh_ref, rwl_ref, rb_ref, t_ref, comb_ref, grp_ref, *, bl, n_lat):
    first = pl.program_id(1) * bl
    shift = _row_mods(ml_ref, mc_ref, first, bl, n_lat, 3)
    scale = _row_mods(ml_ref, mc_ref, first, bl, n_lat, 4)
    t = (_rms(x_ref[0]) * g_ref[...]) * (1.0 + scale) + shift
    t_ref[0] = t.astype(t_ref.dtype)
    t_hi = t.astype(BF16)
    t_lo = (t - t_hi.astype(F32)).astype(BF16)
    logits = (jnp.dot(t_hi, rwh_ref[...], preferred_element_type=F32)
              + jnp.dot(t_lo, rwh_ref[...], preferred_element_type=F32)
              + jnp.dot(t_hi, rwl_ref[...], preferred_element_type=F32))
    lt = logits.T
    score = [_sigmoid(lt[e:e + 1]) for e in range(N_EXPERTS)]
    biased = [score[e] + rb_ref[e:e + 1, :] for e in range(N_EXPERTS)]
    epg = EXPERTS_PER_GROUP
    group_score = []
    for g in range(N_GROUPS):
        vals = biased[g * epg:(g + 1) * epg]
        pair = [vals[i] + vals[j] for i in range(epg) for j in range(i + 1, epg)]
        group_score.append(functools.reduce(jnp.maximum, pair))
    _, grp = _first_argmax(group_score)
    in_b = [_pick([biased[g * epg + j] for g in range(N_GROUPS)], grp) for j in range(epg)]
    in_s = [_pick([score[g * epg + j] for g in range(N_GROUPS)], grp) for j in range(epg)]
    _, loc1 = _first_argmax(in_b)
    _, loc2 = _first_argmax([jnp.where(loc1 == j, -jnp.inf, in_b[j]) for j in range(epg)])
    w1, w2 = _pick(in_s, loc1), _pick(in_s, loc2)
    wsum = w1 + w2
    w1, w2 = w1 / wsum, w2 / wsum
    e1, e2 = grp * epg + loc1, grp * epg + loc2
    sub = lax.broadcasted_iota(jnp.int32, (LANES, bl), 0)
    comb = jnp.zeros((LANES, bl), F32)
    for e in range(N_EXPERTS):
        c_e = jnp.where(e1 == e, w1, 0.0) + jnp.where(e2 == e, w2, 0.0)
        c_hi = c_e.astype(BF16).astype(F32)
        comb = jnp.where(sub == e, jnp.broadcast_to(c_hi, (LANES, bl)), comb)
        comb = jnp.where(sub == N_EXPERTS + e, jnp.broadcast_to(c_e - c_hi, (LANES, bl)), comb)
    comb_ref[0] = comb.T.astype(comb_ref.dtype)
    grp_ref[0] = grp


def _moe_route(x, g, mod_l, mod_c, router_w, router_b, n_lat):
    b, lt, d = x.shape
    bl = TOKEN_BLOCK
    rw = jnp.pad(router_w, ((0, 0), (0, LANES - N_EXPERTS)))
    rw_hi = rw.astype(BF16)
    rw_lo = (rw - rw_hi.astype(F32)).astype(BF16)
    xs = pl.BlockSpec((1, bl, d), lambda i, j: (i, j, 0))
    return pl.pallas_call(
        functools.partial(_route_kernel, bl=bl, n_lat=n_lat),
        grid=(b, lt // bl),
        in_specs=[xs, pl.BlockSpec((1, d), lambda i, j: (0, 0)),
                  pl.BlockSpec((1, 6, d), lambda i, j: (i, 0, 0)), pl.BlockSpec((1, 6, d), lambda i, j: (0, 0, 0)),
                  pl.BlockSpec(rw.shape, lambda i, j: (0, 0)), pl.BlockSpec(rw.shape, lambda i, j: (0, 0)),
                  pl.BlockSpec((N_EXPERTS, 1), lambda i, j: (0, 0))],
        out_specs=[xs, pl.BlockSpec((1, bl, LANES), lambda i, j: (i, j, 0)),
                   pl.BlockSpec((1, 1, bl), lambda i, j: (i, 0, j))],
        out_shape=[jax.ShapeDtypeStruct((b, lt, d), BF16), jax.ShapeDtypeStruct((b, lt, LANES), BF16),
                   jax.ShapeDtypeStruct((b, 1, lt), jnp.int32)],
        compiler_params=_params("parallel", "parallel"),
        name="moe_route",
    )(x, g.reshape(1, d), mod_l, mod_c, rw_hi, rw_lo, router_b.reshape(N_EXPERTS, 1))


def _ffn(t, wg, wu, wd):
    g = jnp.dot(t, wg, preferred_element_type=F32)
    u = jnp.dot(t, wu, preferred_element_type=F32)
    return jnp.dot((_silu(g) * u).astype(BF16), wd, preferred_element_type=F32)


def _experts_kernel(cnt_ref, x_ref, t_ref, comb_ref, grp_ref, tri_ref, ml_ref, mc_ref, wg_ref, wu_ref, wd_ref,
                    sg_ref, su_ref, sd_ref, out_ref, acc_ref, row_ref, col_ref, *, bl, n_lat):
    g = pl.program_id(2)
    t = t_ref[0]

    @pl.when(g == 0)
    def _():
        acc_ref[...] = _ffn(t, sg_ref[...], su_ref[...], sd_ref[...])
        grp = grp_ref[0]
        gid = lax.broadcasted_iota(jnp.int32, (2 * HALO, bl), 0)
        member = jnp.where(grp == gid, 1.0, 0.0)
        before = jnp.dot(member.astype(BF16), tri_ref[...], preferred_element_type=F32)
        slot = jnp.sum(member * before, axis=0, keepdims=True)
        sub = lax.broadcasted_iota(jnp.int32, (LANES, bl), 0)
        info = jnp.where(sub == 0, jnp.broadcast_to(slot, (LANES, bl)),
                         jnp.where(sub == 1, jnp.broadcast_to(grp.astype(F32), (LANES, bl)), 0.0))
        row_ref[...] = info[:HALO]
        col_ref[...] = info.T

    gf = g.astype(F32)
    slot_r, in_r = row_ref[0:1, :], jnp.where(row_ref[1:2, :] == gf, 1.0, 0.0)
    slot_c, in_c = col_ref[:, 0:1], jnp.where(col_ref[:, 1:2] == gf, 1.0, 0.0)
    count = cnt_ref[(pl.program_id(0) * pl.num_programs(1) + pl.program_id(1)) * N_GROUPS + g]
    wd = wd_ref[...]
    wd = wd.reshape(wd.shape[0] * wd.shape[1], wd.shape[2])
    lane = lax.broadcasted_iota(jnp.int32, (MOE_CAP, LANES), 1)

    def one_pass(k, carry):
        base = (k * MOE_CAP).astype(F32)
        want_r = base + lax.broadcasted_iota(jnp.int32, (MOE_CAP, bl), 0).astype(F32)
        want_c = base + lax.broadcasted_iota(jnp.int32, (bl, MOE_CAP), 1).astype(F32)
        gather = jnp.where(slot_r == want_r, in_r, 0.0).astype(BF16)
        scatter = jnp.where(slot_c == want_c, in_c, 0.0).astype(BF16)
        tg = jnp.dot(gather, t, preferred_element_type=F32).astype(BF16)
        cg = jnp.dot(gather, comb_ref[0], preferred_element_type=F32)
        hs = []
        for i in range(EXPERTS_PER_GROUP):
            e = g * EXPERTS_PER_GROUP + i
            cw = jnp.sum(jnp.where(lane % N_EXPERTS == e, cg, 0.0), axis=-1, keepdims=True)
            gate = jnp.dot(tg, wg_ref[i], preferred_element_type=F32)
            up = jnp.dot(tg, wu_ref[i], preferred_element_type=F32)
            hs.append((_silu(gate) * up * cw).astype(BF16))
        yg = jnp.dot(jnp.concatenate(hs, axis=-1), wd, preferred_element_type=F32)
        acc_ref[...] += jnp.dot(scatter, yg.astype(BF16), preferred_element_type=F32)
        return carry

    lax.fori_loop(0, (count + MOE_CAP - 1) // MOE_CAP, one_pass, 0)

    @pl.when(g == pl.num_programs(2) - 1)
    def _():
        gate = _row_mods(ml_ref, mc_ref, pl.program_id(1) * bl, bl, n_lat, 5)
        out_ref[0] = x_ref[0] + gate * acc_ref[...]


def _moe_experts(x, t, comb, grp, mod_l, mod_c, w_gate, w_up, w_down, sh_gate, sh_up, sh_down, n_lat):
    b, lt, d = x.shape
    bl = _row_block(lt, WIDE_ROWS)
    nblk = lt // bl
    ff = w_gate.shape[-1]
    epg = EXPERTS_PER_GROUP
    counts = jnp.sum(grp.reshape(b, nblk, bl, 1) == jnp.arange(N_GROUPS), axis=2, dtype=jnp.int32).reshape(-1)
    tri = jnp.triu(jnp.ones((bl, bl), BF16), 1)
    xs = pl.BlockSpec((1, bl, d), lambda i, j, g, c: (i, j, 0))
    full = lambda a: pl.BlockSpec(a.shape, lambda i, j, g, c: (0,) * a.ndim)
    ws = [w_gate.astype(BF16), w_up.astype(BF16), w_down.astype(BF16),
          sh_gate.astype(BF16), sh_up.astype(BF16), sh_down.astype(BF16)]
    grid_spec = pltpu.PrefetchScalarGridSpec(
        num_scalar_prefetch=1,
        grid=(b, nblk, N_GROUPS),
        in_specs=[xs, xs, pl.BlockSpec((1, bl, LANES), lambda i, j, g, c: (i, j, 0)),
                  pl.BlockSpec((1, 1, bl), lambda i, j, g, c: (i, 0, j)), full(tri),
                  pl.BlockSpec((1, 6, d), lambda i, j, g, c: (i, 0, 0)),
                  pl.BlockSpec((1, 6, d), lambda i, j, g, c: (0, 0, 0)),
                  pl.BlockSpec((epg, d, ff), lambda i, j, g, c: (g, 0, 0)),
                  pl.BlockSpec((epg, d, ff), lambda i, j, g, c: (g, 0, 0)),
                  pl.BlockSpec((epg, ff, d), lambda i, j, g, c: (g, 0, 0)),
                  full(ws[3]), full(ws[4]), full(ws[5])],
        out_specs=xs,
        scratch_shapes=[pltpu.VMEM((bl, d), F32), pltpu.VMEM((HALO, bl), F32), pltpu.VMEM((bl, LANES), F32)],
    )
    return pl.pallas_call(
        functools.partial(_experts_kernel, bl=bl, n_lat=n_lat),
        grid_spec=grid_spec,
        out_shape=jax.ShapeDtypeStruct((b, lt, d), F32),
        compiler_params=_params("parallel", "parallel", "arbitrary"),
        name="moe_experts",
    )(counts, x, t, comb, grp, tri, mod_l, mod_c, *ws)


def _rope_tables(seq, n_ctx, rope_dims, lane_offset, repeat):
    half = rope_dims // 4
    t = jnp.arange(seq)
    rowp = (t // GRID_W).astype(F32)
    colp = (t % GRID_W).astype(F32)
    inv = ROPE_THETA ** (-jnp.arange(half, dtype=F32) / half)
    ar, ac = rowp[:, None] * inv[None, :], colp[:, None] * inv[None, :]
    cos = jnp.concatenate([jnp.cos(ar), jnp.cos(ar), jnp.cos(ac), jnp.cos(ac)], axis=1)
    sin = jnp.concatenate([-jnp.sin(ar), jnp.sin(ar), -jnp.sin(ac), jnp.sin(ac)], axis=1)
    width = LANES // repeat
    padl, padr = lane_offset, width - lane_offset - rope_dims
    cos = jnp.pad(cos, ((0, n_ctx), (0, 0)), constant_values=1.0)
    sin = jnp.pad(sin, ((0, n_ctx), (0, 0)))
    cos = jnp.pad(cos, ((0, 0), (padl, padr)), constant_values=1.0)
    sin = jnp.pad(sin, ((0, 0), (padl, padr)))
    return jnp.tile(cos, (1, repeat)), jnp.tile(sin, (1, repeat))


def kernel(x, c, ctx, c_ctx, ada_w, ada_b, norm1_g, norm2_g, ab_w_in, ab_w_out, rw_mu, rw_w0, rw_w2, rw_a0, rw_a2, rw_k_k, rw_k_a, rw_r_k, rw_g2, rw_ln_w, rw_ln_b, mla_g_qa, mla_w_q_up, mla_g_kva, mla_w_kv_up, mla_g_q, mla_g_k, gqa_w_in, gqa_w_out, gqa_g_q, gqa_g_k, router_w, router_b, moe_w_gate, moe_w_up, moe_w_down, shared_w_gate, shared_w_up, shared_w_down):
    b, n_lat, d = x.shape
    n_ctx = ctx.shape[1]
    depth = ada_w.shape[0]
    xs = jnp.concatenate([x, ctx], axis=1)

    cvec = jnp.concatenate([c, c_ctx[None, :]], axis=0)
    cvec = jnp.pad(cvec, ((0, 8 - (b + 1) % 8), (0, 0))) if (b + 1) % 8 else cvec
    mods = _ada_mods(cvec, ada_w, ada_b).reshape(depth, -1, 6, d)

    mla_cos, mla_sin = _rope_tables(n_lat, n_ctx, MLA_ROPE, MLA_NOPE, 1)
    gqa_cos, gqa_sin = _rope_tables(n_lat, n_ctx, GQA_HEAD_DIM, 0, 2)

    for l in range(depth):
        i = l // 2
        mod_l, mod_c = mods[l, :b], mods[l, b:b + 1]
        rows = n_lat + n_ctx
        if l % 2 == 0:
            w_in = ab_w_in[i]
            w_rw = w_in[:, :RW_COLS].astype(BF16)
            w_m = w_in[:, RW_COLS:]
            w_mla = jnp.concatenate(
                [w_m[:, :MLA_Q_RANK + MLA_KV_RANK],
                 jnp.pad(w_m[:, MLA_Q_RANK + MLA_KV_RANK:], ((0, 0), (MLA_NOPE, LANES - MLA_NOPE - MLA_ROPE)))],
                axis=1).astype(BF16)
            p_rw, p_mla = _ln_mod_mm(xs, norm1_g[l], mod_l, mod_c, [w_rw, w_mla], n_lat)
            r, v, na, lw, kd, bd, gate, bonus = _rwkv_prep(
                p_rw, n_lat, rw_mu[i], rw_w0[i], rw_w2[i], rw_a0[i], rw_a2[i], rw_k_k[i], rw_k_a[i],
                rw_r_k[i].reshape(-1), rw_g2[i])
            y = _wkv_scan(r, v, na, lw, kd, bd, n_lat)
            q, kt, vm = _mla_qkv(p_mla, mla_g_qa[i], mla_w_q_up[i], mla_g_kva[i], mla_w_kv_up[i],
                                 mla_g_q[i], mla_g_k[i], mla_cos, mla_sin)
            o = _attention(q, kt, vm, n_lat)
            xs = _ab_out(xs, mod_l, mod_c, y, bonus, gate, o, rw_ln_w[i], rw_ln_b[i], ab_w_out[i], n_lat, rows)
        else:
            q, kt, vg = _gqa_qkv(xs, norm1_g[l], mod_l, mod_c, gqa_w_in[i].astype(BF16), n_lat,
                                 gqa_g_q[i], gqa_g_k[i], gqa_cos, gqa_sin)
            o = _attention(q, kt, vg, n_lat)
            xs = _gqa_out(xs, mod_l, mod_c, o, gqa_w_out[i], n_lat, rows)
        t, comb, grp = _moe_route(xs, norm2_g[l], mod_l, mod_c, router_w, router_b, n_lat)
        xs = _moe_experts(xs, t, comb, grp, mod_l, mod_c, moe_w_gate[l], moe_w_up[l], moe_w_down[l],
                          shared_w_gate[l], shared_w_up[l], shared_w_down[l], n_lat)
    return xs[:, :n_lat]
```

```python
import functools
import math

import jax
import jax.numpy as jnp
from jax import lax
from jax.experimental import pallas as pl
from jax.experimental.pallas import tpu as pltpu

F32 = jnp.float32
BF16 = jnp.bfloat16
HIGHEST = lax.Precision.HIGHEST

GRID_W = 64
RW_HEADS = 8
RW_HEAD_DIM = 64
RW_DIM = RW_HEADS * RW_HEAD_DIM
DECAY_LORA = 64
ICLR_LORA = 64
GATE_LORA = 128
RW_COLS = 3 * RW_DIM + 2 * DECAY_LORA + 2 * ICLR_LORA + GATE_LORA
LN_X_EPS = 64e-5
MLA_HEADS = 8
MLA_NOPE = 64
MLA_ROPE = 32
MLA_V = 64
MLA_QK = MLA_NOPE + MLA_ROPE
MLA_Q_RANK = 384
MLA_KV_RANK = 256
GQA_HEADS = 16
GQA_KV_HEADS = 4
GQA_HEAD_DIM = 64
N_EXPERTS = 16
N_GROUPS = 4
EXPERTS_PER_GROUP = N_EXPERTS // N_GROUPS
ROPE_THETA = 10000.0
NORM_EPS = 1e-6

LANES = 128
V7X_VMEM_BYTES = 64 * 1024 * 1024
VMEM_LIMIT = V7X_VMEM_BYTES - 8 * 1024 * 1024

TOKEN_BLOCK = 256
HALO = 8
SCAN_CHUNK = 64
SCAN_INTERLEAVE = 16
SCAN_CHUNKS_PER_STEP = 4
ATTN_ROWS = 1024
ATTN_KEYS = 2816
WIDE_ROWS = 768
MOE_CAP = 256
LOG2E = math.log2(math.e)


def _row_block(lt, limit):
    return max(r for r in range(TOKEN_BLOCK, limit + 1, TOKEN_BLOCK) if lt % r == 0)


def _key_chunk(lt):
    return max(c for c in range(TOKEN_BLOCK, ATTN_KEYS + 1, TOKEN_BLOCK) if lt % c == 0)


def _kt_spec(heads, dk, bl, bk):
    per = bk // bl
    return pl.BlockSpec((1, heads, 1, dk, bl), lambda i, j: (i, 0, j // per, 0, j % per))


def _params(*sem):
    return pltpu.CompilerParams(dimension_semantics=sem, vmem_limit_bytes=VMEM_LIMIT)


def _bdot(a, b):
    return jnp.dot(a.astype(BF16), b.astype(BF16), preferred_element_type=F32)


def _batched(a, b, ca, cb):
    return lax.dot_general(a.astype(BF16), b.astype(BF16), (((ca,), (cb,)), ((0,), (0,))),
                           preferred_element_type=F32)


def _bmm(a, b):
    return _batched(a, b, 2, 1)


def _bmm_nt(a, b):
    return _batched(a, b, 2, 2)


def _bmm_tn(a, b):
    return _batched(a, b, 1, 1)


def _sigmoid(x):
    return 1.0 / (1.0 + jnp.exp(-x))


def _silu(x):
    return x * _sigmoid(x)


def _softplus(x):
    return jnp.maximum(x, 0.0) + jnp.log(1.0 + jnp.exp(-jnp.abs(x)))


def _rms(x):
    return x * lax.rsqrt(jnp.mean(x * x, axis=-1, keepdims=True) + NORM_EPS)


def _row_mods(ml_ref, mc_ref, first_row, rows, n_lat, idx):
    r = first_row + lax.broadcasted_iota(jnp.int32, (rows, 1), 0)
    return jnp.where(r >= n_lat, mc_ref[0, idx:idx + 1, :], ml_ref[0, idx:idx + 1, :])


def _ada_kernel(c_ref, w_ref, b_ref, o_ref):
    s = _silu(c_ref[...])
    o_ref[0] = jnp.dot(s, w_ref[0], preferred_element_type=F32, precision=HIGHEST) + b_ref[0]


def _ada_mods(cvec, ada_w, ada_b):
    depth, d, n = ada_w.shape
    rows = cvec.shape[0]
    bn = n // 4
    return pl.pallas_call(
        _ada_kernel,
        grid=(depth, n // bn),
        in_specs=[pl.BlockSpec((rows, d), lambda l, j: (0, 0)),
                  pl.BlockSpec((1, d, bn), lambda l, j: (l, 0, j)),
                  pl.BlockSpec((1, 1, bn), lambda l, j: (l, 0, j))],
        out_specs=pl.BlockSpec((1, rows, bn), lambda l, j: (l, 0, j)),
        out_shape=jax.ShapeDtypeStruct((depth, rows, n), F32),
        compiler_params=_params("parallel", "parallel"),
        name="ada_mods",
    )(cvec, ada_w, ada_b.reshape(depth, 1, n))


def _ln_mod(x_ref, g_ref, ml_ref, mc_ref, bl, n_lat):
    first = pl.program_id(1) * bl
    shift = _row_mods(ml_ref, mc_ref, first, bl, n_lat, 0)
    scale = _row_mods(ml_ref, mc_ref, first, bl, n_lat, 1)
    h = (_rms(x_ref[0]) * g_ref[...]) * (1.0 + scale) + shift
    return h.astype(BF16)


def _ln_mod_specs(d, bl):
    return [pl.BlockSpec((1, bl, d), lambda i, j: (i, j, 0)), pl.BlockSpec((1, d), lambda i, j: (0, 0)),
            pl.BlockSpec((1, 6, d), lambda i, j: (i, 0, 0)), pl.BlockSpec((1, 6, d), lambda i, j: (0, 0, 0))]


def _ln_mod_mm_kernel(x_ref, g_ref, ml_ref, mc_ref, *refs, n_w, bl, n_lat):
    w_refs, o_refs = refs[:n_w], refs[n_w:]
    hb = _ln_mod(x_ref, g_ref, ml_ref, mc_ref, bl, n_lat)
    for w_ref, o_ref in zip(w_refs, o_refs):
        o_ref[0] = jnp.dot(hb, w_ref[...], preferred_element_type=F32).astype(o_ref.dtype)


def _ln_mod_mm(x, g, mod_l, mod_c, ws, n_lat):
    b, lt, d = x.shape
    bl = _row_block(lt, WIDE_ROWS)
    kern = functools.partial(_ln_mod_mm_kernel, n_w=len(ws), bl=bl, n_lat=n_lat)
    return pl.pallas_call(
        kern,
        grid=(b, lt // bl),
        in_specs=[pl.BlockSpec((1, bl, d), lambda i, j: (i, j, 0)),
                  pl.BlockSpec((1, d), lambda i, j: (0, 0)),
                  pl.BlockSpec((1, 6, d), lambda i, j: (i, 0, 0)),
                  pl.BlockSpec((1, 6, d), lambda i, j: (0, 0, 0))]
        + [pl.BlockSpec(w.shape, lambda i, j: (0, 0)) for w in ws],
        out_specs=[pl.BlockSpec((1, bl, w.shape[1]), lambda i, j: (i, j, 0)) for w in ws],
        out_shape=[jax.ShapeDtypeStruct((b, lt, w.shape[1]), F32) for w in ws],
        compiler_params=_params("parallel", "parallel"),
        name="ln_mod_mm",
    )(x, g.reshape(1, d), mod_l, mod_c, *ws)


def _rwkv_prep_kernel(p_ref, prev_ref, next_ref, mu_ref, w0_ref, w2_ref, a0_ref, a2_ref, kk_ref,
                      ka_ref, rk_ref, g2_ref, r_ref, v_ref, na_ref, lw_ref, kd_ref, bd_ref,
                      gate_ref, bonus_ref, *, bl, ctx_block):
    p = p_ref[0]
    j = pl.program_id(1)
    starts = jnp.logical_or(j == 0, j == ctx_block)
    ends = jnp.logical_or(j == ctx_block - 1, j == pl.num_programs(1) - 1)
    before = jnp.where(starts, 0.0, prev_ref[0, HALO - 1:HALO, :])
    after = jnp.where(ends, 0.0, next_ref[0, 0:1, :])
    row = lax.broadcasted_iota(jnp.int32, p.shape, 0)
    prev = jnp.where(row == 0, before, pltpu.roll(p, 1, 0))
    nxt = jnp.where(row == bl - 1, after, pltpu.roll(p, bl - 1, 0))
    pm = p + mu_ref[...] * (0.5 * (prev + nxt) - p)

    o = 3 * RW_DIM
    r = pm[:, 0:RW_DIM]
    k = pm[:, RW_DIM:2 * RW_DIM]
    v = pm[:, 2 * RW_DIM:o]
    w1 = (pm[:, o:o + DECAY_LORA], pm[:, o + DECAY_LORA:o + 2 * DECAY_LORA])
    o += 2 * DECAY_LORA
    a1 = (pm[:, o:o + ICLR_LORA], pm[:, o + ICLR_LORA:o + 2 * ICLR_LORA])
    o += 2 * ICLR_LORA
    g1 = pm[:, o:o + GATE_LORA]

    kk = k * kk_ref[...]
    lw, kd, ad = [], [], []
    for d in range(2):
        z = w0_ref[d:d + 1, :] + _bdot(jnp.tanh(w1[d]), w2_ref[d])
        lw.append(-jnp.exp(-_softplus(-z) - 0.5))
        a = _sigmoid(a0_ref[d:d + 1, :] + _bdot(a1[d], a2_ref[d]))
        ad.append(a)
        kd.append(k * (1.0 + (a - 1.0) * ka_ref[...]))
    gate = _bdot(_sigmoid(g1), g2_ref[...])
    rkk = r * (kd[0] + kd[1]) * rk_ref[...]

    for h in range(RW_HEADS):
        sl = slice(h * RW_HEAD_DIM, (h + 1) * RW_HEAD_DIM)
        kk_h = kk[:, sl]
        kk_h = kk_h * lax.rsqrt(jnp.sum(kk_h * kk_h, axis=-1, keepdims=True) + 1e-12)
        v_h = v[:, sl]
        r_ref[0, h] = r[:, sl]
        v_ref[0, h] = v_h
        na_ref[0, h] = -kk_h
        for d in range(2):
            lw_ref[d, 0, h] = lw[d][:, sl]
            kd_ref[d, 0, h] = kd[d][:, sl]
            bd_ref[d, 0, h] = kk_h * ad[d][:, sl]
        gate_ref[0, h] = gate[:, sl]
        bonus_ref[0, h] = jnp.sum(rkk[:, sl], axis=-1, keepdims=True) * v_h


def _rwkv_prep(p, n_lat, mu, w0, w2, a0, a2, k_k, k_a, r_k, g2):
    b, lt, cols = p.shape
    bl = TOKEN_BLOCK
    nblk = lt // bl
    per = bl // HALO
    last_slab = lt // HALO - 1
    prev_spec = pl.BlockSpec((1, HALO, cols), lambda i, j: (i, jnp.maximum(j * per - 1, 0), 0))
    next_spec = pl.BlockSpec((1, HALO, cols), lambda i, j: (i, jnp.minimum((j + 1) * per, last_slab), 0))

    hm = jax.ShapeDtypeStruct((b, RW_HEADS, lt, RW_HEAD_DIM), F32)
    hm2 = jax.ShapeDtypeStruct((2, b, RW_HEADS, lt, RW_HEAD_DIM), F32)
    hm_spec = pl.BlockSpec((1, RW_HEADS, bl, RW_HEAD_DIM), lambda i, j: (i, 0, j, 0))
    hm2_spec = pl.BlockSpec((2, 1, RW_HEADS, bl, RW_HEAD_DIM), lambda i, j: (0, i, 0, j, 0))
    full = lambda a: pl.BlockSpec(a.shape, lambda i, j: (0,) * a.ndim)
    consts = [mu.reshape(1, cols), w0, w2.astype(BF16), a0, a2.astype(BF16), k_k.reshape(1, RW_DIM),
              k_a.reshape(1, RW_DIM), r_k.reshape(1, RW_DIM), g2.astype(BF16)]
    return pl.pallas_call(
        functools.partial(_rwkv_prep_kernel, bl=bl, ctx_block=n_lat // bl),
        grid=(b, nblk),
        in_specs=[pl.BlockSpec((1, bl, cols), lambda i, j: (i, j, 0)), prev_spec, next_spec]
        + [full(a) for a in consts],
        out_specs=[hm_spec, hm_spec, hm_spec, hm2_spec, hm2_spec, hm2_spec, hm_spec, hm_spec],
        out_shape=[hm, hm, hm, hm2, hm2, hm2, hm, hm],
        compiler_params=_params("parallel", "parallel"),
        name="rwkv_prep",
    )(p, p, p, *consts)


def _wkv_kernel(r_ref, v_ref, a_ref, lw_ref, k_ref, b_ref, y_ref, st_ref, *, cn, chains):
    d = pl.program_id(0)

    @pl.when(pl.program_id(1) == 0)
    def _():
        st_ref[...] = jnp.zeros_like(st_ref)

    row = lax.broadcasted_iota(jnp.int32, (cn, cn), 0)
    col = lax.broadcasted_iota(jnp.int32, (cn, cn), 1)
    lead = (row - col) * jnp.where(d == 0, 1, -1)
    upto = lead >= 0
    row2 = lax.broadcasted_iota(jnp.int32, (2 * cn, 2 * cn), 0)
    col2 = lax.broadcasted_iota(jnp.int32, (2 * cn, 2 * cn), 1)
    lead2 = (row2 % cn - col2 % cn) * jnp.where(d == 0, 1, -1)
    mask4 = lead2 >= jnp.where(row2 < cn, 1, 0)
    eye = (col == row).astype(F32)
    cum_mat = upto.astype(F32)
    hd = r_ref.shape[-1]
    eye_h = (lax.broadcasted_iota(jnp.int32, (hd, hd), 0)
             == lax.broadcasted_iota(jnp.int32, (hd, hd), 1)).astype(F32)
    n_double = int(math.log2(cn)) - 1

    nb = SCAN_INTERLEAVE
    cum_b = jnp.broadcast_to(cum_mat, (nb, cn, cn))

    def chunk(sl, rows, st):
        r, v, a = r_ref[sl, rows], v_ref[sl, rows], a_ref[sl, rows]
        lw, k, b = lw_ref[0, sl, rows], k_ref[0, sl, rows], b_ref[0, sl, rows]
        lw_hi = lw.astype(BF16)
        lw_lo = (lw - lw_hi.astype(F32)).astype(BF16)
        cum2 = _bmm(cum_b, jnp.concatenate([lw_hi, lw_lo], axis=-1))
        cum = cum2[..., :hd] + cum2[..., hd:]
        tot = jnp.sum(lw, axis=1, keepdims=True)
        at = a * jnp.exp(cum - lw)
        rt = r * jnp.exp(cum)
        einv = jnp.exp(-cum)
        bt, kt = b * einv, k * einv
        eend = jnp.exp(tot - cum)
        bh, kh = b * eend, k * eend

        amat = _bmm_nt(jnp.concatenate([at, rt], axis=1), jnp.concatenate([bt, kt], axis=1))
        amat = jnp.where(mask4, amat, 0.0)
        a_ab = amat[:, :cn, :cn]

        t = eye + a_ab
        x = _bmm(a_ab, a_ab)
        for _ in range(n_double - 1):
            tx = _bmm(jnp.concatenate([t, x], axis=1), x)
            t = t + tx[:, :cn]
            x = tx[:, cn:]
        t = t + _bmm(t, x)

        zero = jnp.zeros_like(v)
        av = _bmm(amat[:, :cn], jnp.concatenate([zero, v], axis=1))
        wu = _bmm(t, jnp.concatenate([at, av], axis=-1))
        rhs = jnp.concatenate([wu, jnp.concatenate([zero, v], axis=-1)], axis=1)
        qy = _bmm(amat[:, cn:], rhs)
        mn = _bmm_tn(jnp.concatenate([bh, kh], axis=1), rhs)
        qe = rt + qy[..., :hd]
        mc = eye_h * jnp.exp(tot) + mn[..., :hd]

        out = _bmm(jnp.concatenate([qe, mc], axis=1), st)
        y_ref[0, sl, rows] = out[:, :cn] + qy[..., hd:]
        return out[:, cn:] + mn[..., hd:]

    n_sub = r_ref.shape[1] // cn

    def group(g, carry):
        sl = pl.ds(g * nb, nb)
        st = st_ref[sl]
        for s in range(n_sub):
            sub = jnp.where(d == 0, s, n_sub - 1 - s)
            st = chunk(sl, pl.ds(pl.multiple_of(sub * cn, cn), cn), st)
        st_ref[sl] = st
        return carry

    lax.fori_loop(0, chains // SCAN_INTERLEAVE, group, 0)


def _wkv_scan(r, v, na, lw, kd, bd, n_lat):
    b, h, lt, hd = r.shape
    cn = SCAN_CHUNK
    chains = b * h
    rows = cn * SCAN_CHUNKS_PER_STEP
    assert n_lat % rows == 0 and lt % rows == 0
    n_all, n_l = lt // rows, n_lat // rows
    n_c = n_all - n_l

    def chunk(d, i):
        fwd = jnp.where(i < n_c, n_l + i, i - n_c)
        rev = jnp.where(i < n_c, n_all - 1 - i, n_l - 1 - (i - n_c))
        return jnp.where(d == 0, fwd, rev)

    shared = pl.BlockSpec((chains, rows, hd), lambda d, i: (0, chunk(d, i), 0))
    per_dir = pl.BlockSpec((1, chains, rows, hd), lambda d, i: (d, 0, chunk(d, i), 0))
    flat = lambda t: t.reshape(t.shape[:-4] + (chains, lt, hd))
    y = pl.pallas_call(
        functools.partial(_wkv_kernel, cn=cn, chains=chains),
        grid=(2, n_all),
        in_specs=[shared, shared, shared, per_dir, per_dir, per_dir],
        out_specs=per_dir,
        out_shape=jax.ShapeDtypeStruct((2, chains, lt, hd), F32),
        scratch_shapes=[pltpu.VMEM((chains, hd, hd), F32)],
        compiler_params=_params("parallel", "arbitrary"),
        name="wkv_scan",
    )(flat(r), flat(v), flat(na), flat(lw), flat(kd), flat(bd))
    return y.reshape(2, b, h, lt, hd)


def _rope(x, cos, sin, half):
    lane = lax.broadcasted_iota(jnp.int32, x.shape, 1)
    width = x.shape[1]
    partner = jnp.where(lane % (2 * half) < half, pltpu.roll(x, width - half, 1), pltpu.roll(x, half, 1))
    return x * cos + partner * sin


def _mla_qkv_kernel(p_ref, gqa_ref, gkva_ref, wq_ref, wk_ref, wv_ref, gq_ref, gk_ref, cos_ref, sin_ref,
                    q_ref, kt_ref, v_ref, *, scale):
    p = p_ref[0]
    q_a = p[:, 0:MLA_Q_RANK]
    kv_a = p[:, MLA_Q_RANK:MLA_Q_RANK + MLA_KV_RANK]
    k_rope = p[:, MLA_Q_RANK + MLA_KV_RANK:MLA_Q_RANK + MLA_KV_RANK + LANES]
    qa_n = (_rms(q_a) * gqa_ref[...]).astype(BF16)
    kva_n = (_rms(kv_a) * gkva_ref[...]).astype(BF16)
    q_all = jnp.dot(qa_n, wq_ref[...], preferred_element_type=F32)
    k_all = jnp.dot(kva_n, wk_ref[...], preferred_element_type=F32)
    v_all = jnp.dot(kva_n, wv_ref[...], preferred_element_type=F32)
    cos, sin = cos_ref[...], sin_ref[...]
    for h in range(MLA_HEADS):
        sl = slice(h * LANES, (h + 1) * LANES)
        q = q_all[:, sl]
        q = q * lax.rsqrt(jnp.sum(q * q, axis=-1, keepdims=True) / MLA_QK + NORM_EPS) * gq_ref[...]
        q = _rope(q, cos, sin, MLA_ROPE // 4)
        q_ref[0, h] = (q * scale).astype(q_ref.dtype)
        k = k_all[:, sl] + k_rope
        k = k * lax.rsqrt(jnp.sum(k * k, axis=-1, keepdims=True) / MLA_QK + NORM_EPS) * gk_ref[...]
        k = _rope(k, cos, sin, MLA_ROPE // 4)
        kt_ref[0, h, 0] = k.T.astype(kt_ref.dtype)
        lane = lax.broadcasted_iota(jnp.int32, q.shape, 1)
        v_ref[0, h] = jnp.where(lane < MLA_V, v_all[:, sl], 1.0).astype(v_ref.dtype)


def _pad_heads(w, heads, width):
    k = w.shape[0]
    w = w.reshape(k, heads, width)
    return jnp.pad(w, ((0, 0), (0, 0), (0, LANES - width))).reshape(k, heads * LANES)


def _mla_qkv(p, g_qa, w_q_up, g_kva, w_kv_up, g_q, g_k, cos, sin):
    b, lt, cols = p.shape
    bl = TOKEN_BLOCK
    bk = _key_chunk(lt)
    wq = _pad_heads(w_q_up, MLA_HEADS, MLA_QK).astype(BF16)
    wkv = w_kv_up.reshape(MLA_KV_RANK, MLA_HEADS, MLA_NOPE + MLA_V)
    wk = _pad_heads(wkv[:, :, :MLA_NOPE].reshape(MLA_KV_RANK, -1), MLA_HEADS, MLA_NOPE).astype(BF16)
    wv = _pad_heads(wkv[:, :, MLA_NOPE:].reshape(MLA_KV_RANK, -1), MLA_HEADS, MLA_V).astype(BF16)
    pad = lambda g: jnp.pad(g, (0, LANES - MLA_QK)).reshape(1, LANES)
    consts = [g_qa.reshape(1, -1), g_kva.reshape(1, -1), wq, wk, wv, pad(g_q), pad(g_k)]
    full = lambda a: pl.BlockSpec(a.shape, lambda i, j: (0,) * a.ndim)
    tab = pl.BlockSpec((bl, LANES), lambda i, j: (j, 0))
    return pl.pallas_call(
        functools.partial(_mla_qkv_kernel, scale=MLA_QK ** -0.5 * LOG2E),
        grid=(b, lt // bl),
        in_specs=[pl.BlockSpec((1, bl, cols), lambda i, j: (i, j, 0))] + [full(a) for a in consts] + [tab, tab],
        out_specs=[pl.BlockSpec((1, MLA_HEADS, bl, LANES), lambda i, j: (i, 0, j, 0)),
                   _kt_spec(MLA_HEADS, LANES, bl, bk),
                   pl.BlockSpec((1, MLA_HEADS, bl, LANES), lambda i, j: (i, 0, j, 0))],
        out_shape=[jax.ShapeDtypeStruct((b, MLA_HEADS, lt, LANES), BF16),
                   jax.ShapeDtypeStruct((b, MLA_HEADS, lt // bk, LANES, bk), BF16),
                   jax.ShapeDtypeStruct((b, MLA_HEADS, lt, LANES), BF16)],
        compiler_params=_params("parallel", "parallel"),
        name="mla_qkv",
    )(p, *consts, cos, sin)


def _pair_norm_rope(x, g, cos, sin):
    lane = lax.broadcasted_iota(jnp.int32, x.shape, 1)
    lo = lane < GQA_HEAD_DIM
    xx = x * x
    ss_lo = jnp.sum(jnp.where(lo, xx, 0.0), axis=-1, keepdims=True)
    ss_hi = jnp.sum(jnp.where(lo, 0.0, xx), axis=-1, keepdims=True)
    rs = lax.rsqrt(jnp.where(lo, ss_lo, ss_hi) / GQA_HEAD_DIM + NORM_EPS)
    return _rope(x * rs * g, cos, sin, GQA_HEAD_DIM // 4)


def _gqa_qkv_kernel(x_ref, g_ref, ml_ref, mc_ref, win_ref, gq_ref, gk_ref, cos_ref, sin_ref, q_ref, kt_ref,
                    v_ref, *, scale, bl, n_lat):
    p = jnp.dot(_ln_mod(x_ref, g_ref, ml_ref, mc_ref, bl, n_lat), win_ref[...], preferred_element_type=F32)
    cos, sin = cos_ref[...], sin_ref[...]
    qw = GQA_HEADS * GQA_HEAD_DIM
    kw = GQA_KV_HEADS * GQA_HEAD_DIM
    for j in range(GQA_HEADS // 2):
        q = _pair_norm_rope(p[:, j * LANES:(j + 1) * LANES], gq_ref[...], cos, sin) * scale
        q_ref[0, 2 * j] = q[:, :GQA_HEAD_DIM].astype(q_ref.dtype)
        q_ref[0, 2 * j + 1] = q[:, GQA_HEAD_DIM:].astype(q_ref.dtype)
    for j in range(GQA_KV_HEADS // 2):
        k = _pair_norm_rope(p[:, qw + j * LANES:qw + (j + 1) * LANES], gk_ref[...], cos, sin)
        kt = k.T
        kt_ref[0, 2 * j, 0] = kt[:GQA_HEAD_DIM].astype(kt_ref.dtype)
        kt_ref[0, 2 * j + 1, 0] = kt[GQA_HEAD_DIM:].astype(kt_ref.dtype)
        v = p[:, qw + kw + j * LANES:qw + kw + (j + 1) * LANES]
        lo = lax.broadcasted_iota(jnp.int32, v.shape, 1) < GQA_HEAD_DIM
        v_ref[0, 2 * j] = jnp.where(lo, v, 1.0).astype(v_ref.dtype)
        v_ref[0, 2 * j + 1] = jnp.where(lo, pltpu.roll(v, GQA_HEAD_DIM, 1), 1.0).astype(v_ref.dtype)


def _gqa_qkv(x, g, mod_l, mod_c, w_in, n_lat, g_q, g_k, cos, sin):
    b, lt, d = x.shape
    bl = TOKEN_BLOCK
    bk = _key_chunk(lt)
    hd = GQA_HEAD_DIM
    two = lambda t: jnp.concatenate([t, t]).reshape(1, LANES)
    tab = pl.BlockSpec((bl, LANES), lambda i, j: (j, 0))
    vec = pl.BlockSpec((1, LANES), lambda i, j: (0, 0))
    return pl.pallas_call(
        functools.partial(_gqa_qkv_kernel, scale=hd ** -0.5 * LOG2E, bl=bl, n_lat=n_lat),
        grid=(b, lt // bl),
        in_specs=_ln_mod_specs(d, bl) + [pl.BlockSpec(w_in.shape, lambda i, j: (0, 0)), vec, vec, tab, tab],
        out_specs=[pl.BlockSpec((1, GQA_HEADS, bl, hd), lambda i, j: (i, 0, j, 0)),
                   _kt_spec(GQA_KV_HEADS, hd, bl, bk),
                   pl.BlockSpec((1, GQA_KV_HEADS, bl, LANES), lambda i, j: (i, 0, j, 0))],
        out_shape=[jax.ShapeDtypeStruct((b, GQA_HEADS, lt, hd), BF16),
                   jax.ShapeDtypeStruct((b, GQA_KV_HEADS, lt // bk, hd, bk), BF16),
                   jax.ShapeDtypeStruct((b, GQA_KV_HEADS, lt, LANES), BF16)],
        compiler_params=_params("parallel", "parallel"),
        name="gqa_qkv",
    )(x, g.reshape(1, d), mod_l, mod_c, w_in, two(g_q), two(g_k), cos, sin)


def _attn_kernel(q_ref, kt_ref, v_ref, o_ref, m_scr, acc_scr, *, groups, bq, n_chunks):
    rows = groups * bq
    q = q_ref[0, 0].reshape(rows, q_ref.shape[-1])
    bk = kt_ref.shape[-1]
    dv = o_ref.shape[-1]
    m_scr[...] = jnp.full_like(m_scr, -jnp.inf)
    acc_scr[...] = jnp.zeros_like(acc_scr)

    def step(j, carry):
        s = jnp.dot(q, kt_ref[0, 0, j], preferred_element_type=F32)
        m_prev = m_scr[...]
        m_new = jnp.maximum(m_prev, jnp.max(s, axis=-1, keepdims=True))
        alpha = jnp.exp2(m_prev - m_new)
        p = jnp.exp2(s - pltpu.repeat(m_new, bk // LANES, axis=1))
        vv = v_ref[0, 0, j * bk:(j + 1) * bk, :]
        acc_scr[...] = alpha * acc_scr[...] + jnp.dot(p.astype(BF16), vv, preferred_element_type=F32)
        m_scr[...] = m_new
        return carry

    for j in range(n_chunks):
        step(j, 0)
    acc = acc_scr[...]
    o = acc / pltpu.roll(acc, LANES - dv, 1)
    o_ref[0, 0] = o[:, :dv].reshape(groups, bq, dv).astype(o_ref.dtype)


def _attention_call(q, q_start, lq, ktc, v1):
    b, hq, lt, dk = q.shape
    hk, n_chunks, bk = ktc.shape[1], ktc.shape[2], ktc.shape[-1]
    lk = n_chunks * bk
    dv = LANES // 2
    groups = hq // hk
    bq = min(ATTN_ROWS // groups, lq)
    assert lq % bq == 0 and q_start % bq == 0, "query range must tile into attention blocks"
    rows = groups * bq
    first = q_start // bq
    q5 = q.reshape(b, hk, groups, lt, dk)
    out = pl.pallas_call(
        functools.partial(_attn_kernel, groups=groups, bq=bq, n_chunks=n_chunks),
        grid=(b, hk, lq // bq),
        in_specs=[pl.BlockSpec((1, 1, groups, bq, dk), lambda i, h, j: (i, h, 0, j + first, 0)),
                  pl.BlockSpec((1, 1, n_chunks, dk, bk), lambda i, h, j: (i, h, 0, 0, 0)),
                  pl.BlockSpec((1, 1, lk, LANES), lambda i, h, j: (i, h, 0, 0))],
        out_specs=pl.BlockSpec((1, 1, groups, bq, dv), lambda i, h, j: (i, h, 0, j, 0)),
        out_shape=jax.ShapeDtypeStruct((b, hk, groups, lq, dv), BF16),
        scratch_shapes=[pltpu.VMEM((rows, LANES), F32), pltpu.VMEM((rows, LANES), F32)],
        compiler_params=_params("parallel", "parallel", "arbitrary"),
        name="attention",
    )(q5, ktc, v1)
    return out.reshape(b, hq, lq, dv)


def _attention(q, ktc, v1, n_lat):
    lt = q.shape[2]
    n_ctx = lt - n_lat
    o_l = _attention_call(q, 0, n_lat, ktc, v1)
    o_c = _attention_call(q, n_lat, n_ctx, ktc[:, :, -1:, :, -n_ctx:], v1[:, :, n_lat:])
    return o_l, o_c


def _attn_out_specs(heads, bl, n_lat):
    n_lb = n_lat // bl
    return [pl.BlockSpec((1, heads, bl, LANES // 2), lambda i, j: (i, 0, jnp.minimum(j, n_lb - 1), 0)),
            pl.BlockSpec((1, heads, bl, LANES // 2), lambda i, j: (i, 0, jnp.maximum(j - n_lb, 0), 0))]


def _heads_of(ol_ref, oc_ref, heads, bl, n_lat):
    is_ctx = pl.program_id(1) >= n_lat // bl
    return jnp.concatenate([jnp.where(is_ctx, oc_ref[0, h], ol_ref[0, h]) for h in range(heads)], axis=-1)


def _ab_out_kernel(x_ref, ml_ref, mc_ref, y_ref, bonus_ref, gate_ref, ol_ref, oc_ref, lnw_ref, lnb_ref, wrw_ref,
                   wmla_ref, out_ref, *, bl, n_lat):
    rw = []
    for h in range(RW_HEADS):
        y = y_ref[0, 0, h] + y_ref[1, 0, h]
        mean = jnp.mean(y, axis=-1, keepdims=True)
        yc = y - mean
        var = jnp.mean(yc * yc, axis=-1, keepdims=True)
        yn = yc * lax.rsqrt(var + LN_X_EPS) * lnw_ref[h] + lnb_ref[h]
        rw.append((yn + bonus_ref[0, h]) * gate_ref[0, h])
    rw = jnp.concatenate(rw, axis=-1).astype(BF16)
    o = _heads_of(ol_ref, oc_ref, MLA_HEADS, bl, n_lat)
    acc = (jnp.dot(rw, wrw_ref[...], preferred_element_type=F32)
           + jnp.dot(o, wmla_ref[...], preferred_element_type=F32))
    gate = _row_mods(ml_ref, mc_ref, pl.program_id(1) * bl, bl, n_lat, 2)
    out_ref[0] = x_ref[0] + gate * acc


def _ab_out(x, mod_l, mod_c, y, bonus, gate, o, ln_w, ln_b, w_out, n_lat, lt):
    b, _, d = x.shape
    bl = TOKEN_BLOCK
    hd = RW_HEAD_DIM
    wrw = w_out[:RW_DIM].astype(BF16)
    wmla = w_out[RW_DIM:].astype(BF16)
    xs = pl.BlockSpec((1, bl, d), lambda i, j: (i, j, 0))
    hm = pl.BlockSpec((1, RW_HEADS, bl, hd), lambda i, j: (i, 0, j, 0))
    full = lambda a: pl.BlockSpec(a.shape, lambda i, j: (0,) * a.ndim)
    lnw, lnb = ln_w.reshape(RW_HEADS, 1, hd), ln_b.reshape(RW_HEADS, 1, hd)
    return pl.pallas_call(
        functools.partial(_ab_out_kernel, bl=bl, n_lat=n_lat),
        grid=(b, lt // bl),
        in_specs=[xs, pl.BlockSpec((1, 6, d), lambda i, j: (i, 0, 0)), pl.BlockSpec((1, 6, d), lambda i, j: (0, 0, 0)),
                  pl.BlockSpec((2, 1, RW_HEADS, bl, hd), lambda i, j: (0, i, 0, j, 0)), hm, hm]
        + _attn_out_specs(MLA_HEADS, bl, n_lat) + [full(lnw), full(lnb), full(wrw), full(wmla)],
        out_specs=xs,
        out_shape=jax.ShapeDtypeStruct((b, lt, d), F32),
        compiler_params=_params("parallel", "parallel"),
        name="ab_out",
    )(x, mod_l, mod_c, y, bonus, gate, *o, lnw, lnb, wrw, wmla)


def _gqa_out_kernel(x_ref, ml_ref, mc_ref, ol_ref, oc_ref, w_ref, out_ref, *, bl, n_lat):
    o = _heads_of(ol_ref, oc_ref, GQA_HEADS, bl, n_lat)
    acc = jnp.dot(o, w_ref[...], preferred_element_type=F32)
    gate = _row_mods(ml_ref, mc_ref, pl.program_id(1) * bl, bl, n_lat, 2)
    out_ref[0] = x_ref[0] + gate * acc


def _gqa_out(x, mod_l, mod_c, o, w_out, n_lat, lt):
    b, _, d = x.shape
    bl = TOKEN_BLOCK
    hd = GQA_HEAD_DIM
    w = w_out.astype(BF16)
    xs = pl.BlockSpec((1, bl, d), lambda i, j: (i, j, 0))
    return pl.pallas_call(
        functools.partial(_gqa_out_kernel, bl=bl, n_lat=n_lat),
        grid=(b, lt // bl),
        in_specs=[xs, pl.BlockSpec((1, 6, d), lambda i, j: (i, 0, 0)), pl.BlockSpec((1, 6, d), lambda i, j: (0, 0, 0))]
        + _attn_out_specs(GQA_HEADS, bl, n_lat) + [pl.BlockSpec(w.shape, lambda i, j: (0, 0))],
        out_specs=xs,
        out_shape=jax.ShapeDtypeStruct((b, lt, d), F32),
        compiler_params=_params("parallel", "parallel"),
        name="gqa_out",
    )(x, mod_l, mod_c, *o, w)


def _first_argmax(vals):
    best, idx = vals[0], jnp.zeros(vals[0].shape, jnp.int32)
    for i in range(1, len(vals)):
        better = vals[i] > best
        idx = jnp.where(better, i, idx)
        best = jnp.where(better, vals[i], best)
    return best, idx


def _pick(vals, idx):
    out = vals[0]
    for i in range(1, len(vals)):
        out = jnp.where(idx == i, vals[i], out)
    return out


def _route_kernel(x_ref, g_ref, ml_ref, mc_ref, rwh_ref, rwl_ref, rb_ref, t_ref, comb_ref, grp_ref, *, bl, n_lat):
    first = pl.program_id(1) * bl
    shift = _row_mods(ml_ref, mc_ref, first, bl, n_lat, 3)
    scale = _row_mods(ml_ref, mc_ref, first, bl, n_lat, 4)
    t = (_rms(x_ref[0]) * g_ref[...]) * (1.0 + scale) + shift
    t_ref[0] = t.astype(t_ref.dtype)
    t_hi = t.astype(BF16)
    t_lo = (t - t_hi.astype(F32)).astype(BF16)
    logits = (jnp.dot(t_hi, rwh_ref[...], preferred_element_type=F32)
              + jnp.dot(t_lo, rwh_ref[...], preferred_element_type=F32)
              + jnp.dot(t_hi, rwl_ref[...], preferred_element_type=F32))
    lt = logits.T
    score = [_sigmoid(lt[e:e + 1]) for e in range(N_EXPERTS)]
    biased = [score[e] + rb_ref[e:e + 1, :] for e in range(N_EXPERTS)]
    epg = EXPERTS_PER_GROUP
    group_score = []
    for g in range(N_GROUPS):
        vals = biased[g * epg:(g + 1) * epg]
        pair = [vals[i] + vals[j] for i in range(epg) for j in range(i + 1, epg)]
        group_score.append(functools.reduce(jnp.maximum, pair))
    _, grp = _first_argmax(group_score)
    in_b = [_pick([biased[g * epg + j] for g in range(N_GROUPS)], grp) for j in range(epg)]
    in_s = [_pick([score[g * epg + j] for g in range(N_GROUPS)], grp) for j in range(epg)]
    _, loc1 = _first_argmax(in_b)
    _, loc2 = _first_argmax([jnp.where(loc1 == j, -jnp.inf, in_b[j]) for j in range(epg)])
    w1, w2 = _pick(in_s, loc1), _pick(in_s, loc2)
    wsum = w1 + w2
    w1, w2 = w1 / wsum, w2 / wsum
    e1, e2 = grp * epg + loc1, grp * epg + loc2
    sub = lax.broadcasted_iota(jnp.int32, (LANES, bl), 0)
    comb = jnp.zeros((LANES, bl), F32)
    for e in range(N_EXPERTS):
        c_e = jnp.where(e1 == e, w1, 0.0) + jnp.where(e2 == e, w2, 0.0)
        c_hi = c_e.astype(BF16).astype(F32)
        comb = jnp.where(sub == e, jnp.broadcast_to(c_hi, (LANES, bl)), comb)
        comb = jnp.where(sub == N_EXPERTS + e, jnp.broadcast_to(c_e - c_hi, (LANES, bl)), comb)
    comb_ref[0] = comb.T.astype(comb_ref.dtype)
    grp_ref[0] = grp


def _moe_route(x, g, mod_l, mod_c, router_w, router_b, n_lat):
    b, lt, d = x.shape
    bl = TOKEN_BLOCK
    rw = jnp.pad(router_w, ((0, 0), (0, LANES - N_EXPERTS)))
    rw_hi = rw.astype(BF16)
    rw_lo = (rw - rw_hi.astype(F32)).astype(BF16)
    xs = pl.BlockSpec((1, bl, d), lambda i, j: (i, j, 0))
    return pl.pallas_call(
        functools.partial(_route_kernel, bl=bl, n_lat=n_lat),
        grid=(b, lt // bl),
        in_specs=[xs, pl.BlockSpec((1, d), lambda i, j: (0, 0)),
                  pl.BlockSpec((1, 6, d), lambda i, j: (i, 0, 0)), pl.BlockSpec((1, 6, d), lambda i, j: (0, 0, 0)),
                  pl.BlockSpec(rw.shape, lambda i, j: (0, 0)), pl.BlockSpec(rw.shape, lambda i, j: (0, 0)),
                  pl.BlockSpec((N_EXPERTS, 1), lambda i, j: (0, 0))],
        out_specs=[xs, pl.BlockSpec((1, bl, LANES), lambda i, j: (i, j, 0)),
                   pl.BlockSpec((1, 1, bl), lambda i, j: (i, 0, j))],
        out_shape=[jax.ShapeDtypeStruct((b, lt, d), BF16), jax.ShapeDtypeStruct((b, lt, LANES), BF16),
                   jax.ShapeDtypeStruct((b, 1, lt), jnp.int32)],
        compiler_params=_params("parallel", "parallel"),
        name="moe_route",
    )(x, g.reshape(1, d), mod_l, mod_c, rw_hi, rw_lo, router_b.reshape(N_EXPERTS, 1))


def _ffn(t, wg, wu, wd):
    g = jnp.dot(t, wg, preferred_element_type=F32)
    u = jnp.dot(t, wu, preferred_element_type=F32)
    return jnp.dot((_silu(g) * u).astype(BF16), wd, preferred_element_type=F32)


def _experts_kernel(cnt_ref, x_ref, t_ref, comb_ref, grp_ref, tri_ref, ml_ref, mc_ref, wg_ref, wu_ref, wd_ref,
                    sg_ref, su_ref, sd_ref, out_ref, acc_ref, row_ref, col_ref, *, bl, n_lat):
    g = pl.program_id(2)
    t = t_ref[0]

    @pl.when(g == 0)
    def _():
        acc_ref[...] = _ffn(t, sg_ref[...], su_ref[...], sd_ref[...])
        grp = grp_ref[0]
        gid = lax.broadcasted_iota(jnp.int32, (2 * HALO, bl), 0)
        member = jnp.where(grp == gid, 1.0, 0.0)
        before = jnp.dot(member.astype(BF16), tri_ref[...], preferred_element_type=F32)
        slot = jnp.sum(member * before, axis=0, keepdims=True)
        sub = lax.broadcasted_iota(jnp.int32, (LANES, bl), 0)
        info = jnp.where(sub == 0, jnp.broadcast_to(slot, (LANES, bl)),
                         jnp.where(sub == 1, jnp.broadcast_to(grp.astype(F32), (LANES, bl)), 0.0))
        row_ref[...] = info[:HALO]
        col_ref[...] = info.T

    gf = g.astype(F32)
    slot_r, in_r = row_ref[0:1, :], jnp.where(row_ref[1:2, :] == gf, 1.0, 0.0)
    slot_c, in_c = col_ref[:, 0:1], jnp.where(col_ref[:, 1:2] == gf, 1.0, 0.0)
    count = cnt_ref[(pl.program_id(0) * pl.num_programs(1) + pl.program_id(1)) * N_GROUPS + g]
    wd = wd_ref[...]
    wd = wd.reshape(wd.shape[0] * wd.shape[1], wd.shape[2])
    lane = lax.broadcasted_iota(jnp.int32, (MOE_CAP, LANES), 1)

    def one_pass(k, carry):
        base = (k * MOE_CAP).astype(F32)
        want_r = base + lax.broadcasted_iota(jnp.int32, (MOE_CAP, bl), 0).astype(F32)
        want_c = base + lax.broadcasted_iota(jnp.int32, (bl, MOE_CAP), 1).astype(F32)
        gather = jnp.where(slot_r == want_r, in_r, 0.0).astype(BF16)
        scatter = jnp.where(slot_c == want_c, in_c, 0.0).astype(BF16)
        tg = jnp.dot(gather, t, preferred_element_type=F32).astype(BF16)
        cg = jnp.dot(gather, comb_ref[0], preferred_element_type=F32)
        hs = []
        for i in range(EXPERTS_PER_GROUP):
            e = g * EXPERTS_PER_GROUP + i
            cw = jnp.sum(jnp.where(lane % N_EXPERTS == e, cg, 0.0), axis=-1, keepdims=True)
            gate = jnp.dot(tg, wg_ref[i], preferred_element_type=F32)
            up = jnp.dot(tg, wu_ref[i], preferred_element_type=F32)
            hs.append((_silu(gate) * up * cw).astype(BF16))
        yg = jnp.dot(jnp.concatenate(hs, axis=-1), wd, preferred_element_type=F32)
        acc_ref[...] += jnp.dot(scatter, yg.astype(BF16), preferred_element_type=F32)
        return carry

    lax.fori_loop(0, (count + MOE_CAP - 1) // MOE_CAP, one_pass, 0)

    @pl.when(g == pl.num_programs(2) - 1)
    def _():
        gate = _row_mods(ml_ref, mc_ref, pl.program_id(1) * bl, bl, n_lat, 5)
        out_ref[0] = x_ref[0] + gate * acc_ref[...]


def _moe_experts(x, t, comb, grp, mod_l, mod_c, w_gate, w_up, w_down, sh_gate, sh_up, sh_down, n_lat, out_rows):
    b, lt, d = x.shape
    bl = _row_block(lt, WIDE_ROWS)
    nblk = lt // bl
    ff = w_gate.shape[-1]
    epg = EXPERTS_PER_GROUP
    counts = jnp.sum(grp.reshape(b, nblk, bl, 1) == jnp.arange(N_GROUPS), axis=2, dtype=jnp.int32).reshape(-1)
    tri = jnp.triu(jnp.ones((bl, bl), BF16), 1)
    xs = pl.BlockSpec((1, bl, d), lambda i, j, g, c: (i, j, 0))
    full = lambda a: pl.BlockSpec(a.shape, lambda i, j, g, c: (0,) * a.ndim)
    ws = [w_gate.astype(BF16), w_up.astype(BF16), w_down.astype(BF16),
          sh_gate.astype(BF16), sh_up.astype(BF16), sh_down.astype(BF16)]
    grid_spec = pltpu.PrefetchScalarGridSpec(
        num_scalar_prefetch=1,
        grid=(b, nblk, N_GROUPS),
        in_specs=[xs, xs, pl.BlockSpec((1, bl, LANES), lambda i, j, g, c: (i, j, 0)),
                  pl.BlockSpec((1, 1, bl), lambda i, j, g, c: (i, 0, j)), full(tri),
                  pl.BlockSpec((1, 6, d), lambda i, j, g, c: (i, 0, 0)),
                  pl.BlockSpec((1, 6, d), lambda i, j, g, c: (0, 0, 0)),
                  pl.BlockSpec((epg, d, ff), lambda i, j, g, c: (g, 0, 0)),
                  pl.BlockSpec((epg, d, ff), lambda i, j, g, c: (g, 0, 0)),
                  pl.BlockSpec((epg, ff, d), lambda i, j, g, c: (g, 0, 0)),
                  full(ws[3]), full(ws[4]), full(ws[5])],
        out_specs=xs,
        scratch_shapes=[pltpu.VMEM((bl, d), F32), pltpu.VMEM((HALO, bl), F32), pltpu.VMEM((bl, LANES), F32)],
    )
    return pl.pallas_call(
        functools.partial(_experts_kernel, bl=bl, n_lat=n_lat),
        grid_spec=grid_spec,
        out_shape=jax.ShapeDtypeStruct((b, out_rows, d), F32),
        compiler_params=_params("parallel", "parallel", "arbitrary"),
        name="moe_experts",
    )(counts, x, t, comb, grp, tri, mod_l, mod_c, *ws)


def _rope_tables(seq, n_ctx, rope_dims, lane_offset, repeat):
    half = rope_dims // 4
    t = jnp.arange(seq)
    rowp = (t // GRID_W).astype(F32)
    colp = (t % GRID_W).astype(F32)
    inv = ROPE_THETA ** (-jnp.arange(half, dtype=F32) / half)
    ar, ac = rowp[:, None] * inv[None, :], colp[:, None] * inv[None, :]
    cos = jnp.concatenate([jnp.cos(ar), jnp.cos(ar), jnp.cos(ac), jnp.cos(ac)], axis=1)
    sin = jnp.concatenate([-jnp.sin(ar), jnp.sin(ar), -jnp.sin(ac), jnp.sin(ac)], axis=1)
    width = LANES // repeat
    padl, padr = lane_offset, width - lane_offset - rope_dims
    cos = jnp.pad(cos, ((0, n_ctx), (0, 0)), constant_values=1.0)
    sin = jnp.pad(sin, ((0, n_ctx), (0, 0)))
    cos = jnp.pad(cos, ((0, 0), (padl, padr)), constant_values=1.0)
    sin = jnp.pad(sin, ((0, 0), (padl, padr)))
    return jnp.tile(cos, (1, repeat)), jnp.tile(sin, (1, repeat))


def kernel(x, c, ctx, c_ctx, ada_w, ada_b, norm1_g, norm2_g, ab_w_in, ab_w_out, rw_mu, rw_w0, rw_w2, rw_a0, rw_a2, rw_k_k, rw_k_a, rw_r_k, rw_g2, rw_ln_w, rw_ln_b, mla_g_qa, mla_w_q_up, mla_g_kva, mla_w_kv_up, mla_g_q, mla_g_k, gqa_w_in, gqa_w_out, gqa_g_q, gqa_g_k, router_w, router_b, moe_w_gate, moe_w_up, moe_w_down, shared_w_gate, shared_w_up, shared_w_down):
    b, n_lat, d = x.shape
    n_ctx = ctx.shape[1]
    depth = ada_w.shape[0]
    assert n_lat % TOKEN_BLOCK == 0 and n_ctx % TOKEN_BLOCK == 0, "both sequences must tile into token blocks"
    assert n_lat % GRID_W == 0, "latent tokens must fill whole grid rows"
    xs = jnp.concatenate([x, ctx], axis=1)

    cvec = jnp.concatenate([c, c_ctx[None, :]], axis=0)
    cvec = jnp.pad(cvec, ((0, 8 - (b + 1) % 8), (0, 0))) if (b + 1) % 8 else cvec
    mods = _ada_mods(cvec, ada_w, ada_b).reshape(depth, -1, 6, d)

    mla_cos, mla_sin = _rope_tables(n_lat, n_ctx, MLA_ROPE, MLA_NOPE, 1)
    gqa_cos, gqa_sin = _rope_tables(n_lat, n_ctx, GQA_HEAD_DIM, 0, 2)

    for l in range(depth):
        i = l // 2
        mod_l, mod_c = mods[l, :b], mods[l, b:b + 1]
        rows = n_lat + n_ctx
        if l % 2 == 0:
            w_in = ab_w_in[i]
            w_rw = w_in[:, :RW_COLS].astype(BF16)
            w_m = w_in[:, RW_COLS:]
            w_mla = jnp.concatenate(
                [w_m[:, :MLA_Q_RANK + MLA_KV_RANK],
                 jnp.pad(w_m[:, MLA_Q_RANK + MLA_KV_RANK:], ((0, 0), (MLA_NOPE, LANES - MLA_NOPE - MLA_ROPE)))],
                axis=1).astype(BF16)
            p_rw, p_mla = _ln_mod_mm(xs, norm1_g[l], mod_l, mod_c, [w_rw, w_mla], n_lat)
            r, v, na, lw, kd, bd, gate, bonus = _rwkv_prep(
                p_rw, n_lat, rw_mu[i], rw_w0[i], rw_w2[i], rw_a0[i], rw_a2[i], rw_k_k[i], rw_k_a[i],
                rw_r_k[i].reshape(-1), rw_g2[i])
            y = _wkv_scan(r, v, na, lw, kd, bd, n_lat)
            q, kt, vm = _mla_qkv(p_mla, mla_g_qa[i], mla_w_q_up[i], mla_g_kva[i], mla_w_kv_up[i],
                                 mla_g_q[i], mla_g_k[i], mla_cos, mla_sin)
            o = _attention(q, kt, vm, n_lat)
            xs = _ab_out(xs, mod_l, mod_c, y, bonus, gate, o, rw_ln_w[i], rw_ln_b[i], ab_w_out[i], n_lat, rows)
        else:
            q, kt, vg = _gqa_qkv(xs, norm1_g[l], mod_l, mod_c, gqa_w_in[i].astype(BF16), n_lat,
                                 gqa_g_q[i], gqa_g_k[i], gqa_cos, gqa_sin)
            o = _attention(q, kt, vg, n_lat)
            xs = _gqa_out(xs, mod_l, mod_c, o, gqa_w_out[i], n_lat, rows)
        t, comb, grp = _moe_route(xs, norm2_g[l], mod_l, mod_c, router_w, router_b, n_lat)
        xs = _moe_experts(xs, t, comb, grp, mod_l, mod_c, moe_w_gate[l], moe_w_up[l], moe_w_down[l],
                          shared_w_gate[l], shared_w_up[l], shared_w_down[l], n_lat,
                          n_lat if l == depth - 1 else rows)
    return xs
```

```python
import functools
import math

import jax
import jax.numpy as jnp
from jax import lax
from jax.experimental import pallas as pl
from jax.experimental.pallas import tpu as pltpu

F32 = jnp.float32
BF16 = jnp.bfloat16
HIGHEST = lax.Precision.HIGHEST

GRID_W = 64
RW_HEADS = 8
RW_HEAD_DIM = 64
RW_DIM = RW_HEADS * RW_HEAD_DIM
DECAY_LORA = 64
ICLR_LORA = 64
GATE_LORA = 128
RW_COLS = 3 * RW_DIM + 2 * DECAY_LORA + 2 * ICLR_LORA + GATE_LORA
LN_X_EPS = 64e-5
MLA_HEADS = 8
MLA_NOPE = 64
MLA_ROPE = 32
MLA_V = 64
MLA_QK = MLA_NOPE + MLA_ROPE
MLA_Q_RANK = 384
MLA_KV_RANK = 256
GQA_HEADS = 16
GQA_KV_HEADS = 4
GQA_HEAD_DIM = 64
N_EXPERTS = 16
N_GROUPS = 4
EXPERTS_PER_GROUP = N_EXPERTS // N_GROUPS
ROPE_THETA = 10000.0
NORM_EPS = 1e-6

LANES = 128
V7X_VMEM_BYTES = 64 * 1024 * 1024
VMEM_LIMIT = V7X_VMEM_BYTES - 8 * 1024 * 1024

TOKEN_BLOCK = 256
HALO = 8
SCAN_CHUNK = 64
SCAN_INTERLEAVE = 16
SCAN_CHUNKS_PER_STEP = TOKEN_BLOCK // SCAN_CHUNK
ATTN_ROWS = 1024
ATTN_KEYS = 2816
WIDE_ROWS = 768
MOE_CAP = 256
LOG2E = math.log2(math.e)


def _row_block(lt, limit):
    return max(r for r in range(TOKEN_BLOCK, limit + 1, TOKEN_BLOCK) if lt % r == 0)


def _key_chunk(lt):
    return max(c for c in range(TOKEN_BLOCK, ATTN_KEYS + 1, TOKEN_BLOCK) if lt % c == 0)


def _kt_spec(heads, dk, bl, bk):
    per = bk // bl
    return pl.BlockSpec((1, heads, 1, dk, bl), lambda i, j: (i, 0, j // per, 0, j % per))


def _params(*sem):
    return pltpu.CompilerParams(dimension_semantics=sem, vmem_limit_bytes=VMEM_LIMIT)


def _bdot(a, b):
    return jnp.dot(a.astype(BF16), b.astype(BF16), preferred_element_type=F32)


def _batched(a, b, ca, cb):
    return lax.dot_general(a.astype(BF16), b.astype(BF16), (((ca,), (cb,)), ((0,), (0,))),
                           preferred_element_type=F32)


def _bmm(a, b):
    return _batched(a, b, 2, 1)


def _bmm_nt(a, b):
    return _batched(a, b, 2, 2)


def _bmm_tn(a, b):
    return _batched(a, b, 1, 1)


def _sigmoid(x):
    return 1.0 / (1.0 + jnp.exp(-x))


def _silu(x):
    return x * _sigmoid(x)


def _softplus(x):
    return jnp.maximum(x, 0.0) + jnp.log(1.0 + jnp.exp(-jnp.abs(x)))


def _rms(x):
    return x * lax.rsqrt(jnp.mean(x * x, axis=-1, keepdims=True) + NORM_EPS)


def _row_mods(ml_ref, mc_ref, first_row, rows, n_lat, idx):
    r = first_row + lax.broadcasted_iota(jnp.int32, (rows, 1), 0)
    return jnp.where(r >= n_lat, mc_ref[0, idx:idx + 1, :], ml_ref[0, idx:idx + 1, :])


def _ada_kernel(c_ref, w_ref, b_ref, o_ref):
    s = _silu(c_ref[...])
    o_ref[0] = jnp.dot(s, w_ref[0], preferred_element_type=F32, precision=HIGHEST) + b_ref[0]


def _ada_mods(cvec, ada_w, ada_b):
    depth, d, n = ada_w.shape
    rows = cvec.shape[0]
    bn = n // 4
    return pl.pallas_call(
        _ada_kernel,
        grid=(depth, n // bn),
        in_specs=[pl.BlockSpec((rows, d), lambda l, j: (0, 0)),
                  pl.BlockSpec((1, d, bn), lambda l, j: (l, 0, j)),
                  pl.BlockSpec((1, 1, bn), lambda l, j: (l, 0, j))],
        out_specs=pl.BlockSpec((1, rows, bn), lambda l, j: (l, 0, j)),
        out_shape=jax.ShapeDtypeStruct((depth, rows, n), F32),
        compiler_params=_params("parallel", "parallel"),
        name="ada_mods",
    )(cvec, ada_w, ada_b.reshape(depth, 1, n))


def _ln_mod(x_ref, g_ref, ml_ref, mc_ref, bl, n_lat):
    first = pl.program_id(1) * bl
    shift = _row_mods(ml_ref, mc_ref, first, bl, n_lat, 0)
    scale = _row_mods(ml_ref, mc_ref, first, bl, n_lat, 1)
    h = (_rms(x_ref[0]) * g_ref[...]) * (1.0 + scale) + shift
    return h.astype(BF16)


def _ln_mod_specs(d, bl):
    return [pl.BlockSpec((1, bl, d), lambda i, j: (i, j, 0)), pl.BlockSpec((1, d), lambda i, j: (0, 0)),
            pl.BlockSpec((1, 6, d), lambda i, j: (i, 0, 0)), pl.BlockSpec((1, 6, d), lambda i, j: (0, 0, 0))]


def _ln_mod_mm_kernel(x_ref, g_ref, ml_ref, mc_ref, *refs, n_w, bl, n_lat):
    w_refs, o_refs = refs[:n_w], refs[n_w:]
    hb = _ln_mod(x_ref, g_ref, ml_ref, mc_ref, bl, n_lat)
    for w_ref, o_ref in zip(w_refs, o_refs):
        o_ref[0] = jnp.dot(hb, w_ref[...], preferred_element_type=F32).astype(o_ref.dtype)


def _ln_mod_mm(x, g, mod_l, mod_c, ws, n_lat):
    b, lt, d = x.shape
    bl = _row_block(lt, WIDE_ROWS)
    kern = functools.partial(_ln_mod_mm_kernel, n_w=len(ws), bl=bl, n_lat=n_lat)
    return pl.pallas_call(
        kern,
        grid=(b, lt // bl),
        in_specs=[pl.BlockSpec((1, bl, d), lambda i, j: (i, j, 0)),
                  pl.BlockSpec((1, d), lambda i, j: (0, 0)),
                  pl.BlockSpec((1, 6, d), lambda i, j: (i, 0, 0)),
                  pl.BlockSpec((1, 6, d), lambda i, j: (0, 0, 0))]
        + [pl.BlockSpec(w.shape, lambda i, j: (0, 0)) for w in ws],
        out_specs=[pl.BlockSpec((1, bl, w.shape[1]), lambda i, j: (i, j, 0)) for w in ws],
        out_shape=[jax.ShapeDtypeStruct((b, lt, w.shape[1]), F32) for w in ws],
        compiler_params=_params("parallel", "parallel"),
        name="ln_mod_mm",
    )(x, g.reshape(1, d), mod_l, mod_c, *ws)


def _rwkv_prep_kernel(p_ref, prev_ref, next_ref, mu_ref, w0_ref, w2_ref, a0_ref, a2_ref, kk_ref,
                      ka_ref, rk_ref, g2_ref, r_ref, v_ref, na_ref, lw_ref, kd_ref, bd_ref,
                      gate_ref, bonus_ref, *, bl, ctx_block):
    p = p_ref[0]
    j = pl.program_id(1)
    starts = jnp.logical_or(j == 0, j == ctx_block)
    ends = jnp.logical_or(j == ctx_block - 1, j == pl.num_programs(1) - 1)
    before = jnp.where(starts, 0.0, prev_ref[0, HALO - 1:HALO, :])
    after = jnp.where(ends, 0.0, next_ref[0, 0:1, :])
    row = lax.broadcasted_iota(jnp.int32, p.shape, 0)
    prev = jnp.where(row == 0, before, pltpu.roll(p, 1, 0))
    nxt = jnp.where(row == bl - 1, after, pltpu.roll(p, bl - 1, 0))
    pm = p + mu_ref[...] * (0.5 * (prev + nxt) - p)

    o = 3 * RW_DIM
    r = pm[:, 0:RW_DIM]
    k = pm[:, RW_DIM:2 * RW_DIM]
    v = pm[:, 2 * RW_DIM:o]
    w1 = (pm[:, o:o + DECAY_LORA], pm[:, o + DECAY_LORA:o + 2 * DECAY_LORA])
    o += 2 * DECAY_LORA
    a1 = (pm[:, o:o + ICLR_LORA], pm[:, o + ICLR_LORA:o + 2 * ICLR_LORA])
    o += 2 * ICLR_LORA
    g1 = pm[:, o:o + GATE_LORA]

    kk = k * kk_ref[...]
    lw, kd, ad = [], [], []
    for d in range(2):
        z = w0_ref[d:d + 1, :] + _bdot(jnp.tanh(w1[d]), w2_ref[d])
        lw.append(-jnp.exp(-_softplus(-z) - 0.5))
        a = _sigmoid(a0_ref[d:d + 1, :] + _bdot(a1[d], a2_ref[d]))
        ad.append(a)
        kd.append(k * (1.0 + (a - 1.0) * ka_ref[...]))
    gate = _bdot(_sigmoid(g1), g2_ref[...])
    rkk = r * (kd[0] + kd[1]) * rk_ref[...]

    for h in range(RW_HEADS):
        sl = slice(h * RW_HEAD_DIM, (h + 1) * RW_HEAD_DIM)
        kk_h = kk[:, sl]
        kk_h = kk_h * lax.rsqrt(jnp.sum(kk_h * kk_h, axis=-1, keepdims=True) + 1e-12)
        v_h = v[:, sl]
        r_ref[0, h] = r[:, sl]
        v_ref[0, h] = v_h
        na_ref[0, h] = -kk_h
        for d in range(2):
            lw_ref[d, 0, h] = lw[d][:, sl]
            kd_ref[d, 0, h] = kd[d][:, sl]
            bd_ref[d, 0, h] = kk_h * ad[d][:, sl]
        gate_ref[0, h] = gate[:, sl]
        bonus_ref[0, h] = jnp.sum(rkk[:, sl], axis=-1, keepdims=True) * v_h


def _rwkv_prep(p, n_lat, mu, w0, w2, a0, a2, k_k, k_a, r_k, g2):
    b, lt, cols = p.shape
    bl = TOKEN_BLOCK
    nblk = lt // bl
    per = bl // HALO
    last_slab = lt // HALO - 1
    prev_spec = pl.BlockSpec((1, HALO, cols), lambda i, j: (i, jnp.maximum(j * per - 1, 0), 0))
    next_spec = pl.BlockSpec((1, HALO, cols), lambda i, j: (i, jnp.minimum((j + 1) * per, last_slab), 0))

    hm = jax.ShapeDtypeStruct((b, RW_HEADS, lt, RW_HEAD_DIM), F32)
    hm2 = jax.ShapeDtypeStruct((2, b, RW_HEADS, lt, RW_HEAD_DIM), F32)
    hm_spec = pl.BlockSpec((1, RW_HEADS, bl, RW_HEAD_DIM), lambda i, j: (i, 0, j, 0))
    hm2_spec = pl.BlockSpec((2, 1, RW_HEADS, bl, RW_HEAD_DIM), lambda i, j: (0, i, 0, j, 0))
    full = lambda a: pl.BlockSpec(a.shape, lambda i, j: (0,) * a.ndim)
    consts = [mu.reshape(1, cols), w0, w2.astype(BF16), a0, a2.astype(BF16), k_k.reshape(1, RW_DIM),
              k_a.reshape(1, RW_DIM), r_k.reshape(1, RW_DIM), g2.astype(BF16)]
    return pl.pallas_call(
        functools.partial(_rwkv_prep_kernel, bl=bl, ctx_block=n_lat // bl),
        grid=(b, nblk),
        in_specs=[pl.BlockSpec((1, bl, cols), lambda i, j: (i, j, 0)), prev_spec, next_spec]
        + [full(a) for a in consts],
        out_specs=[hm_spec, hm_spec, hm_spec, hm2_spec, hm2_spec, hm2_spec, hm_spec, hm_spec],
        out_shape=[hm, hm, hm, hm2, hm2, hm2, hm, hm],
        compiler_params=_params("parallel", "parallel"),
        name="rwkv_prep",
    )(p, p, p, *consts)


def _wkv_kernel(r_ref, v_ref, a_ref, lw_ref, k_ref, b_ref, y_ref, st_ref, *, cn, chains):
    d = pl.program_id(0)

    @pl.when(pl.program_id(1) == 0)
    def _():
        st_ref[...] = jnp.zeros_like(st_ref)

    row = lax.broadcasted_iota(jnp.int32, (cn, cn), 0)
    col = lax.broadcasted_iota(jnp.int32, (cn, cn), 1)
    lead = (row - col) * jnp.where(d == 0, 1, -1)
    upto = lead >= 0
    row2 = lax.broadcasted_iota(jnp.int32, (2 * cn, 2 * cn), 0)
    col2 = lax.broadcasted_iota(jnp.int32, (2 * cn, 2 * cn), 1)
    lead2 = (row2 % cn - col2 % cn) * jnp.where(d == 0, 1, -1)
    mask4 = lead2 >= jnp.where(row2 < cn, 1, 0)
    eye = (col == row).astype(F32)
    cum_mat = upto.astype(F32)
    hd = r_ref.shape[-1]
    eye_h = (lax.broadcasted_iota(jnp.int32, (hd, hd), 0)
             == lax.broadcasted_iota(jnp.int32, (hd, hd), 1)).astype(F32)
    n_double = int(math.log2(cn)) - 1

    nb = SCAN_INTERLEAVE
    cum_b = jnp.broadcast_to(cum_mat, (nb, cn, cn))

    def chunk(sl, rows, st):
        r, v, a = r_ref[sl, rows], v_ref[sl, rows], a_ref[sl, rows]
        lw, k, b = lw_ref[0, sl, rows], k_ref[0, sl, rows], b_ref[0, sl, rows]
        lw_hi = lw.astype(BF16)
        lw_lo = (lw - lw_hi.astype(F32)).astype(BF16)
        cum2 = _bmm(cum_b, jnp.concatenate([lw_hi, lw_lo], axis=-1))
        cum = cum2[..., :hd] + cum2[..., hd:]
        tot = jnp.sum(lw, axis=1, keepdims=True)
        at = a * jnp.exp(cum - lw)
        rt = r * jnp.exp(cum)
        einv = jnp.exp(-cum)
        bt, kt = b * einv, k * einv
        eend = jnp.exp(tot - cum)
        bh, kh = b * eend, k * eend

        amat = _bmm_nt(jnp.concatenate([at, rt], axis=1), jnp.concatenate([bt, kt], axis=1))
        amat = jnp.where(mask4, amat, 0.0)
        a_ab = amat[:, :cn, :cn]

        t = eye + a_ab
        x = _bmm(a_ab, a_ab)
        for _ in range(n_double - 1):
            tx = _bmm(jnp.concatenate([t, x], axis=1), x)
            t = t + tx[:, :cn]
            x = tx[:, cn:]
        t = t + _bmm(t, x)

        zero = jnp.zeros_like(v)
        av = _bmm(amat[:, :cn], jnp.concatenate([zero, v], axis=1))
        wu = _bmm(t, jnp.concatenate([at, av], axis=-1))
        rhs = jnp.concatenate([wu, jnp.concatenate([zero, v], axis=-1)], axis=1)
        qy = _bmm(amat[:, cn:], rhs)
        mn = _bmm_tn(jnp.concatenate([bh, kh], axis=1), rhs)
        qe = rt + qy[..., :hd]
        mc = eye_h * jnp.exp(tot) + mn[..., :hd]

        out = _bmm(jnp.concatenate([qe, mc], axis=1), st)
        y_ref[0, sl, rows] = out[:, :cn] + qy[..., hd:]
        return out[:, cn:] + mn[..., hd:]

    n_sub = r_ref.shape[1] // cn

    def group(g, carry):
        sl = pl.ds(g * nb, nb)
        st = st_ref[sl]
        for s in range(n_sub):
            sub = jnp.where(d == 0, s, n_sub - 1 - s)
            st = chunk(sl, pl.ds(pl.multiple_of(sub * cn, cn), cn), st)
        st_ref[sl] = st
        return carry

    lax.fori_loop(0, chains // SCAN_INTERLEAVE, group, 0)


def _wkv_scan(r, v, na, lw, kd, bd, n_lat):
    b, h, lt, hd = r.shape
    cn = SCAN_CHUNK
    chains = b * h
    rows = cn * SCAN_CHUNKS_PER_STEP
    assert n_lat % rows == 0 and lt % rows == 0
    n_all, n_l = lt // rows, n_lat // rows
    n_c = n_all - n_l

    def chunk(d, i):
        fwd = jnp.where(i < n_c, n_l + i, i - n_c)
        rev = jnp.where(i < n_c, n_all - 1 - i, n_l - 1 - (i - n_c))
        return jnp.where(d == 0, fwd, rev)

    shared = pl.BlockSpec((chains, rows, hd), lambda d, i: (0, chunk(d, i), 0))
    per_dir = pl.BlockSpec((1, chains, rows, hd), lambda d, i: (d, 0, chunk(d, i), 0))
    flat = lambda t: t.reshape(t.shape[:-4] + (chains, lt, hd))
    y = pl.pallas_call(
        functools.partial(_wkv_kernel, cn=cn, chains=chains),
        grid=(2, n_all),
        in_specs=[shared, shared, shared, per_dir, per_dir, per_dir],
        out_specs=per_dir,
        out_shape=jax.ShapeDtypeStruct((2, chains, lt, hd), F32),
        scratch_shapes=[pltpu.VMEM((chains, hd, hd), F32)],
        compiler_params=_params("parallel", "arbitrary"),
        name="wkv_scan",
    )(flat(r), flat(v), flat(na), flat(lw), flat(kd), flat(bd))
    return y.reshape(2, b, h, lt, hd)


def _rope(x, cos, sin, half):
    lane = lax.broadcasted_iota(jnp.int32, x.shape, 1)
    width = x.shape[1]
    partner = jnp.where(lane % (2 * half) < half, pltpu.roll(x, width - half, 1), pltpu.roll(x, half, 1))
    return x * cos + partner * sin


def _mla_qkv_kernel(p_ref, gqa_ref, gkva_ref, wq_ref, wk_ref, wv_ref, gq_ref, gk_ref, cos_ref, sin_ref,
                    q_ref, kt_ref, v_ref, *, scale):
    p = p_ref[0]
    q_a = p[:, 0:MLA_Q_RANK]
    kv_a = p[:, MLA_Q_RANK:MLA_Q_RANK + MLA_KV_RANK]
    k_rope = p[:, MLA_Q_RANK + MLA_KV_RANK:MLA_Q_RANK + MLA_KV_RANK + LANES]
    qa_n = (_rms(q_a) * gqa_ref[...]).astype(BF16)
    kva_n = (_rms(kv_a) * gkva_ref[...]).astype(BF16)
    q_all = jnp.dot(qa_n, wq_ref[...], preferred_element_type=F32)
    k_all = jnp.dot(kva_n, wk_ref[...], preferred_element_type=F32)
    v_all = jnp.dot(kva_n, wv_ref[...], preferred_element_type=F32)
    cos, sin = cos_ref[...], sin_ref[...]
    for h in range(MLA_HEADS):
        sl = slice(h * LANES, (h + 1) * LANES)
        q = q_all[:, sl]
        q = q * lax.rsqrt(jnp.sum(q * q, axis=-1, keepdims=True) / MLA_QK + NORM_EPS) * gq_ref[...]
        q = _rope(q, cos, sin, MLA_ROPE // 4)
        q_ref[0, h] = (q * scale).astype(q_ref.dtype)
        k = k_all[:, sl] + k_rope
        k = k * lax.rsqrt(jnp.sum(k * k, axis=-1, keepdims=True) / MLA_QK + NORM_EPS) * gk_ref[...]
        k = _rope(k, cos, sin, MLA_ROPE // 4)
        kt_ref[0, h, 0] = k.T.astype(kt_ref.dtype)
        lane = lax.broadcasted_iota(jnp.int32, q.shape, 1)
        v_ref[0, h] = jnp.where(lane < MLA_V, v_all[:, sl], 1.0).astype(v_ref.dtype)


def _pad_heads(w, heads, width):
    k = w.shape[0]
    w = w.reshape(k, heads, width)
    return jnp.pad(w, ((0, 0), (0, 0), (0, LANES - width))).reshape(k, heads * LANES)


def _mla_qkv(p, g_qa, w_q_up, g_kva, w_kv_up, g_q, g_k, cos, sin):
    b, lt, cols = p.shape
    bl = TOKEN_BLOCK
    bk = _key_chunk(lt)
    wq = _pad_heads(w_q_up, MLA_HEADS, MLA_QK).astype(BF16)
    wkv = w_kv_up.reshape(MLA_KV_RANK, MLA_HEADS, MLA_NOPE + MLA_V)
    wk = _pad_heads(wkv[:, :, :MLA_NOPE].reshape(MLA_KV_RANK, -1), MLA_HEADS, MLA_NOPE).astype(BF16)
    wv = _pad_heads(wkv[:, :, MLA_NOPE:].reshape(MLA_KV_RANK, -1), MLA_HEADS, MLA_V).astype(BF16)
    pad = lambda g: jnp.pad(g, (0, LANES - MLA_QK)).reshape(1, LANES)
    consts = [g_qa.reshape(1, -1), g_kva.reshape(1, -1), wq, wk, wv, pad(g_q), pad(g_k)]
    full = lambda a: pl.BlockSpec(a.shape, lambda i, j: (0,) * a.ndim)
    tab = pl.BlockSpec((bl, LANES), lambda i, j: (j, 0))
    return pl.pallas_call(
        functools.partial(_mla_qkv_kernel, scale=MLA_QK ** -0.5 * LOG2E),
        grid=(b, lt // bl),
        in_specs=[pl.BlockSpec((1, bl, cols), lambda i, j: (i, j, 0))] + [full(a) for a in consts] + [tab, tab],
        out_specs=[pl.BlockSpec((1, MLA_HEADS, bl, LANES), lambda i, j: (i, 0, j, 0)),
                   _kt_spec(MLA_HEADS, LANES, bl, bk),
                   pl.BlockSpec((1, MLA_HEADS, bl, LANES), lambda i, j: (i, 0, j, 0))],
        out_shape=[jax.ShapeDtypeStruct((b, MLA_HEADS, lt, LANES), BF16),
                   jax.ShapeDtypeStruct((b, MLA_HEADS, lt // bk, LANES, bk), BF16),
                   jax.ShapeDtypeStruct((b, MLA_HEADS, lt, LANES), BF16)],
        compiler_params=_params("parallel", "parallel"),
        name="mla_qkv",
    )(p, *consts, cos, sin)


def _pair_norm_rope(x, g, cos, sin):
    lane = lax.broadcasted_iota(jnp.int32, x.shape, 1)
    lo = lane < GQA_HEAD_DIM
    xx = x * x
    ss_lo = jnp.sum(jnp.where(lo, xx, 0.0), axis=-1, keepdims=True)
    ss_hi = jnp.sum(jnp.where(lo, 0.0, xx), axis=-1, keepdims=True)
    rs = lax.rsqrt(jnp.where(lo, ss_lo, ss_hi) / GQA_HEAD_DIM + NORM_EPS)
    return _rope(x * rs * g, cos, sin, GQA_HEAD_DIM // 4)


def _gqa_qkv_kernel(x_ref, g_ref, ml_ref, mc_ref, win_ref, gq_ref, gk_ref, cos_ref, sin_ref, q_ref, kt_ref,
                    v_ref, *, scale, bl, n_lat):
    p = jnp.dot(_ln_mod(x_ref, g_ref, ml_ref, mc_ref, bl, n_lat), win_ref[...], preferred_element_type=F32)
    cos, sin = cos_ref[...], sin_ref[...]
    qw = GQA_HEADS * GQA_HEAD_DIM
    kw = GQA_KV_HEADS * GQA_HEAD_DIM
    for j in range(GQA_HEADS // 2):
        q = _pair_norm_rope(p[:, j * LANES:(j + 1) * LANES], gq_ref[...], cos, sin) * scale
        q_ref[0, 2 * j] = q[:, :GQA_HEAD_DIM].astype(q_ref.dtype)
        q_ref[0, 2 * j + 1] = q[:, GQA_HEAD_DIM:].astype(q_ref.dtype)
    for j in range(GQA_KV_HEADS // 2):
        k = _pair_norm_rope(p[:, qw + j * LANES:qw + (j + 1) * LANES], gk_ref[...], cos, sin)
        kt = k.T
        kt_ref[0, 2 * j, 0] = kt[:GQA_HEAD_DIM].astype(kt_ref.dtype)
        kt_ref[0, 2 * j + 1, 0] = kt[GQA_HEAD_DIM:].astype(kt_ref.dtype)
        v = p[:, qw + kw + j * LANES:qw + kw + (j + 1) * LANES]
        lo = lax.broadcasted_iota(jnp.int32, v.shape, 1) < GQA_HEAD_DIM
        v_ref[0, 2 * j] = jnp.where(lo, v, 1.0).astype(v_ref.dtype)
        v_ref[0, 2 * j + 1] = jnp.where(lo, pltpu.roll(v, GQA_HEAD_DIM, 1), 1.0).astype(v_ref.dtype)


def _gqa_qkv(x, g, mod_l, mod_c, w_in, n_lat, g_q, g_k, cos, sin):
    b, lt, d = x.shape
    bl = TOKEN_BLOCK
    bk = _key_chunk(lt)
    hd = GQA_HEAD_DIM
    two = lambda t: jnp.concatenate([t, t]).reshape(1, LANES)
    tab = pl.BlockSpec((bl, LANES), lambda i, j: (j, 0))
    vec = pl.BlockSpec((1, LANES), lambda i, j: (0, 0))
    return pl.pallas_call(
        functools.partial(_gqa_qkv_kernel, scale=hd ** -0.5 * LOG2E, bl=bl, n_lat=n_lat),
        grid=(b, lt // bl),
        in_specs=_ln_mod_specs(d, bl) + [pl.BlockSpec(w_in.shape, lambda i, j: (0, 0)), vec, vec, tab, tab],
        out_specs=[pl.BlockSpec((1, GQA_HEADS, bl, hd), lambda i, j: (i, 0, j, 0)),
                   _kt_spec(GQA_KV_HEADS, hd, bl, bk),
                   pl.BlockSpec((1, GQA_KV_HEADS, bl, LANES), lambda i, j: (i, 0, j, 0))],
        out_shape=[jax.ShapeDtypeStruct((b, GQA_HEADS, lt, hd), BF16),
                   jax.ShapeDtypeStruct((b, GQA_KV_HEADS, lt // bk, hd, bk), BF16),
                   jax.ShapeDtypeStruct((b, GQA_KV_HEADS, lt, LANES), BF16)],
        compiler_params=_params("parallel", "parallel"),
        name="gqa_qkv",
    )(x, g.reshape(1, d), mod_l, mod_c, w_in, two(g_q), two(g_k), cos, sin)


def _attn_kernel(q_ref, kt_ref, v_ref, o_ref, m_scr, acc_scr, *, groups, bq, n_chunks):
    rows = groups * bq
    q = q_ref[0, 0].reshape(rows, q_ref.shape[-1])
    bk = kt_ref.shape[-1]
    dv = o_ref.shape[-1]
    m_scr[...] = jnp.full_like(m_scr, -jnp.inf)
    acc_scr[...] = jnp.zeros_like(acc_scr)

    def step(j, carry):
        s = jnp.dot(q, kt_ref[0, 0, j], preferred_element_type=F32)
        m_prev = m_scr[...]
        m_new = jnp.maximum(m_prev, jnp.max(s, axis=-1, keepdims=True))
        alpha = jnp.exp2(m_prev - m_new)
        p = jnp.exp2(s - pltpu.repeat(m_new, bk // LANES, axis=1))
        vv = v_ref[0, 0, j * bk:(j + 1) * bk, :]
        acc_scr[...] = alpha * acc_scr[...] + jnp.dot(p.astype(BF16), vv, preferred_element_type=F32)
        m_scr[...] = m_new
        return carry

    for j in range(n_chunks):
        step(j, 0)
    acc = acc_scr[...]
    o = acc / pltpu.roll(acc, LANES - dv, 1)
    o_ref[0, 0] = o[:, :dv].reshape(groups, bq, dv).astype(o_ref.dtype)


def _attention_call(q, q_start, lq, ktc, v1):
    b, hq, lt, dk = q.shape
    hk, n_chunks, bk = ktc.shape[1], ktc.shape[2], ktc.shape[-1]
    lk = n_chunks * bk
    dv = LANES // 2
    groups = hq // hk
    bq = min(ATTN_ROWS // groups, lq)
    assert lq % bq == 0 and q_start % bq == 0, "query range must tile into attention blocks"
    rows = groups * bq
    first = q_start // bq
    q5 = q.reshape(b, hk, groups, lt, dk)
    out = pl.pallas_call(
        functools.partial(_attn_kernel, groups=groups, bq=bq, n_chunks=n_chunks),
        grid=(b, hk, lq // bq),
        in_specs=[pl.BlockSpec((1, 1, groups, bq, dk), lambda i, h, j: (i, h, 0, j + first, 0)),
                  pl.BlockSpec((1, 1, n_chunks, dk, bk), lambda i, h, j: (i, h, 0, 0, 0)),
                  pl.BlockSpec((1, 1, lk, LANES), lambda i, h, j: (i, h, 0, 0))],
        out_specs=pl.BlockSpec((1, 1, groups, bq, dv), lambda i, h, j: (i, h, 0, j, 0)),
        out_shape=jax.ShapeDtypeStruct((b, hk, groups, lq, dv), BF16),
        scratch_shapes=[pltpu.VMEM((rows, LANES), F32), pltpu.VMEM((rows, LANES), F32)],
        compiler_params=_params("parallel", "parallel", "arbitrary"),
        name="attention",
    )(q5, ktc, v1)
    return out.reshape(b, hq, lq, dv)


def _attention(q, ktc, v1, n_lat):
    lt = q.shape[2]
    n_ctx = lt - n_lat
    o_l = _attention_call(q, 0, n_lat, ktc, v1)
    o_c = _attention_call(q, n_lat, n_ctx, ktc[:, :, -1:, :, -n_ctx:], v1[:, :, n_lat:])
    return o_l, o_c


def _attn_out_specs(heads, bl, n_lat):
    n_lb = n_lat // bl
    return [pl.BlockSpec((1, heads, bl, LANES // 2), lambda i, j: (i, 0, jnp.minimum(j, n_lb - 1), 0)),
            pl.BlockSpec((1, heads, bl, LANES // 2), lambda i, j: (i, 0, jnp.maximum(j - n_lb, 0), 0))]


def _heads_of(ol_ref, oc_ref, heads, bl, n_lat):
    is_ctx = pl.program_id(1) >= n_lat // bl
    return jnp.concatenate([jnp.where(is_ctx, oc_ref[0, h], ol_ref[0, h]) for h in range(heads)], axis=-1)


def _ab_out_kernel(x_ref, ml_ref, mc_ref, y_ref, bonus_ref, gate_ref, ol_ref, oc_ref, lnw_ref, lnb_ref, wrw_ref,
                   wmla_ref, out_ref, *, bl, n_lat):
    rw = []
    for h in range(RW_HEADS):
        y = y_ref[0, 0, h] + y_ref[1, 0, h]
        mean = jnp.mean(y, axis=-1, keepdims=True)
        yc = y - mean
        var = jnp.mean(yc * yc, axis=-1, keepdims=True)
        yn = yc * lax.rsqrt(var + LN_X_EPS) * lnw_ref[h] + lnb_ref[h]
        rw.append((yn + bonus_ref[0, h]) * gate_ref[0, h])
    rw = jnp.concatenate(rw, axis=-1).astype(BF16)
    o = _heads_of(ol_ref, oc_ref, MLA_HEADS, bl, n_lat)
    acc = (jnp.dot(rw, wrw_ref[...], preferred_element_type=F32)
           + jnp.dot(o, wmla_ref[...], preferred_element_type=F32))
    gate = _row_mods(ml_ref, mc_ref, pl.program_id(1) * bl, bl, n_lat, 2)
    out_ref[0] = x_ref[0] + gate * acc


def _ab_out(x, mod_l, mod_c, y, bonus, gate, o, ln_w, ln_b, w_out, n_lat, lt):
    b, _, d = x.shape
    bl = TOKEN_BLOCK
    hd = RW_HEAD_DIM
    wrw = w_out[:RW_DIM].astype(BF16)
    wmla = w_out[RW_DIM:].astype(BF16)
    xs = pl.BlockSpec((1, bl, d), lambda i, j: (i, j, 0))
    hm = pl.BlockSpec((1, RW_HEADS, bl, hd), lambda i, j: (i, 0, j, 0))
    full = lambda a: pl.BlockSpec(a.shape, lambda i, j: (0,) * a.ndim)
    lnw, lnb = ln_w.reshape(RW_HEADS, 1, hd), ln_b.reshape(RW_HEADS, 1, hd)
    return pl.pallas_call(
        functools.partial(_ab_out_kernel, bl=bl, n_lat=n_lat),
        grid=(b, lt // bl),
        in_specs=[xs, pl.BlockSpec((1, 6, d), lambda i, j: (i, 0, 0)), pl.BlockSpec((1, 6, d), lambda i, j: (0, 0, 0)),
                  pl.BlockSpec((2, 1, RW_HEADS, bl, hd), lambda i, j: (0, i, 0, j, 0)), hm, hm]
        + _attn_out_specs(MLA_HEADS, bl, n_lat) + [full(lnw), full(lnb), full(wrw), full(wmla)],
        out_specs=xs,
        out_shape=jax.ShapeDtypeStruct((b, lt, d), F32),
        compiler_params=_params("parallel", "parallel"),
        name="ab_out",
    )(x, mod_l, mod_c, y, bonus, gate, *o, lnw, lnb, wrw, wmla)


def _gqa_out_kernel(x_ref, ml_ref, mc_ref, ol_ref, oc_ref, w_ref, out_ref, *, bl, n_lat):
    o = _heads_of(ol_ref, oc_ref, GQA_HEADS, bl, n_lat)
    acc = jnp.dot(o, w_ref[...], preferred_element_type=F32)
    gate = _row_mods(ml_ref, mc_ref, pl.program_id(1) * bl, bl, n_lat, 2)
    out_ref[0] = x_ref[0] + gate * acc


def _gqa_out(x, mod_l, mod_c, o, w_out, n_lat, lt):
    b, _, d = x.shape
    bl = TOKEN_BLOCK
    hd = GQA_HEAD_DIM
    w = w_out.astype(BF16)
    xs = pl.BlockSpec((1, bl, d), lambda i, j: (i, j, 0))
    return pl.pallas_call(
        functools.partial(_gqa_out_kernel, bl=bl, n_lat=n_lat),
        grid=(b, lt // bl),
        in_specs=[xs, pl.BlockSpec((1, 6, d), lambda i, j: (i, 0, 0)), pl.BlockSpec((1, 6, d), lambda i, j: (0, 0, 0))]
        + _attn_out_specs(GQA_HEADS, bl, n_lat) + [pl.BlockSpec(w.shape, lambda i, j: (0, 0))],
        out_specs=xs,
        out_shape=jax.ShapeDtypeStruct((b, lt, d), F32),
        compiler_params=_params("parallel", "parallel"),
        name="gqa_out",
    )(x, mod_l, mod_c, *o, w)


def _first_argmax(vals):
    best, idx = vals[0], jnp.zeros(vals[0].shape, jnp.int32)
    for i in range(1, len(vals)):
        better = vals[i] > best
        idx = jnp.where(better, i, idx)
        best = jnp.where(better, vals[i], best)
    return best, idx


def _pick(vals, idx):
    out = vals[0]
    for i in range(1, len(vals)):
        out = jnp.where(idx == i, vals[i], out)
    return out


def _route_kernel(x_ref, g_ref, ml_ref, mc_ref, rwh_ref, rwl_ref, rb_ref, t_ref, comb_ref, grp_ref, *, bl, n_lat):
    first = pl.program_id(1) * bl
    shift = _row_mods(ml_ref, mc_ref, first, bl, n_lat, 3)
    scale = _row_mods(ml_ref, mc_ref, first, bl, n_lat, 4)
    t = (_rms(x_ref[0]) * g_ref[...]) * (1.0 + scale) + shift
    t_ref[0] = t.astype(t_ref.dtype)
    t_hi = t.astype(BF16)
    t_lo = (t - t_hi.astype(F32)).astype(BF16)
    logits = (jnp.dot(t_hi, rwh_ref[...], preferred_element_type=F32)
              + jnp.dot(t_lo, rwh_ref[...], preferred_element_type=F32)
              + jnp.dot(t_hi, rwl_ref[...], preferred_element_type=F32))
    lt = logits.T
    score = [_sigmoid(lt[e:e + 1]) for e in range(N_EXPERTS)]
    biased = [score[e] + rb_ref[e:e + 1, :] for e in range(N_EXPERTS)]
    epg = EXPERTS_PER_GROUP
    group_score = []
    for g in range(N_GROUPS):
        vals = biased[g * epg:(g + 1) * epg]
        pair = [vals[i] + vals[j] for i in range(epg) for j in range(i + 1, epg)]
        group_score.append(functools.reduce(jnp.maximum, pair))
    _, grp = _first_argmax(group_score)
    in_b = [_pick([biased[g * epg + j] for g in range(N_GROUPS)], grp) for j in range(epg)]
    in_s = [_pick([score[g * epg + j] for g in range(N_GROUPS)], grp) for j in range(epg)]
    _, loc1 = _first_argmax(in_b)
    _, loc2 = _first_argmax([jnp.where(loc1 == j, -jnp.inf, in_b[j]) for j in range(epg)])
    w1, w2 = _pick(in_s, loc1), _pick(in_s, loc2)
    wsum = w1 + w2
    w1, w2 = w1 / wsum, w2 / wsum
    e1, e2 = grp * epg + loc1, grp * epg + loc2
    sub = lax.broadcasted_iota(jnp.int32, (LANES, bl), 0)
    comb = jnp.zeros((LANES, bl), F32)
    for e in range(N_EXPERTS):
        c_e = jnp.where(e1 == e, w1, 0.0) + jnp.where(e2 == e, w2, 0.0)
        c_hi = c_e.astype(BF16).astype(F32)
        comb = jnp.where(sub == e, jnp.broadcast_to(c_hi, (LANES, bl)), comb)
        comb = jnp.where(sub == N_EXPERTS + e, jnp.broadcast_to(c_e - c_hi, (LANES, bl)), comb)
    comb_ref[0] = comb.T.astype(comb_ref.dtype)
    grp_ref[0] = grp


def _moe_route(x, g, mod_l, mod_c, router_w, router_b, n_lat):
    b, lt, d = x.shape
    bl = TOKEN_BLOCK
    rw = jnp.pad(router_w, ((0, 0), (0, LANES - N_EXPERTS)))
    rw_hi = rw.astype(BF16)
    rw_lo = (rw - rw_hi.astype(F32)).astype(BF16)
    xs = pl.BlockSpec((1, bl, d), lambda i, j: (i, j, 0))
    return pl.pallas_call(
        functools.partial(_route_kernel, bl=bl, n_lat=n_lat),
        grid=(b, lt // bl),
        in_specs=[xs, pl.BlockSpec((1, d), lambda i, j: (0, 0)),
                  pl.BlockSpec((1, 6, d), lambda i, j: (i, 0, 0)), pl.BlockSpec((1, 6, d), lambda i, j: (0, 0, 0)),
                  pl.BlockSpec(rw.shape, lambda i, j: (0, 0)), pl.BlockSpec(rw.shape, lambda i, j: (0, 0)),
                  pl.BlockSpec((N_EXPERTS, 1), lambda i, j: (0, 0))],
        out_specs=[xs, pl.BlockSpec((1, bl, LANES), lambda i, j: (i, j, 0)),
                   pl.BlockSpec((1, 1, bl), lambda i, j: (i, 0, j))],
        out_shape=[jax.ShapeDtypeStruct((b, lt, d), BF16), jax.ShapeDtypeStruct((b, lt, LANES), BF16),
                   jax.ShapeDtypeStruct((b, 1, lt), jnp.int32)],
        compiler_params=_params("parallel", "parallel"),
        name="moe_route",
    )(x, g.reshape(1, d), mod_l, mod_c, rw_hi, rw_lo, router_b.reshape(N_EXPERTS, 1))


def _ffn(t, wg, wu, wd):
    g = jnp.dot(t, wg, preferred_element_type=F32)
    u = jnp.dot(t, wu, preferred_element_type=F32)
    return jnp.dot((_silu(g) * u).astype(BF16), wd, preferred_element_type=F32)


def _experts_kernel(cnt_ref, x_ref, t_ref, comb_ref, grp_ref, tri_ref, ml_ref, mc_ref, wg_ref, wu_ref, wd_ref,
                    sg_ref, su_ref, sd_ref, out_ref, acc_ref, row_ref, col_ref, *, bl, n_lat):
    g = pl.program_id(2)
    t = t_ref[0]

    @pl.when(g == 0)
    def _():
        acc_ref[...] = _ffn(t, sg_ref[...], su_ref[...], sd_ref[...])
        grp = grp_ref[0]
        gid = lax.broadcasted_iota(jnp.int32, (2 * HALO, bl), 0)
        member = jnp.where(grp == gid, 1.0, 0.0)
        before = jnp.dot(member.astype(BF16), tri_ref[...], preferred_element_type=F32)
        slot = jnp.sum(member * before, axis=0, keepdims=True)
        sub = lax.broadcasted_iota(jnp.int32, (LANES, bl), 0)
        info = jnp.where(sub == 0, jnp.broadcast_to(slot, (LANES, bl)),
                         jnp.where(sub == 1, jnp.broadcast_to(grp.astype(F32), (LANES, bl)), 0.0))
        row_ref[...] = info[:HALO]
        col_ref[...] = info.T

    gf = g.astype(F32)
    slot_r, in_r = row_ref[0:1, :], jnp.where(row_ref[1:2, :] == gf, 1.0, 0.0)
    slot_c, in_c = col_ref[:, 0:1], jnp.where(col_ref[:, 1:2] == gf, 1.0, 0.0)
    count = cnt_ref[(pl.program_id(0) * pl.num_programs(1) + pl.program_id(1)) * N_GROUPS + g]
    wd = wd_ref[...]
    wd = wd.reshape(wd.shape[0] * wd.shape[1], wd.shape[2])
    lane = lax.broadcasted_iota(jnp.int32, (MOE_CAP, LANES), 1)

    def one_pass(k, carry):
        base = (k * MOE_CAP).astype(F32)
        want_r = base + lax.broadcasted_iota(jnp.int32, (MOE_CAP, bl), 0).astype(F32)
        want_c = base + lax.broadcasted_iota(jnp.int32, (bl, MOE_CAP), 1).astype(F32)
        gather = jnp.where(slot_r == want_r, in_r, 0.0).astype(BF16)
        scatter = jnp.where(slot_c == want_c, in_c, 0.0).astype(BF16)
        tg = jnp.dot(gather, t, preferred_element_type=F32).astype(BF16)
        cg = jnp.dot(gather, comb_ref[0], preferred_element_type=F32)
        hs = []
        for i in range(EXPERTS_PER_GROUP):
            e = g * EXPERTS_PER_GROUP + i
            cw = jnp.sum(jnp.where(lane % N_EXPERTS == e, cg, 0.0), axis=-1, keepdims=True)
            gate = jnp.dot(tg, wg_ref[i], preferred_element_type=F32)
            up = jnp.dot(tg, wu_ref[i], preferred_element_type=F32)
            hs.append((_silu(gate) * up * cw).astype(BF16))
        yg = jnp.dot(jnp.concatenate(hs, axis=-1), wd, preferred_element_type=F32)
        acc_ref[...] += jnp.dot(scatter, yg.astype(BF16), preferred_element_type=F32)
        return carry

    lax.fori_loop(0, (count + MOE_CAP - 1) // MOE_CAP, one_pass, 0)

    @pl.when(g == pl.num_programs(2) - 1)
    def _():
        gate = _row_mods(ml_ref, mc_ref, pl.program_id(1) * bl, bl, n_lat, 5)
        out_ref[0] = x_ref[0] + gate * acc_ref[...]


def _moe_experts(x, t, comb, grp, mod_l, mod_c, w_gate, w_up, w_down, sh_gate, sh_up, sh_down, n_lat, out_rows):
    b, lt, d = x.shape
    bl = _row_block(lt, WIDE_ROWS)
    nblk = lt // bl
    ff = w_gate.shape[-1]
    epg = EXPERTS_PER_GROUP
    counts = jnp.sum(grp.reshape(b, nblk, bl, 1) == jnp.arange(N_GROUPS), axis=2, dtype=jnp.int32).reshape(-1)
    tri = jnp.triu(jnp.ones((bl, bl), BF16), 1)
    xs = pl.BlockSpec((1, bl, d), lambda i, j, g, c: (i, j, 0))
    full = lambda a: pl.BlockSpec(a.shape, lambda i, j, g, c: (0,) * a.ndim)
    ws = [w_gate.astype(BF16), w_up.astype(BF16), w_down.astype(BF16),
          sh_gate.astype(BF16), sh_up.astype(BF16), sh_down.astype(BF16)]
    grid_spec = pltpu.PrefetchScalarGridSpec(
        num_scalar_prefetch=1,
        grid=(b, nblk, N_GROUPS),
        in_specs=[xs, xs, pl.BlockSpec((1, bl, LANES), lambda i, j, g, c: (i, j, 0)),
                  pl.BlockSpec((1, 1, bl), lambda i, j, g, c: (i, 0, j)), full(tri),
                  pl.BlockSpec((1, 6, d), lambda i, j, g, c: (i, 0, 0)),
                  pl.BlockSpec((1, 6, d), lambda i, j, g, c: (0, 0, 0)),
                  pl.BlockSpec((epg, d, ff), lambda i, j, g, c: (g, 0, 0)),
                  pl.BlockSpec((epg, d, ff), lambda i, j, g, c: (g, 0, 0)),
                  pl.BlockSpec((epg, ff, d), lambda i, j, g, c: (g, 0, 0)),
                  full(ws[3]), full(ws[4]), full(ws[5])],
        out_specs=xs,
        scratch_shapes=[pltpu.VMEM((bl, d), F32), pltpu.VMEM((HALO, bl), F32), pltpu.VMEM((bl, LANES), F32)],
    )
    return pl.pallas_call(
        functools.partial(_experts_kernel, bl=bl, n_lat=n_lat),
        grid_spec=grid_spec,
        out_shape=jax.ShapeDtypeStruct((b, out_rows, d), F32),
        compiler_params=_params("parallel", "parallel", "arbitrary"),
        name="moe_experts",
    )(counts, x, t, comb, grp, tri, mod_l, mod_c, *ws)


def _rope_tables(seq, n_ctx, rope_dims, lane_offset, repeat):
    half = rope_dims // 4
    t = jnp.arange(seq)
    rowp = (t // GRID_W).astype(F32)
    colp = (t % GRID_W).astype(F32)
    inv = ROPE_THETA ** (-jnp.arange(half, dtype=F32) / half)
    ar, ac = rowp[:, None] * inv[None, :], colp[:, None] * inv[None, :]
    cos = jnp.concatenate([jnp.cos(ar), jnp.cos(ar), jnp.cos(ac), jnp.cos(ac)], axis=1)
    sin = jnp.concatenate([-jnp.sin(ar), jnp.sin(ar), -jnp.sin(ac), jnp.sin(ac)], axis=1)
    width = LANES // repeat
    padl, padr = lane_offset, width - lane_offset - rope_dims
    cos = jnp.pad(cos, ((0, n_ctx), (0, 0)), constant_values=1.0)
    sin = jnp.pad(sin, ((0, n_ctx), (0, 0)))
    cos = jnp.pad(cos, ((0, 0), (padl, padr)), constant_values=1.0)
    sin = jnp.pad(sin, ((0, 0), (padl, padr)))
    return jnp.tile(cos, (1, repeat)), jnp.tile(sin, (1, repeat))


def kernel(x, c, ctx, c_ctx, ada_w, ada_b, norm1_g, norm2_g, ab_w_in, ab_w_out, rw_mu, rw_w0, rw_w2, rw_a0, rw_a2, rw_k_k, rw_k_a, rw_r_k, rw_g2, rw_ln_w, rw_ln_b, mla_g_qa, mla_w_q_up, mla_g_kva, mla_w_kv_up, mla_g_q, mla_g_k, gqa_w_in, gqa_w_out, gqa_g_q, gqa_g_k, router_w, router_b, moe_w_gate, moe_w_up, moe_w_down, shared_w_gate, shared_w_up, shared_w_down):
    b, n_lat, d = x.shape
    n_ctx = ctx.shape[1]
    depth = ada_w.shape[0]
    assert n_lat % TOKEN_BLOCK == 0 and n_ctx % TOKEN_BLOCK == 0, "both sequences must tile into token blocks"
    assert n_lat % GRID_W == 0, "latent tokens must fill whole grid rows"
    xs = jnp.concatenate([x, ctx], axis=1)

    cvec = jnp.concatenate([c, c_ctx[None, :]], axis=0)
    cvec = jnp.pad(cvec, ((0, 8 - (b + 1) % 8), (0, 0))) if (b + 1) % 8 else cvec
    mods = _ada_mods(cvec, ada_w, ada_b).reshape(depth, -1, 6, d)

    mla_cos, mla_sin = _rope_tables(n_lat, n_ctx, MLA_ROPE, MLA_NOPE, 1)
    gqa_cos, gqa_sin = _rope_tables(n_lat, n_ctx, GQA_HEAD_DIM, 0, 2)

    for l in range(depth):
        i = l // 2
        mod_l, mod_c = mods[l, :b], mods[l, b:b + 1]
        rows = n_lat + n_ctx
        if l % 2 == 0:
            w_in = ab_w_in[i]
            w_rw = w_in[:, :RW_COLS].astype(BF16)
            w_m = w_in[:, RW_COLS:]
            w_mla = jnp.concatenate(
                [w_m[:, :MLA_Q_RANK + MLA_KV_RANK],
                 jnp.pad(w_m[:, MLA_Q_RANK + MLA_KV_RANK:], ((0, 0), (MLA_NOPE, LANES - MLA_NOPE - MLA_ROPE)))],
                axis=1).astype(BF16)
            p_rw, p_mla = _ln_mod_mm(xs, norm1_g[l], mod_l, mod_c, [w_rw, w_mla], n_lat)
            r, v, na, lw, kd, bd, gate, bonus = _rwkv_prep(
                p_rw, n_lat, rw_mu[i], rw_w0[i], rw_w2[i], rw_a0[i], rw_a2[i], rw_k_k[i], rw_k_a[i],
                rw_r_k[i].reshape(-1), rw_g2[i])
            y = _wkv_scan(r, v, na, lw, kd, bd, n_lat)
            q, kt, vm = _mla_qkv(p_mla, mla_g_qa[i], mla_w_q_up[i], mla_g_kva[i], mla_w_kv_up[i],
                                 mla_g_q[i], mla_g_k[i], mla_cos, mla_sin)
            o = _attention(q, kt, vm, n_lat)
            xs = _ab_out(xs, mod_l, mod_c, y, bonus, gate, o, rw_ln_w[i], rw_ln_b[i], ab_w_out[i], n_lat, rows)
        else:
            q, kt, vg = _gqa_qkv(xs, norm1_g[l], mod_l, mod_c, gqa_w_in[i].astype(BF16), n_lat,
                                 gqa_g_q[i], gqa_g_k[i], gqa_cos, gqa_sin)
            o = _attention(q, kt, vg, n_lat)
            xs = _gqa_out(xs, mod_l, mod_c, o, gqa_w_out[i], n_lat, rows)
        t, comb, grp = _moe_route(xs, norm2_g[l], mod_l, mod_c, router_w, router_b, n_lat)
        xs = _moe_experts(xs, t, comb, grp, mod_l, mod_c, moe_w_gate[l], moe_w_up[l], moe_w_down[l],
                          shared_w_gate[l], shared_w_up[l], shared_w_down[l], n_lat,
                          n_lat if l == depth - 1 else rows)
    return xs
```

```python
import functools
import math

import jax
import jax.numpy as jnp
from jax import lax
from jax.experimental import pallas as pl
from jax.experimental.pallas import tpu as pltpu

F32 = jnp.float32
BF16 = jnp.bfloat16
HIGHEST = lax.Precision.HIGHEST

GRID_W = 64
RW_HEADS = 8
RW_HEAD_DIM = 64
RW_DIM = RW_HEADS * RW_HEAD_DIM
DECAY_LORA = 64
ICLR_LORA = 64
GATE_LORA = 128
RW_COLS = 3 * RW_DIM + 2 * DECAY_LORA + 2 * ICLR_LORA + GATE_LORA
LN_X_EPS = 64e-5
MLA_HEADS = 8
MLA_NOPE = 64
MLA_ROPE = 32
MLA_V = 64
MLA_QK = MLA_NOPE + MLA_ROPE
MLA_Q_RANK = 384
MLA_KV_RANK = 256
GQA_HEADS = 16
GQA_KV_HEADS = 4
GQA_HEAD_DIM = 64
N_EXPERTS = 16
N_GROUPS = 4
EXPERTS_PER_GROUP = N_EXPERTS // N_GROUPS
ROPE_THETA = 10000.0
NORM_EPS = 1e-6

LANES = 128
V7X_VMEM_BYTES = 64 * 1024 * 1024
VMEM_LIMIT = V7X_VMEM_BYTES - 8 * 1024 * 1024

TOKEN_BLOCK = 256
HALO = 8
SCAN_CHUNK = 64
SCAN_INTERLEAVE = 16
SCAN_CHUNKS_PER_STEP = TOKEN_BLOCK // SCAN_CHUNK
ATTN_ROWS = 1024
ATTN_KEYS = 2816
WIDE_ROWS = 768
MOE_CAP = 256
LOG2E = math.log2(math.e)


def _row_block(lt, limit):
    return max(r for r in range(TOKEN_BLOCK, limit + 1, TOKEN_BLOCK) if lt % r == 0)


def _key_chunk(lt):
    return max(c for c in range(TOKEN_BLOCK, ATTN_KEYS + 1, TOKEN_BLOCK) if lt % c == 0)


def _kt_spec(heads, dk, bl, bk):
    per = bk // bl
    return pl.BlockSpec((1, heads, 1, dk, bl), lambda i, j: (i, 0, j // per, 0, j % per))


def _params(*sem):
    return pltpu.CompilerParams(dimension_semantics=sem, vmem_limit_bytes=VMEM_LIMIT)


def _bdot(a, b):
    return jnp.dot(a.astype(BF16), b.astype(BF16), preferred_element_type=F32)


def _batched(a, b, ca, cb):
    return lax.dot_general(a.astype(BF16), b.astype(BF16), (((ca,), (cb,)), ((0,), (0,))),
                           preferred_element_type=F32)


def _bmm(a, b):
    return _batched(a, b, 2, 1)


def _bmm_nt(a, b):
    return _batched(a, b, 2, 2)


def _bmm_tn(a, b):
    return _batched(a, b, 1, 1)


def _sigmoid(x):
    return 1.0 / (1.0 + jnp.exp(-x))


def _silu(x):
    return x * _sigmoid(x)


def _rms(x):
    return x * lax.rsqrt(jnp.mean(x * x, axis=-1, keepdims=True) + NORM_EPS)


def _row_mods(ml_ref, mc_ref, first_row, rows, n_lat, idx):
    r = first_row + lax.broadcasted_iota(jnp.int32, (rows, 1), 0)
    return jnp.where(r >= n_lat, mc_ref[0, idx:idx + 1, :], ml_ref[0, idx:idx + 1, :])


def _ada_kernel(c_ref, w_ref, b_ref, o_ref):
    s = _silu(c_ref[...])
    o_ref[0] = jnp.dot(s, w_ref[0], preferred_element_type=F32, precision=HIGHEST) + b_ref[0]


def _ada_mods(cvec, ada_w, ada_b):
    depth, d, n = ada_w.shape
    rows = cvec.shape[0]
    bn = n // 4
    return pl.pallas_call(
        _ada_kernel,
        grid=(depth, n // bn),
        in_specs=[pl.BlockSpec((rows, d), lambda l, j: (0, 0)),
                  pl.BlockSpec((1, d, bn), lambda l, j: (l, 0, j)),
                  pl.BlockSpec((1, 1, bn), lambda l, j: (l, 0, j))],
        out_specs=pl.BlockSpec((1, rows, bn), lambda l, j: (l, 0, j)),
        out_shape=jax.ShapeDtypeStruct((depth, rows, n), F32),
        compiler_params=_params("parallel", "parallel"),
        name="ada_mods",
    )(cvec, ada_w, ada_b.reshape(depth, 1, n))


def _ln_mod(x_ref, g_ref, ml_ref, mc_ref, bl, n_lat):
    first = pl.program_id(1) * bl
    shift = _row_mods(ml_ref, mc_ref, first, bl, n_lat, 0)
    scale = _row_mods(ml_ref, mc_ref, first, bl, n_lat, 1)
    h = (_rms(x_ref[0]) * g_ref[...]) * (1.0 + scale) + shift
    return h.astype(BF16)


def _ln_mod_specs(d, bl):
    return [pl.BlockSpec((1, bl, d), lambda i, j: (i, j, 0)), pl.BlockSpec((1, d), lambda i, j: (0, 0)),
            pl.BlockSpec((1, 6, d), lambda i, j: (i, 0, 0)), pl.BlockSpec((1, 6, d), lambda i, j: (0, 0, 0))]


def _ln_mod_mm_kernel(x_ref, g_ref, ml_ref, mc_ref, *refs, n_w, bl, n_lat):
    w_refs, o_refs = refs[:n_w], refs[n_w:]
    hb = _ln_mod(x_ref, g_ref, ml_ref, mc_ref, bl, n_lat)
    for w_ref, o_ref in zip(w_refs, o_refs):
        o_ref[0] = jnp.dot(hb, w_ref[...], preferred_element_type=F32).astype(o_ref.dtype)


def _ln_mod_mm(x, g, mod_l, mod_c, ws, n_lat):
    b, lt, d = x.shape
    bl = _row_block(lt, WIDE_ROWS)
    kern = functools.partial(_ln_mod_mm_kernel, n_w=len(ws), bl=bl, n_lat=n_lat)
    return pl.pallas_call(
        kern,
        grid=(b, lt // bl),
        in_specs=[pl.BlockSpec((1, bl, d), lambda i, j: (i, j, 0)),
                  pl.BlockSpec((1, d), lambda i, j: (0, 0)),
                  pl.BlockSpec((1, 6, d), lambda i, j: (i, 0, 0)),
                  pl.BlockSpec((1, 6, d), lambda i, j: (0, 0, 0))]
        + [pl.BlockSpec(w.shape, lambda i, j: (0, 0)) for w in ws],
        out_specs=[pl.BlockSpec((1, bl, w.shape[1]), lambda i, j: (i, j, 0)) for w in ws],
        out_shape=[jax.ShapeDtypeStruct((b, lt, w.shape[1]), F32) for w in ws],
        compiler_params=_params("parallel", "parallel"),
        name="ln_mod_mm",
    )(x, g.reshape(1, d), mod_l, mod_c, *ws)


def _rwkv_prep_kernel(p_ref, prev_ref, next_ref, mu_ref, w0_ref, w2_ref, a0_ref, a2_ref, kk_ref,
                      ka_ref, rk_ref, g2_ref, r_ref, v_ref, na_ref, lw_ref, kd_ref, bd_ref,
                      gate_ref, bonus_ref, *, bl, ctx_block):
    p = p_ref[0]
    j = pl.program_id(1)
    starts = jnp.logical_or(j == 0, j == ctx_block)
    ends = jnp.logical_or(j == ctx_block - 1, j == pl.num_programs(1) - 1)
    before = jnp.where(starts, 0.0, prev_ref[0, HALO - 1:HALO, :])
    after = jnp.where(ends, 0.0, next_ref[0, 0:1, :])
    row = lax.broadcasted_iota(jnp.int32, p.shape, 0)
    prev = jnp.where(row == 0, before, pltpu.roll(p, 1, 0))
    nxt = jnp.where(row == bl - 1, after, pltpu.roll(p, bl - 1, 0))
    pm = p + mu_ref[...] * (0.5 * (prev + nxt) - p)

    o = 3 * RW_DIM
    r = pm[:, 0:RW_DIM]
    k = pm[:, RW_DIM:2 * RW_DIM]
    v = pm[:, 2 * RW_DIM:o]
    w1 = (pm[:, o:o + DECAY_LORA], pm[:, o + DECAY_LORA:o + 2 * DECAY_LORA])
    o += 2 * DECAY_LORA
    a1 = (pm[:, o:o + ICLR_LORA], pm[:, o + ICLR_LORA:o + 2 * ICLR_LORA])
    o += 2 * ICLR_LORA
    g1 = pm[:, o:o + GATE_LORA]

    kk = k * kk_ref[...]
    lw, kd, ad = [], [], []
    for d in range(2):
        z = w0_ref[d:d + 1, :] + _bdot(jnp.tanh(w1[d]), w2_ref[d])
        lw.append(-math.exp(-0.5) * _sigmoid(z))
        a = _sigmoid(a0_ref[d:d + 1, :] + _bdot(a1[d], a2_ref[d]))
        ad.append(a)
        kd.append(k * (1.0 + (a - 1.0) * ka_ref[...]))
    gate = _bdot(_sigmoid(g1), g2_ref[...])
    rkk = r * (kd[0] + kd[1]) * rk_ref[...]

    for h in range(RW_HEADS):
        sl = slice(h * RW_HEAD_DIM, (h + 1) * RW_HEAD_DIM)
        kk_h = kk[:, sl]
        kk_h = kk_h * lax.rsqrt(jnp.sum(kk_h * kk_h, axis=-1, keepdims=True) + 1e-12)
        v_h = v[:, sl]
        r_ref[0, h] = r[:, sl]
        v_ref[0, h] = v_h
        na_ref[0, h] = -kk_h
        for d in range(2):
            lw_ref[d, 0, h] = lw[d][:, sl]
            kd_ref[d, 0, h] = kd[d][:, sl]
            bd_ref[d, 0, h] = kk_h * ad[d][:, sl]
        gate_ref[0, h] = gate[:, sl]
        bonus_ref[0, h] = jnp.sum(rkk[:, sl], axis=-1, keepdims=True) * v_h


def _rwkv_prep(p, n_lat, mu, w0, w2, a0, a2, k_k, k_a, r_k, g2):
    b, lt, cols = p.shape
    bl = TOKEN_BLOCK
    nblk = lt // bl
    per = bl // HALO
    last_slab = lt // HALO - 1
    prev_spec = pl.BlockSpec((1, HALO, cols), lambda i, j: (i, jnp.maximum(j * per - 1, 0), 0))
    next_spec = pl.BlockSpec((1, HALO, cols), lambda i, j: (i, jnp.minimum((j + 1) * per, last_slab), 0))

    hm = jax.ShapeDtypeStruct((b, RW_HEADS, lt, RW_HEAD_DIM), F32)
    hm2 = jax.ShapeDtypeStruct((2, b, RW_HEADS, lt, RW_HEAD_DIM), F32)
    hm_spec = pl.BlockSpec((1, RW_HEADS, bl, RW_HEAD_DIM), lambda i, j: (i, 0, j, 0))
    hm2_spec = pl.BlockSpec((2, 1, RW_HEADS, bl, RW_HEAD_DIM), lambda i, j: (0, i, 0, j, 0))
    full = lambda a: pl.BlockSpec(a.shape, lambda i, j: (0,) * a.ndim)
    consts = [mu.reshape(1, cols), w0, w2.astype(BF16), a0, a2.astype(BF16), k_k.reshape(1, RW_DIM),
              k_a.reshape(1, RW_DIM), r_k.reshape(1, RW_DIM), g2.astype(BF16)]
    return pl.pallas_call(
        functools.partial(_rwkv_prep_kernel, bl=bl, ctx_block=n_lat // bl),
        grid=(b, nblk),
        in_specs=[pl.BlockSpec((1, bl, cols), lambda i, j: (i, j, 0)), prev_spec, next_spec]
        + [full(a) for a in consts],
        out_specs=[hm_spec, hm_spec, hm_spec, hm2_spec, hm2_spec, hm2_spec, hm_spec, hm_spec],
        out_shape=[hm, hm, hm, hm2, hm2, hm2, hm, hm],
        compiler_params=_params("parallel", "parallel"),
        name="rwkv_prep",
    )(p, p, p, *consts)


def _wkv_kernel(r_ref, v_ref, a_ref, lw_ref, k_ref, b_ref, y_ref, st_ref, *, cn, chains):
    d = pl.program_id(0)

    @pl.when(pl.program_id(1) == 0)
    def _():
        st_ref[...] = jnp.zeros_like(st_ref)

    row = lax.broadcasted_iota(jnp.int32, (cn, cn), 0)
    col = lax.broadcasted_iota(jnp.int32, (cn, cn), 1)
    lead = (row - col) * jnp.where(d == 0, 1, -1)
    upto = lead >= 0
    row2 = lax.broadcasted_iota(jnp.int32, (2 * cn, 2 * cn), 0)
    col2 = lax.broadcasted_iota(jnp.int32, (2 * cn, 2 * cn), 1)
    lead2 = (row2 % cn - col2 % cn) * jnp.where(d == 0, 1, -1)
    mask4 = lead2 >= jnp.where(row2 < cn, 1, 0)
    eye = (col == row).astype(F32)
    cum_mat = upto.astype(F32)
    hd = r_ref.shape[-1]
    eye_h = (lax.broadcasted_iota(jnp.int32, (hd, hd), 0)
             == lax.broadcasted_iota(jnp.int32, (hd, hd), 1)).astype(F32)
    n_double = int(math.log2(cn)) - 1

    nb = SCAN_INTERLEAVE
    cum_b = jnp.broadcast_to(cum_mat, (nb, cn, cn))

    def chunk(sl, rows, st):
        r, v, a = r_ref[sl, rows], v_ref[sl, rows], a_ref[sl, rows]
        lw, k, b = lw_ref[0, sl, rows], k_ref[0, sl, rows], b_ref[0, sl, rows]
        lw_hi = lw.astype(BF16)
        lw_lo = (lw - lw_hi.astype(F32)).astype(BF16)
        cum2 = _bmm(cum_b, jnp.concatenate([lw_hi, lw_lo], axis=-1))
        cum = cum2[..., :hd] + cum2[..., hd:]
        tot = jnp.sum(lw, axis=1, keepdims=True)
        at = a * jnp.exp(cum - lw)
        rt = r * jnp.exp(cum)
        einv = jnp.exp(-cum)
        bt, kt = b * einv, k * einv
        eend = jnp.exp(tot - cum)
        bh, kh = b * eend, k * eend

        amat = _bmm_nt(jnp.concatenate([at, rt], axis=1), jnp.concatenate([bt, kt], axis=1))
        amat = jnp.where(mask4, amat, 0.0)
        a_ab = amat[:, :cn, :cn]

        t = eye + a_ab
        x = _bmm(a_ab, a_ab)
        for _ in range(n_double - 1):
            tx = _bmm(jnp.concatenate([t, x], axis=1), x)
            t = t + tx[:, :cn]
            x = tx[:, cn:]
        t = t + _bmm(t, x)

        zero = jnp.zeros_like(v)
        av = _bmm(amat[:, :cn], jnp.concatenate([zero, v], axis=1))
        wu = _bmm(t, jnp.concatenate([at, av], axis=-1))
        rhs = jnp.concatenate([wu, jnp.concatenate([zero, v], axis=-1)], axis=1)
        qy = _bmm(amat[:, cn:], rhs)
        mn = _bmm_tn(jnp.concatenate([bh, kh], axis=1), rhs)
        qe = rt + qy[..., :hd]
        mc = eye_h * jnp.exp(tot) + mn[..., :hd]

        out = _bmm(jnp.concatenate([qe, mc], axis=1), st)
        y_ref[0, sl, rows] = out[:, :cn] + qy[..., hd:]
        return out[:, cn:] + mn[..., hd:]

    n_sub = r_ref.shape[1] // cn

    def group(g, carry):
        sl = pl.ds(g * nb, nb)
        st = st_ref[sl]
        for s in range(n_sub):
            sub = jnp.where(d == 0, s, n_sub - 1 - s)
            st = chunk(sl, pl.ds(pl.multiple_of(sub * cn, cn), cn), st)
        st_ref[sl] = st
        return carry

    lax.fori_loop(0, chains // SCAN_INTERLEAVE, group, 0)


def _wkv_scan(r, v, na, lw, kd, bd, n_lat):
    b, h, lt, hd = r.shape
    cn = SCAN_CHUNK
    chains = b * h
    rows = cn * SCAN_CHUNKS_PER_STEP
    assert n_lat % rows == 0 and lt % rows == 0
    n_all, n_l = lt // rows, n_lat // rows
    n_c = n_all - n_l

    def chunk(d, i):
        fwd = jnp.where(i < n_c, n_l + i, i - n_c)
        rev = jnp.where(i < n_c, n_all - 1 - i, n_l - 1 - (i - n_c))
        return jnp.where(d == 0, fwd, rev)

    shared = pl.BlockSpec((chains, rows, hd), lambda d, i: (0, chunk(d, i), 0))
    per_dir = pl.BlockSpec((1, chains, rows, hd), lambda d, i: (d, 0, chunk(d, i), 0))
    flat = lambda t: t.reshape(t.shape[:-4] + (chains, lt, hd))
    y = pl.pallas_call(
        functools.partial(_wkv_kernel, cn=cn, chains=chains),
        grid=(2, n_all),
        in_specs=[shared, shared, shared, per_dir, per_dir, per_dir],
        out_specs=per_dir,
        out_shape=jax.ShapeDtypeStruct((2, chains, lt, hd), F32),
        scratch_shapes=[pltpu.VMEM((chains, hd, hd), F32)],
        compiler_params=_params("parallel", "arbitrary"),
        name="wkv_scan",
    )(flat(r), flat(v), flat(na), flat(lw), flat(kd), flat(bd))
    return y.reshape(2, b, h, lt, hd)


def _rope(x, cos, sin, half):
    lane = lax.broadcasted_iota(jnp.int32, x.shape, 1)
    width = x.shape[1]
    partner = jnp.where(lane % (2 * half) < half, pltpu.roll(x, width - half, 1), pltpu.roll(x, half, 1))
    return x * cos + partner * sin


def _mla_qkv_kernel(p_ref, gqa_ref, gkva_ref, wq_ref, wk_ref, wv_ref, gq_ref, gk_ref, cos_ref, sin_ref,
                    q_ref, kt_ref, v_ref, *, scale):
    p = p_ref[0]
    q_a = p[:, 0:MLA_Q_RANK]
    kv_a = p[:, MLA_Q_RANK:MLA_Q_RANK + MLA_KV_RANK]
    k_rope = p[:, MLA_Q_RANK + MLA_KV_RANK:MLA_Q_RANK + MLA_KV_RANK + LANES]
    qa_n = (_rms(q_a) * gqa_ref[...]).astype(BF16)
    kva_n = (_rms(kv_a) * gkva_ref[...]).astype(BF16)
    q_all = jnp.dot(qa_n, wq_ref[...], preferred_element_type=F32)
    k_all = jnp.dot(kva_n, wk_ref[...], preferred_element_type=F32)
    v_all = jnp.dot(kva_n, wv_ref[...], preferred_element_type=F32)
    cos, sin = cos_ref[...], sin_ref[...]
    for h in range(MLA_HEADS):
        sl = slice(h * LANES, (h + 1) * LANES)
        q = q_all[:, sl]
        q = q * lax.rsqrt(jnp.sum(q * q, axis=-1, keepdims=True) / MLA_QK + NORM_EPS) * gq_ref[...]
        q = _rope(q, cos, sin, MLA_ROPE // 4)
        q_ref[0, h] = (q * scale).astype(q_ref.dtype)
        k = k_all[:, sl] + k_rope
        k = k * lax.rsqrt(jnp.sum(k * k, axis=-1, keepdims=True) / MLA_QK + NORM_EPS) * gk_ref[...]
        k = _rope(k, cos, sin, MLA_ROPE // 4)
        kt_ref[0, h, 0] = k.T.astype(kt_ref.dtype)
        lane = lax.broadcasted_iota(jnp.int32, q.shape, 1)
        v_ref[0, h] = jnp.where(lane < MLA_V, v_all[:, sl], 1.0).astype(v_ref.dtype)


def _pad_heads(w, heads, width):
    k = w.shape[0]
    w = w.reshape(k, heads, width)
    return jnp.pad(w, ((0, 0), (0, 0), (0, LANES - width))).reshape(k, heads * LANES)


def _mla_qkv(p, g_qa, w_q_up, g_kva, w_kv_up, g_q, g_k, cos, sin):
    b, lt, cols = p.shape
    bl = TOKEN_BLOCK
    bk = _key_chunk(lt)
    wq = _pad_heads(w_q_up, MLA_HEADS, MLA_QK).astype(BF16)
    wkv = w_kv_up.reshape(MLA_KV_RANK, MLA_HEADS, MLA_NOPE + MLA_V)
    wk = _pad_heads(wkv[:, :, :MLA_NOPE].reshape(MLA_KV_RANK, -1), MLA_HEADS, MLA_NOPE).astype(BF16)
    wv = _pad_heads(wkv[:, :, MLA_NOPE:].reshape(MLA_KV_RANK, -1), MLA_HEADS, MLA_V).astype(BF16)
    pad = lambda g: jnp.pad(g, (0, LANES - MLA_QK)).reshape(1, LANES)
    consts = [g_qa.reshape(1, -1), g_kva.reshape(1, -1), wq, wk, wv, pad(g_q), pad(g_k)]
    full = lambda a: pl.BlockSpec(a.shape, lambda i, j: (0,) * a.ndim)
    tab = pl.BlockSpec((bl, LANES), lambda i, j: (j, 0))
    return pl.pallas_call(
        functools.partial(_mla_qkv_kernel, scale=MLA_QK ** -0.5 * LOG2E),
        grid=(b, lt // bl),
        in_specs=[pl.BlockSpec((1, bl, cols), lambda i, j: (i, j, 0))] + [full(a) for a in consts] + [tab, tab],
        out_specs=[pl.BlockSpec((1, MLA_HEADS, bl, LANES), lambda i, j: (i, 0, j, 0)),
                   _kt_spec(MLA_HEADS, LANES, bl, bk),
                   pl.BlockSpec((1, MLA_HEADS, bl, LANES), lambda i, j: (i, 0, j, 0))],
        out_shape=[jax.ShapeDtypeStruct((b, MLA_HEADS, lt, LANES), BF16),
                   jax.ShapeDtypeStruct((b, MLA_HEADS, lt // bk, LANES, bk), BF16),
                   jax.ShapeDtypeStruct((b, MLA_HEADS, lt, LANES), BF16)],
        compiler_params=_params("parallel", "parallel"),
        name="mla_qkv",
    )(p, *consts, cos, sin)


def _pair_norm_rope(x, g, cos, sin):
    lane = lax.broadcasted_iota(jnp.int32, x.shape, 1)
    lo = lane < GQA_HEAD_DIM
    xx = x * x
    ss_lo = jnp.sum(jnp.where(lo, xx, 0.0), axis=-1, keepdims=True)
    ss_hi = jnp.sum(jnp.where(lo, 0.0, xx), axis=-1, keepdims=True)
    rs = lax.rsqrt(jnp.where(lo, ss_lo, ss_hi) / GQA_HEAD_DIM + NORM_EPS)
    return _rope(x * rs * g, cos, sin, GQA_HEAD_DIM // 4)


def _gqa_qkv_kernel(x_ref, g_ref, ml_ref, mc_ref, win_ref, gq_ref, gk_ref, cos_ref, sin_ref, q_ref, kt_ref,
                    v_ref, *, scale, bl, n_lat):
    p = jnp.dot(_ln_mod(x_ref, g_ref, ml_ref, mc_ref, bl, n_lat), win_ref[...], preferred_element_type=F32)
    cos, sin = cos_ref[...], sin_ref[...]
    qw = GQA_HEADS * GQA_HEAD_DIM
    kw = GQA_KV_HEADS * GQA_HEAD_DIM
    for j in range(GQA_HEADS // 2):
        q = _pair_norm_rope(p[:, j * LANES:(j + 1) * LANES], gq_ref[...], cos, sin) * scale
        q_ref[0, 2 * j] = q[:, :GQA_HEAD_DIM].astype(q_ref.dtype)
        q_ref[0, 2 * j + 1] = q[:, GQA_HEAD_DIM:].astype(q_ref.dtype)
    for j in range(GQA_KV_HEADS // 2):
        k = _pair_norm_rope(p[:, qw + j * LANES:qw + (j + 1) * LANES], gk_ref[...], cos, sin)
        kt = k.T
        kt_ref[0, 2 * j, 0] = kt[:GQA_HEAD_DIM].astype(kt_ref.dtype)
        kt_ref[0, 2 * j + 1, 0] = kt[GQA_HEAD_DIM:].astype(kt_ref.dtype)
        v = p[:, qw + kw + j * LANES:qw + kw + (j + 1) * LANES]
        lo = lax.broadcasted_iota(jnp.int32, v.shape, 1) < GQA_HEAD_DIM
        v_ref[0, 2 * j] = jnp.where(lo, v, 1.0).astype(v_ref.dtype)
        v_ref[0, 2 * j + 1] = jnp.where(lo, pltpu.roll(v, GQA_HEAD_DIM, 1), 1.0).astype(v_ref.dtype)


def _gqa_qkv(x, g, mod_l, mod_c, w_in, n_lat, g_q, g_k, cos, sin):
    b, lt, d = x.shape
    bl = TOKEN_BLOCK
    bk = _key_chunk(lt)
    hd = GQA_HEAD_DIM
    two = lambda t: jnp.concatenate([t, t]).reshape(1, LANES)
    tab = pl.BlockSpec((bl, LANES), lambda i, j: (j, 0))
    vec = pl.BlockSpec((1, LANES), lambda i, j: (0, 0))
    return pl.pallas_call(
        functools.partial(_gqa_qkv_kernel, scale=hd ** -0.5 * LOG2E, bl=bl, n_lat=n_lat),
        grid=(b, lt // bl),
        in_specs=_ln_mod_specs(d, bl) + [pl.BlockSpec(w_in.shape, lambda i, j: (0, 0)), vec, vec, tab, tab],
        out_specs=[pl.BlockSpec((1, GQA_HEADS, bl, hd), lambda i, j: (i, 0, j, 0)),
                   _kt_spec(GQA_KV_HEADS, hd, bl, bk),
                   pl.BlockSpec((1, GQA_KV_HEADS, bl, LANES), lambda i, j: (i, 0, j, 0))],
        out_shape=[jax.ShapeDtypeStruct((b, GQA_HEADS, lt, hd), BF16),
                   jax.ShapeDtypeStruct((b, GQA_KV_HEADS, lt // bk, hd, bk), BF16),
                   jax.ShapeDtypeStruct((b, GQA_KV_HEADS, lt, LANES), BF16)],
        compiler_params=_params("parallel", "parallel"),
        name="gqa_qkv",
    )(x, g.reshape(1, d), mod_l, mod_c, w_in, two(g_q), two(g_k), cos, sin)


def _attn_kernel(q_ref, kt_ref, v_ref, o_ref, m_scr, acc_scr, *, groups, bq, n_chunks):
    rows = groups * bq
    q = q_ref[0, 0].reshape(rows, q_ref.shape[-1])
    bk = kt_ref.shape[-1]
    dv = o_ref.shape[-1]
    m_scr[...] = jnp.full_like(m_scr, -jnp.inf)
    acc_scr[...] = jnp.zeros_like(acc_scr)

    def step(j, carry):
        s = jnp.dot(q, kt_ref[0, 0, j], preferred_element_type=F32)
        m_prev = m_scr[...]
        m_new = jnp.maximum(m_prev, jnp.max(s, axis=-1, keepdims=True))
        alpha = jnp.exp2(m_prev - m_new)
        p = jnp.exp2(s - pltpu.repeat(m_new, bk // LANES, axis=1))
        vv = v_ref[0, 0, j * bk:(j + 1) * bk, :]
        acc_scr[...] = alpha * acc_scr[...] + jnp.dot(p.astype(BF16), vv, preferred_element_type=F32)
        m_scr[...] = m_new
        return carry

    for j in range(n_chunks):
        step(j, 0)
    acc = acc_scr[...]
    o = acc / pltpu.roll(acc, LANES - dv, 1)
    o_ref[0, 0] = o[:, :dv].reshape(groups, bq, dv).astype(o_ref.dtype)


def _attention_call(q, q_start, lq, ktc, v1):
    b, hq, lt, dk = q.shape
    hk, n_chunks, bk = ktc.shape[1], ktc.shape[2], ktc.shape[-1]
    lk = n_chunks * bk
    dv = LANES // 2
    groups = hq // hk
    bq = min(ATTN_ROWS // groups, lq)
    assert lq % bq == 0 and q_start % bq == 0, "query range must tile into attention blocks"
    rows = groups * bq
    first = q_start // bq
    q5 = q.reshape(b, hk, groups, lt, dk)
    out = pl.pallas_call(
        functools.partial(_attn_kernel, groups=groups, bq=bq, n_chunks=n_chunks),
        grid=(b, hk, lq // bq),
        in_specs=[pl.BlockSpec((1, 1, groups, bq, dk), lambda i, h, j: (i, h, 0, j + first, 0)),
                  pl.BlockSpec((1, 1, n_chunks, dk, bk), lambda i, h, j: (i, h, 0, 0, 0)),
                  pl.BlockSpec((1, 1, lk, LANES), lambda i, h, j: (i, h, 0, 0))],
        out_specs=pl.BlockSpec((1, 1, groups, bq, dv), lambda i, h, j: (i, h, 0, j, 0)),
        out_shape=jax.ShapeDtypeStruct((b, hk, groups, lq, dv), BF16),
        scratch_shapes=[pltpu.VMEM((rows, LANES), F32), pltpu.VMEM((rows, LANES), F32)],
        compiler_params=_params("parallel", "parallel", "arbitrary"),
        name="attention",
    )(q5, ktc, v1)
    return out.reshape(b, hq, lq, dv)


def _attention(q, ktc, v1, n_lat):
    lt = q.shape[2]
    n_ctx = lt - n_lat
    o_l = _attention_call(q, 0, n_lat, ktc, v1)
    o_c = _attention_call(q, n_lat, n_ctx, ktc[:, :, -1:, :, -n_ctx:], v1[:, :, n_lat:])
    return o_l, o_c


def _attn_out_specs(heads, bl, n_lat):
    n_lb = n_lat // bl
    return [pl.BlockSpec((1, heads, bl, LANES // 2), lambda i, j: (i, 0, jnp.minimum(j, n_lb - 1), 0)),
            pl.BlockSpec((1, heads, bl, LANES // 2), lambda i, j: (i, 0, jnp.maximum(j - n_lb, 0), 0))]


def _heads_of(ol_ref, oc_ref, heads, bl, n_lat):
    is_ctx = pl.program_id(1) >= n_lat // bl
    return jnp.concatenate([jnp.where(is_ctx, oc_ref[0, h], ol_ref[0, h]) for h in range(heads)], axis=-1)


def _ab_out_kernel(x_ref, ml_ref, mc_ref, y_ref, bonus_ref, gate_ref, ol_ref, oc_ref, lnw_ref, lnb_ref, wrw_ref,
                   wmla_ref, out_ref, *, bl, n_lat):
    rw = []
    for h in range(RW_HEADS):
        y = y_ref[0, 0, h] + y_ref[1, 0, h]
        mean = jnp.mean(y, axis=-1, keepdims=True)
        yc = y - mean
        var = jnp.mean(yc * yc, axis=-1, keepdims=True)
        yn = yc * lax.rsqrt(var + LN_X_EPS) * lnw_ref[h] + lnb_ref[h]
        rw.append((yn + bonus_ref[0, h]) * gate_ref[0, h])
    rw = jnp.concatenate(rw, axis=-1).astype(BF16)
    o = _heads_of(ol_ref, oc_ref, MLA_HEADS, bl, n_lat)
    acc = (jnp.dot(rw, wrw_ref[...], preferred_element_type=F32)
           + jnp.dot(o, wmla_ref[...], preferred_element_type=F32))
    gate = _row_mods(ml_ref, mc_ref, pl.program_id(1) * bl, bl, n_lat, 2)
    out_ref[0] = x_ref[0] + gate * acc


def _ab_out(x, mod_l, mod_c, y, bonus, gate, o, ln_w, ln_b, w_out, n_lat, lt):
    b, _, d = x.shape
    bl = TOKEN_BLOCK
    hd = RW_HEAD_DIM
    wrw = w_out[:RW_DIM].astype(BF16)
    wmla = w_out[RW_DIM:].astype(BF16)
    xs = pl.BlockSpec((1, bl, d), lambda i, j: (i, j, 0))
    hm = pl.BlockSpec((1, RW_HEADS, bl, hd), lambda i, j: (i, 0, j, 0))
    full = lambda a: pl.BlockSpec(a.shape, lambda i, j: (0,) * a.ndim)
    lnw, lnb = ln_w.reshape(RW_HEADS, 1, hd), ln_b.reshape(RW_HEADS, 1, hd)
    return pl.pallas_call(
        functools.partial(_ab_out_kernel, bl=bl, n_lat=n_lat),
        grid=(b, lt // bl),
        in_specs=[xs, pl.BlockSpec((1, 6, d), lambda i, j: (i, 0, 0)), pl.BlockSpec((1, 6, d), lambda i, j: (0, 0, 0)),
                  pl.BlockSpec((2, 1, RW_HEADS, bl, hd), lambda i, j: (0, i, 0, j, 0)), hm, hm]
        + _attn_out_specs(MLA_HEADS, bl, n_lat) + [full(lnw), full(lnb), full(wrw), full(wmla)],
        out_specs=xs,
        out_shape=jax.ShapeDtypeStruct((b, lt, d), F32),
        compiler_params=_params("parallel", "parallel"),
        name="ab_out",
    )(x, mod_l, mod_c, y, bonus, gate, *o, lnw, lnb, wrw, wmla)


def _gqa_out_kernel(x_ref, ml_ref, mc_ref, ol_ref, oc_ref, w_ref, out_ref, *, bl, n_lat):
    o = _heads_of(ol_ref, oc_ref, GQA_HEADS, bl, n_lat)
    acc = jnp.dot(o, w_ref[...], preferred_element_type=F32)
    gate = _row_mods(ml_ref, mc_ref, pl.program_id(1) * bl, bl, n_lat, 2)
    out_ref[0] = x_ref[0] + gate * acc


def _gqa_out(x, mod_l, mod_c, o, w_out, n_lat, lt):
    b, _, d = x.shape
    bl = TOKEN_BLOCK
    hd = GQA_HEAD_DIM
    w = w_out.astype(BF16)
    xs = pl.BlockSpec((1, bl, d), lambda i, j: (i, j, 0))
    return pl.pallas_call(
        functools.partial(_gqa_out_kernel, bl=bl, n_lat=n_lat),
        grid=(b, lt // bl),
        in_specs=[xs, pl.BlockSpec((1, 6, d), lambda i, j: (i, 0, 0)), pl.BlockSpec((1, 6, d), lambda i, j: (0, 0, 0))]
        + _attn_out_specs(GQA_HEADS, bl, n_lat) + [pl.BlockSpec(w.shape, lambda i, j: (0, 0))],
        out_specs=xs,
        out_shape=jax.ShapeDtypeStruct((b, lt, d), F32),
        compiler_params=_params("parallel", "parallel"),
        name="gqa_out",
    )(x, mod_l, mod_c, *o, w)


def _first_argmax(vals):
    best, idx = vals[0], jnp.zeros(vals[0].shape, jnp.int32)
    for i in range(1, len(vals)):
        better = vals[i] > best
        idx = jnp.where(better, i, idx)
        best = jnp.where(better, vals[i], best)
    return best, idx


def _pick(vals, idx):
    out = vals[0]
    for i in range(1, len(vals)):
        out = jnp.where(idx == i, vals[i], out)
    return out


def _route_kernel(x_ref, g_ref, ml_ref, mc_ref, rwh_ref, rwl_ref, rb_ref, t_ref, comb_ref, grp_ref, *, bl, n_lat):
    first = pl.program_id(1) * bl
    shift = _row_mods(ml_ref, mc_ref, first, bl, n_lat, 3)
    scale = _row_mods(ml_ref, mc_ref, first, bl, n_lat, 4)
    t = (_rms(x_ref[0]) * g_ref[...]) * (1.0 + scale) + shift
    t_ref[0] = t.astype(t_ref.dtype)
    t_hi = t.astype(BF16)
    t_lo = (t - t_hi.astype(F32)).astype(BF16)
    logits = (jnp.dot(t_hi, rwh_ref[...], preferred_element_type=F32)
              + jnp.dot(t_lo, rwh_ref[...], preferred_element_type=F32)
              + jnp.dot(t_hi, rwl_ref[...], preferred_element_type=F32))
    lt = logits.T
    score = [_sigmoid(lt[e:e + 1]) for e in range(N_EXPERTS)]
    biased = [score[e] + rb_ref[e:e + 1, :] for e in range(N_EXPERTS)]
    epg = EXPERTS_PER_GROUP
    group_score = []
    for g in range(N_GROUPS):
        vals = biased[g * epg:(g + 1) * epg]
        pair = [vals[i] + vals[j] for i in range(epg) for j in range(i + 1, epg)]
        group_score.append(functools.reduce(jnp.maximum, pair))
    _, grp = _first_argmax(group_score)
    in_b = [_pick([biased[g * epg + j] for g in range(N_GROUPS)], grp) for j in range(epg)]
    in_s = [_pick([score[g * epg + j] for g in range(N_GROUPS)], grp) for j in range(epg)]
    _, loc1 = _first_argmax(in_b)
    _, loc2 = _first_argmax([jnp.where(loc1 == j, -jnp.inf, in_b[j]) for j in range(epg)])
    w1, w2 = _pick(in_s, loc1), _pick(in_s, loc2)
    wsum = w1 + w2
    w1, w2 = w1 / wsum, w2 / wsum
    e1, e2 = grp * epg + loc1, grp * epg + loc2
    sub = lax.broadcasted_iota(jnp.int32, (LANES, bl), 0)
    comb = jnp.zeros((LANES, bl), F32)
    for e in range(N_EXPERTS):
        c_e = jnp.where(e1 == e, w1, 0.0) + jnp.where(e2 == e, w2, 0.0)
        c_hi = c_e.astype(BF16).astype(F32)
        comb = jnp.where(sub == e, jnp.broadcast_to(c_hi, (LANES, bl)), comb)
        comb = jnp.where(sub == N_EXPERTS + e, jnp.broadcast_to(c_e - c_hi, (LANES, bl)), comb)
    comb_ref[0] = comb.T.astype(comb_ref.dtype)
    grp_ref[0] = grp


def _moe_route(x, g, mod_l, mod_c, router_w, router_b, n_lat):
    b, lt, d = x.shape
    bl = _row_block(lt, WIDE_ROWS)
    rw = jnp.pad(router_w, ((0, 0), (0, LANES - N_EXPERTS)))
    rw_hi = rw.astype(BF16)
    rw_lo = (rw - rw_hi.astype(F32)).astype(BF16)
    xs = pl.BlockSpec((1, bl, d), lambda i, j: (i, j, 0))
    return pl.pallas_call(
        functools.partial(_route_kernel, bl=bl, n_lat=n_lat),
        grid=(b, lt // bl),
        in_specs=[xs, pl.BlockSpec((1, d), lambda i, j: (0, 0)),
                  pl.BlockSpec((1, 6, d), lambda i, j: (i, 0, 0)), pl.BlockSpec((1, 6, d), lambda i, j: (0, 0, 0)),
                  pl.BlockSpec(rw.shape, lambda i, j: (0, 0)), pl.BlockSpec(rw.shape, lambda i, j: (0, 0)),
                  pl.BlockSpec((N_EXPERTS, 1), lambda i, j: (0, 0))],
        out_specs=[xs, pl.BlockSpec((1, bl, LANES), lambda i, j: (i, j, 0)),
                   pl.BlockSpec((1, 1, bl), lambda i, j: (i, 0, j))],
        out_shape=[jax.ShapeDtypeStruct((b, lt, d), BF16), jax.ShapeDtypeStruct((b, lt, LANES), BF16),
                   jax.ShapeDtypeStruct((b, 1, lt), jnp.int32)],
        compiler_params=_params("parallel", "parallel"),
        name="moe_route",
    )(x, g.reshape(1, d), mod_l, mod_c, rw_hi, rw_lo, router_b.reshape(N_EXPERTS, 1))


def _ffn(t, wg, wu, wd):
    g = jnp.dot(t, wg, preferred_element_type=F32)
    u = jnp.dot(t, wu, preferred_element_type=F32)
    return jnp.dot((_silu(g) * u).astype(BF16), wd, preferred_element_type=F32)


def _experts_kernel(cnt_ref, x_ref, t_ref, comb_ref, grp_ref, tri_ref, ml_ref, mc_ref, wg_ref, wu_ref, wd_ref,
                    sg_ref, su_ref, sd_ref, out_ref, acc_ref, row_ref, col_ref, *, bl, n_lat):
    g = pl.program_id(2)
    t = t_ref[0]

    @pl.when(g == 0)
    def _():
        acc_ref[...] = _ffn(t, sg_ref[...], su_ref[...], sd_ref[...])
        grp = grp_ref[0]
        gid = lax.broadcasted_iota(jnp.int32, (2 * HALO, bl), 0)
        member = jnp.where(grp == gid, 1.0, 0.0)
        before = jnp.dot(member.astype(BF16), tri_ref[...], preferred_element_type=F32)
        slot = jnp.sum(member * before, axis=0, keepdims=True)
        sub = lax.broadcasted_iota(jnp.int32, (LANES, bl), 0)
        info = jnp.where(sub == 0, jnp.broadcast_to(slot, (LANES, bl)),
                         jnp.where(sub == 1, jnp.broadcast_to(grp.astype(F32), (LANES, bl)), 0.0))
        row_ref[...] = info[:HALO]
        col_ref[...] = info.T

    gf = g.astype(F32)
    slot_r, in_r = row_ref[0:1, :], jnp.where(row_ref[1:2, :] == gf, 1.0, 0.0)
    slot_c, in_c = col_ref[:, 0:1], jnp.where(col_ref[:, 1:2] == gf, 1.0, 0.0)
    count = cnt_ref[(pl.program_id(0) * pl.num_programs(1) + pl.program_id(1)) * N_GROUPS + g]
    wd = wd_ref[...]
    wd = wd.reshape(wd.shape[0] * wd.shape[1], wd.shape[2])
    lane = lax.broadcasted_iota(jnp.int32, (MOE_CAP, LANES), 1)

    def one_pass(k, carry):
        base = (k * MOE_CAP).astype(F32)
        want_r = base + lax.broadcasted_iota(jnp.int32, (MOE_CAP, bl), 0).astype(F32)
        want_c = base + lax.broadcasted_iota(jnp.int32, (bl, MOE_CAP), 1).astype(F32)
        gather = jnp.where(slot_r == want_r, in_r, 0.0).astype(BF16)
        scatter = jnp.where(slot_c == want_c, in_c, 0.0).astype(BF16)
        tg = jnp.dot(gather, t, preferred_element_type=F32).astype(BF16)
        cg = jnp.dot(gather, comb_ref[0], preferred_element_type=F32)
        hs = []
        for i in range(EXPERTS_PER_GROUP):
            e = g * EXPERTS_PER_GROUP + i
            cw = jnp.sum(jnp.where(lane % N_EXPERTS == e, cg, 0.0), axis=-1, keepdims=True)
            gate = jnp.dot(tg, wg_ref[i], preferred_element_type=F32)
            up = jnp.dot(tg, wu_ref[i], preferred_element_type=F32)
            hs.append((_silu(gate) * up * cw).astype(BF16))
        yg = jnp.dot(jnp.concatenate(hs, axis=-1), wd, preferred_element_type=F32)
        acc_ref[...] += jnp.dot(scatter, yg.astype(BF16), preferred_element_type=F32)
        return carry

    lax.fori_loop(0, (count + MOE_CAP - 1) // MOE_CAP, one_pass, 0)

    @pl.when(g == pl.num_programs(2) - 1)
    def _():
        gate = _row_mods(ml_ref, mc_ref, pl.program_id(1) * bl, bl, n_lat, 5)
        out_ref[0] = x_ref[0] + gate * acc_ref[...]


def _moe_experts(x, t, comb, grp, mod_l, mod_c, w_gate, w_up, w_down, sh_gate, sh_up, sh_down, n_lat, out_rows):
    b, lt, d = x.shape
    bl = _row_block(lt, WIDE_ROWS)
    nblk = lt // bl
    ff = w_gate.shape[-1]
    epg = EXPERTS_PER_GROUP
    counts = jnp.sum(grp.reshape(b, nblk, bl, 1) == jnp.arange(N_GROUPS), axis=2, dtype=jnp.int32).reshape(-1)
    tri = jnp.triu(jnp.ones((bl, bl), BF16), 1)
    xs = pl.BlockSpec((1, bl, d), lambda i, j, g, c: (i, j, 0))
    full = lambda a: pl.BlockSpec(a.shape, lambda i, j, g, c: (0,) * a.ndim)
    ws = [w_gate.astype(BF16), w_up.astype(BF16), w_down.astype(BF16),
          sh_gate.astype(BF16), sh_up.astype(BF16), sh_down.astype(BF16)]
    grid_spec = pltpu.PrefetchScalarGridSpec(
        num_scalar_prefetch=1,
        grid=(b, nblk, N_GROUPS),
        in_specs=[xs, xs, pl.BlockSpec((1, bl, LANES), lambda i, j, g, c: (i, j, 0)),
                  pl.BlockSpec((1, 1, bl), lambda i, j, g, c: (i, 0, j)), full(tri),
                  pl.BlockSpec((1, 6, d), lambda i, j, g, c: (i, 0, 0)),
                  pl.BlockSpec((1, 6, d), lambda i, j, g, c: (0, 0, 0)),
                  pl.BlockSpec((epg, d, ff), lambda i, j, g, c: (g, 0, 0)),
                  pl.BlockSpec((epg, d, ff), lambda i, j, g, c: (g, 0, 0)),
                  pl.BlockSpec((epg, ff, d), lambda i, j, g, c: (g, 0, 0)),
                  full(ws[3]), full(ws[4]), full(ws[5])],
        out_specs=xs,
        scratch_shapes=[pltpu.VMEM((bl, d), F32), pltpu.VMEM((HALO, bl), F32), pltpu.VMEM((bl, LANES), F32)],
    )
    return pl.pallas_call(
        functools.partial(_experts_kernel, bl=bl, n_lat=n_lat),
        grid_spec=grid_spec,
        out_shape=jax.ShapeDtypeStruct((b, out_rows, d), F32),
        compiler_params=_params("parallel", "parallel", "arbitrary"),
        name="moe_experts",
    )(counts, x, t, comb, grp, tri, mod_l, mod_c, *ws)


def _rope_tables(seq, n_ctx, rope_dims, lane_offset, repeat):
    half = rope_dims // 4
    t = jnp.arange(seq)
    rowp = (t // GRID_W).astype(F32)
    colp = (t % GRID_W).astype(F32)
    inv = ROPE_THETA ** (-jnp.arange(half, dtype=F32) / half)
    ar, ac = rowp[:, None] * inv[None, :], colp[:, None] * inv[None, :]
    cos = jnp.concatenate([jnp.cos(ar), jnp.cos(ar), jnp.cos(ac), jnp.cos(ac)], axis=1)
    sin = jnp.concatenate([-jnp.sin(ar), jnp.sin(ar), -jnp.sin(ac), jnp.sin(ac)], axis=1)
    width = LANES // repeat
    padl, padr = lane_offset, width - lane_offset - rope_dims
    cos = jnp.pad(cos, ((0, n_ctx), (0, 0)), constant_values=1.0)
    sin = jnp.pad(sin, ((0, n_ctx), (0, 0)))
    cos = jnp.pad(cos, ((0, 0), (padl, padr)), constant_values=1.0)
    sin = jnp.pad(sin, ((0, 0), (padl, padr)))
    return jnp.tile(cos, (1, repeat)), jnp.tile(sin, (1, repeat))


def kernel(x, c, ctx, c_ctx, ada_w, ada_b, norm1_g, norm2_g, ab_w_in, ab_w_out, rw_mu, rw_w0, rw_w2, rw_a0, rw_a2, rw_k_k, rw_k_a, rw_r_k, rw_g2, rw_ln_w, rw_ln_b, mla_g_qa, mla_w_q_up, mla_g_kva, mla_w_kv_up, mla_g_q, mla_g_k, gqa_w_in, gqa_w_out, gqa_g_q, gqa_g_k, router_w, router_b, moe_w_gate, moe_w_up, moe_w_down, shared_w_gate, shared_w_up, shared_w_down):
    b, n_lat, d = x.shape
    n_ctx = ctx.shape[1]
    depth = ada_w.shape[0]
    assert n_lat % TOKEN_BLOCK == 0 and n_ctx % TOKEN_BLOCK == 0, "both sequences must tile into token blocks"
    assert n_lat % GRID_W == 0, "latent tokens must fill whole grid rows"
    xs = jnp.concatenate([x, ctx], axis=1)

    cvec = jnp.concatenate([c, c_ctx[None, :]], axis=0)
    cvec = jnp.pad(cvec, ((0, 8 - (b + 1) % 8), (0, 0))) if (b + 1) % 8 else cvec
    mods = _ada_mods(cvec, ada_w, ada_b).reshape(depth, -1, 6, d)

    mla_cos, mla_sin = _rope_tables(n_lat, n_ctx, MLA_ROPE, MLA_NOPE, 1)
    gqa_cos, gqa_sin = _rope_tables(n_lat, n_ctx, GQA_HEAD_DIM, 0, 2)

    for l in range(depth):
        i = l // 2
        mod_l, mod_c = mods[l, :b], mods[l, b:b + 1]
        rows = n_lat + n_ctx
        if l % 2 == 0:
            w_in = ab_w_in[i]
            w_rw = w_in[:, :RW_COLS].astype(BF16)
            w_m = w_in[:, RW_COLS:]
            w_mla = jnp.concatenate(
                [w_m[:, :MLA_Q_RANK + MLA_KV_RANK],
                 jnp.pad(w_m[:, MLA_Q_RANK + MLA_KV_RANK:], ((0, 0), (MLA_NOPE, LANES - MLA_NOPE - MLA_ROPE)))],
                axis=1).astype(BF16)
            p_rw, p_mla = _ln_mod_mm(xs, norm1_g[l], mod_l, mod_c, [w_rw, w_mla], n_lat)
            r, v, na, lw, kd, bd, gate, bonus = _rwkv_prep(
                p_rw, n_lat, rw_mu[i], rw_w0[i], rw_w2[i], rw_a0[i], rw_a2[i], rw_k_k[i], rw_k_a[i],
                rw_r_k[i].reshape(-1), rw_g2[i])
            y = _wkv_scan(r, v, na, lw, kd, bd, n_lat)
            q, kt, vm = _mla_qkv(p_mla, mla_g_qa[i], mla_w_q_up[i], mla_g_kva[i], mla_w_kv_up[i],
                                 mla_g_q[i], mla_g_k[i], mla_cos, mla_sin)
            o = _attention(q, kt, vm, n_lat)
            xs = _ab_out(xs, mod_l, mod_c, y, bonus, gate, o, rw_ln_w[i], rw_ln_b[i], ab_w_out[i], n_lat, rows)
        else:
            q, kt, vg = _gqa_qkv(xs, norm1_g[l], mod_l, mod_c, gqa_w_in[i].astype(BF16), n_lat,
                                 gqa_g_q[i], gqa_g_k[i], gqa_cos, gqa_sin)
            o = _attention(q, kt, vg, n_lat)
            xs = _gqa_out(xs, mod_l, mod_c, o, gqa_w_out[i], n_lat, rows)
        t, comb, grp = _moe_route(xs, norm2_g[l], mod_l, mod_c, router_w, router_b, n_lat)
        xs = _moe_experts(xs, t, comb, grp, mod_l, mod_c, moe_w_gate[l], moe_w_up[l], moe_w_down[l],
                          shared_w_gate[l], shared_w_up[l], shared_w_down[l], n_lat,
                          n_lat if l == depth - 1 else rows)
    return xs
```

```python
import functools
import math

import jax
import jax.numpy as jnp
from jax import lax
from jax.experimental import pallas as pl
from jax.experimental.pallas import tpu as pltpu

F32 = jnp.float32
BF16 = jnp.bfloat16
HIGHEST = lax.Precision.HIGHEST

GRID_W = 64
RW_HEADS = 8
RW_HEAD_DIM = 64
RW_DIM = RW_HEADS * RW_HEAD_DIM
DECAY_LORA = 64
ICLR_LORA = 64
GATE_LORA = 128
RW_COLS = 3 * RW_DIM + 2 * DECAY_LORA + 2 * ICLR_LORA + GATE_LORA
LN_X_EPS = 64e-5
MLA_HEADS = 8
MLA_NOPE = 64
MLA_ROPE = 32
MLA_V = 64
MLA_QK = MLA_NOPE + MLA_ROPE
MLA_Q_RANK = 384
MLA_KV_RANK = 256
GQA_HEADS = 16
GQA_KV_HEADS = 4
GQA_HEAD_DIM = 64
N_EXPERTS = 16
N_GROUPS = 4
EXPERTS_PER_GROUP = N_EXPERTS // N_GROUPS
ROPE_THETA = 10000.0
NORM_EPS = 1e-6

LANES = 128
V7X_VMEM_BYTES = 64 * 1024 * 1024
VMEM_LIMIT = V7X_VMEM_BYTES - 8 * 1024 * 1024

TOKEN_BLOCK = 256
HALO = 8
SCAN_CHUNK = 64
SCAN_INTERLEAVE = 16
SCAN_CHUNKS_PER_STEP = TOKEN_BLOCK // SCAN_CHUNK
ATTN_ROWS = 1024
ATTN_KEYS = 2816
WIDE_ROWS = 768
LOG2E = math.log2(math.e)


def _row_block(lt, limit):
    return max(r for r in range(TOKEN_BLOCK, limit + 1, TOKEN_BLOCK) if lt % r == 0)


def _key_chunk(lt):
    return max(c for c in range(TOKEN_BLOCK, ATTN_KEYS + 1, TOKEN_BLOCK) if lt % c == 0)


def _kt_spec(heads, dk, bl, bk):
    per = bk // bl
    return pl.BlockSpec((1, heads, 1, dk, bl), lambda i, j: (i, 0, j // per, 0, j % per))


def _params(*sem):
    return pltpu.CompilerParams(dimension_semantics=sem, vmem_limit_bytes=VMEM_LIMIT)


def _bdot(a, b):
    return jnp.dot(a.astype(BF16), b.astype(BF16), preferred_element_type=F32)


def _batched(a, b, ca, cb):
    return lax.dot_general(a.astype(BF16), b.astype(BF16), (((ca,), (cb,)), ((0,), (0,))),
                           preferred_element_type=F32)


def _bmm(a, b):
    return _batched(a, b, 2, 1)


def _bmm_nt(a, b):
    return _batched(a, b, 2, 2)


def _bmm_tn(a, b):
    return _batched(a, b, 1, 1)


def _sigmoid(x):
    return 1.0 / (1.0 + jnp.exp(-x))


def _silu(x):
    return x * _sigmoid(x)


def _rms(x):
    return x * lax.rsqrt(jnp.mean(x * x, axis=-1, keepdims=True) + NORM_EPS)


def _row_mods(ml_ref, mc_ref, first_row, rows, n_lat, idx):
    r = first_row + lax.broadcasted_iota(jnp.int32, (rows, 1), 0)
    return jnp.where(r >= n_lat, mc_ref[0, idx:idx + 1, :], ml_ref[0, idx:idx + 1, :])


def _ada_kernel(c_ref, w_ref, b_ref, o_ref):
    s = _silu(c_ref[...])
    o_ref[0] = jnp.dot(s, w_ref[0], preferred_element_type=F32, precision=HIGHEST) + b_ref[0]


def _ada_mods(cvec, ada_w, ada_b):
    depth, d, n = ada_w.shape
    rows = cvec.shape[0]
    bn = n // 4
    return pl.pallas_call(
        _ada_kernel,
        grid=(depth, n // bn),
        in_specs=[pl.BlockSpec((rows, d), lambda l, j: (0, 0)),
                  pl.BlockSpec((1, d, bn), lambda l, j: (l, 0, j)),
                  pl.BlockSpec((1, 1, bn), lambda l, j: (l, 0, j))],
        out_specs=pl.BlockSpec((1, rows, bn), lambda l, j: (l, 0, j)),
        out_shape=jax.ShapeDtypeStruct((depth, rows, n), F32),
        compiler_params=_params("parallel", "parallel"),
        name="ada_mods",
    )(cvec, ada_w, ada_b.reshape(depth, 1, n))


def _ln_mod(x_ref, g_ref, ml_ref, mc_ref, bl, n_lat):
    first = pl.program_id(1) * bl
    shift = _row_mods(ml_ref, mc_ref, first, bl, n_lat, 0)
    scale = _row_mods(ml_ref, mc_ref, first, bl, n_lat, 1)
    h = (_rms(x_ref[0]) * g_ref[...]) * (1.0 + scale) + shift
    return h.astype(BF16)


def _ln_mod_specs(d, bl):
    return [pl.BlockSpec((1, bl, d), lambda i, j: (i, j, 0)), pl.BlockSpec((1, d), lambda i, j: (0, 0)),
            pl.BlockSpec((1, 6, d), lambda i, j: (i, 0, 0)), pl.BlockSpec((1, 6, d), lambda i, j: (0, 0, 0))]


def _ln_mod_mm_kernel(x_ref, g_ref, ml_ref, mc_ref, *refs, n_w, bl, n_lat):
    w_refs, o_refs = refs[:n_w], refs[n_w:]
    hb = _ln_mod(x_ref, g_ref, ml_ref, mc_ref, bl, n_lat)
    for w_ref, o_ref in zip(w_refs, o_refs):
        o_ref[0] = jnp.dot(hb, w_ref[...], preferred_element_type=F32).astype(o_ref.dtype)


def _ln_mod_mm(x, g, mod_l, mod_c, ws, n_lat):
    b, lt, d = x.shape
    bl = _row_block(lt, WIDE_ROWS)
    kern = functools.partial(_ln_mod_mm_kernel, n_w=len(ws), bl=bl, n_lat=n_lat)
    return pl.pallas_call(
        kern,
        grid=(b, lt // bl),
        in_specs=[pl.BlockSpec((1, bl, d), lambda i, j: (i, j, 0)),
                  pl.BlockSpec((1, d), lambda i, j: (0, 0)),
                  pl.BlockSpec((1, 6, d), lambda i, j: (i, 0, 0)),
                  pl.BlockSpec((1, 6, d), lambda i, j: (0, 0, 0))]
        + [pl.BlockSpec(w.shape, lambda i, j: (0, 0)) for w in ws],
        out_specs=[pl.BlockSpec((1, bl, w.shape[1]), lambda i, j: (i, j, 0)) for w in ws],
        out_shape=[jax.ShapeDtypeStruct((b, lt, w.shape[1]), F32) for w in ws],
        compiler_params=_params("parallel", "parallel"),
        name="ln_mod_mm",
    )(x, g.reshape(1, d), mod_l, mod_c, *ws)


def _rwkv_prep_kernel(p_ref, prev_ref, next_ref, mu_ref, w0_ref, w2_ref, a0_ref, a2_ref, kk_ref,
                      ka_ref, rk_ref, g2_ref, r_ref, v_ref, na_ref, lw_ref, kd_ref, bd_ref,
                      gate_ref, bonus_ref, *, bl, ctx_block):
    p = p_ref[0]
    j = pl.program_id(1)
    starts = jnp.logical_or(j == 0, j == ctx_block)
    ends = jnp.logical_or(j == ctx_block - 1, j == pl.num_programs(1) - 1)
    before = jnp.where(starts, 0.0, prev_ref[0, HALO - 1:HALO, :])
    after = jnp.where(ends, 0.0, next_ref[0, 0:1, :])
    row = lax.broadcasted_iota(jnp.int32, p.shape, 0)
    prev = jnp.where(row == 0, before, pltpu.roll(p, 1, 0))
    nxt = jnp.where(row == bl - 1, after, pltpu.roll(p, bl - 1, 0))
    pm = p + mu_ref[...] * (0.5 * (prev + nxt) - p)

    o = 3 * RW_DIM
    r = pm[:, 0:RW_DIM]
    k = pm[:, RW_DIM:2 * RW_DIM]
    v = pm[:, 2 * RW_DIM:o]
    w1 = (pm[:, o:o + DECAY_LORA], pm[:, o + DECAY_LORA:o + 2 * DECAY_LORA])
    o += 2 * DECAY_LORA
    a1 = (pm[:, o:o + ICLR_LORA], pm[:, o + ICLR_LORA:o + 2 * ICLR_LORA])
    o += 2 * ICLR_LORA
    g1 = pm[:, o:o + GATE_LORA]

    kk = k * kk_ref[...]
    lw, kd, ad = [], [], []
    for d in range(2):
        z = w0_ref[d:d + 1, :] + _bdot(jnp.tanh(w1[d]), w2_ref[d])
        lw.append(-math.exp(-0.5) * _sigmoid(z))
        a = _sigmoid(a0_ref[d:d + 1, :] + _bdot(a1[d], a2_ref[d]))
        ad.append(a)
        kd.append(k * (1.0 + (a - 1.0) * ka_ref[...]))
    gate = _bdot(_sigmoid(g1), g2_ref[...])
    rkk = r * (kd[0] + kd[1]) * rk_ref[...]

    for h in range(RW_HEADS):
        sl = slice(h * RW_HEAD_DIM, (h + 1) * RW_HEAD_DIM)
        kk_h = kk[:, sl]
        kk_h = kk_h * lax.rsqrt(jnp.sum(kk_h * kk_h, axis=-1, keepdims=True) + 1e-12)
        v_h = v[:, sl]
        r_ref[0, h] = r[:, sl]
        v_ref[0, h] = v_h
        na_ref[0, h] = -kk_h
        for d in range(2):
            lw_ref[d, 0, h] = lw[d][:, sl]
            kd_ref[d, 0, h] = kd[d][:, sl]
            bd_ref[d, 0, h] = kk_h * ad[d][:, sl]
        gate_ref[0, h] = gate[:, sl]
        bonus_ref[0, h] = jnp.sum(rkk[:, sl], axis=-1, keepdims=True) * v_h


def _rwkv_prep(p, n_lat, mu, w0, w2, a0, a2, k_k, k_a, r_k, g2):
    b, lt, cols = p.shape
    bl = TOKEN_BLOCK
    nblk = lt // bl
    per = bl // HALO
    last_slab = lt // HALO - 1
    prev_spec = pl.BlockSpec((1, HALO, cols), lambda i, j: (i, jnp.maximum(j * per - 1, 0), 0))
    next_spec = pl.BlockSpec((1, HALO, cols), lambda i, j: (i, jnp.minimum((j + 1) * per, last_slab), 0))

    hm = jax.ShapeDtypeStruct((b, RW_HEADS, lt, RW_HEAD_DIM), F32)
    hm2 = jax.ShapeDtypeStruct((2, b, RW_HEADS, lt, RW_HEAD_DIM), F32)
    hm_spec = pl.BlockSpec((1, RW_HEADS, bl, RW_HEAD_DIM), lambda i, j: (i, 0, j, 0))
    hm2_spec = pl.BlockSpec((2, 1, RW_HEADS, bl, RW_HEAD_DIM), lambda i, j: (0, i, 0, j, 0))
    full = lambda a: pl.BlockSpec(a.shape, lambda i, j: (0,) * a.ndim)
    consts = [mu.reshape(1, cols), w0, w2.astype(BF16), a0, a2.astype(BF16), k_k.reshape(1, RW_DIM),
              k_a.reshape(1, RW_DIM), r_k.reshape(1, RW_DIM), g2.astype(BF16)]
    return pl.pallas_call(
        functools.partial(_rwkv_prep_kernel, bl=bl, ctx_block=n_lat // bl),
        grid=(b, nblk),
        in_specs=[pl.BlockSpec((1, bl, cols), lambda i, j: (i, j, 0)), prev_spec, next_spec]
        + [full(a) for a in consts],
        out_specs=[hm_spec, hm_spec, hm_spec, hm2_spec, hm2_spec, hm2_spec, hm_spec, hm_spec],
        out_shape=[hm, hm, hm, hm2, hm2, hm2, hm, hm],
        compiler_params=_params("parallel", "parallel"),
        name="rwkv_prep",
    )(p, p, p, *consts)


def _wkv_kernel(r_ref, v_ref, a_ref, lw_ref, k_ref, b_ref, y_ref, st_ref, *, cn, chains):
    d = pl.program_id(0)

    @pl.when(pl.program_id(1) == 0)
    def _():
        st_ref[...] = jnp.zeros_like(st_ref)

    row = lax.broadcasted_iota(jnp.int32, (cn, cn), 0)
    col = lax.broadcasted_iota(jnp.int32, (cn, cn), 1)
    lead = (row - col) * jnp.where(d == 0, 1, -1)
    upto = lead >= 0
    row2 = lax.broadcasted_iota(jnp.int32, (2 * cn, 2 * cn), 0)
    col2 = lax.broadcasted_iota(jnp.int32, (2 * cn, 2 * cn), 1)
    lead2 = (row2 % cn - col2 % cn) * jnp.where(d == 0, 1, -1)
    mask4 = lead2 >= jnp.where(row2 < cn, 1, 0)
    eye = (col == row).astype(F32)
    cum_mat = upto.astype(F32)
    hd = r_ref.shape[-1]
    eye_h = (lax.broadcasted_iota(jnp.int32, (hd, hd), 0)
             == lax.broadcasted_iota(jnp.int32, (hd, hd), 1)).astype(F32)
    n_double = int(math.log2(cn)) - 1

    nb = SCAN_INTERLEAVE
    cum_b = jnp.broadcast_to(cum_mat, (nb, cn, cn))

    def chunk(sl, rows, st):
        r, v, a = r_ref[sl, rows], v_ref[sl, rows], a_ref[sl, rows]
        lw, k, b = lw_ref[0, sl, rows], k_ref[0, sl, rows], b_ref[0, sl, rows]
        lw_hi = lw.astype(BF16)
        lw_lo = (lw - lw_hi.astype(F32)).astype(BF16)
        cum2 = _bmm(cum_b, jnp.concatenate([lw_hi, lw_lo], axis=-1))
        cum = cum2[..., :hd] + cum2[..., hd:]
        tot = jnp.sum(lw, axis=1, keepdims=True)
        at = a * jnp.exp(cum - lw)
        rt = r * jnp.exp(cum)
        einv = jnp.exp(-cum)
        bt, kt = b * einv, k * einv
        eend = jnp.exp(tot - cum)
        bh, kh = b * eend, k * eend

        amat = _bmm_nt(jnp.concatenate([at, rt], axis=1), jnp.concatenate([bt, kt], axis=1))
        amat = jnp.where(mask4, amat, 0.0)
        a_ab = amat[:, :cn, :cn]

        t = eye + a_ab
        x = _bmm(a_ab, a_ab)
        for _ in range(n_double - 1):
            tx = _bmm(jnp.concatenate([t, x], axis=1), x)
            t = t + tx[:, :cn]
            x = tx[:, cn:]
        t = t + _bmm(t, x)

        zero = jnp.zeros_like(v)
        av = _bmm(amat[:, :cn], jnp.concatenate([zero, v], axis=1))
        wu = _bmm(t, jnp.concatenate([at, av], axis=-1))
        rhs = jnp.concatenate([wu, jnp.concatenate([zero, v], axis=-1)], axis=1)
        qy = _bmm(amat[:, cn:], rhs)
        mn = _bmm_tn(jnp.concatenate([bh, kh], axis=1), rhs)
        qe = rt + qy[..., :hd]
        mc = eye_h * jnp.exp(tot) + mn[..., :hd]

        out = _bmm(jnp.concatenate([qe, mc], axis=1), st)
        y_ref[0, sl, rows] = out[:, :cn] + qy[..., hd:]
        return out[:, cn:] + mn[..., hd:]

    n_sub = r_ref.shape[1] // cn

    def group(g, carry):
        sl = pl.ds(g * nb, nb)
        st = st_ref[sl]
        for s in range(n_sub):
            sub = jnp.where(d == 0, s, n_sub - 1 - s)
            st = chunk(sl, pl.ds(pl.multiple_of(sub * cn, cn), cn), st)
        st_ref[sl] = st
        return carry

    lax.fori_loop(0, chains // SCAN_INTERLEAVE, group, 0)


def _wkv_scan(r, v, na, lw, kd, bd, n_lat):
    b, h, lt, hd = r.shape
    cn = SCAN_CHUNK
    chains = b * h
    rows = cn * SCAN_CHUNKS_PER_STEP
    assert n_lat % rows == 0 and lt % rows == 0
    n_all, n_l = lt // rows, n_lat // rows
    n_c = n_all - n_l

    def chunk(d, i):
        fwd = jnp.where(i < n_c, n_l + i, i - n_c)
        rev = jnp.where(i < n_c, n_all - 1 - i, n_l - 1 - (i - n_c))
        return jnp.where(d == 0, fwd, rev)

    shared = pl.BlockSpec((chains, rows, hd), lambda d, i: (0, chunk(d, i), 0))
    per_dir = pl.BlockSpec((1, chains, rows, hd), lambda d, i: (d, 0, chunk(d, i), 0))
    flat = lambda t: t.reshape(t.shape[:-4] + (chains, lt, hd))
    y = pl.pallas_call(
        functools.partial(_wkv_kernel, cn=cn, chains=chains),
        grid=(2, n_all),
        in_specs=[shared, shared, shared, per_dir, per_dir, per_dir],
        out_specs=per_dir,
        out_shape=jax.ShapeDtypeStruct((2, chains, lt, hd), F32),
        scratch_shapes=[pltpu.VMEM((chains, hd, hd), F32)],
        compiler_params=_params("parallel", "arbitrary"),
        name="wkv_scan",
    )(flat(r), flat(v), flat(na), flat(lw), flat(kd), flat(bd))
    return y.reshape(2, b, h, lt, hd)


def _rope(x, cos, sin, half):
    lane = lax.broadcasted_iota(jnp.int32, x.shape, 1)
    width = x.shape[1]
    partner = jnp.where(lane % (2 * half) < half, pltpu.roll(x, width - half, 1), pltpu.roll(x, half, 1))
    return x * cos + partner * sin


def _mla_qkv_kernel(p_ref, gqa_ref, gkva_ref, wq_ref, wk_ref, wv_ref, gq_ref, gk_ref, cos_ref, sin_ref,
                    q_ref, kt_ref, v_ref, *, scale):
    p = p_ref[0]
    q_a = p[:, 0:MLA_Q_RANK]
    kv_a = p[:, MLA_Q_RANK:MLA_Q_RANK + MLA_KV_RANK]
    k_rope = p[:, MLA_Q_RANK + MLA_KV_RANK:MLA_Q_RANK + MLA_KV_RANK + LANES]
    qa_n = (_rms(q_a) * gqa_ref[...]).astype(BF16)
    kva_n = (_rms(kv_a) * gkva_ref[...]).astype(BF16)
    q_all = jnp.dot(qa_n, wq_ref[...], preferred_element_type=F32)
    k_all = jnp.dot(kva_n, wk_ref[...], preferred_element_type=F32)
    v_all = jnp.dot(kva_n, wv_ref[...], preferred_element_type=F32)
    cos, sin = cos_ref[...], sin_ref[...]
    for h in range(MLA_HEADS):
        sl = slice(h * LANES, (h + 1) * LANES)
        q = q_all[:, sl]
        q = q * lax.rsqrt(jnp.sum(q * q, axis=-1, keepdims=True) / MLA_QK + NORM_EPS) * gq_ref[...]
        q = _rope(q, cos, sin, MLA_ROPE // 4)
        q_ref[0, h] = (q * scale).astype(q_ref.dtype)
        k = k_all[:, sl] + k_rope
        k = k * lax.rsqrt(jnp.sum(k * k, axis=-1, keepdims=True) / MLA_QK + NORM_EPS) * gk_ref[...]
        k = _rope(k, cos, sin, MLA_ROPE // 4)
        kt_ref[0, h, 0] = k.T.astype(kt_ref.dtype)
        lane = lax.broadcasted_iota(jnp.int32, q.shape, 1)
        v_ref[0, h] = jnp.where(lane < MLA_V, v_all[:, sl], 1.0).astype(v_ref.dtype)


def _pad_heads(w, heads, width):
    k = w.shape[0]
    w = w.reshape(k, heads, width)
    return jnp.pad(w, ((0, 0), (0, 0), (0, LANES - width))).reshape(k, heads * LANES)


def _mla_qkv(p, g_qa, w_q_up, g_kva, w_kv_up, g_q, g_k, cos, sin):
    b, lt, cols = p.shape
    bl = TOKEN_BLOCK
    bk = _key_chunk(lt)
    wq = _pad_heads(w_q_up, MLA_HEADS, MLA_QK).astype(BF16)
    wkv = w_kv_up.reshape(MLA_KV_RANK, MLA_HEADS, MLA_NOPE + MLA_V)
    wk = _pad_heads(wkv[:, :, :MLA_NOPE].reshape(MLA_KV_RANK, -1), MLA_HEADS, MLA_NOPE).astype(BF16)
    wv = _pad_heads(wkv[:, :, MLA_NOPE:].reshape(MLA_KV_RANK, -1), MLA_HEADS, MLA_V).astype(BF16)
    pad = lambda g: jnp.pad(g, (0, LANES - MLA_QK)).reshape(1, LANES)
    consts = [g_qa.reshape(1, -1), g_kva.reshape(1, -1), wq, wk, wv, pad(g_q), pad(g_k)]
    full = lambda a: pl.BlockSpec(a.shape, lambda i, j: (0,) * a.ndim)
    tab = pl.BlockSpec((bl, LANES), lambda i, j: (j, 0))
    return pl.pallas_call(
        functools.partial(_mla_qkv_kernel, scale=MLA_QK ** -0.5 * LOG2E),
        grid=(b, lt // bl),
        in_specs=[pl.BlockSpec((1, bl, cols), lambda i, j: (i, j, 0))] + [full(a) for a in consts] + [tab, tab],
        out_specs=[pl.BlockSpec((1, MLA_HEADS, bl, LANES), lambda i, j: (i, 0, j, 0)),
                   _kt_spec(MLA_HEADS, LANES, bl, bk),
                   pl.BlockSpec((1, MLA_HEADS, bl, LANES), lambda i, j: (i, 0, j, 0))],
        out_shape=[jax.ShapeDtypeStruct((b, MLA_HEADS, lt, LANES), BF16),
                   jax.ShapeDtypeStruct((b, MLA_HEADS, lt // bk, LANES, bk), BF16),
                   jax.ShapeDtypeStruct((b, MLA_HEADS, lt, LANES), BF16)],
        compiler_params=_params("parallel", "parallel"),
        name="mla_qkv",
    )(p, *consts, cos, sin)


def _pair_norm_rope(x, g, cos, sin):
    lane = lax.broadcasted_iota(jnp.int32, x.shape, 1)
    lo = lane < GQA_HEAD_DIM
    xx = x * x
    ss_lo = jnp.sum(jnp.where(lo, xx, 0.0), axis=-1, keepdims=True)
    ss_hi = jnp.sum(jnp.where(lo, 0.0, xx), axis=-1, keepdims=True)
    rs = lax.rsqrt(jnp.where(lo, ss_lo, ss_hi) / GQA_HEAD_DIM + NORM_EPS)
    return _rope(x * rs * g, cos, sin, GQA_HEAD_DIM // 4)


def _gqa_qkv_kernel(x_ref, g_ref, ml_ref, mc_ref, win_ref, gq_ref, gk_ref, cos_ref, sin_ref, q_ref, kt_ref,
                    v_ref, *, scale, bl, n_lat):
    p = jnp.dot(_ln_mod(x_ref, g_ref, ml_ref, mc_ref, bl, n_lat), win_ref[...], preferred_element_type=F32)
    cos, sin = cos_ref[...], sin_ref[...]
    qw = GQA_HEADS * GQA_HEAD_DIM
    kw = GQA_KV_HEADS * GQA_HEAD_DIM
    for j in range(GQA_HEADS // 2):
        q = _pair_norm_rope(p[:, j * LANES:(j + 1) * LANES], gq_ref[...], cos, sin) * scale
        q_ref[0, 2 * j] = q[:, :GQA_HEAD_DIM].astype(q_ref.dtype)
        q_ref[0, 2 * j + 1] = q[:, GQA_HEAD_DIM:].astype(q_ref.dtype)
    for j in range(GQA_KV_HEADS // 2):
        k = _pair_norm_rope(p[:, qw + j * LANES:qw + (j + 1) * LANES], gk_ref[...], cos, sin)
        kt = k.T
        kt_ref[0, 2 * j, 0] = kt[:GQA_HEAD_DIM].astype(kt_ref.dtype)
        kt_ref[0, 2 * j + 1, 0] = kt[GQA_HEAD_DIM:].astype(kt_ref.dtype)
        v = p[:, qw + kw + j * LANES:qw + kw + (j + 1) * LANES]
        lo = lax.broadcasted_iota(jnp.int32, v.shape, 1) < GQA_HEAD_DIM
        v_ref[0, 2 * j] = jnp.where(lo, v, 1.0).astype(v_ref.dtype)
        v_ref[0, 2 * j + 1] = jnp.where(lo, pltpu.roll(v, GQA_HEAD_DIM, 1), 1.0).astype(v_ref.dtype)


def _gqa_qkv(x, g, mod_l, mod_c, w_in, n_lat, g_q, g_k, cos, sin):
    b, lt, d = x.shape
    bl = TOKEN_BLOCK
    bk = _key_chunk(lt)
    hd = GQA_HEAD_DIM
    two = lambda t: jnp.concatenate([t, t]).reshape(1, LANES)
    tab = pl.BlockSpec((bl, LANES), lambda i, j: (j, 0))
    vec = pl.BlockSpec((1, LANES), lambda i, j: (0, 0))
    return pl.pallas_call(
        functools.partial(_gqa_qkv_kernel, scale=hd ** -0.5 * LOG2E, bl=bl, n_lat=n_lat),
        grid=(b, lt // bl),
        in_specs=_ln_mod_specs(d, bl) + [pl.BlockSpec(w_in.shape, lambda i, j: (0, 0)), vec, vec, tab, tab],
        out_specs=[pl.BlockSpec((1, GQA_HEADS, bl, hd), lambda i, j: (i, 0, j, 0)),
                   _kt_spec(GQA_KV_HEADS, hd, bl, bk),
                   pl.BlockSpec((1, GQA_KV_HEADS, bl, LANES), lambda i, j: (i, 0, j, 0))],
        out_shape=[jax.ShapeDtypeStruct((b, GQA_HEADS, lt, hd), BF16),
                   jax.ShapeDtypeStruct((b, GQA_KV_HEADS, lt // bk, hd, bk), BF16),
                   jax.ShapeDtypeStruct((b, GQA_KV_HEADS, lt, LANES), BF16)],
        compiler_params=_params("parallel", "parallel"),
        name="gqa_qkv",
    )(x, g.reshape(1, d), mod_l, mod_c, w_in, two(g_q), two(g_k), cos, sin)


def _attn_kernel(q_ref, kt_ref, v_ref, o_ref, m_scr, acc_scr, *, groups, bq, n_chunks):
    rows = groups * bq
    q = q_ref[0, 0].reshape(rows, q_ref.shape[-1])
    bk = kt_ref.shape[-1]
    dv = o_ref.shape[-1]
    m_scr[...] = jnp.full_like(m_scr, -jnp.inf)
    acc_scr[...] = jnp.zeros_like(acc_scr)

    def step(j, carry):
        s = jnp.dot(q, kt_ref[0, 0, j], preferred_element_type=F32)
        m_prev = m_scr[...]
        m_new = jnp.maximum(m_prev, jnp.max(s, axis=-1, keepdims=True))
        alpha = jnp.exp2(m_prev - m_new)
        p = jnp.exp2(s - pltpu.repeat(m_new, bk // LANES, axis=1))
        vv = v_ref[0, 0, j * bk:(j + 1) * bk, :]
        acc_scr[...] = alpha * acc_scr[...] + jnp.dot(p.astype(BF16), vv, preferred_element_type=F32)
        m_scr[...] = m_new
        return carry

    for j in range(n_chunks):
        step(j, 0)
    acc = acc_scr[...]
    o = acc / pltpu.roll(acc, LANES - dv, 1)
    o_ref[0, 0] = o[:, :dv].reshape(groups, bq, dv).astype(o_ref.dtype)


def _attention_call(q, q_start, lq, ktc, v1):
    b, hq, lt, dk = q.shape
    hk, n_chunks, bk = ktc.shape[1], ktc.shape[2], ktc.shape[-1]
    lk = n_chunks * bk
    dv = LANES // 2
    groups = hq // hk
    bq = min(ATTN_ROWS // groups, lq)
    assert lq % bq == 0 and q_start % bq == 0, "query range must tile into attention blocks"
    rows = groups * bq
    first = q_start // bq
    q5 = q.reshape(b, hk, groups, lt, dk)
    out = pl.pallas_call(
        functools.partial(_attn_kernel, groups=groups, bq=bq, n_chunks=n_chunks),
        grid=(b, hk, lq // bq),
        in_specs=[pl.BlockSpec((1, 1, groups, bq, dk), lambda i, h, j: (i, h, 0, j + first, 0)),
                  pl.BlockSpec((1, 1, n_chunks, dk, bk), lambda i, h, j: (i, h, 0, 0, 0)),
                  pl.BlockSpec((1, 1, lk, LANES), lambda i, h, j: (i, h, 0, 0))],
        out_specs=pl.BlockSpec((1, 1, groups, bq, dv), lambda i, h, j: (i, h, 0, j, 0)),
        out_shape=jax.ShapeDtypeStruct((b, hk, groups, lq, dv), BF16),
        scratch_shapes=[pltpu.VMEM((rows, LANES), F32), pltpu.VMEM((rows, LANES), F32)],
        compiler_params=_params("parallel", "parallel", "arbitrary"),
        name="attention",
    )(q5, ktc, v1)
    return out.reshape(b, hq, lq, dv)


def _attention(q, ktc, v1, n_lat):
    lt = q.shape[2]
    n_ctx = lt - n_lat
    o_l = _attention_call(q, 0, n_lat, ktc, v1)
    o_c = _attention_call(q, n_lat, n_ctx, ktc[:, :, -1:, :, -n_ctx:], v1[:, :, n_lat:])
    return o_l, o_c


def _attn_out_specs(heads, bl, n_lat):
    n_lb = n_lat // bl
    return [pl.BlockSpec((1, heads, bl, LANES // 2), lambda i, j: (i, 0, jnp.minimum(j, n_lb - 1), 0)),
            pl.BlockSpec((1, heads, bl, LANES // 2), lambda i, j: (i, 0, jnp.maximum(j - n_lb, 0), 0))]


def _heads_of(ol_ref, oc_ref, heads, bl, n_lat):
    is_ctx = pl.program_id(1) >= n_lat // bl
    return jnp.concatenate([jnp.where(is_ctx, oc_ref[0, h], ol_ref[0, h]) for h in range(heads)], axis=-1)


def _ab_out_kernel(x_ref, ml_ref, mc_ref, y_ref, bonus_ref, gate_ref, ol_ref, oc_ref, lnw_ref, lnb_ref, wrw_ref,
                   wmla_ref, out_ref, *, bl, n_lat):
    rw = []
    for h in range(RW_HEADS):
        y = y_ref[0, 0, h] + y_ref[1, 0, h]
        mean = jnp.mean(y, axis=-1, keepdims=True)
        yc = y - mean
        var = jnp.mean(yc * yc, axis=-1, keepdims=True)
        yn = yc * lax.rsqrt(var + LN_X_EPS) * lnw_ref[h] + lnb_ref[h]
        rw.append((yn + bonus_ref[0, h]) * gate_ref[0, h])
    rw = jnp.concatenate(rw, axis=-1).astype(BF16)
    o = _heads_of(ol_ref, oc_ref, MLA_HEADS, bl, n_lat)
    acc = (jnp.dot(rw, wrw_ref[...], preferred_element_type=F32)
           + jnp.dot(o, wmla_ref[...], preferred_element_type=F32))
    gate = _row_mods(ml_ref, mc_ref, pl.program_id(1) * bl, bl, n_lat, 2)
    out_ref[0] = x_ref[0] + gate * acc


def _ab_out(x, mod_l, mod_c, y, bonus, gate, o, ln_w, ln_b, w_out, n_lat, lt):
    b, _, d = x.shape
    bl = TOKEN_BLOCK
    hd = RW_HEAD_DIM
    wrw = w_out[:RW_DIM].astype(BF16)
    wmla = w_out[RW_DIM:].astype(BF16)
    xs = pl.BlockSpec((1, bl, d), lambda i, j: (i, j, 0))
    hm = pl.BlockSpec((1, RW_HEADS, bl, hd), lambda i, j: (i, 0, j, 0))
    full = lambda a: pl.BlockSpec(a.shape, lambda i, j: (0,) * a.ndim)
    lnw, lnb = ln_w.reshape(RW_HEADS, 1, hd), ln_b.reshape(RW_HEADS, 1, hd)
    return pl.pallas_call(
        functools.partial(_ab_out_kernel, bl=bl, n_lat=n_lat),
        grid=(b, lt // bl),
        in_specs=[xs, pl.BlockSpec((1, 6, d), lambda i, j: (i, 0, 0)), pl.BlockSpec((1, 6, d), lambda i, j: (0, 0, 0)),
                  pl.BlockSpec((2, 1, RW_HEADS, bl, hd), lambda i, j: (0, i, 0, j, 0)), hm, hm]
        + _attn_out_specs(MLA_HEADS, bl, n_lat) + [full(lnw), full(lnb), full(wrw), full(wmla)],
        out_specs=xs,
        out_shape=jax.ShapeDtypeStruct((b, lt, d), F32),
        compiler_params=_params("parallel", "parallel"),
        name="ab_out",
    )(x, mod_l, mod_c, y, bonus, gate, *o, lnw, lnb, wrw, wmla)


def _gqa_out_kernel(x_ref, ml_ref, mc_ref, ol_ref, oc_ref, w_ref, out_ref, *, bl, n_lat):
    o = _heads_of(ol_ref, oc_ref, GQA_HEADS, bl, n_lat)
    acc = jnp.dot(o, w_ref[...], preferred_element_type=F32)
    gate = _row_mods(ml_ref, mc_ref, pl.program_id(1) * bl, bl, n_lat, 2)
    out_ref[0] = x_ref[0] + gate * acc


def _gqa_out(x, mod_l, mod_c, o, w_out, n_lat, lt):
    b, _, d = x.shape
    bl = TOKEN_BLOCK
    hd = GQA_HEAD_DIM
    w = w_out.astype(BF16)
    xs = pl.BlockSpec((1, bl, d), lambda i, j: (i, j, 0))
    return pl.pallas_call(
        functools.partial(_gqa_out_kernel, bl=bl, n_lat=n_lat),
        grid=(b, lt // bl),
        in_specs=[xs, pl.BlockSpec((1, 6, d), lambda i, j: (i, 0, 0)), pl.BlockSpec((1, 6, d), lambda i, j: (0, 0, 0))]
        + _attn_out_specs(GQA_HEADS, bl, n_lat) + [pl.BlockSpec(w.shape, lambda i, j: (0, 0))],
        out_specs=xs,
        out_shape=jax.ShapeDtypeStruct((b, lt, d), F32),
        compiler_params=_params("parallel", "parallel"),
        name="gqa_out",
    )(x, mod_l, mod_c, *o, w)


def _first_argmax(vals):
    best, idx = vals[0], jnp.zeros(vals[0].shape, jnp.int32)
    for i in range(1, len(vals)):
        better = vals[i] > best
        idx = jnp.where(better, i, idx)
        best = jnp.where(better, vals[i], best)
    return best, idx


def _pick(vals, idx):
    out = vals[0]
    for i in range(1, len(vals)):
        out = jnp.where(idx == i, vals[i], out)
    return out


def _route_kernel(x_ref, g_ref, ml_ref, mc_ref, rwh_ref, rwl_ref, rb_ref, t_ref, comb_ref, *, bl, n_lat):
    first = pl.program_id(1) * bl
    shift = _row_mods(ml_ref, mc_ref, first, bl, n_lat, 3)
    scale = _row_mods(ml_ref, mc_ref, first, bl, n_lat, 4)
    t = (_rms(x_ref[0]) * g_ref[...]) * (1.0 + scale) + shift
    t_ref[0] = t.astype(t_ref.dtype)
    t_hi = t.astype(BF16)
    t_lo = (t - t_hi.astype(F32)).astype(BF16)
    logits = (jnp.dot(t_hi, rwh_ref[...], preferred_element_type=F32)
              + jnp.dot(t_lo, rwh_ref[...], preferred_element_type=F32)
              + jnp.dot(t_hi, rwl_ref[...], preferred_element_type=F32))
    lt = logits.T
    score = [_sigmoid(lt[e:e + 1]) for e in range(N_EXPERTS)]
    biased = [score[e] + rb_ref[e:e + 1, :] for e in range(N_EXPERTS)]
    epg = EXPERTS_PER_GROUP
    group_score = []
    for g in range(N_GROUPS):
        vals = biased[g * epg:(g + 1) * epg]
        pair = [vals[i] + vals[j] for i in range(epg) for j in range(i + 1, epg)]
        group_score.append(functools.reduce(jnp.maximum, pair))
    _, grp = _first_argmax(group_score)
    in_b = [_pick([biased[g * epg + j] for g in range(N_GROUPS)], grp) for j in range(epg)]
    in_s = [_pick([score[g * epg + j] for g in range(N_GROUPS)], grp) for j in range(epg)]
    _, loc1 = _first_argmax(in_b)
    _, loc2 = _first_argmax([jnp.where(loc1 == j, -jnp.inf, in_b[j]) for j in range(epg)])
    w1, w2 = _pick(in_s, loc1), _pick(in_s, loc2)
    wsum = w1 + w2
    w1, w2 = w1 / wsum, w2 / wsum
    e1, e2 = grp * epg + loc1, grp * epg + loc2
    sub = lax.broadcasted_iota(jnp.int32, (LANES, bl), 0)
    comb = jnp.zeros((LANES, bl), F32)
    for e in range(N_EXPERTS):
        c_e = jnp.where(e1 == e, w1, 0.0) + jnp.where(e2 == e, w2, 0.0)
        comb = jnp.where(sub == e, jnp.broadcast_to(c_e, (LANES, bl)), comb)
    comb_ref[0] = comb.T


def _moe_route(x, g, mod_l, mod_c, router_w, router_b, n_lat):
    b, lt, d = x.shape
    bl = _row_block(lt, WIDE_ROWS)
    rw = jnp.pad(router_w, ((0, 0), (0, LANES - N_EXPERTS)))
    rw_hi = rw.astype(BF16)
    rw_lo = (rw - rw_hi.astype(F32)).astype(BF16)
    xs = pl.BlockSpec((1, bl, d), lambda i, j: (i, j, 0))
    return pl.pallas_call(
        functools.partial(_route_kernel, bl=bl, n_lat=n_lat),
        grid=(b, lt // bl),
        in_specs=[xs, pl.BlockSpec((1, d), lambda i, j: (0, 0)),
                  pl.BlockSpec((1, 6, d), lambda i, j: (i, 0, 0)), pl.BlockSpec((1, 6, d), lambda i, j: (0, 0, 0)),
                  pl.BlockSpec(rw.shape, lambda i, j: (0, 0)), pl.BlockSpec(rw.shape, lambda i, j: (0, 0)),
                  pl.BlockSpec((N_EXPERTS, 1), lambda i, j: (0, 0))],
        out_specs=[xs, pl.BlockSpec((1, bl, LANES), lambda i, j: (i, j, 0))],
        out_shape=[jax.ShapeDtypeStruct((b, lt, d), BF16), jax.ShapeDtypeStruct((b, lt, LANES), F32)],
        compiler_params=_params("parallel", "parallel"),
        name="moe_route",
    )(x, g.reshape(1, d), mod_l, mod_c, rw_hi, rw_lo, router_b.reshape(N_EXPERTS, 1))


def _ffn(t, wg, wu, wd):
    g = jnp.dot(t, wg, preferred_element_type=F32)
    u = jnp.dot(t, wu, preferred_element_type=F32)
    return jnp.dot((_silu(g) * u).astype(BF16), wd, preferred_element_type=F32)


def _experts_kernel(x_ref, t_ref, comb_ref, ml_ref, mc_ref, wg_ref, wu_ref, wd_ref, sg_ref, su_ref, sd_ref,
                    out_ref, acc_ref, *, bl, n_lat):
    g = pl.program_id(2)
    t = t_ref[0]

    @pl.when(g == 0)
    def _():
        acc_ref[...] = _ffn(t, sg_ref[...], su_ref[...], sd_ref[...])

    lane = lax.broadcasted_iota(jnp.int32, comb_ref.shape[1:], 1)
    comb = comb_ref[0]
    hs = []
    for i in range(EXPERTS_PER_GROUP):
        cw = jnp.sum(jnp.where(lane == g * EXPERTS_PER_GROUP + i, comb, 0.0), axis=-1, keepdims=True)
        gate = jnp.dot(t, wg_ref[i], preferred_element_type=F32)
        up = jnp.dot(t, wu_ref[i], preferred_element_type=F32)
        hs.append((_silu(gate) * up * cw).astype(BF16))
    wd = wd_ref[...]
    acc_ref[...] += jnp.dot(jnp.concatenate(hs, axis=-1), wd.reshape(wd.shape[0] * wd.shape[1], wd.shape[2]),
                            preferred_element_type=F32)

    @pl.when(g == pl.num_programs(2) - 1)
    def _():
        gate = _row_mods(ml_ref, mc_ref, pl.program_id(1) * bl, bl, n_lat, 5)
        out_ref[0] = x_ref[0] + gate * acc_ref[...]


def _moe_experts(x, t, comb, mod_l, mod_c, w_gate, w_up, w_down, sh_gate, sh_up, sh_down, n_lat, out_rows):
    b, lt, d = x.shape
    bl = _row_block(lt, WIDE_ROWS)
    ff = w_gate.shape[-1]
    epg = EXPERTS_PER_GROUP
    xs = pl.BlockSpec((1, bl, d), lambda i, j, g: (i, j, 0))
    full = lambda a: pl.BlockSpec(a.shape, lambda i, j, g: (0,) * a.ndim)
    ws = [w_gate.astype(BF16), w_up.astype(BF16), w_down.astype(BF16),
          sh_gate.astype(BF16), sh_up.astype(BF16), sh_down.astype(BF16)]
    return pl.pallas_call(
        functools.partial(_experts_kernel, bl=bl, n_lat=n_lat),
        grid=(b, lt // bl, N_GROUPS),
        in_specs=[xs, xs, pl.BlockSpec((1, bl, LANES), lambda i, j, g: (i, j, 0)),
                  pl.BlockSpec((1, 6, d), lambda i, j, g: (i, 0, 0)), pl.BlockSpec((1, 6, d), lambda i, j, g: (0, 0, 0)),
                  pl.BlockSpec((epg, d, ff), lambda i, j, g: (g, 0, 0)),
                  pl.BlockSpec((epg, d, ff), lambda i, j, g: (g, 0, 0)),
                  pl.BlockSpec((epg, ff, d), lambda i, j, g: (g, 0, 0)),
                  full(ws[3]), full(ws[4]), full(ws[5])],
        out_specs=xs,
        out_shape=jax.ShapeDtypeStruct((b, out_rows, d), F32),
        scratch_shapes=[pltpu.VMEM((bl, d), F32)],
        compiler_params=_params("parallel", "parallel", "arbitrary"),
        name="moe_experts",
    )(x, t, comb, mod_l, mod_c, *ws)


def _rope_tables(seq, n_ctx, rope_dims, lane_offset, repeat):
    half = rope_dims // 4
    t = jnp.arange(seq)
    rowp = (t // GRID_W).astype(F32)
    colp = (t % GRID_W).astype(F32)
    inv = ROPE_THETA ** (-jnp.arange(half, dtype=F32) / half)
    ar, ac = rowp[:, None] * inv[None, :], colp[:, None] * inv[None, :]
    cos = jnp.concatenate([jnp.cos(ar), jnp.cos(ar), jnp.cos(ac), jnp.cos(ac)], axis=1)
    sin = jnp.concatenate([-jnp.sin(ar), jnp.sin(ar), -jnp.sin(ac), jnp.sin(ac)], axis=1)
    width = LANES // repeat
    padl, padr = lane_offset, width - lane_offset - rope_dims
    cos = jnp.pad(cos, ((0, n_ctx), (0, 0)), constant_values=1.0)
    sin = jnp.pad(sin, ((0, n_ctx), (0, 0)))
    cos = jnp.pad(cos, ((0, 0), (padl, padr)), constant_values=1.0)
    sin = jnp.pad(sin, ((0, 0), (padl, padr)))
    return jnp.tile(cos, (1, repeat)), jnp.tile(sin, (1, repeat))


def kernel(x, c, ctx, c_ctx, ada_w, ada_b, norm1_g, norm2_g, ab_w_in, ab_w_out, rw_mu, rw_w0, rw_w2, rw_a0, rw_a2, rw_k_k, rw_k_a, rw_r_k, rw_g2, rw_ln_w, rw_ln_b, mla_g_qa, mla_w_q_up, mla_g_kva, mla_w_kv_up, mla_g_q, mla_g_k, gqa_w_in, gqa_w_out, gqa_g_q, gqa_g_k, router_w, router_b, moe_w_gate, moe_w_up, moe_w_down, shared_w_gate, shared_w_up, shared_w_down):
    b, n_lat, d = x.shape
    n_ctx = ctx.shape[1]
    depth = ada_w.shape[0]
    assert n_lat % TOKEN_BLOCK == 0 and n_ctx % TOKEN_BLOCK == 0, "both sequences must tile into token blocks"
    assert n_lat % GRID_W == 0, "latent tokens must fill whole grid rows"
    xs = jnp.concatenate([x, ctx], axis=1)

    cvec = jnp.concatenate([c, c_ctx[None, :]], axis=0)
    cvec = jnp.pad(cvec, ((0, 8 - (b + 1) % 8), (0, 0))) if (b + 1) % 8 else cvec
    mods = _ada_mods(cvec, ada_w, ada_b).reshape(depth, -1, 6, d)

    mla_cos, mla_sin = _rope_tables(n_lat, n_ctx, MLA_ROPE, MLA_NOPE, 1)
    gqa_cos, gqa_sin = _rope_tables(n_lat, n_ctx, GQA_HEAD_DIM, 0, 2)

    for l in range(depth):
        i = l // 2
        mod_l, mod_c = mods[l, :b], mods[l, b:b + 1]
        rows = n_lat + n_ctx
        if l % 2 == 0:
            w_in = ab_w_in[i]
            w_rw = w_in[:, :RW_COLS].astype(BF16)
            w_m = w_in[:, RW_COLS:]
            w_mla = jnp.concatenate(
                [w_m[:, :MLA_Q_RANK + MLA_KV_RANK],
                 jnp.pad(w_m[:, MLA_Q_RANK + MLA_KV_RANK:], ((0, 0), (MLA_NOPE, LANES - MLA_NOPE - MLA_ROPE)))],
                axis=1).astype(BF16)
            p_rw, p_mla = _ln_mod_mm(xs, norm1_g[l], mod_l, mod_c, [w_rw, w_mla], n_lat)
            r, v, na, lw, kd, bd, gate, bonus = _rwkv_prep(
                p_rw, n_lat, rw_mu[i], rw_w0[i], rw_w2[i], rw_a0[i], rw_a2[i], rw_k_k[i], rw_k_a[i],
                rw_r_k[i].reshape(-1), rw_g2[i])
            y = _wkv_scan(r, v, na, lw, kd, bd, n_lat)
            q, kt, vm = _mla_qkv(p_mla, mla_g_qa[i], mla_w_q_up[i], mla_g_kva[i], mla_w_kv_up[i],
                                 mla_g_q[i], mla_g_k[i], mla_cos, mla_sin)
            o = _attention(q, kt, vm, n_lat)
            xs = _ab_out(xs, mod_l, mod_c, y, bonus, gate, o, rw_ln_w[i], rw_ln_b[i], ab_w_out[i], n_lat, rows)
        else:
            q, kt, vg = _gqa_qkv(xs, norm1_g[l], mod_l, mod_c, gqa_w_in[i].astype(BF16), n_lat,
                                 gqa_g_q[i], gqa_g_k[i], gqa_cos, gqa_sin)
            o = _attention(q, kt, vg, n_lat)
            xs = _gqa_out(xs, mod_l, mod_c, o, gqa_w_out[i], n_lat, rows)
        t, comb = _moe_route(xs, norm2_g[l], mod_l, mod_c, router_w, router_b, n_lat)
        xs = _moe_experts(xs, t, comb, mod_l, mod_c, moe_w_gate[l], moe_w_up[l], moe_w_down[l],
                          shared_w_gate[l], shared_w_up[l], shared_w_down[l], n_lat,
                          n_lat if l == depth - 1 else rows)
    return xs
```
